```python
import math, functools
import jax, jax.numpy as jnp
from jax import lax
import numpy as np

D_MODEL = 1024
BATCH = 8
SEQ = 8192
DEPTH = 2

GRID_W = 64
CTX_LEN = 256
N_EVEN = (DEPTH + 1) // 2
N_ODD = DEPTH // 2
N_MOD = 6
NORM_EPS = 1e-6

RWKV_HEADS = 8
RWKV_HEAD_DIM = 64
RWKV_WIDTH = RWKV_HEADS * RWKV_HEAD_DIM
DECAY_LORA = 64
ICLR_LORA = 64
GATE_LORA = 128
RWKV_GN_EPS = 64e-5
RWKV_SPLITS = (RWKV_WIDTH, RWKV_WIDTH, RWKV_WIDTH, DECAY_LORA, DECAY_LORA, ICLR_LORA, ICLR_LORA, GATE_LORA)
RWKV_COLS = sum(RWKV_SPLITS)

SSD_HEADS = 8
SSD_HEAD_DIM = 64
SSD_WIDTH = SSD_HEADS * SSD_HEAD_DIM
SSD_GROUPS = 2
SSD_STATE = 128
SSD_CONV = 3
SSD_CHUNK = 128
SSD_XBC = SSD_WIDTH + 2 * SSD_GROUPS * SSD_STATE
SSD_SPLITS = (SSD_WIDTH, SSD_XBC, SSD_HEADS, SSD_HEADS)
SSD_COLS = sum(SSD_SPLITS)

EVEN_IN = RWKV_COLS + SSD_COLS
MIX_WIDTH = RWKV_WIDTH + SSD_WIDTH

RET_HEADS = 8
RET_QK_DIM = 128
RET_V_DIM = 256
RET_CHUNK = 128
RET_V = RET_HEADS * RET_V_DIM
RET_SPLITS = (RET_HEADS * RET_QK_DIM, RET_HEADS * RET_QK_DIM, RET_V, RET_V)
RET_IN = sum(RET_SPLITS)
ROPE_BASE = 10000.0

D_FF = 2816
FFN_CONV = 3

kernel_name = 'hybrid_rwkv7_ssd_retention_dit'


def split_last(x, sizes):
    return jnp.split(x, np.cumsum(sizes)[:-1].tolist(), axis=-1)


def to_heads(t, n_heads):
    return t.reshape(t.shape[:-1] + (n_heads, t.shape[-1] // n_heads))


def rms_norm(x, g=None, eps=NORM_EPS):
    xf = x.astype(jnp.float32)
    y = xf * lax.rsqrt(jnp.mean(xf * xf, axis=-1, keepdims=True) + eps)
    if g is not None:
        y = y * g.astype(jnp.float32)
    return y.astype(x.dtype)


def modulate(h, shift, scale):
    return h * (1 + scale) + shift


def depthwise_conv_seq(x, w, b):
    k = w.shape[0]
    y = lax.conv_general_dilated(x, w[:, None, :], (1,), [(k // 2, k // 2)],
                                 dimension_numbers=('NWC', 'WIO', 'NWC'), feature_group_count=x.shape[-1])
    return y + b


def depthwise_conv_grid(x, w, b):
    bsz, t, ch = x.shape
    rows = t // GRID_W
    k = w.shape[0]
    y = lax.conv_general_dilated(x.reshape(bsz, rows, GRID_W, ch), w[:, :, None, :], (1, 1),
                                 [(k // 2, k // 2), (k // 2, k // 2)],
                                 dimension_numbers=('NHWC', 'HWIO', 'NHWC'), feature_group_count=ch)
    return y.reshape(bsz, t, ch) + b


def token_shift_bidir(u, mu_prev, mu_next):
    prev = jnp.pad(u, ((0, 0), (1, 0), (0, 0)))[:, :-1]
    nxt = jnp.pad(u, ((0, 0), (0, 1), (0, 0)))[:, 1:]
    return u + mu_prev * (prev - u) + mu_next * (nxt - u)


def rope_2d(x):
    t, dk = x.shape[1], x.shape[-1]
    n = dk // 4
    pos = jnp.arange(t)
    row = (pos // GRID_W).astype(jnp.float32)
    col = (pos % GRID_W).astype(jnp.float32)
    inv = ROPE_BASE ** (-jnp.arange(n, dtype=jnp.float32) / n)
    ang = jnp.concatenate([row[:, None] * inv, col[:, None] * inv], axis=-1)[:, None, :]
    cos, sin = jnp.cos(ang).astype(x.dtype), jnp.sin(ang).astype(x.dtype)
    x1, x2 = jnp.split(x, 2, axis=-1)
    return jnp.concatenate([x1 * cos - x2 * sin, x2 * cos + x1 * sin], axis=-1)


def run_bidirectional(scan_f, scan_b, ctx_f, ctx_b, lat_f, lat_b, state0):
    flip = lambda ts: tuple(jnp.flip(t, 1) for t in ts)
    yc_f, sc_f = scan_f(*ctx_f, state0)
    yl_f, _ = scan_f(*lat_f, sc_f)
    yc_b, sc_b = scan_b(*flip(ctx_b), state0)
    yl_b, _ = scan_b(*flip(lat_b), sc_b)
    return yc_f + jnp.flip(yc_b, 1), yl_f + jnp.flip(yl_b, 1)


def rwkv7_scan(r, decay, k, v, a, b, s0):
    def step(s, inp):
        r_t, w_t, k_t, v_t, a_t, b_t = inp
        sa = jnp.einsum('bhvk,bhk->bhv', s, a_t)
        s = s * w_t[:, :, None, :] + sa[..., None] * b_t[:, :, None, :] + v_t[..., None] * k_t[:, :, None, :]
        return s, jnp.einsum('bhvk,bhk->bhv', s, r_t)
    xs = tuple(jnp.moveaxis(t.astype(jnp.float32), 1, 0) for t in (r, decay, k, v, a, b))
    s, ys = lax.scan(step, s0.astype(jnp.float32), xs)
    return jnp.moveaxis(ys, 0, 1), s


def ssd_scan(x, dt, bm, cm, h0, A):
    f32 = jnp.float32
    bsz, t, nh, hp = x.shape
    ng, ns = bm.shape[2], bm.shape[3]
    hg = nh // ng
    L = SSD_CHUNK
    nc = t // L
    x = x.astype(f32).reshape(bsz, nc, L, ng, hg, hp)
    dt = dt.astype(f32).reshape(bsz, nc, L, ng, hg)
    bm = bm.astype(f32).reshape(bsz, nc, L, ng, ns)
    cm = cm.astype(f32).reshape(bsz, nc, L, ng, ns)
    cs = jnp.cumsum(dt * A.astype(f32).reshape(ng, hg), axis=2)
    xdt = x * dt[..., None]
    causal = jnp.tril(jnp.ones((L, L), bool))[:, :, None, None]
    seg = cs[:, :, :, None] - cs[:, :, None, :]
    decay = jnp.exp(jnp.where(causal, seg, -jnp.inf))
    cb = jnp.einsum('bclgn,bcsgn->bclsg', cm, bm)
    y = jnp.einsum('bclsg,bclsgh,bcsghp->bclghp', cb, decay, xdt)
    states = jnp.einsum('bcsgn,bcsgh,bcsghp->bcghpn', bm, jnp.exp(cs[:, :, -1:] - cs), xdt)
    def step(h, inp):
        s_c, a_c = inp
        return a_c[..., None, None] * h + s_c, h
    h_t, h_prev = lax.scan(step, h0.astype(f32).reshape(bsz, ng, hg, hp, ns),
                           (jnp.moveaxis(states, 1, 0), jnp.moveaxis(jnp.exp(cs[:, :, -1]), 1, 0)))
    h_prev = jnp.moveaxis(h_prev, 0, 1)
    y = y + jnp.einsum('bclgn,bcghpn,bclgh->bclghp', cm, h_prev, jnp.exp(cs))
    return y.reshape(bsz, t, nh, hp), h_t.reshape(bsz, nh, hp, ns)


def retention_scan(q, k, v, s0, log_gamma):
    f32 = jnp.float32
    bsz, t, nh, dk = q.shape
    dv = v.shape[-1]
    L = RET_CHUNK
    nc = t // L
    q = q.astype(f32).reshape(bsz, nc, L, nh, dk)
    k = k.astype(f32).reshape(bsz, nc, L, nh, dk)
    v = v.astype(f32).reshape(bsz, nc, L, nh, dv)
    lg = log_gamma.astype(f32)
    pos = jnp.arange(L, dtype=f32)
    rel = pos[:, None] - pos[None, :]
    inner = jnp.where(rel[..., None] >= 0, jnp.exp(jnp.maximum(rel, 0.0)[..., None] * lg), 0.0)
    scores = jnp.einsum('bclhd,bcshd->bclsh', q, k) * inner
    y = jnp.einsum('bclsh,bcshe->bclhe', scores, v)
    k_end = k * jnp.exp((L - 1 - pos)[:, None] * lg)[:, :, None]
    states = jnp.einsum('bcshd,bcshe->bchde', k_end, v)
    chunk_decay = jnp.exp(L * lg)[:, None, None]
    def step(s, s_c):
        return chunk_decay * s + s_c, s
    s_t, s_prev = lax.scan(step, s0.astype(f32), jnp.moveaxis(states, 1, 0))
    s_prev = jnp.moveaxis(s_prev, 0, 1)
    q_dec = q * jnp.exp((pos + 1)[:, None] * lg)[:, :, None]
    y = y + jnp.einsum('bclhd,bchde->bclhe', q_dec, s_prev)
    return y.reshape(bsz, t, nh, dv), s_t


def even_features(h, p):
    f32 = jnp.float32
    proj = h @ p['w_in']
    rw = token_shift_bidir(proj[..., :RWKV_COLS], p['mu_prev'], p['mu_next'])
    r, k, v, wd_f, wd_b, ad_f, ad_b, gd = split_last(rw, RWKV_SPLITS)
    rh, vh, kh = to_heads(r, RWKV_HEADS), to_heads(v, RWKV_HEADS), to_heads(k, RWKV_HEADS)
    kk = to_heads(k * p['k_k'], RWKV_HEADS).astype(f32)
    kk = kk / jnp.maximum(jnp.sqrt(jnp.sum(kk * kk, axis=-1, keepdims=True)), 1e-12)
    def direction(wd, ad, w0, w2, a0, a2):
        logw = (-jax.nn.softplus(-(w0 + jnp.tanh(wd) @ w2)) - 0.5).astype(f32)
        iclr = jax.nn.sigmoid(a0 + ad @ a2)
        k_dir = to_heads(k * (1 + (iclr - 1) * p['k_a']), RWKV_HEADS)
        iclr = to_heads(iclr, RWKV_HEADS)
        return (rh, to_heads(jnp.exp(-jnp.exp(logw)), RWKV_HEADS), k_dir, vh, -kk, kk * iclr)
    rw_f = direction(wd_f, ad_f, p['w0_f'], p['w2_f'], p['a0_f'], p['a2_f'])
    rw_b = direction(wd_b, ad_b, p['w0_b'], p['w2_b'], p['a0_b'], p['a2_b'])
    bonus = jnp.sum(rh * kh * p['r_k'], axis=-1, keepdims=True) * vh
    gate = jax.nn.sigmoid(gd) @ p['g2']
    z, xbc, dtr_f, dtr_b = split_last(proj[..., RWKV_COLS:], SSD_SPLITS)
    xbc = jax.nn.silu(depthwise_conv_seq(xbc, p['conv_w'], p['conv_b']))
    xs, bm, cm = split_last(xbc, (SSD_WIDTH, SSD_GROUPS * SSD_STATE, SSD_GROUPS * SSD_STATE))
    xs = to_heads(xs, SSD_HEADS)
    bm = to_heads(bm, SSD_GROUPS)
    cm = to_heads(cm, SSD_GROUPS)
    dt_f = jax.nn.softplus(dtr_f + p['dt_bias_f'])
    dt_b = jax.nn.softplus(dtr_b + p['dt_bias_b'])
    return {'rwkv_f': rw_f, 'rwkv_b': rw_b, 'bonus': bonus, 'gate': gate, 'z': z, 'xs': xs,
            'ssd_f': (xs, dt_f, bm, cm), 'ssd_b': (xs, dt_b, bm, cm)}


def even_mixer(hc, hx, p, need_ctx):
    f32 = jnp.float32
    fc, fx = even_features(hc, p), even_features(hx, p)
    bsz = hx.shape[0]
    s0 = jnp.zeros((bsz, RWKV_HEADS, RWKV_HEAD_DIM, RWKV_HEAD_DIM), f32)
    rk_c, rk_x = run_bidirectional(rwkv7_scan, rwkv7_scan, fc['rwkv_f'], fc['rwkv_b'],
                                   fx['rwkv_f'], fx['rwkv_b'], s0)
    h0 = jnp.zeros((bsz, SSD_HEADS, SSD_HEAD_DIM, SSD_STATE), f32)
    scan_f = functools.partial(ssd_scan, A=-jnp.exp(p['A_log_f'].astype(f32)))
    scan_b = functools.partial(ssd_scan, A=-jnp.exp(p['A_log_b'].astype(f32)))
    sd_c, sd_x = run_bidirectional(scan_f, scan_b, fc['ssd_f'], fc['ssd_b'], fx['ssd_f'], fx['ssd_b'], h0)

    def finish(f, y_rk, y_sd, dtype):
        mean = jnp.mean(y_rk, axis=-1, keepdims=True)
        var = jnp.var(y_rk, axis=-1, keepdims=True)
        y_rk = (y_rk - mean) * lax.rsqrt(var + RWKV_GN_EPS)
        y_rk = y_rk * p['ln_w'].reshape(RWKV_HEADS, RWKV_HEAD_DIM) + p['ln_b'].reshape(RWKV_HEADS, RWKV_HEAD_DIM)
        y_rk = (y_rk + f['bonus']).reshape(y_rk.shape[:-2] + (RWKV_WIDTH,)) * f['gate']
        y_sd = y_sd + p['D'][:, None] * f['xs']
        y_sd = rms_norm(y_sd.reshape(y_sd.shape[:-2] + (SSD_WIDTH,)) * jax.nn.silu(f['z']), p['norm_w'])
        return jnp.concatenate([y_rk, y_sd], axis=-1).astype(dtype) @ p['w_out']

    yx = finish(fx, rk_x, sd_x, hx.dtype)
    yc = finish(fc, rk_c, sd_c, hc.dtype) if need_ctx else None
    return yc, yx


def odd_features(h, p, on_grid):
    q, k, v, g = split_last(h @ p['w_in'], RET_SPLITS)
    q = to_heads(q, RET_HEADS)
    k = to_heads(k, RET_HEADS) * (RET_QK_DIM ** -0.5)
    if on_grid:
        q, k = rope_2d(q), rope_2d(k)
    return (q, k, to_heads(v, RET_HEADS)), g


def odd_mixer(hc, hx, p, need_ctx):
    f32 = jnp.float32
    (sc, gc), (sx, gx) = odd_features(hc, p, False), odd_features(hx, p, True)
    lg_f = jnp.log1p(-jnp.exp2(-p['log2_f'].astype(f32)))
    lg_b = jnp.log1p(-jnp.exp2(-p['log2_b'].astype(f32)))
    s0 = jnp.zeros((hx.shape[0], RET_HEADS, RET_QK_DIM, RET_V_DIM), f32)
    yc, yx = run_bidirectional(functools.partial(retention_scan, log_gamma=lg_f),
                               functools.partial(retention_scan, log_gamma=lg_b), sc, sc, sx, sx, s0)

    def finish(y, g, dtype):
        y = rms_norm(y).reshape(y.shape[:-2] + (RET_V,))
        return (jax.nn.silu(g) * y).astype(dtype) @ p['w_out']

    out_x = finish(yx, gx, hx.dtype)
    out_c = finish(yc, gc, hc.dtype) if need_ctx else None
    return out_c, out_x


def conv_ffn(h, w_up, conv_w, conv_b, w_down, on_grid):
    gate, val = jnp.split(h @ w_up, 2, axis=-1)
    if on_grid:
        gate = depthwise_conv_grid(gate, conv_w, conv_b)
    else:
        gate = depthwise_conv_seq(gate, conv_w[FFN_CONV // 2], conv_b)
    return (jax.nn.gelu(gate) * val) @ w_down


def setup_inputs(seed: int = 0) -> dict:
    key = jax.random.key(seed)
    keys = iter(jax.random.split(key, 64))
    f32 = jnp.float32
    def nrm(shape, scale):
        return scale * jax.random.normal(next(keys), shape, f32)
    def uni(shape, lo, hi):
        return jax.random.uniform(next(keys), shape, f32, lo, hi)
    def dt_bias(shape):
        dt = jnp.exp(uni(shape, math.log(1e-3), math.log(1e-1)))
        return dt + jnp.log(-jnp.expm1(-dt))
    D = D_MODEL
    E, O = N_EVEN, N_ODD
    ret_base = 5.0 + jnp.arange(RET_HEADS, dtype=f32)[None, :]
    return {
        'x': nrm((BATCH, SEQ, D), 1.0),
        'c': nrm((BATCH, D), 1.0),
        'ctx': nrm((BATCH, CTX_LEN, D), 1.0),
        'c_ctx': nrm((D,), 1.0),
        'mod_w': nrm((DEPTH, D, N_MOD * D), 0.3 * D ** -0.5),
        'mod_b': nrm((DEPTH, N_MOD * D), 0.02),
        'norm1_g': 1.0 + nrm((DEPTH, D), 0.02),
        'norm2_g': 1.0 + nrm((DEPTH, D), 0.02),
        'ffn_w_up': nrm((DEPTH, D, 2 * D_FF), D ** -0.5),
        'ffn_conv_w': nrm((DEPTH, FFN_CONV, FFN_CONV, D_FF), 1.0 / FFN_CONV),
        'ffn_conv_b': nrm((DEPTH, D_FF), 0.02),
        'ffn_w_down': nrm((DEPTH, D_FF, D), D_FF ** -0.5),
        'ev_w_in': nrm((E, D, EVEN_IN), D ** -0.5),
        'ev_mu_prev': uni((E, RWKV_COLS), 0.0, 0.5),
        'ev_mu_next': uni((E, RWKV_COLS), 0.0, 0.5),
        'rk_w0_f': uni((E, RWKV_WIDTH), -6.0, -1.0),
        'rk_w0_b': uni((E, RWKV_WIDTH), -6.0, -1.0),
        'rk_w2_f': nrm((E, DECAY_LORA, RWKV_WIDTH), 0.5 * DECAY_LORA ** -0.5),
        'rk_w2_b': nrm((E, DECAY_LORA, RWKV_WIDTH), 0.5 * DECAY_LORA ** -0.5),
        'rk_a0_f': nrm((E, RWKV_WIDTH), 0.1),
        'rk_a0_b': nrm((E, RWKV_WIDTH), 0.1),
        'rk_a2_f': nrm((E, ICLR_LORA, RWKV_WIDTH), 0.5 * ICLR_LORA ** -0.5),
        'rk_a2_b': nrm((E, ICLR_LORA, RWKV_WIDTH), 0.5 * ICLR_LORA ** -0.5),
        'rk_g2': nrm((E, GATE_LORA, RWKV_WIDTH), GATE_LORA ** -0.5),
        'rk_k_k': 0.85 + nrm((E, RWKV_WIDTH), 0.05),
        'rk_k_a': 1.0 + nrm((E, RWKV_WIDTH), 0.05),
        'rk_r_k': nrm((E, RWKV_HEADS, RWKV_HEAD_DIM), 0.1),
        'rk_ln_w': 1.0 + nrm((E, RWKV_WIDTH), 0.02),
        'rk_ln_b': nrm((E, RWKV_WIDTH), 0.02),
        'ssd_conv_w': nrm((E, SSD_CONV, SSD_XBC), SSD_CONV ** -0.5),
        'ssd_conv_b': nrm((E, SSD_XBC), 0.02),
        'ssd_dt_bias_f': dt_bias((E, SSD_HEADS)),
        'ssd_dt_bias_b': dt_bias((E, SSD_HEADS)),
        'ssd_a_log_f': jnp.log(uni((E, SSD_HEADS), 1.0, 16.0)),
        'ssd_a_log_b': jnp.log(uni((E, SSD_HEADS), 1.0, 16.0)),
        'ssd_d': 1.0 + nrm((E, SSD_HEADS), 0.1),
        'ssd_norm_w': 1.0 + nrm((E, SSD_WIDTH), 0.02),
        'ev_w_out': nrm((E, MIX_WIDTH, D), MIX_WIDTH ** -0.5),
        'ret_w_in': nrm((O, D, RET_IN), D ** -0.5),
        'ret_log2_f': ret_base + uni((O, RET_HEADS), -0.3, 0.3),
        'ret_log2_b': ret_base + uni((O, RET_HEADS), -0.3, 0.3),
        'ret_w_out': nrm((O, RET_V, D), RET_V ** -0.5),
        'final_norm_g': 1.0 + nrm((D,), 0.02),
    }


def reference(x, c, ctx, c_ctx, mod_w, mod_b, norm1_g, norm2_g, ffn_w_up, ffn_conv_w, ffn_conv_b,
              ffn_w_down, ev_w_in, ev_mu_prev, ev_mu_next, rk_w0_f, rk_w0_b, rk_w2_f, rk_w2_b, rk_a0_f,
              rk_a0_b, rk_a2_f, rk_a2_b, rk_g2, rk_k_k, rk_k_a, rk_r_k, rk_ln_w, rk_ln_b, ssd_conv_w,
              ssd_conv_b, ssd_dt_bias_f, ssd_dt_bias_b, ssd_a_log_f, ssd_a_log_b, ssd_d, ssd_norm_w,
              ev_w_out, ret_w_in, ret_log2_f, ret_log2_b, ret_w_out, final_norm_g):
    cond_x = jax.nn.silu(c)[:, None, :]
    cond_c = jax.nn.silu(c_ctx)[None, None, :]
    for i in range(DEPTH):
        need_ctx = i < DEPTH - 1
        mx = jnp.split(cond_x @ mod_w[i] + mod_b[i], N_MOD, axis=-1)
        mc = jnp.split(cond_c @ mod_w[i] + mod_b[i], N_MOD, axis=-1)
        hx = modulate(rms_norm(x, norm1_g[i]), mx[0], mx[1])
        hc = modulate(rms_norm(ctx, norm1_g[i]), mc[0], mc[1])
        j = i // 2
        if i % 2 == 0:
            p = {'w_in': ev_w_in[j], 'mu_prev': ev_mu_prev[j], 'mu_next': ev_mu_next[j],
                 'w0_f': rk_w0_f[j], 'w0_b': rk_w0_b[j], 'w2_f': rk_w2_f[j], 'w2_b': rk_w2_b[j],
                 'a0_f': rk_a0_f[j], 'a0_b': rk_a0_b[j], 'a2_f': rk_a2_f[j], 'a2_b': rk_a2_b[j],
                 'g2': rk_g2[j], 'k_k': rk_k_k[j], 'k_a': rk_k_a[j], 'r_k': rk_r_k[j],
                 'ln_w': rk_ln_w[j], 'ln_b': rk_ln_b[j], 'conv_w': ssd_conv_w[j], 'conv_b': ssd_conv_b[j],
                 'dt_bias_f': ssd_dt_bias_f[j], 'dt_bias_b': ssd_dt_bias_b[j],
                 'A_log_f': ssd_a_log_f[j], 'A_log_b': ssd_a_log_b[j], 'D': ssd_d[j],
                 'norm_w': ssd_norm_w[j], 'w_out': ev_w_out[j]}
            yc, yx = even_mixer(hc, hx, p, need_ctx)
        else:
            p = {'w_in': ret_w_in[j], 'log2_f': ret_log2_f[j], 'log2_b': ret_log2_b[j], 'w_out': ret_w_out[j]}
            yc, yx = odd_mixer(hc, hx, p, need_ctx)
        x = x + mx[2] * yx
        x = x + mx[5] * conv_ffn(modulate(rms_norm(x, norm2_g[i]), mx[3], mx[4]),
                                 ffn_w_up[i], ffn_conv_w[i], ffn_conv_b[i], ffn_w_down[i], True)
        if need_ctx:
            ctx = ctx + mc[2] * yc
            ctx = ctx + mc[5] * conv_ffn(modulate(rms_norm(ctx, norm2_g[i]), mc[3], mc[4]),
                                         ffn_w_up[i], ffn_conv_w[i], ffn_conv_b[i], ffn_w_down[i], False)
    return rms_norm(x, final_norm_g)
```

```python
import functools
import math

import jax
import jax.numpy as jnp
from jax import lax
from jax.experimental import pallas as pl
from jax.experimental.pallas import tpu as pltpu

F32 = jnp.float32
BF16 = jnp.bfloat16

D_MODEL = 1024
GRID_W = 64
N_MOD = 6
NORM_EPS = 1e-6
RWKV_HEADS = 8
RWKV_HEAD_DIM = 64
RWKV_WIDTH = RWKV_HEADS * RWKV_HEAD_DIM
DECAY_LORA = 64
ICLR_LORA = 64
GATE_LORA = 128
RWKV_GN_EPS = 64e-5
RWKV_COLS = 3 * RWKV_WIDTH + 2 * DECAY_LORA + 2 * ICLR_LORA + GATE_LORA
SSD_HEADS = 8
SSD_HEAD_DIM = 64
SSD_WIDTH = SSD_HEADS * SSD_HEAD_DIM
SSD_GROUPS = 2
SSD_STATE = 128
SSD_XBC = SSD_WIDTH + 2 * SSD_GROUPS * SSD_STATE
RET_HEADS = 8
RET_QK_DIM = 128
RET_V_DIM = 256
RET_QK = RET_HEADS * RET_QK_DIM
RET_V = RET_HEADS * RET_V_DIM
ROPE_BASE = 10000.0
D_FF = 2816

V7X_LANES = 128
V7X_SUBLANES = 8
V7X_VMEM_LIMIT_BYTES = 56 * 1024 * 1024

RWKV_CHUNK = 64
SCAN_CHUNK = 128
PAIR = 2 * RWKV_HEAD_DIM
N_PAIRS = RWKV_HEADS // 2
EV_RW_BLOCK = 2048
EV_DT_OFF = RWKV_COLS
EV_XBC_OFF = EV_RW_BLOCK
EV_Z_OFF = EV_RW_BLOCK + SSD_XBC
EV_COLS = EV_Z_OFF + SSD_WIDTH


def _cparams(sem):
    return pltpu.CompilerParams(dimension_semantics=sem, vmem_limit_bytes=V7X_VMEM_LIMIT_BYTES)


def _split3(x):
    hi = x.astype(BF16)
    r1 = x - hi.astype(F32)
    mid = r1.astype(BF16)
    lo = (r1 - mid.astype(F32)).astype(BF16)
    return hi, mid, lo


def _dot(a, b):
    return jnp.dot(a, b, preferred_element_type=F32)


def _dot_nt(a, b):
    return lax.dot_general(a, b, (((1,), (1,)), ((), ())), preferred_element_type=F32)


def _dot_tn(a, b):
    return lax.dot_general(a, b, (((0,), (0,)), ((), ())), preferred_element_type=F32)


def _dot01(x, m01):
    hi, mid, lo = _split3(x)
    return _dot(hi, m01) + _dot(mid, m01) + _dot(lo, m01)


def _dot01_left(m01, x):
    hi, mid, lo = _split3(x)
    return _dot(m01, hi) + _dot(m01, mid) + _dot(m01, lo)


def _sigmoid(x):
    return 1.0 / (1.0 + jnp.exp(-x))


def _silu(x):
    return x * _sigmoid(x)


def _softplus(x):
    return jnp.maximum(x, 0.0) + jnp.log1p(jnp.exp(-jnp.abs(x)))


def _gelu_tanh(x):
    return 0.5 * x * (1.0 + jnp.tanh(math.sqrt(2.0 / math.pi) * (x + 0.044715 * (x * x * x))))


def _order_masks(d, n):
    row = lax.broadcasted_iota(jnp.int32, (n, n), 0)
    col = lax.broadcasted_iota(jnp.int32, (n, n), 1)
    diff = (row - col) * (1 - 2 * d)
    return diff > 0, diff >= 0


def _mod_kernel(c_ref, w_ref, b_ref, o_ref):
    h = _silu(c_ref[...])
    hi, mid, lo = _split3(h)
    w = w_ref[...]
    wh = w.astype(BF16)
    wl = (w - wh.astype(F32)).astype(BF16)
    acc = _dot(hi, wh) + _dot(mid, wh) + _dot(hi, wl)
    o_ref[...] = acc + b_ref[...]


def _mod_call(cond, mod_w, mod_b):
    depth, d, n = mod_w.shape
    rows = cond.shape[0]
    tn = 1024
    return pl.pallas_call(
        _mod_kernel,
        grid=(depth, n // tn),
        in_specs=[pl.BlockSpec((rows, d), lambda l, j: (0, 0)),
                  pl.BlockSpec((None, d, tn), lambda l, j: (l, 0, j)),
                  pl.BlockSpec((None, 1, tn), lambda l, j: (l, 0, j))],
        out_specs=pl.BlockSpec((None, rows, tn), lambda l, j: (l, 0, j)),
        out_shape=jax.ShapeDtypeStruct((depth, rows, n), F32),
        compiler_params=_cparams(("parallel", "parallel")),
    )(cond, mod_w, mod_b.reshape(depth, 1, n))


def _nm_kernel(x_ref, g_ref, m_ref, w_ref, o_ref, h_scr, *, shift_row):
    @pl.when(pl.program_id(2) == 0)
    def _():
        x = x_ref[...]
        h = x * lax.rsqrt(jnp.mean(x * x, axis=-1, keepdims=True) + NORM_EPS) * g_ref[...]
        h = h * (1.0 + m_ref[shift_row + 1:shift_row + 2, :]) + m_ref[shift_row:shift_row + 1, :]
        h_scr[...] = h.astype(BF16)
    o_ref[...] = _dot(h_scr[...], w_ref[...]).astype(o_ref.dtype)


def _nm_call(x, g, mods, w, *, shift_row, tm, tn, out_dtype=F32):
    b, t, d = x.shape
    n = w.shape[1]
    tm = min(tm, t)
    per_batch = mods.shape[0] > 1
    return pl.pallas_call(
        functools.partial(_nm_kernel, shift_row=shift_row),
        grid=(b, t // tm, n // tn),
        in_specs=[pl.BlockSpec((None, tm, d), lambda bi, i, j: (bi, i, 0)),
                  pl.BlockSpec((1, d), lambda bi, i, j: (0, 0)),
                  pl.BlockSpec((None, N_MOD, d), (lambda bi, i, j: (bi, 0, 0)) if per_batch
                               else (lambda bi, i, j: (0, 0, 0))),
                  pl.BlockSpec((d, tn), lambda bi, i, j: (0, j))],
        out_specs=pl.BlockSpec((None, tm, tn), lambda bi, i, j: (bi, i, j)),
        out_shape=jax.ShapeDtypeStruct((b, t, n), out_dtype),
        scratch_shapes=[pltpu.VMEM((tm, d), BF16)],
        compiler_params=_cparams(("parallel", "parallel", "arbitrary")),
    )(x, g, mods, w)


def _halo_specs(tm, width, col_block, n_row_tiles, t):
    r = tm // V7X_SUBLANES
    last = t // V7X_SUBLANES - 1
    return [pl.BlockSpec((None, tm, width), lambda bi, i: (bi, i, col_block)),
            pl.BlockSpec((None, V7X_SUBLANES, width),
                         lambda bi, i: (bi, jnp.maximum(i * r - 1, 0), col_block)),
            pl.BlockSpec((None, V7X_SUBLANES, width),
                         lambda bi, i: (bi, jnp.minimum((i + 1) * r, last), col_block))]


def _shifted(u, prev_blk, next_blk, i, n_tiles):
    tm = u.shape[0]
    row = lax.broadcasted_iota(jnp.int32, u.shape, 0)
    prev_row = jnp.where(i > 0, prev_blk[V7X_SUBLANES - 1:V7X_SUBLANES, :], 0.0)
    next_row = jnp.where(i < n_tiles - 1, next_blk[0:1, :], 0.0)
    prev = jnp.where(row == 0, prev_row, pltpu.roll(u, 1, axis=0))
    nxt = jnp.where(row == tm - 1, next_row, pltpu.roll(u, tm - 1, axis=0))
    return prev, nxt


def _rwkv_feat_kernel(u_ref, up_ref, un_ref, mup_ref, mun_ref, w0_ref, w2_ref, a0_ref, a2_ref,
                      g2_ref, kk_ref, ka_ref, rk_ref, j_ref,
                      r_ref, v_ref, kkn_ref, bonus_ref, gate_ref, ld_ref, kd_ref, bd_ref):
    i = pl.program_id(1)
    n_tiles = pl.num_programs(1)
    u = u_ref[...]
    prev, nxt = _shifted(u, up_ref[...], un_ref[...], i, n_tiles)
    rw = u + mup_ref[...] * (prev - u) + mun_ref[...] * (nxt - u)
    w = RWKV_WIDTH
    r = rw[:, 0:w]
    k = rw[:, w:2 * w]
    v = rw[:, 2 * w:3 * w]
    wd = rw[:, 3 * w:3 * w + 2 * DECAY_LORA]
    ad = rw[:, 3 * w + 2 * DECAY_LORA:3 * w + 2 * DECAY_LORA + 2 * ICLR_LORA]
    gd = rw[:, 3 * w + 2 * DECAY_LORA + 2 * ICLR_LORA:RWKV_COLS]
    jm = j_ref[...]

    kk = k * kk_ref[...]
    ss = _dot01(kk * kk, jm)
    kk = kk / jnp.maximum(jnp.sqrt(ss), 1e-12)
    r_ref[...] = r
    v_ref[...] = v
    kkn_ref[...] = kk
    bonus_ref[...] = _dot01(r * k * rk_ref[...], jm) * v
    gate_ref[...] = _dot(_sigmoid(gd).astype(BF16), g2_ref[...])

    zw = _dot(jnp.tanh(wd).astype(BF16), w2_ref[...]) + w0_ref[...]
    za = _dot(ad.astype(BF16), a2_ref[...]) + a0_ref[...]
    for di in range(2):
        ld_ref[di] = -math.exp(-0.5) * _sigmoid(zw[:, di * w:(di + 1) * w])
        iclr = _sigmoid(za[:, di * w:(di + 1) * w])
        kd_ref[di] = k * (1.0 + (iclr - 1.0) * ka_ref[...])
        bd_ref[di] = kk * iclr


def _rwkv_feat_call(proj, p, tm):
    b, t, _ = proj.shape
    tm = min(tm, t)
    n_tiles = t // tm
    w = RWKV_WIDTH
    full = lambda shape: pl.BlockSpec(shape, lambda bi, i: (0,) * len(shape))
    tok = pl.BlockSpec((None, tm, w), lambda bi, i: (bi, i, 0))
    tok2 = pl.BlockSpec((2, None, tm, w), lambda bi, i: (0, bi, i, 0))
    sd = jax.ShapeDtypeStruct((b, t, w), F32)
    sd2 = jax.ShapeDtypeStruct((2, b, t, w), F32)
    return pl.pallas_call(
        _rwkv_feat_kernel,
        grid=(b, n_tiles),
        in_specs=_halo_specs(tm, EV_RW_BLOCK, 0, n_tiles, t) + [
            full((1, EV_RW_BLOCK)), full((1, EV_RW_BLOCK)),
            full((1, 2 * w)), full((2 * DECAY_LORA, 2 * w)),
            full((1, 2 * w)), full((2 * ICLR_LORA, 2 * w)),
            full((GATE_LORA, w)), full((1, w)), full((1, w)), full((1, w)), full((w, w))],
        out_specs=[tok, tok, tok, tok, tok, tok2, tok2, tok2],
        out_shape=[sd, sd, sd, sd, sd, sd2, sd2, sd2],
        compiler_params=_cparams(("parallel", "parallel")),
    )(proj, proj, proj, p['mu_prev'], p['mu_next'], p['w0'], p['w2'], p['a0'], p['a2'],
      p['g2'], p['k_k'], p['k_a'], p['r_k'], p['head_sum'])


def _rwkv_scan_kernel(r_ref, v_ref, kk_ref, ld_ref, kd_ref, bd_ref, s0_ref, y_ref, sf_ref, st_scr,
                      *, n_chunks):
    d = pl.program_id(0)
    i = pl.program_id(2)
    c = RWKV_CHUNK

    @pl.when(i == 0)
    def _():
        st_scr[...] = s0_ref[...]

    before, before_eq = _order_masks(d, c)
    incl01 = before_eq.astype(BF16)
    row2 = lax.broadcasted_iota(jnp.int32, (2 * c, 2 * c), 0)
    col2 = lax.broadcasted_iota(jnp.int32, (2 * c, 2 * c), 1)
    same_head = (row2 // c) == (col2 // c)
    rr, cc = row2 % c, col2 % c
    strict_bd = jnp.logical_and(same_head, (rr - cc) * (1 - 2 * d) > 0)
    eye_bd = (row2 == col2).astype(F32)
    rowc = lax.broadcasted_iota(jnp.int32, (c, 2 * c), 0)
    colc = lax.broadcasted_iota(jnp.int32, (c, 2 * c), 1) % c
    incl_wide = (rowc - colc) * (1 - 2 * d) >= 0
    lane = lax.broadcasted_iota(jnp.int32, (c, PAIR), 1)
    m0 = lane < RWKV_HEAD_DIM
    last_row = jnp.where(d == 0, c - 1, 0)
    rsel = lax.broadcasted_iota(jnp.int32, (c, 1), 0) == last_row

    def stack(x):
        return jnp.concatenate([jnp.where(m0, x, 0.0), jnp.where(m0, 0.0, x)], axis=0)

    def chunk_body(j, carry):
        cj = jnp.where(d == 0, j, n_chunks - 1 - j)
        rows = pl.ds(pl.multiple_of(cj * c, c), c)
        cs_all = _dot01_left(incl01, ld_ref[rows, :])
        for p in range(N_PAIRS):
            lanes = slice(p * PAIR, (p + 1) * PAIR)
            ld = ld_ref[rows, lanes]
            cs = cs_all[:, lanes]
            cs_last = jnp.sum(jnp.where(rsel, cs, 0.0), axis=0, keepdims=True)
            g_in = jnp.exp(cs)
            g_ex = jnp.exp(cs - ld)
            g_neg = jnp.exp(-cs)
            g_end = jnp.exp(cs_last - cs)
            g_tot = jnp.exp(cs_last)
            kkv = kk_ref[rows, lanes]
            kdv = kd_ref[rows, lanes]
            bdv = bd_ref[rows, lanes]
            vv = v_ref[rows, lanes]
            a_s = stack(-kkv * g_ex).astype(BF16)
            r_t = (r_ref[rows, lanes] * g_in).astype(BF16)
            b_s = stack(bdv * g_neg).astype(BF16)
            k_s = stack(kdv * g_neg).astype(BF16)
            v_s = stack(vv).astype(BF16)
            lhs = jnp.concatenate([a_s, r_t], axis=0)
            gram = _dot_nt(lhs, jnp.concatenate([b_s, k_s], axis=0))
            a_ab = jnp.where(strict_bd, gram[0:2 * c, 0:2 * c], 0.0)
            a_ak = jnp.where(strict_bd, gram[0:2 * c, 2 * c:4 * c], 0.0)
            p_rb = jnp.where(incl_wide, gram[2 * c:3 * c, 0:2 * c], 0.0)
            p_rk = jnp.where(incl_wide, gram[2 * c:3 * c, 2 * c:4 * c], 0.0)
            minv = eye_bd + a_ab
            pw = a_ab
            for _ in range(int(math.log2(c)) - 1):
                pwb = pw.astype(BF16)
                pw = _dot(pwb, pwb)
                minv = minv + _dot(minv.astype(BF16), pw.astype(BF16))
            st = st_scr[p]
            from_state = _dot_nt(lhs, st.astype(BF16))
            rhs = from_state[0:2 * c] + _dot(a_ak.astype(BF16), v_s)
            u_s = _dot(minv.astype(BF16), rhs.astype(BF16)).astype(BF16)
            y = from_state[2 * c:3 * c] + _dot(p_rb.astype(BF16), u_s) + _dot(p_rk.astype(BF16), v_s)
            y_ref[rows, lanes] = y
            bk_end = jnp.concatenate([stack(bdv * g_end), stack(kdv * g_end)], axis=0).astype(BF16)
            uv = jnp.concatenate([u_s, v_s], axis=0)
            st_scr[p] = st * g_tot + _dot_tn(uv, bk_end)
        return carry

    lax.fori_loop(0, n_chunks, chunk_body, 0)

    @pl.when(i == pl.num_programs(2) - 1)
    def _():
        sf_ref[...] = st_scr[...]


def _rwkv_scan_call(feat, s0, tb):
    r, v, kk, _, _, ld, kd, bd = feat
    b, t, w = r.shape
    tb = min(tb, t)
    nb = t // tb
    blk = lambda dd, i: i + dd * (nb - 1 - 2 * i)
    tok = pl.BlockSpec((None, tb, w), lambda dd, bi, i: (bi, blk(dd, i), 0))
    tok2 = pl.BlockSpec((None, None, tb, w), lambda dd, bi, i: (dd, bi, blk(dd, i), 0))
    st = pl.BlockSpec((None, None, N_PAIRS, PAIR, PAIR), lambda dd, bi, i: (dd, bi, 0, 0, 0))
    return pl.pallas_call(
        functools.partial(_rwkv_scan_kernel, n_chunks=tb // RWKV_CHUNK),
        grid=(2, b, nb),
        in_specs=[tok, tok, tok, tok2, tok2, tok2, st],
        out_specs=[tok2, st],
        out_shape=[jax.ShapeDtypeStruct((2, b, t, w), F32),
                   jax.ShapeDtypeStruct((2, b, N_PAIRS, PAIR, PAIR), F32)],
        scratch_shapes=[pltpu.VMEM((N_PAIRS, PAIR, PAIR), F32)],
        compiler_params=_cparams(("parallel", "parallel", "arbitrary")),
    )(r, v, kk, ld, kd, bd, s0)


def _ssd_feat_kernel(x_ref, xp_ref, xn_ref, dt_ref, cw_ref, cb_ref, dtb_ref, e_ref,
                     xbc_ref, dtbc_ref):
    i = pl.program_id(1)
    n_tiles = pl.num_programs(1)
    x = x_ref[...]
    prev, nxt = _shifted(x, xp_ref[...], xn_ref[...], i, n_tiles)
    y = prev * cw_ref[0:1, :] + x * cw_ref[1:2, :] + nxt * cw_ref[2:3, :] + cb_ref[...]
    xbc_ref[...] = _silu(y)
    dt_rep = _dot01(dt_ref[...], e_ref[...])
    dt = _softplus(dt_rep + dtb_ref[...])
    dtbc_ref[0] = dt[:, 0:SSD_WIDTH]
    dtbc_ref[1] = dt[:, SSD_WIDTH:2 * SSD_WIDTH]


def _ssd_feat_call(proj, p, tm):
    b, t, _ = proj.shape
    tm = min(tm, t)
    n_tiles = t // tm
    full = lambda shape: pl.BlockSpec(shape, lambda bi, i: (0,) * len(shape))
    return pl.pallas_call(
        _ssd_feat_kernel,
        grid=(b, n_tiles),
        in_specs=_halo_specs(tm, SSD_XBC, EV_XBC_OFF // SSD_XBC, n_tiles, t) + [
            pl.BlockSpec((None, tm, V7X_LANES), lambda bi, i: (bi, i, EV_DT_OFF // V7X_LANES)),
            full((3, SSD_XBC)), full((1, SSD_XBC)), full((1, 2 * SSD_WIDTH)),
            full((V7X_LANES, 2 * SSD_WIDTH))],
        out_specs=[pl.BlockSpec((None, tm, SSD_XBC), lambda bi, i: (bi, i, 0)),
                   pl.BlockSpec((2, None, tm, SSD_WIDTH), lambda bi, i: (0, bi, i, 0))],
        out_shape=[jax.ShapeDtypeStruct((b, t, SSD_XBC), F32),
                   jax.ShapeDtypeStruct((2, b, t, SSD_WIDTH), F32)],
        compiler_params=_cparams(("parallel", "parallel")),
    )(proj, proj, proj, proj, p['conv_w'], p['conv_b'], p['dt_bias'], p['dt_expand'])


def _ssd_scan_kernel(xbc_ref, dt_ref, a_ref, s0_ref, y_ref, sf_ref, st_scr, *, n_chunks):
    d = pl.program_id(0)
    i = pl.program_id(2)
    c = SCAN_CHUNK
    hd = SSD_HEAD_DIM

    @pl.when(i == 0)
    def _():
        st_scr[...] = s0_ref[...]

    _, before_eq = _order_masks(d, c)
    incl01 = before_eq.astype(BF16)
    last_row = jnp.where(d == 0, c - 1, 0)
    rsel = lax.broadcasted_iota(jnp.int32, (c, 1), 0) == last_row
    lane = lax.broadcasted_iota(jnp.int32, (c, PAIR), 1)
    m0 = lane < hd
    a_row = a_ref[...]

    def chunk_body(j, carry):
        cj = jnp.where(d == 0, j, n_chunks - 1 - j)
        rows = pl.ds(pl.multiple_of(cj * c, c), c)
        dt = dt_ref[rows, :]
        cs_all = _dot01_left(incl01, dt * a_row)
        for g in range(SSD_GROUPS):
            bm = xbc_ref[rows, SSD_WIDTH + g * SSD_STATE:SSD_WIDTH + (g + 1) * SSD_STATE]
            cm = xbc_ref[rows, SSD_WIDTH + (SSD_GROUPS + g) * SSD_STATE:
                         SSD_WIDTH + (SSD_GROUPS + g + 1) * SSD_STATE]
            bm_b = bm.astype(BF16)
            cm_b = cm.astype(BF16)
            cb = _dot_nt(cm_b, bm_b)
            for pp in range(N_PAIRS // SSD_GROUPS):
                p = g * (N_PAIRS // SSD_GROUPS) + pp
                lanes = slice(p * PAIR, (p + 1) * PAIR)
                cs = cs_all[:, lanes]
                cs_t = cs.T
                xdt = xbc_ref[rows, lanes] * dt[:, lanes]
                probs = []
                for hh in range(2):
                    col = cs[:, hh * hd:hh * hd + 1]
                    rowv = cs_t[hh * hd:hh * hd + 1, :]
                    dec = jnp.exp(jnp.where(before_eq, col - rowv, -jnp.inf))
                    probs.append((cb * dec).astype(BF16))
                xs2 = jnp.concatenate([jnp.where(m0, xdt, 0.0), jnp.where(m0, 0.0, xdt)],
                                      axis=0).astype(BF16)
                y = _dot(jnp.concatenate(probs, axis=1), xs2)
                st = st_scr[p]
                y = y + jnp.exp(cs) * _dot(cm_b, st.astype(BF16))
                y_ref[rows, lanes] = y
                cs_last = jnp.sum(jnp.where(rsel, cs, 0.0), axis=0, keepdims=True)
                xe = (xdt * jnp.exp(cs_last - cs)).astype(BF16)
                st_scr[p] = st * jnp.exp(cs_last) + _dot_tn(bm_b, xe)
        return carry

    lax.fori_loop(0, n_chunks, chunk_body, 0)

    @pl.when(i == pl.num_programs(2) - 1)
    def _():
        sf_ref[...] = st_scr[...]


def _ssd_scan_call(xbc, dtbc, a_rep, s0, tb):
    b, t, _ = xbc.shape
    tb = min(tb, t)
    nb = t // tb
    blk = lambda dd, i: i + dd * (nb - 1 - 2 * i)
    st = pl.BlockSpec((None, None, N_PAIRS, SSD_STATE, PAIR), lambda dd, bi, i: (dd, bi, 0, 0, 0))
    tok2 = pl.BlockSpec((None, None, tb, SSD_WIDTH), lambda dd, bi, i: (dd, bi, blk(dd, i), 0))
    return pl.pallas_call(
        functools.partial(_ssd_scan_kernel, n_chunks=tb // SCAN_CHUNK),
        grid=(2, b, nb),
        in_specs=[pl.BlockSpec((None, tb, SSD_XBC), lambda dd, bi, i: (bi, blk(dd, i), 0)),
                  tok2,
                  pl.BlockSpec((None, 1, SSD_WIDTH), lambda dd, bi, i: (dd, 0, 0)),
                  st],
        out_specs=[tok2, st],
        out_shape=[jax.ShapeDtypeStruct((2, b, t, SSD_WIDTH), F32),
                   jax.ShapeDtypeStruct((2, b, N_PAIRS, SSD_STATE, PAIR), F32)],
        scratch_shapes=[pltpu.VMEM((N_PAIRS, SSD_STATE, PAIR), F32)],
        compiler_params=_cparams(("parallel", "parallel", "arbitrary")),
    )(xbc, dtbc, a_rep, s0)


def _even_finish_kernel(yrk_ref, bonus_ref, gate_ref, ysd_ref, xs_ref, z_ref, x_ref, m_ref,
                        lnw_ref, lnb_ref, dsk_ref, nw_ref, j_ref, wo_ref, o_ref):
    jm = j_ref[...]
    y = yrk_ref[0] + yrk_ref[1]
    inv_n = 1.0 / RWKV_HEAD_DIM
    mean = _dot01(y, jm) * inv_n
    yc = y - mean
    var = _dot01(yc * yc, jm) * inv_n
    y = yc * lax.rsqrt(var + RWKV_GN_EPS) * lnw_ref[...] + lnb_ref[...]
    y_rk = (y + bonus_ref[...]) * gate_ref[...]
    s = ysd_ref[0] + ysd_ref[1] + dsk_ref[...] * xs_ref[...]
    s = s * _silu(z_ref[...])
    s = s * lax.rsqrt(jnp.mean(s * s, axis=-1, keepdims=True) + NORM_EPS) * nw_ref[...]
    out = _dot(y_rk.astype(BF16), wo_ref[0:RWKV_WIDTH, :]) + \
        _dot(s.astype(BF16), wo_ref[RWKV_WIDTH:RWKV_WIDTH + SSD_WIDTH, :])
    o_ref[...] = x_ref[...] + m_ref[2:3, :] * out


def _even_finish_call(yrk, bonus, gate, ysd, xbc, proj, x, mods, p, tm):
    b, t, d = x.shape
    tm = min(tm, t)
    w = RWKV_WIDTH
    per_batch = mods.shape[0] > 1
    full = lambda shape: pl.BlockSpec(shape, lambda bi, i: (0,) * len(shape))
    tok = pl.BlockSpec((None, tm, w), lambda bi, i: (bi, i, 0))
    tok2 = pl.BlockSpec((2, None, tm, w), lambda bi, i: (0, bi, i, 0))
    return pl.pallas_call(
        _even_finish_kernel,
        grid=(b, t // tm),
        in_specs=[tok2, tok, tok, tok2,
                  pl.BlockSpec((None, tm, SSD_WIDTH), lambda bi, i: (bi, i, 0)),
                  pl.BlockSpec((None, tm, SSD_WIDTH), lambda bi, i: (bi, i, EV_Z_OFF // SSD_WIDTH)),
                  pl.BlockSpec((None, tm, d), lambda bi, i: (bi, i, 0)),
                  pl.BlockSpec((None, N_MOD, d), (lambda bi, i: (bi, 0, 0)) if per_batch
                               else (lambda bi, i: (0, 0, 0))),
                  full((1, w)), full((1, w)), full((1, w)), full((1, w)), full((w, w)),
                  full((2 * w, d))],
        out_specs=pl.BlockSpec((None, tm, d), lambda bi, i: (bi, i, 0)),
        out_shape=jax.ShapeDtypeStruct((b, t, d), F32),
        compiler_params=_cparams(("parallel", "parallel")),
    )(yrk, bonus, gate, ysd, xbc, proj, x, mods, p['ln_w'], p['ln_b'], p['d_skip'], p['norm_w'],
      p['head_sum'], p['w_out'])


def _ret_scan_kernel(*refs, n_chunks, on_grid):
    if on_grid:
        q_ref, k_ref, v_ref, cos_ref, sin_ref, lg_ref, s0_ref, y_ref, sf_ref, st_scr = refs
    else:
        q_ref, k_ref, v_ref, lg_ref, s0_ref, y_ref, sf_ref, st_scr = refs
    d = pl.program_id(0)
    i = pl.program_id(2)
    c = SCAN_CHUNK
    dk, dv = RET_QK_DIM, RET_V_DIM

    @pl.when(i == 0)
    def _():
        st_scr[...] = s0_ref[...]

    _, before_eq = _order_masks(d, c)
    row = lax.broadcasted_iota(jnp.int32, (c, c), 0)
    col = lax.broadcasted_iota(jnp.int32, (c, c), 1)
    rel = jnp.abs(row - col).astype(F32)
    pos_col = lax.broadcasted_iota(jnp.int32, (c, 1), 0)
    pos = jnp.where(d == 0, pos_col, c - 1 - pos_col).astype(F32)
    lg_all = lg_ref[...]

    def chunk_body(j, carry):
        cj = jnp.where(d == 0, j, n_chunks - 1 - j)
        rows = pl.ds(pl.multiple_of(cj * c, c), c)
        for h in range(RET_HEADS):
            ql = slice(h * dk, (h + 1) * dk)
            vl = slice(h * dv, (h + 1) * dv)
            lg = lg_all[:, h * dk:h * dk + 1]
            q = q_ref[rows, ql]
            k = k_ref[rows, ql] * (dk ** -0.5)
            if on_grid:
                cs_, sn_ = cos_ref[rows, :], sin_ref[rows, :]
                q = q * cs_ + pltpu.roll(q, dk // 2, axis=1) * sn_
                k = k * cs_ + pltpu.roll(k, dk // 2, axis=1) * sn_
            v_b = v_ref[rows, vl].astype(BF16)
            inner = jnp.where(before_eq, jnp.exp(rel * lg), 0.0)
            scores = (_dot_nt(q.astype(BF16), k.astype(BF16)) * inner).astype(BF16)
            st = st_scr[h]
            q_dec = (q * jnp.exp((pos + 1.0) * lg)).astype(BF16)
            y = _dot(scores, v_b) + _dot(q_dec, st.astype(BF16))
            y_ref[rows, vl] = y
            k_end = (k * jnp.exp((c - 1.0 - pos) * lg)).astype(BF16)
            st_scr[h] = st * jnp.exp(c * lg) + _dot_tn(k_end, v_b)
        return carry

    lax.fori_loop(0, n_chunks, chunk_body, 0)

    @pl.when(i == pl.num_programs(2) - 1)
    def _():
        sf_ref[...] = st_scr[...]


def _ret_scan_call(proj, lg_rep, s0, tb, rope):
    b, t, _ = proj.shape
    tb = min(tb, t)
    nb = t // tb
    on_grid = rope is not None
    blk = lambda dd, i: i + dd * (nb - 1 - 2 * i)
    st = pl.BlockSpec((None, None, RET_HEADS, RET_QK_DIM, RET_V_DIM), lambda dd, bi, i: (dd, bi, 0, 0, 0))
    in_specs = [pl.BlockSpec((None, tb, RET_QK), lambda dd, bi, i: (bi, blk(dd, i), 0)),
                pl.BlockSpec((None, tb, RET_QK), lambda dd, bi, i: (bi, blk(dd, i), 1)),
                pl.BlockSpec((None, tb, RET_V), lambda dd, bi, i: (bi, blk(dd, i), 2 * RET_QK // RET_V))]
    args = [proj, proj, proj]
    if on_grid:
        tab = pl.BlockSpec((tb, RET_QK_DIM), lambda dd, bi, i: (blk(dd, i), 0))
        in_specs += [tab, tab]
        args += list(rope)
    in_specs += [pl.BlockSpec((None, 1, RET_QK), lambda dd, bi, i: (dd, 0, 0)), st]
    args += [lg_rep, s0]
    return pl.pallas_call(
        functools.partial(_ret_scan_kernel, n_chunks=tb // SCAN_CHUNK, on_grid=on_grid),
        grid=(2, b, nb),
        in_specs=in_specs,
        out_specs=[pl.BlockSpec((None, None, tb, RET_V), lambda dd, bi, i: (dd, bi, blk(dd, i), 0)), st],
        out_shape=[jax.ShapeDtypeStruct((2, b, t, RET_V), F32),
                   jax.ShapeDtypeStruct((2, b, RET_HEADS, RET_QK_DIM, RET_V_DIM), F32)],
        scratch_shapes=[pltpu.VMEM((RET_HEADS, RET_QK_DIM, RET_V_DIM), F32)],
        compiler_params=_cparams(("parallel", "parallel", "arbitrary")),
    )(*args)


def _odd_finish_kernel(y_ref, g_ref, x_ref, m_ref, wo_ref, o_ref):
    y = y_ref[0] + y_ref[1]
    dv = RET_V_DIM
    parts = []
    for h in range(RET_HEADS):
        yh = y[:, h * dv:(h + 1) * dv]
        parts.append(yh * lax.rsqrt(jnp.mean(yh * yh, axis=-1, keepdims=True) + NORM_EPS))
    yn = jnp.concatenate(parts, axis=1)
    act = (_silu(g_ref[...]) * yn).astype(BF16)
    o_ref[...] = x_ref[...] + m_ref[2:3, :] * _dot(act, wo_ref[...])


def _odd_finish_call(y, proj, x, mods, w_out, tm):
    b, t, d = x.shape
    tm = min(tm, t)
    return pl.pallas_call(
        _odd_finish_kernel,
        grid=(b, t // tm),
        in_specs=[pl.BlockSpec((2, None, tm, RET_V), lambda bi, i: (0, bi, i, 0)),
                  pl.BlockSpec((None, tm, RET_V), lambda bi, i: (bi, i, (2 * RET_QK + RET_V) // RET_V)),
                  pl.BlockSpec((None, tm, d), lambda bi, i: (bi, i, 0)),
                  pl.BlockSpec((None, N_MOD, d), lambda bi, i: (bi, 0, 0)),
                  pl.BlockSpec((RET_V, d), lambda bi, i: (0, 0))],
        out_specs=pl.BlockSpec((None, tm, d), lambda bi, i: (bi, i, 0)),
        out_shape=jax.ShapeDtypeStruct((b, t, d), F32),
        compiler_params=_cparams(("parallel", "parallel")),
    )(y, proj, x, mods, w_out)


def _ffn_tail_kernel(*refs, on_grid, final_norm):
    if on_grid:
        g_ref, gp_ref, gn_ref, val_ref, x_ref, m_ref, cw_ref, cb_ref, wd_ref = refs[:9]
        rest = refs[9:]
    else:
        g_ref, val_ref, x_ref, m_ref, cw_ref, cb_ref, wd_ref = refs[:7]
        rest = refs[7:]
    if final_norm:
        fg_ref, o_ref = rest
    else:
        (o_ref,) = rest
    i = pl.program_id(1)
    n_tiles = pl.num_programs(1)
    g = g_ref[...]
    tm = g.shape[0]
    row = lax.broadcasted_iota(jnp.int32, (tm, 1), 0)
    if on_grid:
        col = row % GRID_W
        ok_left = col > 0
        ok_right = col < GRID_W - 1
        hb = gp_ref.shape[0]
        up = jnp.where(i > 0, gp_ref[hb - GRID_W:hb, :], 0.0)
        dn = jnp.where(i < n_tiles - 1, gn_ref[0:GRID_W, :], 0.0)
        slabs = (jnp.concatenate([up, g[0:tm - GRID_W]], axis=0), g,
                 jnp.concatenate([g[GRID_W:tm], dn], axis=0))
        acc = jnp.zeros_like(g) + cb_ref[...]
        for dr in range(3):
            s = slabs[dr]
            left = jnp.where(ok_left, pltpu.roll(s, 1, axis=0), 0.0)
            right = jnp.where(ok_right, pltpu.roll(s, tm - 1, axis=0), 0.0)
            acc = acc + left * cw_ref[3 * dr:3 * dr + 1, :] + s * cw_ref[3 * dr + 1:3 * dr + 2, :] \
                + right * cw_ref[3 * dr + 2:3 * dr + 3, :]
    else:
        left = jnp.where(row > 0, pltpu.roll(g, 1, axis=0), 0.0)
        right = jnp.where(row < tm - 1, pltpu.roll(g, tm - 1, axis=0), 0.0)
        acc = cb_ref[...] + left * cw_ref[3:4, :] + g * cw_ref[4:5, :] + right * cw_ref[5:6, :]
    act = (_gelu_tanh(acc) * val_ref[...]).astype(BF16)
    out = x_ref[...] + m_ref[5:6, :] * _dot(act, wd_ref[...])
    if final_norm:
        out = out * lax.rsqrt(jnp.mean(out * out, axis=-1, keepdims=True) + NORM_EPS) * fg_ref[...]
    o_ref[...] = out


def _ffn_tail_call(up, x, mods, conv_w9, conv_b, w_down, *, tm, on_grid, final_g=None):
    b, t, d = x.shape
    tm = min(tm, t)
    n_tiles = t // tm
    hb = 2 * GRID_W
    per_batch = mods.shape[0] > 1
    full = lambda shape: pl.BlockSpec(shape, lambda bi, i: (0,) * len(shape))
    main = pl.BlockSpec((None, tm, D_FF), lambda bi, i: (bi, i, 0))
    in_specs = [main]
    args = [up]
    if on_grid:
        r = tm // hb
        last = t // hb - 1
        in_specs += [pl.BlockSpec((None, hb, D_FF), lambda bi, i: (bi, jnp.maximum(i * r - 1, 0), 0)),
                     pl.BlockSpec((None, hb, D_FF), lambda bi, i: (bi, jnp.minimum((i + 1) * r, last), 0))]
        args += [up, up]
    else:
        assert n_tiles == 1
    in_specs += [pl.BlockSpec((None, tm, D_FF), lambda bi, i: (bi, i, 1)),
                 pl.BlockSpec((None, tm, d), lambda bi, i: (bi, i, 0)),
                 pl.BlockSpec((None, N_MOD, d), (lambda bi, i: (bi, 0, 0)) if per_batch
                              else (lambda bi, i: (0, 0, 0))),
                 full((9, D_FF)), full((1, D_FF)), full((D_FF, d))]
    args += [up, x, mods, conv_w9, conv_b, w_down]
    if final_g is not None:
        in_specs.append(full((1, d)))
        args.append(final_g)
    return pl.pallas_call(
        functools.partial(_ffn_tail_kernel, on_grid=on_grid, final_norm=final_g is not None),
        grid=(b, n_tiles),
        in_specs=in_specs,
        out_specs=pl.BlockSpec((None, tm, d), lambda bi, i: (bi, i, 0)),
        out_shape=jax.ShapeDtypeStruct((b, t, d), F32),
        compiler_params=_cparams(("parallel", "parallel")),
    )(*args)


def _block_diag2(a, b):
    za = jnp.zeros((a.shape[0], b.shape[1]), a.dtype)
    zb = jnp.zeros((b.shape[0], a.shape[1]), a.dtype)
    return jnp.concatenate([jnp.concatenate([a, za], axis=1), jnp.concatenate([zb, b], axis=1)], axis=0)


def _pad_cols(a, n):
    return jnp.pad(a, ((0, 0), (0, n - a.shape[1])))


def _even_params(j, ev_w_in, ev_mu_prev, ev_mu_next, rk_w0_f, rk_w0_b, rk_w2_f, rk_w2_b, rk_a0_f,
                 rk_a0_b, rk_a2_f, rk_a2_b, rk_g2, rk_k_k, rk_k_a, rk_r_k, rk_ln_w, rk_ln_b,
                 ssd_conv_w, ssd_conv_b, ssd_dt_bias_f, ssd_dt_bias_b, ssd_a_log_f, ssd_a_log_b,
                 ssd_d, ssd_norm_w, ev_w_out):
    w_in = ev_w_in[j]
    rw = w_in[:, :RWKV_COLS]
    z = w_in[:, RWKV_COLS:RWKV_COLS + SSD_WIDTH]
    xbc = w_in[:, RWKV_COLS + SSD_WIDTH:RWKV_COLS + SSD_WIDTH + SSD_XBC]
    dts = w_in[:, RWKV_COLS + SSD_WIDTH + SSD_XBC:]
    w_packed = jnp.concatenate([_pad_cols(jnp.concatenate([rw, dts], axis=1), EV_RW_BLOCK), xbc, z], axis=1)
    head = jnp.arange(RWKV_WIDTH) // RWKV_HEAD_DIM
    head_sum = (head[:, None] == head[None, :]).astype(BF16)
    lane = jnp.arange(V7X_LANES)[:, None]
    tgt = jnp.arange(2 * SSD_WIDTH)[None, :]
    dt_expand = (lane == (tgt // SSD_WIDTH) * SSD_HEADS + (tgt % SSD_WIDTH) // SSD_HEAD_DIM).astype(BF16)
    rep = lambda a: jnp.repeat(a, SSD_HEAD_DIM)[None, :]
    row = lambda a: a[None, :]
    return {
        'w_in': w_packed.astype(BF16),
        'mu_prev': _pad_cols(row(ev_mu_prev[j]), EV_RW_BLOCK),
        'mu_next': _pad_cols(row(ev_mu_next[j]), EV_RW_BLOCK),
        'w0': row(jnp.concatenate([rk_w0_f[j], rk_w0_b[j]])),
        'w2': _block_diag2(rk_w2_f[j], rk_w2_b[j]).astype(BF16),
        'a0': row(jnp.concatenate([rk_a0_f[j], rk_a0_b[j]])),
        'a2': _block_diag2(rk_a2_f[j], rk_a2_b[j]).astype(BF16),
        'g2': rk_g2[j].astype(BF16),
        'k_k': row(rk_k_k[j]), 'k_a': row(rk_k_a[j]), 'r_k': row(rk_r_k[j].reshape(-1)),
        'ln_w': row(rk_ln_w[j]), 'ln_b': row(rk_ln_b[j]),
        'head_sum': head_sum,
        'conv_w': ssd_conv_w[j], 'conv_b': row(ssd_conv_b[j]),
        'dt_bias': jnp.concatenate([rep(ssd_dt_bias_f[j]), rep(ssd_dt_bias_b[j])], axis=1),
        'dt_expand': dt_expand,
        'a_rep': jnp.stack([rep(-jnp.exp(ssd_a_log_f[j])), rep(-jnp.exp(ssd_a_log_b[j]))]),
        'd_skip': rep(ssd_d[j]),
        'norm_w': row(ssd_norm_w[j]),
        'w_out': ev_w_out[j].astype(BF16),
    }


def _rope_tables(t):
    n = RET_QK_DIM // 4
    pos = jnp.arange(t)
    row = (pos // GRID_W).astype(F32)
    col = (pos % GRID_W).astype(F32)
    inv = ROPE_BASE ** (-jnp.arange(n, dtype=F32) / n)
    ang = jnp.concatenate([row[:, None] * inv, col[:, None] * inv], axis=-1)
    cos, sin = jnp.cos(ang), jnp.sin(ang)
    return jnp.concatenate([cos, cos], axis=-1), jnp.concatenate([-sin, sin], axis=-1)


def _conv_ffn(x, mods, norm_g, w_up, conv_w9, conv_b, w_down, *, on_grid, final_g=None):
    up = _nm_call(x, norm_g, mods, w_up, shift_row=3, tm=512, tn=D_FF)
    return _ffn_tail_call(up, x, mods, conv_w9, conv_b, w_down, tm=512, on_grid=on_grid, final_g=final_g)


def _even_layer(x, ctx, mods_x, mods_c, norm_g, p):
    b = x.shape[0]

    def features(h, mods):
        proj = _nm_call(h, norm_g, mods, p['w_in'], shift_row=0, tm=512, tn=EV_COLS)
        feat = _rwkv_feat_call(proj, p, 256)
        xbc, dtbc = _ssd_feat_call(proj, p, 256)
        return proj, feat, xbc, dtbc

    proj_c, feat_c, xbc_c, dt_c = features(ctx, mods_c)
    proj_x, feat_x, xbc_x, dt_x = features(x, mods_x)
    s0 = jnp.zeros((2, b, N_PAIRS, PAIR, PAIR), F32)
    yrk_c, s_ctx = _rwkv_scan_call(feat_c, s0, 256)
    yrk_x, _ = _rwkv_scan_call(feat_x, s_ctx, 256)
    h0 = jnp.zeros((2, b, N_PAIRS, SSD_STATE, PAIR), F32)
    ysd_c, h_ctx = _ssd_scan_call(xbc_c, dt_c, p['a_rep'], h0, 256)
    ysd_x, _ = _ssd_scan_call(xbc_x, dt_x, p['a_rep'], h_ctx, 256)
    x = _even_finish_call(yrk_x, feat_x[3], feat_x[4], ysd_x, xbc_x, proj_x, x, mods_x, p, 256)
    ctx = _even_finish_call(yrk_c, feat_c[3], feat_c[4], ysd_c, xbc_c, proj_c, ctx, mods_c, p, 256)
    return x, ctx


def _odd_layer(x, ctx, mods_x, mods_c, norm_g, w_in, lg_rep, w_out):
    b, t, _ = x.shape
    proj_c = _nm_call(ctx, norm_g, mods_c, w_in, shift_row=0, tm=256, tn=2048)
    proj_x = _nm_call(x, norm_g, mods_x, w_in, shift_row=0, tm=512, tn=2048)
    s0 = jnp.zeros((2, b, RET_HEADS, RET_QK_DIM, RET_V_DIM), F32)
    _, s_ctx = _ret_scan_call(proj_c, lg_rep, s0, 256, None)
    y, _ = _ret_scan_call(proj_x, lg_rep, s_ctx, 256, _rope_tables(t))
    return _odd_finish_call(y, proj_x, x, mods_x, w_out, 256)


def kernel(x, c, ctx, c_ctx, mod_w, mod_b, norm1_g, norm2_g, ffn_w_up, ffn_conv_w, ffn_conv_b, ffn_w_down, ev_w_in, ev_mu_prev, ev_mu_next, rk_w0_f, rk_w0_b, rk_w2_f, rk_w2_b, rk_a0_f, rk_a0_b, rk_a2_f, rk_a2_b, rk_g2, rk_k_k, rk_k_a, rk_r_k, rk_ln_w, rk_ln_b, ssd_conv_w, ssd_conv_b, ssd_dt_bias_f, ssd_dt_bias_b, ssd_a_log_f, ssd_a_log_b, ssd_d, ssd_norm_w, ev_w_out, ret_w_in, ret_log2_f, ret_log2_b, ret_w_out, final_norm_g):
    b, t, d = x.shape
    depth = mod_w.shape[0]
    rows = -(-(b + 1) // V7X_SUBLANES) * V7X_SUBLANES
    cond = jnp.concatenate([c, c_ctx[None, :], jnp.zeros((rows - b - 1, d), F32)], axis=0)
    mods = _mod_call(cond, mod_w, mod_b).reshape(depth, rows, N_MOD, d)
    for i in range(depth):
        need_ctx = i < depth - 1
        mods_x = mods[i, :b]
        mods_c = mods[i, b:b + 1]
        j = i // 2
        g1 = norm1_g[i][None, :]
        if i % 2 == 0:
            p = _even_params(j, ev_w_in, ev_mu_prev, ev_mu_next, rk_w0_f, rk_w0_b, rk_w2_f, rk_w2_b,
                             rk_a0_f, rk_a0_b, rk_a2_f, rk_a2_b, rk_g2, rk_k_k, rk_k_a, rk_r_k,
                             rk_ln_w, rk_ln_b, ssd_conv_w, ssd_conv_b, ssd_dt_bias_f, ssd_dt_bias_b,
                             ssd_a_log_f, ssd_a_log_b, ssd_d, ssd_norm_w, ev_w_out)
            x, ctx_mixed = _even_layer(x, ctx, mods_x, mods_c, g1, p)
        else:
            lg = jnp.stack([jnp.log1p(-jnp.exp2(-ret_log2_f[j])), jnp.log1p(-jnp.exp2(-ret_log2_b[j]))])
            lg_rep = jnp.repeat(lg, RET_QK_DIM, axis=-1)[:, None, :]
            x = _odd_layer(x, ctx, mods_x, mods_c, g1, ret_w_in[j].astype(BF16), lg_rep,
                           ret_w_out[j].astype(BF16))
            ctx_mixed = None
        g2 = norm2_g[i][None, :]
        w_up = ffn_w_up[i].astype(BF16)
        w_down = ffn_w_down[i].astype(BF16)
        conv_w9 = ffn_conv_w[i].reshape(9, D_FF)
        conv_b = ffn_conv_b[i][None, :]
        last = i == depth - 1
        x = _conv_ffn(x, mods_x, g2, w_up, conv_w9, conv_b, w_down, on_grid=True,
                      final_g=final_norm_g[None, :] if last else None)
        if need_ctx:
            ctx = _conv_ffn(ctx_mixed, mods_c, g2, w_up, conv_w9, conv_b, w_down, on_grid=False)
    return x
```

```python
import functools
import math

import jax
import jax.numpy as jnp
from jax import lax
from jax.experimental import pallas as pl
from jax.experimental.pallas import tpu as pltpu

F32 = jnp.float32
BF16 = jnp.bfloat16

D_MODEL = 1024
GRID_W = 64
N_MOD = 6
NORM_EPS = 1e-6
RWKV_HEADS = 8
RWKV_HEAD_DIM = 64
RWKV_WIDTH = RWKV_HEADS * RWKV_HEAD_DIM
DECAY_LORA = 64
ICLR_LORA = 64
GATE_LORA = 128
RWKV_GN_EPS = 64e-5
RWKV_COLS = 3 * RWKV_WIDTH + 2 * DECAY_LORA + 2 * ICLR_LORA + GATE_LORA
SSD_HEADS = 8
SSD_HEAD_DIM = 64
SSD_WIDTH = SSD_HEADS * SSD_HEAD_DIM
SSD_GROUPS = 2
SSD_STATE = 128
SSD_XBC = SSD_WIDTH + 2 * SSD_GROUPS * SSD_STATE
RET_HEADS = 8
RET_QK_DIM = 128
RET_V_DIM = 256
RET_QK = RET_HEADS * RET_QK_DIM
RET_V = RET_HEADS * RET_V_DIM
ROPE_BASE = 10000.0
D_FF = 2816

V7X_LANES = 128
V7X_SUBLANES = 8
V7X_VMEM_LIMIT_BYTES = 56 * 1024 * 1024

RWKV_CHUNK = 64
SCAN_CHUNK = 128
PAIR = 2 * RWKV_HEAD_DIM
N_PAIRS = RWKV_HEADS // 2
EV_RW_BLOCK = 2048
EV_DT_OFF = RWKV_COLS
EV_XBC_OFF = EV_RW_BLOCK
EV_Z_OFF = EV_RW_BLOCK + SSD_XBC
EV_COLS = EV_Z_OFF + SSD_WIDTH


def _cparams(sem):
    return pltpu.CompilerParams(dimension_semantics=sem, vmem_limit_bytes=V7X_VMEM_LIMIT_BYTES)


def _split3(x):
    hi = x.astype(BF16)
    r1 = x - hi.astype(F32)
    mid = r1.astype(BF16)
    lo = (r1 - mid.astype(F32)).astype(BF16)
    return hi, mid, lo


def _dot(a, b):
    return jnp.dot(a, b, preferred_element_type=F32)


def _dot_nt(a, b):
    return lax.dot_general(a, b, (((1,), (1,)), ((), ())), preferred_element_type=F32)


def _dot_tn(a, b):
    return lax.dot_general(a, b, (((0,), (0,)), ((), ())), preferred_element_type=F32)


def _dot01(x, m01):
    hi, mid, lo = _split3(x)
    return _dot(hi, m01) + _dot(mid, m01) + _dot(lo, m01)


def _dot01_left(m01, x):
    hi, mid, lo = _split3(x)
    return _dot(m01, hi) + _dot(m01, mid) + _dot(m01, lo)


def _sigmoid(x):
    return 1.0 / (1.0 + jnp.exp(-x))


def _silu(x):
    return x * _sigmoid(x)


def _softplus(x):
    return jnp.maximum(x, 0.0) + jnp.log1p(jnp.exp(-jnp.abs(x)))


def _gelu_tanh(x):
    return 0.5 * x * (1.0 + jnp.tanh(math.sqrt(2.0 / math.pi) * (x + 0.044715 * (x * x * x))))


def _order_masks(d, n):
    row = lax.broadcasted_iota(jnp.int32, (n, n), 0)
    col = lax.broadcasted_iota(jnp.int32, (n, n), 1)
    diff = (row - col) * (1 - 2 * d)
    return diff > 0, diff >= 0


def _mod_kernel(c_ref, w_ref, b_ref, o_ref):
    h = _silu(c_ref[...])
    hi, mid, lo = _split3(h)
    w = w_ref[...]
    wh = w.astype(BF16)
    wl = (w - wh.astype(F32)).astype(BF16)
    acc = _dot(hi, wh) + _dot(mid, wh) + _dot(hi, wl)
    o_ref[...] = acc + b_ref[...]


def _mod_call(cond, mod_w, mod_b):
    depth, d, n = mod_w.shape
    rows = cond.shape[0]
    tn = 1024
    return pl.pallas_call(
        _mod_kernel,
        name="adaln_mod",
        grid=(depth, n // tn),
        in_specs=[pl.BlockSpec((rows, d), lambda l, j: (0, 0)),
                  pl.BlockSpec((None, d, tn), lambda l, j: (l, 0, j)),
                  pl.BlockSpec((None, 1, tn), lambda l, j: (l, 0, j))],
        out_specs=pl.BlockSpec((None, rows, tn), lambda l, j: (l, 0, j)),
        out_shape=jax.ShapeDtypeStruct((depth, rows, n), F32),
        compiler_params=_cparams(("parallel", "parallel")),
    )(cond, mod_w, mod_b.reshape(depth, 1, n))


def _nm_kernel(x_ref, g_ref, m_ref, w_ref, o_ref, *, shift_row, tn):
    x = x_ref[...]
    h = x * lax.rsqrt(jnp.mean(x * x, axis=-1, keepdims=True) + NORM_EPS) * g_ref[...]
    h = h * (1.0 + m_ref[shift_row + 1:shift_row + 2, :]) + m_ref[shift_row:shift_row + 1, :]
    h = h.astype(BF16)
    for j in range(w_ref.shape[1] // tn):
        cols = slice(j * tn, (j + 1) * tn)
        o_ref[:, cols] = _dot(h, w_ref[:, cols]).astype(o_ref.dtype)


def _nm_call(x, g, mods, w, *, shift_row, tm, tn, out_dtype=F32):
    b, t, d = x.shape
    n = w.shape[1]
    tm = min(tm, t)
    per_batch = mods.shape[0] > 1
    return pl.pallas_call(
        functools.partial(_nm_kernel, shift_row=shift_row, tn=tn),
        name="norm_mod_matmul",
        grid=(b, t // tm),
        in_specs=[pl.BlockSpec((None, tm, d), lambda bi, i: (bi, i, 0)),
                  pl.BlockSpec((1, d), lambda bi, i: (0, 0)),
                  pl.BlockSpec((None, N_MOD, d), (lambda bi, i: (bi, 0, 0)) if per_batch
                               else (lambda bi, i: (0, 0, 0))),
                  pl.BlockSpec((d, n), lambda bi, i: (0, 0))],
        out_specs=pl.BlockSpec((None, tm, n), lambda bi, i: (bi, i, 0)),
        out_shape=jax.ShapeDtypeStruct((b, t, n), out_dtype),
        compiler_params=_cparams(("parallel", "parallel")),
    )(x, g, mods, w)


def _halo_specs(tm, width, col_block, n_row_tiles, t):
    r = tm // V7X_SUBLANES
    last = t // V7X_SUBLANES - 1
    return [pl.BlockSpec((None, tm, width), lambda bi, i: (bi, i, col_block)),
            pl.BlockSpec((None, V7X_SUBLANES, width),
                         lambda bi, i: (bi, jnp.maximum(i * r - 1, 0), col_block)),
            pl.BlockSpec((None, V7X_SUBLANES, width),
                         lambda bi, i: (bi, jnp.minimum((i + 1) * r, last), col_block))]


def _shifted(u, prev_blk, next_blk, i, n_tiles):
    tm = u.shape[0]
    row = lax.broadcasted_iota(jnp.int32, u.shape, 0)
    prev_row = jnp.where(i > 0, prev_blk[V7X_SUBLANES - 1:V7X_SUBLANES, :], 0.0)
    next_row = jnp.where(i < n_tiles - 1, next_blk[0:1, :], 0.0)
    prev = jnp.where(row == 0, prev_row, pltpu.roll(u, 1, axis=0))
    nxt = jnp.where(row == tm - 1, next_row, pltpu.roll(u, tm - 1, axis=0))
    return prev, nxt


def _rwkv_feat_kernel(u_ref, up_ref, un_ref, mup_ref, mun_ref, w0_ref, w2_ref, a0_ref, a2_ref,
                      g2_ref, kk_ref, ka_ref, rk_ref, j_ref,
                      r_ref, v_ref, kkn_ref, bonus_ref, gate_ref, ld_ref, kd_ref, bd_ref):
    i = pl.program_id(1)
    n_tiles = pl.num_programs(1)
    u = u_ref[...]
    prev, nxt = _shifted(u, up_ref[...], un_ref[...], i, n_tiles)
    rw = u + mup_ref[...] * (prev - u) + mun_ref[...] * (nxt - u)
    w = RWKV_WIDTH
    r = rw[:, 0:w]
    k = rw[:, w:2 * w]
    v = rw[:, 2 * w:3 * w]
    wd = rw[:, 3 * w:3 * w + 2 * DECAY_LORA]
    ad = rw[:, 3 * w + 2 * DECAY_LORA:3 * w + 2 * DECAY_LORA + 2 * ICLR_LORA]
    gd = rw[:, 3 * w + 2 * DECAY_LORA + 2 * ICLR_LORA:RWKV_COLS]
    jm = j_ref[...]

    kk = k * kk_ref[...]
    ss = _dot01(kk * kk, jm)
    kk = kk / jnp.maximum(jnp.sqrt(ss), 1e-12)
    r_ref[...] = r
    v_ref[...] = v
    kkn_ref[...] = kk
    bonus_ref[...] = _dot01(r * k * rk_ref[...], jm) * v
    gate_ref[...] = _dot(_sigmoid(gd).astype(BF16), g2_ref[...])

    zw = _dot(jnp.tanh(wd).astype(BF16), w2_ref[...]) + w0_ref[...]
    za = _dot(ad.astype(BF16), a2_ref[...]) + a0_ref[...]
    for di in range(2):
        ld_ref[di] = -math.exp(-0.5) * _sigmoid(zw[:, di * w:(di + 1) * w])
        iclr = _sigmoid(za[:, di * w:(di + 1) * w])
        kd_ref[di] = k * (1.0 + (iclr - 1.0) * ka_ref[...])
        bd_ref[di] = kk * iclr


def _rwkv_feat_call(proj, p, tm):
    b, t, _ = proj.shape
    tm = min(tm, t)
    n_tiles = t // tm
    w = RWKV_WIDTH
    full = lambda shape: pl.BlockSpec(shape, lambda bi, i: (0,) * len(shape))
    tok = pl.BlockSpec((None, tm, w), lambda bi, i: (bi, i, 0))
    tok2 = pl.BlockSpec((2, None, tm, w), lambda bi, i: (0, bi, i, 0))
    sd = jax.ShapeDtypeStruct((b, t, w), F32)
    sd2 = jax.ShapeDtypeStruct((2, b, t, w), F32)
    return pl.pallas_call(
        _rwkv_feat_kernel,
        name="rwkv_feat",
        grid=(b, n_tiles),
        in_specs=_halo_specs(tm, EV_RW_BLOCK, 0, n_tiles, t) + [
            full((1, EV_RW_BLOCK)), full((1, EV_RW_BLOCK)),
            full((1, 2 * w)), full((2 * DECAY_LORA, 2 * w)),
            full((1, 2 * w)), full((2 * ICLR_LORA, 2 * w)),
            full((GATE_LORA, w)), full((1, w)), full((1, w)), full((1, w)), full((w, w))],
        out_specs=[tok, tok, tok, tok, tok, tok2, tok2, tok2],
        out_shape=[sd, sd, sd, sd, sd, sd2, sd2, sd2],
        compiler_params=_cparams(("parallel", "parallel")),
    )(proj, proj, proj, p['mu_prev'], p['mu_next'], p['w0'], p['w2'], p['a0'], p['a2'],
      p['g2'], p['k_k'], p['k_a'], p['r_k'], p['head_sum'])


def _rwkv_scan_kernel(rf_ref, vf_ref, kkf_ref, ldf_ref, kdf_ref, bdf_ref,
                      rb_ref, vb_ref, kkb_ref, ldb_ref, kdb_ref, bdb_ref, s0_ref,
                      yf_ref, yb_ref, sf_ref, st_scr, *, n_chunks):
    i = pl.program_id(1)
    c = RWKV_CHUNK

    @pl.when(i == 0)
    def _():
        st_scr[...] = s0_ref[...]

    row2 = lax.broadcasted_iota(jnp.int32, (2 * c, 2 * c), 0)
    col2 = lax.broadcasted_iota(jnp.int32, (2 * c, 2 * c), 1)
    same_head = (row2 // c) == (col2 // c)
    eye_bd = (row2 == col2).astype(F32)
    rowc = lax.broadcasted_iota(jnp.int32, (c, 2 * c), 0)
    colc = lax.broadcasted_iota(jnp.int32, (c, 2 * c), 1) % c
    m0 = lax.broadcasted_iota(jnp.int32, (c, PAIR), 1) < RWKV_HEAD_DIM
    row1 = lax.broadcasted_iota(jnp.int32, (c, 1), 0)
    incl01, strict_bd, incl_wide, rsel = [], [], [], []
    for d in range(2):
        sgn = 1 - 2 * d
        incl01.append(_order_masks(d, c)[1].astype(BF16))
        strict_bd.append(jnp.logical_and(same_head, (row2 % c - col2 % c) * sgn > 0))
        incl_wide.append((rowc - colc) * sgn >= 0)
        rsel.append(row1 == (c - 1 if d == 0 else 0))
    refs = ((rf_ref, vf_ref, kkf_ref, ldf_ref, kdf_ref, bdf_ref, yf_ref),
            (rb_ref, vb_ref, kkb_ref, ldb_ref, kdb_ref, bdb_ref, yb_ref))
    chains = [(d, p) for d in range(2) for p in range(N_PAIRS)]
    lanes = [slice(p * PAIR, (p + 1) * PAIR) for p in range(N_PAIRS)]

    def stack(x):
        return jnp.concatenate([jnp.where(m0, x, 0.0), jnp.where(m0, 0.0, x)], axis=0)

    def chunk_body(j, carry):
        rows = (pl.ds(pl.multiple_of(j * c, c), c),
                pl.ds(pl.multiple_of((n_chunks - 1 - j) * c, c), c))
        cs_all = [_dot01_left(incl01[d], refs[d][3][rows[d], :]) for d in range(2)]
        a_s, r_t, b_s, k_s, v_s, b_end, k_end, g_tot = [], [], [], [], [], [], [], []
        for d, p in chains:
            r_ref, v_ref, kk_ref, ld_ref, kd_ref, bd_ref, _ = refs[d]
            rw, ln = rows[d], lanes[p]
            cs = cs_all[d][:, ln]
            cs_last = jnp.sum(jnp.where(rsel[d], cs, 0.0), axis=0, keepdims=True)
            g_neg = jnp.exp(-cs)
            g_end = jnp.exp(cs_last - cs)
            g_tot.append(jnp.exp(cs_last))
            kdv = kd_ref[rw, ln]
            bdv = bd_ref[rw, ln]
            a_s.append(stack(-kk_ref[rw, ln] * jnp.exp(cs - ld_ref[rw, ln])).astype(BF16))
            r_t.append((r_ref[rw, ln] * jnp.exp(cs)).astype(BF16))
            b_s.append(stack(bdv * g_neg).astype(BF16))
            k_s.append(stack(kdv * g_neg).astype(BF16))
            v_s.append(stack(v_ref[rw, ln]).astype(BF16))
            b_end.append(stack(bdv * g_end).astype(BF16))
            k_end.append(stack(kdv * g_end).astype(BF16))
        n = range(len(chains))
        dirs = [d for d, _ in chains]
        gram = [_dot_nt(jnp.concatenate([a_s[q], r_t[q]], axis=0),
                        jnp.concatenate([b_s[q], k_s[q]], axis=0)) for q in n]
        a_ab = [jnp.where(strict_bd[dirs[q]], gram[q][0:2 * c, 0:2 * c], 0.0) for q in n]
        a_ak = [jnp.where(strict_bd[dirs[q]], gram[q][0:2 * c, 2 * c:4 * c], 0.0).astype(BF16) for q in n]
        p_rb = [jnp.where(incl_wide[dirs[q]], gram[q][2 * c:3 * c, 0:2 * c], 0.0).astype(BF16) for q in n]
        p_rk = [jnp.where(incl_wide[dirs[q]], gram[q][2 * c:3 * c, 2 * c:4 * c], 0.0).astype(BF16) for q in n]
        minv = [eye_bd + a_ab[q] for q in n]
        pw = a_ab
        for _ in range(int(math.log2(c)) - 1):
            pwb = [pw[q].astype(BF16) for q in n]
            pw = [_dot(pwb[q], pwb[q]) for q in n]
            minv = [minv[q] + _dot(minv[q].astype(BF16), pw[q].astype(BF16)) for q in n]
        akv = [_dot(a_ak[q], v_s[q]).astype(BF16) for q in n]
        eff = [_dot(minv[q].astype(BF16), jnp.concatenate([a_s[q], akv[q]], axis=1)) for q in n]
        prkv = [_dot(p_rk[q], v_s[q]) for q in n]
        zst = [_dot_tn(v_s[q], k_end[q]) for q in n]
        st = [st_scr[d, p] for d, p in chains]
        fs = [_dot_nt(jnp.concatenate([eff[q][:, 0:PAIR].astype(BF16), r_t[q]], axis=0),
                      st[q].astype(BF16)) for q in n]
        u_s = [(fs[q][0:2 * c] + eff[q][:, PAIR:2 * PAIR]).astype(BF16) for q in n]
        for q, (d, p) in enumerate(chains):
            st_scr[d, p] = st[q] * g_tot[q] + _dot_tn(u_s[q], b_end[q]) + zst[q]
        for q, (d, p) in enumerate(chains):
            refs[d][6][rows[d], lanes[p]] = fs[q][2 * c:3 * c] + _dot(p_rb[q], u_s[q]) + prkv[q]
        return carry

    lax.fori_loop(0, n_chunks, chunk_body, 0)

    @pl.when(i == pl.num_programs(1) - 1)
    def _():
        sf_ref[...] = st_scr[...]


def _rwkv_scan_call(feat, s0, tb):
    r, v, kk, _, _, ld, kd, bd = feat
    b, t, w = r.shape
    tb = min(tb, t)
    nb = t // tb
    tok_f = pl.BlockSpec((None, tb, w), lambda bi, i: (bi, i, 0))
    tok_b = pl.BlockSpec((None, tb, w), lambda bi, i: (bi, nb - 1 - i, 0))
    dir_f = pl.BlockSpec((None, None, tb, w), lambda bi, i: (0, bi, i, 0))
    dir_b = pl.BlockSpec((None, None, tb, w), lambda bi, i: (1, bi, nb - 1 - i, 0))
    st = pl.BlockSpec((2, None, N_PAIRS, PAIR, PAIR), lambda bi, i: (0, bi, 0, 0, 0))
    y_sd = jax.ShapeDtypeStruct((b, t, w), F32)
    return pl.pallas_call(
        functools.partial(_rwkv_scan_kernel, n_chunks=tb // RWKV_CHUNK),
        name="rwkv_scan",
        grid=(b, nb),
        in_specs=[tok_f, tok_f, tok_f, dir_f, dir_f, dir_f, tok_b, tok_b, tok_b, dir_b, dir_b, dir_b, st],
        out_specs=[tok_f, tok_b, st],
        out_shape=[y_sd, y_sd, jax.ShapeDtypeStruct((2, b, N_PAIRS, PAIR, PAIR), F32)],
        scratch_shapes=[pltpu.VMEM((2, N_PAIRS, PAIR, PAIR), F32)],
        compiler_params=_cparams(("parallel", "arbitrary")),
    )(r, v, kk, ld, kd, bd, r, v, kk, ld, kd, bd, s0)


def _ssd_feat_kernel(x_ref, xp_ref, xn_ref, dt_ref, cw_ref, cb_ref, dtb_ref, e_ref,
                     xbc_ref, dtbc_ref):
    i = pl.program_id(1)
    n_tiles = pl.num_programs(1)
    x = x_ref[...]
    prev, nxt = _shifted(x, xp_ref[...], xn_ref[...], i, n_tiles)
    y = prev * cw_ref[0:1, :] + x * cw_ref[1:2, :] + nxt * cw_ref[2:3, :] + cb_ref[...]
    xbc_ref[...] = _silu(y)
    dt_rep = _dot01(dt_ref[...], e_ref[...])
    dt = _softplus(dt_rep + dtb_ref[...])
    dtbc_ref[0] = dt[:, 0:SSD_WIDTH]
    dtbc_ref[1] = dt[:, SSD_WIDTH:2 * SSD_WIDTH]


def _ssd_feat_call(proj, p, tm):
    b, t, _ = proj.shape
    tm = min(tm, t)
    n_tiles = t // tm
    full = lambda shape: pl.BlockSpec(shape, lambda bi, i: (0,) * len(shape))
    return pl.pallas_call(
        _ssd_feat_kernel,
        name="ssd_feat",
        grid=(b, n_tiles),
        in_specs=_halo_specs(tm, SSD_XBC, EV_XBC_OFF // SSD_XBC, n_tiles, t) + [
            pl.BlockSpec((None, tm, V7X_LANES), lambda bi, i: (bi, i, EV_DT_OFF // V7X_LANES)),
            full((3, SSD_XBC)), full((1, SSD_XBC)), full((1, 2 * SSD_WIDTH)),
            full((V7X_LANES, 2 * SSD_WIDTH))],
        out_specs=[pl.BlockSpec((None, tm, SSD_XBC), lambda bi, i: (bi, i, 0)),
                   pl.BlockSpec((2, None, tm, SSD_WIDTH), lambda bi, i: (0, bi, i, 0))],
        out_shape=[jax.ShapeDtypeStruct((b, t, SSD_XBC), F32),
                   jax.ShapeDtypeStruct((2, b, t, SSD_WIDTH), F32)],
        compiler_params=_cparams(("parallel", "parallel")),
    )(proj, proj, proj, proj, p['conv_w'], p['conv_b'], p['dt_bias'], p['dt_expand'])


def _ssd_scan_kernel(xbc_ref, dt_ref, a_ref, s0_ref, y_ref, sf_ref, st_scr, *, n_chunks):
    d = pl.program_id(0)
    i = pl.program_id(2)
    c = SCAN_CHUNK
    hd = SSD_HEAD_DIM

    @pl.when(i == 0)
    def _():
        st_scr[...] = s0_ref[...]

    _, before_eq = _order_masks(d, c)
    incl01 = before_eq.astype(BF16)
    last_row = jnp.where(d == 0, c - 1, 0)
    rsel = lax.broadcasted_iota(jnp.int32, (c, 1), 0) == last_row
    lane = lax.broadcasted_iota(jnp.int32, (c, PAIR), 1)
    m0 = lane < hd
    a_row = a_ref[...]

    def chunk_body(j, carry):
        cj = jnp.where(d == 0, j, n_chunks - 1 - j)
        rows = pl.ds(pl.multiple_of(cj * c, c), c)
        dt = dt_ref[rows, :]
        cs_all = _dot01_left(incl01, dt * a_row)
        for g in range(SSD_GROUPS):
            bm = xbc_ref[rows, SSD_WIDTH + g * SSD_STATE:SSD_WIDTH + (g + 1) * SSD_STATE]
            cm = xbc_ref[rows, SSD_WIDTH + (SSD_GROUPS + g) * SSD_STATE:
                         SSD_WIDTH + (SSD_GROUPS + g + 1) * SSD_STATE]
            bm_b = bm.astype(BF16)
            cm_b = cm.astype(BF16)
            cb = _dot_nt(cm_b, bm_b)
            for pp in range(N_PAIRS // SSD_GROUPS):
                p = g * (N_PAIRS // SSD_GROUPS) + pp
                lanes = slice(p * PAIR, (p + 1) * PAIR)
                cs = cs_all[:, lanes]
                cs_t = cs.T
                xdt = xbc_ref[rows, lanes] * dt[:, lanes]
                probs = []
                for hh in range(2):
                    col = cs[:, hh * hd:hh * hd + 1]
                    rowv = cs_t[hh * hd:hh * hd + 1, :]
                    dec = jnp.exp(jnp.where(before_eq, col - rowv, -jnp.inf))
                    probs.append((cb * dec).astype(BF16))
                xs2 = jnp.concatenate([jnp.where(m0, xdt, 0.0), jnp.where(m0, 0.0, xdt)],
                                      axis=0).astype(BF16)
                y = _dot(jnp.concatenate(probs, axis=1), xs2)
                st = st_scr[p]
                y = y + jnp.exp(cs) * _dot(cm_b, st.astype(BF16))
                y_ref[rows, lanes] = y
                cs_last = jnp.sum(jnp.where(rsel, cs, 0.0), axis=0, keepdims=True)
                xe = (xdt * jnp.exp(cs_last - cs)).astype(BF16)
                st_scr[p] = st * jnp.exp(cs_last) + _dot_tn(bm_b, xe)
        return carry

    lax.fori_loop(0, n_chunks, chunk_body, 0)

    @pl.when(i == pl.num_programs(2) - 1)
    def _():
        sf_ref[...] = st_scr[...]


def _ssd_scan_call(xbc, dtbc, a_rep, s0, tb):
    b, t, _ = xbc.shape
    tb = min(tb, t)
    nb = t // tb
    blk = lambda dd, i: i + dd * (nb - 1 - 2 * i)
    st = pl.BlockSpec((None, None, N_PAIRS, SSD_STATE, PAIR), lambda dd, bi, i: (dd, bi, 0, 0, 0))
    tok2 = pl.BlockSpec((None, None, tb, SSD_WIDTH), lambda dd, bi, i: (dd, bi, blk(dd, i), 0))
    return pl.pallas_call(
        functools.partial(_ssd_scan_kernel, n_chunks=tb // SCAN_CHUNK),
        name="ssd_scan",
        grid=(2, b, nb),
        in_specs=[pl.BlockSpec((None, tb, SSD_XBC), lambda dd, bi, i: (bi, blk(dd, i), 0)),
                  tok2,
                  pl.BlockSpec((None, 1, SSD_WIDTH), lambda dd, bi, i: (dd, 0, 0)),
                  st],
        out_specs=[tok2, st],
        out_shape=[jax.ShapeDtypeStruct((2, b, t, SSD_WIDTH), F32),
                   jax.ShapeDtypeStruct((2, b, N_PAIRS, SSD_STATE, PAIR), F32)],
        scratch_shapes=[pltpu.VMEM((N_PAIRS, SSD_STATE, PAIR), F32)],
        compiler_params=_cparams(("parallel", "parallel", "arbitrary")),
    )(xbc, dtbc, a_rep, s0)


def _even_finish_kernel(yrf_ref, yrb_ref, bonus_ref, gate_ref, ysd_ref, xs_ref, z_ref, x_ref, m_ref,
                        lnw_ref, lnb_ref, dsk_ref, nw_ref, j_ref, wo_ref, o_ref):
    jm = j_ref[...]
    y = yrf_ref[...] + yrb_ref[...]
    inv_n = 1.0 / RWKV_HEAD_DIM
    mean = _dot01(y, jm) * inv_n
    yc = y - mean
    var = _dot01(yc * yc, jm) * inv_n
    y = yc * lax.rsqrt(var + RWKV_GN_EPS) * lnw_ref[...] + lnb_ref[...]
    y_rk = (y + bonus_ref[...]) * gate_ref[...]
    s = ysd_ref[0] + ysd_ref[1] + dsk_ref[...] * xs_ref[...]
    s = s * _silu(z_ref[...])
    s = s * lax.rsqrt(jnp.mean(s * s, axis=-1, keepdims=True) + NORM_EPS) * nw_ref[...]
    out = _dot(y_rk.astype(BF16), wo_ref[0:RWKV_WIDTH, :]) + \
        _dot(s.astype(BF16), wo_ref[RWKV_WIDTH:RWKV_WIDTH + SSD_WIDTH, :])
    o_ref[...] = x_ref[...] + m_ref[2:3, :] * out


def _even_finish_call(yrk, bonus, gate, ysd, xbc, proj, x, mods, p, tm):
    b, t, d = x.shape
    tm = min(tm, t)
    w = RWKV_WIDTH
    per_batch = mods.shape[0] > 1
    full = lambda shape: pl.BlockSpec(shape, lambda bi, i: (0,) * len(shape))
    tok = pl.BlockSpec((None, tm, w), lambda bi, i: (bi, i, 0))
    tok2 = pl.BlockSpec((2, None, tm, w), lambda bi, i: (0, bi, i, 0))
    return pl.pallas_call(
        _even_finish_kernel,
        name="even_finish",
        grid=(b, t // tm),
        in_specs=[tok, tok, tok, tok, tok2,
                  pl.BlockSpec((None, tm, SSD_WIDTH), lambda bi, i: (bi, i, 0)),
                  pl.BlockSpec((None, tm, SSD_WIDTH), lambda bi, i: (bi, i, EV_Z_OFF // SSD_WIDTH)),
                  pl.BlockSpec((None, tm, d), lambda bi, i: (bi, i, 0)),
                  pl.BlockSpec((None, N_MOD, d), (lambda bi, i: (bi, 0, 0)) if per_batch
                               else (lambda bi, i: (0, 0, 0))),
                  full((1, w)), full((1, w)), full((1, w)), full((1, w)), full((w, w)),
                  full((2 * w, d))],
        out_specs=pl.BlockSpec((None, tm, d), lambda bi, i: (bi, i, 0)),
        out_shape=jax.ShapeDtypeStruct((b, t, d), F32),
        compiler_params=_cparams(("parallel", "parallel")),
    )(yrk[0], yrk[1], bonus, gate, ysd, xbc, proj, x, mods, p['ln_w'], p['ln_b'], p['d_skip'], p['norm_w'],
      p['head_sum'], p['w_out'])


def _ret_scan_kernel(*refs, n_chunks, on_grid):
    if on_grid:
        q_ref, k_ref, v_ref, cos_ref, sin_ref, lg_ref, s0_ref, y_ref, sf_ref, st_scr, dec_scr = refs
    else:
        q_ref, k_ref, v_ref, lg_ref, s0_ref, y_ref, sf_ref, st_scr, dec_scr = refs
    d = pl.program_id(0)
    i = pl.program_id(2)
    c = SCAN_CHUNK
    dk, dv = RET_QK_DIM, RET_V_DIM
    heads = range(RET_HEADS)
    lg_all = lg_ref[...]

    @pl.when(i == 0)
    def _():
        st_scr[...] = s0_ref[...]
        _, before_eq = _order_masks(d, c)
        row = lax.broadcasted_iota(jnp.int32, (c, c), 0)
        col = lax.broadcasted_iota(jnp.int32, (c, c), 1)
        rel = jnp.abs(row - col).astype(F32)
        pos = (row + d * (c - 1 - 2 * row)).astype(F32)
        for h in heads:
            lg = lg_all[:, h * dk:h * dk + 1]
            dec_scr[h, 0] = jnp.where(before_eq, jnp.exp(rel * lg), 0.0)
            dec_scr[h, 1] = jnp.exp((pos + 1.0) * lg)
            dec_scr[h, 2] = jnp.exp((c - 1.0 - pos) * lg) * (dk ** -0.5)

    def chunk_body(j, carry):
        cj = jnp.where(d == 0, j, n_chunks - 1 - j)
        rows = pl.ds(pl.multiple_of(cj * c, c), c)
        qs, ks, vs = [], [], []
        for h in heads:
            ql = slice(h * dk, (h + 1) * dk)
            q = q_ref[rows, ql].astype(F32)
            k = k_ref[rows, ql].astype(F32)
            if on_grid:
                cs_, sn_ = cos_ref[rows, :], sin_ref[rows, :]
                q = q * cs_ + pltpu.roll(q, dk // 2, axis=1) * sn_
                k = k * cs_ + pltpu.roll(k, dk // 2, axis=1) * sn_
            qs.append(q)
            ks.append(k)
            vs.append(v_ref[rows, h * dv:(h + 1) * dv])
        qk = [_dot_nt(qs[h].astype(BF16), (ks[h] * (dk ** -0.5)).astype(BF16)) for h in heads]
        scores = [(qk[h] * dec_scr[h, 0]).astype(BF16) for h in heads]
        st = [st_scr[h] for h in heads]
        y_st = [_dot((qs[h] * dec_scr[h, 1]).astype(BF16), st[h].astype(BF16)) for h in heads]
        for h in heads:
            y_ref[rows, h * dv:(h + 1) * dv] = _dot(scores[h], vs[h]) + y_st[h]
        for h in heads:
            lg = lg_all[:, h * dk:h * dk + 1]
            st_scr[h] = st[h] * jnp.exp(c * lg) + _dot_tn((ks[h] * dec_scr[h, 2]).astype(BF16), vs[h])
        return carry

    lax.fori_loop(0, n_chunks, chunk_body, 0)

    @pl.when(i == pl.num_programs(2) - 1)
    def _():
        sf_ref[...] = st_scr[...]


def _ret_scan_call(proj, lg_rep, s0, tb, rope):
    b, t, _ = proj.shape
    tb = min(tb, t)
    nb = t // tb
    on_grid = rope is not None
    blk = lambda dd, i: i + dd * (nb - 1 - 2 * i)
    st = pl.BlockSpec((None, None, RET_HEADS, RET_QK_DIM, RET_V_DIM), lambda dd, bi, i: (dd, bi, 0, 0, 0))
    in_specs = [pl.BlockSpec((None, tb, RET_QK), lambda dd, bi, i: (bi, blk(dd, i), 0)),
                pl.BlockSpec((None, tb, RET_QK), lambda dd, bi, i: (bi, blk(dd, i), 1)),
                pl.BlockSpec((None, tb, RET_V), lambda dd, bi, i: (bi, blk(dd, i), 2 * RET_QK // RET_V))]
    args = [proj, proj, proj]
    if on_grid:
        tab = pl.BlockSpec((tb, RET_QK_DIM), lambda dd, bi, i: (blk(dd, i), 0))
        in_specs += [tab, tab]
        args += list(rope)
    in_specs += [pl.BlockSpec((None, 1, RET_QK), lambda dd, bi, i: (dd, 0, 0)), st]
    args += [lg_rep, s0]
    return pl.pallas_call(
        functools.partial(_ret_scan_kernel, n_chunks=tb // SCAN_CHUNK, on_grid=on_grid),
        name="ret_scan",
        grid=(2, b, nb),
        in_specs=in_specs,
        out_specs=[pl.BlockSpec((None, None, tb, RET_V), lambda dd, bi, i: (dd, bi, blk(dd, i), 0)), st],
        out_shape=[jax.ShapeDtypeStruct((2, b, t, RET_V), F32),
                   jax.ShapeDtypeStruct((2, b, RET_HEADS, RET_QK_DIM, RET_V_DIM), F32)],
        scratch_shapes=[pltpu.VMEM((RET_HEADS, RET_QK_DIM, RET_V_DIM), F32),
                        pltpu.VMEM((RET_HEADS, 3, SCAN_CHUNK, RET_QK_DIM), F32)],
        compiler_params=_cparams(("parallel", "parallel", "arbitrary")),
    )(*args)


def _odd_finish_kernel(y_ref, g_ref, x_ref, m_ref, wo_ref, o_ref):
    y = y_ref[0] + y_ref[1]
    dv = RET_V_DIM
    parts = []
    for h in range(RET_HEADS):
        yh = y[:, h * dv:(h + 1) * dv]
        parts.append(yh * lax.rsqrt(jnp.mean(yh * yh, axis=-1, keepdims=True) + NORM_EPS))
    yn = jnp.concatenate(parts, axis=1)
    act = (_silu(g_ref[...].astype(F32)) * yn).astype(BF16)
    o_ref[...] = x_ref[...] + m_ref[2:3, :] * _dot(act, wo_ref[...])


def _odd_finish_call(y, proj, x, mods, w_out, tm):
    b, t, d = x.shape
    tm = min(tm, t)
    return pl.pallas_call(
        _odd_finish_kernel,
        name="odd_finish",
        grid=(b, t // tm),
        in_specs=[pl.BlockSpec((2, None, tm, RET_V), lambda bi, i: (0, bi, i, 0)),
                  pl.BlockSpec((None, tm, RET_V), lambda bi, i: (bi, i, (2 * RET_QK + RET_V) // RET_V)),
                  pl.BlockSpec((None, tm, d), lambda bi, i: (bi, i, 0)),
                  pl.BlockSpec((None, N_MOD, d), lambda bi, i: (bi, 0, 0)),
                  pl.BlockSpec((RET_V, d), lambda bi, i: (0, 0))],
        out_specs=pl.BlockSpec((None, tm, d), lambda bi, i: (bi, i, 0)),
        out_shape=jax.ShapeDtypeStruct((b, t, d), F32),
        compiler_params=_cparams(("parallel", "parallel")),
    )(y, proj, x, mods, w_out)


def _ffn_tail_kernel(*refs, on_grid, final_norm):
    if on_grid:
        g_ref, gp_ref, gn_ref, val_ref, x_ref, m_ref, cw_ref, cb_ref, wd_ref = refs[:9]
        rest = refs[9:]
    else:
        g_ref, val_ref, x_ref, m_ref, cw_ref, cb_ref, wd_ref = refs[:7]
        rest = refs[7:]
    if final_norm:
        fg_ref, o_ref = rest
    else:
        (o_ref,) = rest
    i = pl.program_id(1)
    n_tiles = pl.num_programs(1)
    g = g_ref[...].astype(F32)
    tm = g.shape[0]
    row = lax.broadcasted_iota(jnp.int32, (tm, 1), 0)
    if on_grid:
        col = row % GRID_W
        ok_left = col > 0
        ok_right = col < GRID_W - 1
        hb = gp_ref.shape[0]
        up = jnp.where(i > 0, gp_ref[hb - GRID_W:hb, :].astype(F32), 0.0)
        dn = jnp.where(i < n_tiles - 1, gn_ref[0:GRID_W, :].astype(F32), 0.0)
        slabs = (jnp.concatenate([up, g[0:tm - GRID_W]], axis=0), g,
                 jnp.concatenate([g[GRID_W:tm], dn], axis=0))
        taps = [sum(slabs[dr] * cw_ref[3 * dr + dc:3 * dr + dc + 1, :] for dr in range(3))
                for dc in range(3)]
    else:
        ok_left = row > 0
        ok_right = row < tm - 1
        taps = [g * cw_ref[3 + dc:4 + dc, :] for dc in range(3)]
    acc = cb_ref[...] + taps[1] + jnp.where(ok_left, pltpu.roll(taps[0], 1, axis=0), 0.0) \
        + jnp.where(ok_right, pltpu.roll(taps[2], tm - 1, axis=0), 0.0)
    act = (_gelu_tanh(acc) * val_ref[...].astype(F32)).astype(BF16)
    out = x_ref[...] + m_ref[5:6, :] * _dot(act, wd_ref[...])
    if final_norm:
        out = out * lax.rsqrt(jnp.mean(out * out, axis=-1, keepdims=True) + NORM_EPS) * fg_ref[...]
    o_ref[...] = out


def _ffn_tail_call(up, x, mods, conv_w9, conv_b, w_down, *, tm, on_grid, final_g=None):
    b, t, d = x.shape
    tm = min(tm, t)
    n_tiles = t // tm
    hb = 2 * GRID_W
    per_batch = mods.shape[0] > 1
    full = lambda shape: pl.BlockSpec(shape, lambda bi, i: (0,) * len(shape))
    main = pl.BlockSpec((None, tm, D_FF), lambda bi, i: (bi, i, 0))
    in_specs = [main]
    args = [up]
    if on_grid:
        r = tm // hb
        last = t // hb - 1
        in_specs += [pl.BlockSpec((None, hb, D_FF), lambda bi, i: (bi, jnp.maximum(i * r - 1, 0), 0)),
                     pl.BlockSpec((None, hb, D_FF), lambda bi, i: (bi, jnp.minimum((i + 1) * r, last), 0))]
        args += [up, up]
    else:
        assert n_tiles == 1
    in_specs += [pl.BlockSpec((None, tm, D_FF), lambda bi, i: (bi, i, 1)),
                 pl.BlockSpec((None, tm, d), lambda bi, i: (bi, i, 0)),
                 pl.BlockSpec((None, N_MOD, d), (lambda bi, i: (bi, 0, 0)) if per_batch
                              else (lambda bi, i: (0, 0, 0))),
                 full((9, D_FF)), full((1, D_FF)), full((D_FF, d))]
    args += [up, x, mods, conv_w9, conv_b, w_down]
    if final_g is not None:
        in_specs.append(full((1, d)))
        args.append(final_g)
    return pl.pallas_call(
        functools.partial(_ffn_tail_kernel, on_grid=on_grid, final_norm=final_g is not None),
        name="ffn_tail",
        grid=(b, n_tiles),
        in_specs=in_specs,
        out_specs=pl.BlockSpec((None, tm, d), lambda bi, i: (bi, i, 0)),
        out_shape=jax.ShapeDtypeStruct((b, t, d), F32),
        compiler_params=_cparams(("parallel", "parallel")),
    )(*args)


def _block_diag2(a, b):
    za = jnp.zeros((a.shape[0], b.shape[1]), a.dtype)
    zb = jnp.zeros((b.shape[0], a.shape[1]), a.dtype)
    return jnp.concatenate([jnp.concatenate([a, za], axis=1), jnp.concatenate([zb, b], axis=1)], axis=0)


def _pad_cols(a, n):
    return jnp.pad(a, ((0, 0), (0, n - a.shape[1])))


def _even_params(j, ev_w_in, ev_mu_prev, ev_mu_next, rk_w0_f, rk_w0_b, rk_w2_f, rk_w2_b, rk_a0_f,
                 rk_a0_b, rk_a2_f, rk_a2_b, rk_g2, rk_k_k, rk_k_a, rk_r_k, rk_ln_w, rk_ln_b,
                 ssd_conv_w, ssd_conv_b, ssd_dt_bias_f, ssd_dt_bias_b, ssd_a_log_f, ssd_a_log_b,
                 ssd_d, ssd_norm_w, ev_w_out):
    w_in = ev_w_in[j]
    rw = w_in[:, :RWKV_COLS]
    z = w_in[:, RWKV_COLS:RWKV_COLS + SSD_WIDTH]
    xbc = w_in[:, RWKV_COLS + SSD_WIDTH:RWKV_COLS + SSD_WIDTH + SSD_XBC]
    dts = w_in[:, RWKV_COLS + SSD_WIDTH + SSD_XBC:]
    w_packed = jnp.concatenate([_pad_cols(jnp.concatenate([rw, dts], axis=1), EV_RW_BLOCK), xbc, z], axis=1)
    head = jnp.arange(RWKV_WIDTH) // RWKV_HEAD_DIM
    head_sum = (head[:, None] == head[None, :]).astype(BF16)
    lane = jnp.arange(V7X_LANES)[:, None]
    tgt = jnp.arange(2 * SSD_WIDTH)[None, :]
    dt_expand = (lane == (tgt // SSD_WIDTH) * SSD_HEADS + (tgt % SSD_WIDTH) // SSD_HEAD_DIM).astype(BF16)
    rep = lambda a: jnp.repeat(a, SSD_HEAD_DIM)[None, :]
    row = lambda a: a[None, :]
    return {
        'w_in': w_packed.astype(BF16),
        'mu_prev': _pad_cols(row(ev_mu_prev[j]), EV_RW_BLOCK),
        'mu_next': _pad_cols(row(ev_mu_next[j]), EV_RW_BLOCK),
        'w0': row(jnp.concatenate([rk_w0_f[j], rk_w0_b[j]])),
        'w2': _block_diag2(rk_w2_f[j], rk_w2_b[j]).astype(BF16),
        'a0': row(jnp.concatenate([rk_a0_f[j], rk_a0_b[j]])),
        'a2': _block_diag2(rk_a2_f[j], rk_a2_b[j]).astype(BF16),
        'g2': rk_g2[j].astype(BF16),
        'k_k': row(rk_k_k[j]), 'k_a': row(rk_k_a[j]), 'r_k': row(rk_r_k[j].reshape(-1)),
        'ln_w': row(rk_ln_w[j]), 'ln_b': row(rk_ln_b[j]),
        'head_sum': head_sum,
        'conv_w': ssd_conv_w[j], 'conv_b': row(ssd_conv_b[j]),
        'dt_bias': jnp.concatenate([rep(ssd_dt_bias_f[j]), rep(ssd_dt_bias_b[j])], axis=1),
        'dt_expand': dt_expand,
        'a_rep': jnp.stack([rep(-jnp.exp(ssd_a_log_f[j])), rep(-jnp.exp(ssd_a_log_b[j]))]),
        'd_skip': rep(ssd_d[j]),
        'norm_w': row(ssd_norm_w[j]),
        'w_out': ev_w_out[j].astype(BF16),
    }


def _rope_tables(t):
    n = RET_QK_DIM // 4
    pos = jnp.arange(t)
    row = (pos // GRID_W).astype(F32)
    col = (pos % GRID_W).astype(F32)
    inv = ROPE_BASE ** (-jnp.arange(n, dtype=F32) / n)
    ang = jnp.concatenate([row[:, None] * inv, col[:, None] * inv], axis=-1)
    cos, sin = jnp.cos(ang), jnp.sin(ang)
    return jnp.concatenate([cos, cos], axis=-1), jnp.concatenate([-sin, sin], axis=-1)


def _conv_ffn(x, mods, norm_g, w_up, conv_w9, conv_b, w_down, *, on_grid, final_g=None):
    up = _nm_call(x, norm_g, mods, w_up, shift_row=3, tm=512, tn=512, out_dtype=BF16)
    return _ffn_tail_call(up, x, mods, conv_w9, conv_b, w_down, tm=512, on_grid=on_grid, final_g=final_g)


def _even_layer(x, ctx, mods_x, mods_c, norm_g, p):
    b = x.shape[0]

    def features(h, mods):
        proj = _nm_call(h, norm_g, mods, p['w_in'], shift_row=0, tm=512, tn=512)
        feat = _rwkv_feat_call(proj, p, 256)
        xbc, dtbc = _ssd_feat_call(proj, p, 256)
        return proj, feat, xbc, dtbc

    proj_c, feat_c, xbc_c, dt_c = features(ctx, mods_c)
    proj_x, feat_x, xbc_x, dt_x = features(x, mods_x)
    s0 = jnp.zeros((2, b, N_PAIRS, PAIR, PAIR), F32)
    *yrk_c, s_ctx = _rwkv_scan_call(feat_c, s0, 256)
    *yrk_x, _ = _rwkv_scan_call(feat_x, s_ctx, 256)
    h0 = jnp.zeros((2, b, N_PAIRS, SSD_STATE, PAIR), F32)
    ysd_c, h_ctx = _ssd_scan_call(xbc_c, dt_c, p['a_rep'], h0, 256)
    ysd_x, _ = _ssd_scan_call(xbc_x, dt_x, p['a_rep'], h_ctx, 256)
    x = _even_finish_call(yrk_x, feat_x[3], feat_x[4], ysd_x, xbc_x, proj_x, x, mods_x, p, 256)
    ctx = _even_finish_call(yrk_c, feat_c[3], feat_c[4], ysd_c, xbc_c, proj_c, ctx, mods_c, p, 256)
    return x, ctx


def _odd_layer(x, ctx, mods_x, mods_c, norm_g, w_in, lg_rep, w_out):
    b, t, _ = x.shape
    proj_c = _nm_call(ctx, norm_g, mods_c, w_in, shift_row=0, tm=256, tn=512, out_dtype=BF16)
    proj_x = _nm_call(x, norm_g, mods_x, w_in, shift_row=0, tm=512, tn=512, out_dtype=BF16)
    s0 = jnp.zeros((2, b, RET_HEADS, RET_QK_DIM, RET_V_DIM), F32)
    _, s_ctx = _ret_scan_call(proj_c, lg_rep, s0, 256, None)
    y, _ = _ret_scan_call(proj_x, lg_rep, s_ctx, 256, _rope_tables(t))
    return _odd_finish_call(y, proj_x, x, mods_x, w_out, 256)


def kernel(x, c, ctx, c_ctx, mod_w, mod_b, norm1_g, norm2_g, ffn_w_up, ffn_conv_w, ffn_conv_b, ffn_w_down, ev_w_in, ev_mu_prev, ev_mu_next, rk_w0_f, rk_w0_b, rk_w2_f, rk_w2_b, rk_a0_f, rk_a0_b, rk_a2_f, rk_a2_b, rk_g2, rk_k_k, rk_k_a, rk_r_k, rk_ln_w, rk_ln_b, ssd_conv_w, ssd_conv_b, ssd_dt_bias_f, ssd_dt_bias_b, ssd_a_log_f, ssd_a_log_b, ssd_d, ssd_norm_w, ev_w_out, ret_w_in, ret_log2_f, ret_log2_b, ret_w_out, final_norm_g):
    b, t, d = x.shape
    depth = mod_w.shape[0]
    rows = -(-(b + 1) // V7X_SUBLANES) * V7X_SUBLANES
    cond = jnp.concatenate([c, c_ctx[None, :], jnp.zeros((rows - b - 1, d), F32)], axis=0)
    mods = _mod_call(cond, mod_w, mod_b).reshape(depth, rows, N_MOD, d)
    for i in range(depth):
        need_ctx = i < depth - 1
        mods_x = mods[i, :b]
        mods_c = mods[i, b:b + 1]
        j = i // 2
        g1 = norm1_g[i][None, :]
        if i % 2 == 0:
            p = _even_params(j, ev_w_in, ev_mu_prev, ev_mu_next, rk_w0_f, rk_w0_b, rk_w2_f, rk_w2_b,
                             rk_a0_f, rk_a0_b, rk_a2_f, rk_a2_b, rk_g2, rk_k_k, rk_k_a, rk_r_k,
                             rk_ln_w, rk_ln_b, ssd_conv_w, ssd_conv_b, ssd_dt_bias_f, ssd_dt_bias_b,
                             ssd_a_log_f, ssd_a_log_b, ssd_d, ssd_norm_w, ev_w_out)
            x, ctx_mixed = _even_layer(x, ctx, mods_x, mods_c, g1, p)
        else:
            lg = jnp.stack([jnp.log1p(-jnp.exp2(-ret_log2_f[j])), jnp.log1p(-jnp.exp2(-ret_log2_b[j]))])
            lg_rep = jnp.repeat(lg, RET_QK_DIM, axis=-1)[:, None, :]
            x = _odd_layer(x, ctx, mods_x, mods_c, g1, ret_w_in[j].astype(BF16), lg_rep,
                           ret_w_out[j].astype(BF16))
            ctx_mixed = None
        g2 = norm2_g[i][None, :]
        w_up = ffn_w_up[i].astype(BF16)
        w_down = ffn_w_down[i].astype(BF16)
        conv_w9 = ffn_conv_w[i].reshape(9, D_FF)
        conv_b = ffn_conv_b[i][None, :]
        last = i == depth - 1
        x = _conv_ffn(x, mods_x, g2, w_up, conv_w9, conv_b, w_down, on_grid=True,
                      final_g=final_norm_g[None, :] if last else None)
        if need_ctx:
            ctx = _conv_ffn(ctx_mixed, mods_c, g2, w_up, conv_w9, conv_b, w_down, on_grid=False)
    return x
```

```python
import functools
import math

import jax
import jax.numpy as jnp
from jax import lax
from jax.experimental import pallas as pl
from jax.experimental.pallas import tpu as pltpu

F32 = jnp.float32
BF16 = jnp.bfloat16

D_MODEL = 1024
GRID_W = 64
N_MOD = 6
NORM_EPS = 1e-6
RWKV_HEADS = 8
RWKV_HEAD_DIM = 64
RWKV_WIDTH = RWKV_HEADS * RWKV_HEAD_DIM
DECAY_LORA = 64
ICLR_LORA = 64
GATE_LORA = 128
RWKV_GN_EPS = 64e-5
RWKV_COLS = 3 * RWKV_WIDTH + 2 * DECAY_LORA + 2 * ICLR_LORA + GATE_LORA
SSD_HEADS = 8
SSD_HEAD_DIM = 64
SSD_WIDTH = SSD_HEADS * SSD_HEAD_DIM
SSD_GROUPS = 2
SSD_STATE = 128
SSD_XBC = SSD_WIDTH + 2 * SSD_GROUPS * SSD_STATE
RET_HEADS = 8
RET_QK_DIM = 128
RET_V_DIM = 256
RET_QK = RET_HEADS * RET_QK_DIM
RET_V = RET_HEADS * RET_V_DIM
ROPE_BASE = 10000.0
D_FF = 2816

V7X_LANES = 128
V7X_SUBLANES = 8
V7X_VMEM_LIMIT_BYTES = 56 * 1024 * 1024

RWKV_CHUNK = 64
SCAN_CHUNK = 128
FFN_COL_CHUNK = 256
FFN_DOWN_GROUP = 4
PAIR = 2 * RWKV_HEAD_DIM
N_PAIRS = RWKV_HEADS // 2
EV_RW_BLOCK = 2048
EV_DT_OFF = RWKV_COLS
EV_XBC_OFF = EV_RW_BLOCK
EV_Z_OFF = EV_RW_BLOCK + SSD_XBC
EV_COLS = EV_Z_OFF + SSD_WIDTH


def _cparams(sem):
    return pltpu.CompilerParams(dimension_semantics=sem, vmem_limit_bytes=V7X_VMEM_LIMIT_BYTES)


def _split3(x):
    hi = x.astype(BF16)
    r1 = x - hi.astype(F32)
    mid = r1.astype(BF16)
    lo = (r1 - mid.astype(F32)).astype(BF16)
    return hi, mid, lo


def _dot(a, b):
    return jnp.dot(a, b, preferred_element_type=F32)


def _dot_nt(a, b):
    return lax.dot_general(a, b, (((1,), (1,)), ((), ())), preferred_element_type=F32)


def _dot_tn(a, b):
    return lax.dot_general(a, b, (((0,), (0,)), ((), ())), preferred_element_type=F32)


def _dot01(x, m01):
    hi, mid, lo = _split3(x)
    return _dot(hi, m01) + _dot(mid, m01) + _dot(lo, m01)


def _dot01_left(m01, x):
    hi, mid, lo = _split3(x)
    return _dot(m01, hi) + _dot(m01, mid) + _dot(m01, lo)


def _sigmoid(x):
    return 1.0 / (1.0 + jnp.exp(-x))


def _silu(x):
    return x * _sigmoid(x)


def _softplus(x):
    return jnp.maximum(x, 0.0) + jnp.log1p(jnp.exp(-jnp.abs(x)))


def _gelu_tanh(x):
    c = math.sqrt(2.0 / math.pi)
    half = 0.5 * x
    return half + half * jnp.tanh(x * (c + (0.044715 * c) * (x * x)))


def _order_masks(d, n):
    row = lax.broadcasted_iota(jnp.int32, (n, n), 0)
    col = lax.broadcasted_iota(jnp.int32, (n, n), 1)
    diff = (row - col) * (1 - 2 * d)
    return diff > 0, diff >= 0


def _mod_kernel(c_ref, w_ref, b_ref, o_ref):
    h = _silu(c_ref[...])
    hi, mid, lo = _split3(h)
    w = w_ref[...]
    wh = w.astype(BF16)
    wl = (w - wh.astype(F32)).astype(BF16)
    acc = _dot(hi, wh) + _dot(mid, wh) + _dot(hi, wl)
    o_ref[...] = acc + b_ref[...]


def _mod_call(cond, mod_w, mod_b):
    depth, d, n = mod_w.shape
    rows = cond.shape[0]
    tn = 1024
    return pl.pallas_call(
        _mod_kernel,
        name="adaln_mod",
        grid=(depth, n // tn),
        in_specs=[pl.BlockSpec((rows, d), lambda l, j: (0, 0)),
                  pl.BlockSpec((None, d, tn), lambda l, j: (l, 0, j)),
                  pl.BlockSpec((None, 1, tn), lambda l, j: (l, 0, j))],
        out_specs=pl.BlockSpec((None, rows, tn), lambda l, j: (l, 0, j)),
        out_shape=jax.ShapeDtypeStruct((depth, rows, n), F32),
        compiler_params=_cparams(("parallel", "parallel")),
    )(cond, mod_w, mod_b.reshape(depth, 1, n))


def _nm_kernel(x_ref, g_ref, m_ref, w_ref, o_ref, *, shift_row, tn):
    x = x_ref[...]
    h = x * lax.rsqrt(jnp.mean(x * x, axis=-1, keepdims=True) + NORM_EPS) * g_ref[...]
    h = h * (1.0 + m_ref[shift_row + 1:shift_row + 2, :]) + m_ref[shift_row:shift_row + 1, :]
    h = h.astype(BF16)
    for j in range(w_ref.shape[1] // tn):
        cols = slice(j * tn, (j + 1) * tn)
        o_ref[:, cols] = _dot(h, w_ref[:, cols]).astype(o_ref.dtype)


def _nm_call(x, g, mods, w, *, shift_row, tm, tn, out_dtype=F32):
    b, t, d = x.shape
    n = w.shape[1]
    tm = min(tm, t)
    per_batch = mods.shape[0] > 1
    return pl.pallas_call(
        functools.partial(_nm_kernel, shift_row=shift_row, tn=tn),
        name="norm_mod_matmul",
        grid=(b, t // tm),
        in_specs=[pl.BlockSpec((None, tm, d), lambda bi, i: (bi, i, 0)),
                  pl.BlockSpec((1, d), lambda bi, i: (0, 0)),
                  pl.BlockSpec((None, N_MOD, d), (lambda bi, i: (bi, 0, 0)) if per_batch
                               else (lambda bi, i: (0, 0, 0))),
                  pl.BlockSpec((d, n), lambda bi, i: (0, 0))],
        out_specs=pl.BlockSpec((None, tm, n), lambda bi, i: (bi, i, 0)),
        out_shape=jax.ShapeDtypeStruct((b, t, n), out_dtype),
        compiler_params=_cparams(("parallel", "parallel")),
    )(x, g, mods, w)


def _halo_specs(tm, width, col_block, n_row_tiles, t):
    r = tm // V7X_SUBLANES
    last = t // V7X_SUBLANES - 1
    return [pl.BlockSpec((None, tm, width), lambda bi, i: (bi, i, col_block)),
            pl.BlockSpec((None, V7X_SUBLANES, width),
                         lambda bi, i: (bi, jnp.maximum(i * r - 1, 0), col_block)),
            pl.BlockSpec((None, V7X_SUBLANES, width),
                         lambda bi, i: (bi, jnp.minimum((i + 1) * r, last), col_block))]


def _shifted(u, prev_blk, next_blk, i, n_tiles):
    tm = u.shape[0]
    row = lax.broadcasted_iota(jnp.int32, u.shape, 0)
    prev_row = jnp.where(i > 0, prev_blk[V7X_SUBLANES - 1:V7X_SUBLANES, :], 0.0)
    next_row = jnp.where(i < n_tiles - 1, next_blk[0:1, :], 0.0)
    prev = jnp.where(row == 0, prev_row, pltpu.roll(u, 1, axis=0))
    nxt = jnp.where(row == tm - 1, next_row, pltpu.roll(u, tm - 1, axis=0))
    return prev, nxt


def _rwkv_feat_kernel(u_ref, up_ref, un_ref, mup_ref, mun_ref, w0_ref, w2_ref, a0_ref, a2_ref,
                      g2_ref, kk_ref, ka_ref, rk_ref, j_ref,
                      r_ref, v_ref, kkn_ref, bonus_ref, gate_ref, ld_ref, kd_ref, bd_ref):
    i = pl.program_id(1)
    n_tiles = pl.num_programs(1)
    u = u_ref[...]
    prev, nxt = _shifted(u, up_ref[...], un_ref[...], i, n_tiles)
    rw = u + mup_ref[...] * (prev - u) + mun_ref[...] * (nxt - u)
    w = RWKV_WIDTH
    r = rw[:, 0:w]
    k = rw[:, w:2 * w]
    v = rw[:, 2 * w:3 * w]
    wd = rw[:, 3 * w:3 * w + 2 * DECAY_LORA]
    ad = rw[:, 3 * w + 2 * DECAY_LORA:3 * w + 2 * DECAY_LORA + 2 * ICLR_LORA]
    gd = rw[:, 3 * w + 2 * DECAY_LORA + 2 * ICLR_LORA:RWKV_COLS]
    jm = j_ref[...]

    kk = k * kk_ref[...]
    ss = _dot01(kk * kk, jm)
    kk = kk / jnp.maximum(jnp.sqrt(ss), 1e-12)
    r_ref[...] = r.astype(r_ref.dtype)
    v_ref[...] = v.astype(v_ref.dtype)
    kkn_ref[...] = kk.astype(kkn_ref.dtype)
    bonus_ref[...] = (_dot01(r * k * rk_ref[...], jm) * v).astype(bonus_ref.dtype)
    gate_ref[...] = _dot(_sigmoid(gd).astype(BF16), g2_ref[...]).astype(gate_ref.dtype)

    zw = _dot(jnp.tanh(wd).astype(BF16), w2_ref[...]) + w0_ref[...]
    za = _dot(ad.astype(BF16), a2_ref[...]) + a0_ref[...]
    for di in range(2):
        ld_ref[di] = -math.exp(-0.5) * _sigmoid(zw[:, di * w:(di + 1) * w])
        iclr = _sigmoid(za[:, di * w:(di + 1) * w])
        kd_ref[di] = (k * (1.0 + (iclr - 1.0) * ka_ref[...])).astype(kd_ref.dtype)
        bd_ref[di] = (kk * iclr).astype(bd_ref.dtype)


def _rwkv_feat_call(proj, p, tm):
    b, t, _ = proj.shape
    tm = min(tm, t)
    n_tiles = t // tm
    w = RWKV_WIDTH
    full = lambda shape: pl.BlockSpec(shape, lambda bi, i: (0,) * len(shape))
    tok = pl.BlockSpec((None, tm, w), lambda bi, i: (bi, i, 0))
    tok2 = pl.BlockSpec((2, None, tm, w), lambda bi, i: (0, bi, i, 0))
    sd = jax.ShapeDtypeStruct((b, t, w), BF16)
    sd2 = jax.ShapeDtypeStruct((2, b, t, w), BF16)
    ld2 = jax.ShapeDtypeStruct((2, b, t, w), F32)
    return pl.pallas_call(
        _rwkv_feat_kernel,
        name="rwkv_feat",
        grid=(b, n_tiles),
        in_specs=_halo_specs(tm, EV_RW_BLOCK, 0, n_tiles, t) + [
            full((1, EV_RW_BLOCK)), full((1, EV_RW_BLOCK)),
            full((1, 2 * w)), full((2 * DECAY_LORA, 2 * w)),
            full((1, 2 * w)), full((2 * ICLR_LORA, 2 * w)),
            full((GATE_LORA, w)), full((1, w)), full((1, w)), full((1, w)), full((w, w))],
        out_specs=[tok, tok, tok, tok, tok, tok2, tok2, tok2],
        out_shape=[sd, sd, sd, sd, sd, ld2, sd2, sd2],
        compiler_params=_cparams(("parallel", "parallel")),
    )(proj, proj, proj, p['mu_prev'], p['mu_next'], p['w0'], p['w2'], p['a0'], p['a2'],
      p['g2'], p['k_k'], p['k_a'], p['r_k'], p['head_sum'])


def _rwkv_scan_kernel(rf_ref, vf_ref, kkf_ref, ldf_ref, kdf_ref, bdf_ref,
                      rb_ref, vb_ref, kkb_ref, ldb_ref, kdb_ref, bdb_ref, s0_ref,
                      yf_ref, yb_ref, sf_ref, st_scr, *, n_chunks):
    i = pl.program_id(1)
    c = RWKV_CHUNK

    @pl.when(i == 0)
    def _():
        st_scr[...] = s0_ref[...]

    row2 = lax.broadcasted_iota(jnp.int32, (2 * c, 2 * c), 0)
    col2 = lax.broadcasted_iota(jnp.int32, (2 * c, 2 * c), 1)
    same_head = (row2 // c) == (col2 // c)
    eye_bd = (row2 == col2).astype(F32)
    rowc = lax.broadcasted_iota(jnp.int32, (c, 2 * c), 0)
    colc = lax.broadcasted_iota(jnp.int32, (c, 2 * c), 1) % c
    m0 = lax.broadcasted_iota(jnp.int32, (c, PAIR), 1) < RWKV_HEAD_DIM
    row1 = lax.broadcasted_iota(jnp.int32, (c, 1), 0)
    incl01, strict_bd, incl_wide, rsel = [], [], [], []
    for d in range(2):
        sgn = 1 - 2 * d
        incl01.append(_order_masks(d, c)[1].astype(BF16))
        strict_bd.append(jnp.logical_and(same_head, (row2 % c - col2 % c) * sgn > 0))
        incl_wide.append((rowc - colc) * sgn >= 0)
        rsel.append(row1 == (c - 1 if d == 0 else 0))
    refs = ((rf_ref, vf_ref, kkf_ref, ldf_ref, kdf_ref, bdf_ref, yf_ref),
            (rb_ref, vb_ref, kkb_ref, ldb_ref, kdb_ref, bdb_ref, yb_ref))
    chains = [(d, p) for d in range(2) for p in range(N_PAIRS)]
    lanes = [slice(p * PAIR, (p + 1) * PAIR) for p in range(N_PAIRS)]

    def stack(x):
        return jnp.concatenate([jnp.where(m0, x, 0.0), jnp.where(m0, 0.0, x)], axis=0)

    def chunk_body(j, carry):
        rows = (pl.ds(pl.multiple_of(j * c, c), c),
                pl.ds(pl.multiple_of((n_chunks - 1 - j) * c, c), c))
        cs_all = [_dot01_left(incl01[d], refs[d][3][rows[d], :]) for d in range(2)]
        a_s, r_t, b_s, k_s, v_s, b_end, k_end, g_tot = [], [], [], [], [], [], [], []
        for d, p in chains:
            r_ref, v_ref, kk_ref, ld_ref, kd_ref, bd_ref, _ = refs[d]
            rw, ln = rows[d], lanes[p]
            cs = cs_all[d][:, ln]
            cs_last = jnp.sum(jnp.where(rsel[d], cs, 0.0), axis=0, keepdims=True)
            g_neg = jnp.exp(-cs)
            g_end = jnp.exp(cs_last - cs)
            g_tot.append(jnp.exp(cs_last))
            kdv = kd_ref[rw, ln].astype(F32)
            bdv = bd_ref[rw, ln].astype(F32)
            a_s.append(stack(-kk_ref[rw, ln].astype(F32) * jnp.exp(cs - ld_ref[rw, ln])).astype(BF16))
            r_t.append((r_ref[rw, ln].astype(F32) * jnp.exp(cs)).astype(BF16))
            b_s.append(stack(bdv * g_neg).astype(BF16))
            k_s.append(stack(kdv * g_neg).astype(BF16))
            v_s.append(stack(v_ref[rw, ln].astype(F32)).astype(BF16))
            b_end.append(stack(bdv * g_end).astype(BF16))
            k_end.append(stack(kdv * g_end).astype(BF16))
        n = range(len(chains))
        dirs = [d for d, _ in chains]
        gram = [_dot_nt(jnp.concatenate([a_s[q], r_t[q]], axis=0),
                        jnp.concatenate([b_s[q], k_s[q]], axis=0)) for q in n]
        a_ab = [jnp.where(strict_bd[dirs[q]], gram[q][0:2 * c, 0:2 * c], 0.0) for q in n]
        a_ak = [jnp.where(strict_bd[dirs[q]], gram[q][0:2 * c, 2 * c:4 * c], 0.0).astype(BF16) for q in n]
        p_rb = [jnp.where(incl_wide[dirs[q]], gram[q][2 * c:3 * c, 0:2 * c], 0.0).astype(BF16) for q in n]
        p_rk = [jnp.where(incl_wide[dirs[q]], gram[q][2 * c:3 * c, 2 * c:4 * c], 0.0).astype(BF16) for q in n]
        minv = [eye_bd + a_ab[q] for q in n]
        pw = a_ab
        for _ in range(int(math.log2(c)) - 1):
            pwb = [pw[q].astype(BF16) for q in n]
            pw = [_dot(pwb[q], pwb[q]) for q in n]
            minv = [minv[q] + _dot(minv[q].astype(BF16), pw[q].astype(BF16)) for q in n]
        akv = [_dot(a_ak[q], v_s[q]).astype(BF16) for q in n]
        eff = [_dot(minv[q].astype(BF16), jnp.concatenate([a_s[q], akv[q]], axis=1)) for q in n]
        prkv = [_dot(p_rk[q], v_s[q]) for q in n]
        zst = [_dot_tn(v_s[q], k_end[q]) for q in n]
        st = [st_scr[d, p] for d, p in chains]
        fs = [_dot_nt(jnp.concatenate([eff[q][:, 0:PAIR].astype(BF16), r_t[q]], axis=0),
                      st[q].astype(BF16)) for q in n]
        u_s = [(fs[q][0:2 * c] + eff[q][:, PAIR:2 * PAIR]).astype(BF16) for q in n]
        for q, (d, p) in enumerate(chains):
            st_scr[d, p] = st[q] * g_tot[q] + _dot_tn(u_s[q], b_end[q]) + zst[q]
        for q, (d, p) in enumerate(chains):
            refs[d][6][rows[d], lanes[p]] = fs[q][2 * c:3 * c] + _dot(p_rb[q], u_s[q]) + prkv[q]
        return carry

    lax.fori_loop(0, n_chunks, chunk_body, 0)

    @pl.when(i == pl.num_programs(1) - 1)
    def _():
        sf_ref[...] = st_scr[...]


def _rwkv_scan_call(feat, s0, tb):
    r, v, kk, _, _, ld, kd, bd = feat
    b, t, w = r.shape
    tb = min(tb, t)
    nb = t // tb
    tok_f = pl.BlockSpec((None, tb, w), lambda bi, i: (bi, i, 0))
    tok_b = pl.BlockSpec((None, tb, w), lambda bi, i: (bi, nb - 1 - i, 0))
    dir_f = pl.BlockSpec((None, None, tb, w), lambda bi, i: (0, bi, i, 0))
    dir_b = pl.BlockSpec((None, None, tb, w), lambda bi, i: (1, bi, nb - 1 - i, 0))
    st = pl.BlockSpec((2, None, N_PAIRS, PAIR, PAIR), lambda bi, i: (0, bi, 0, 0, 0))
    y_sd = jax.ShapeDtypeStruct((b, t, w), F32)
    return pl.pallas_call(
        functools.partial(_rwkv_scan_kernel, n_chunks=tb // RWKV_CHUNK),
        name="rwkv_scan",
        grid=(b, nb),
        in_specs=[tok_f, tok_f, tok_f, dir_f, dir_f, dir_f, tok_b, tok_b, tok_b, dir_b, dir_b, dir_b, st],
        out_specs=[tok_f, tok_b, st],
        out_shape=[y_sd, y_sd, jax.ShapeDtypeStruct((2, b, N_PAIRS, PAIR, PAIR), F32)],
        scratch_shapes=[pltpu.VMEM((2, N_PAIRS, PAIR, PAIR), F32)],
        compiler_params=_cparams(("parallel", "arbitrary")),
    )(r, v, kk, ld, kd, bd, r, v, kk, ld, kd, bd, s0)


def _ssd_feat_kernel(x_ref, xp_ref, xn_ref, dt_ref, cw_ref, cb_ref, dtb_ref, e_ref,
                     xbc_ref, dtbc_ref):
    i = pl.program_id(1)
    n_tiles = pl.num_programs(1)
    x = x_ref[...]
    prev, nxt = _shifted(x, xp_ref[...], xn_ref[...], i, n_tiles)
    y = prev * cw_ref[0:1, :] + x * cw_ref[1:2, :] + nxt * cw_ref[2:3, :] + cb_ref[...]
    xbc_ref[...] = _silu(y).astype(xbc_ref.dtype)
    dt_rep = _dot01(dt_ref[...], e_ref[...])
    dt = _softplus(dt_rep + dtb_ref[...])
    dtbc_ref[0] = dt[:, 0:SSD_WIDTH]
    dtbc_ref[1] = dt[:, SSD_WIDTH:2 * SSD_WIDTH]


def _ssd_feat_call(proj, p, tm):
    b, t, _ = proj.shape
    tm = min(tm, t)
    n_tiles = t // tm
    full = lambda shape: pl.BlockSpec(shape, lambda bi, i: (0,) * len(shape))
    return pl.pallas_call(
        _ssd_feat_kernel,
        name="ssd_feat",
        grid=(b, n_tiles),
        in_specs=_halo_specs(tm, SSD_XBC, EV_XBC_OFF // SSD_XBC, n_tiles, t) + [
            pl.BlockSpec((None, tm, V7X_LANES), lambda bi, i: (bi, i, EV_DT_OFF // V7X_LANES)),
            full((3, SSD_XBC)), full((1, SSD_XBC)), full((1, 2 * SSD_WIDTH)),
            full((V7X_LANES, 2 * SSD_WIDTH))],
        out_specs=[pl.BlockSpec((None, tm, SSD_XBC), lambda bi, i: (bi, i, 0)),
                   pl.BlockSpec((2, None, tm, SSD_WIDTH), lambda bi, i: (0, bi, i, 0))],
        out_shape=[jax.ShapeDtypeStruct((b, t, SSD_XBC), BF16),
                   jax.ShapeDtypeStruct((2, b, t, SSD_WIDTH), F32)],
        compiler_params=_cparams(("parallel", "parallel")),
    )(proj, proj, proj, proj, p['conv_w'], p['conv_b'], p['dt_bias'], p['dt_expand'])


def _ssd_scan_kernel(xbc_ref, dt_ref, a_ref, s0_ref, y_ref, sf_ref, st_scr, *, n_chunks):
    d = pl.program_id(0)
    i = pl.program_id(2)
    c = SCAN_CHUNK
    hd = SSD_HEAD_DIM

    @pl.when(i == 0)
    def _():
        st_scr[...] = s0_ref[...]

    _, before_eq = _order_masks(d, c)
    incl01 = before_eq.astype(BF16)
    last_row = jnp.where(d == 0, c - 1, 0)
    rsel = lax.broadcasted_iota(jnp.int32, (c, 1), 0) == last_row
    lane = lax.broadcasted_iota(jnp.int32, (c, PAIR), 1)
    m0 = lane < hd
    a_row = a_ref[...]

    def chunk_body(j, carry):
        cj = jnp.where(d == 0, j, n_chunks - 1 - j)
        rows = pl.ds(pl.multiple_of(cj * c, c), c)
        dt = dt_ref[rows, :]
        cs_all = _dot01_left(incl01, dt * a_row)
        pairs = range(N_PAIRS)
        group = [p // (N_PAIRS // SSD_GROUPS) for p in pairs]
        lanes = [slice(p * PAIR, (p + 1) * PAIR) for p in pairs]
        bm = [xbc_ref[rows, SSD_WIDTH + g * SSD_STATE:SSD_WIDTH + (g + 1) * SSD_STATE]
              for g in range(SSD_GROUPS)]
        cm = [xbc_ref[rows, SSD_WIDTH + (SSD_GROUPS + g) * SSD_STATE:
                      SSD_WIDTH + (SSD_GROUPS + g + 1) * SSD_STATE] for g in range(SSD_GROUPS)]
        cb = [_dot_nt(cm[g], bm[g]) for g in range(SSD_GROUPS)]
        st = [st_scr[p] for p in pairs]
        y_st = [_dot(cm[group[p]], st[p].astype(BF16)) for p in pairs]
        cs = [cs_all[:, lanes[p]] for p in pairs]
        xdt = [xbc_ref[rows, lanes[p]].astype(F32) * dt[:, lanes[p]] for p in pairs]
        probs = []
        for p in pairs:
            cs_t = cs[p].T
            both = []
            for hh in range(2):
                col = cs[p][:, hh * hd:hh * hd + 1]
                rowv = cs_t[hh * hd:hh * hd + 1, :]
                dec = jnp.exp(jnp.where(before_eq, col - rowv, -jnp.inf))
                both.append((cb[group[p]] * dec).astype(BF16))
            probs.append(jnp.concatenate(both, axis=1))
        xs2 = [jnp.concatenate([jnp.where(m0, xdt[p], 0.0), jnp.where(m0, 0.0, xdt[p])],
                               axis=0).astype(BF16) for p in pairs]
        y_in = [_dot(probs[p], xs2[p]) for p in pairs]
        for p in pairs:
            y_ref[rows, lanes[p]] = y_in[p] + jnp.exp(cs[p]) * y_st[p]
        for p in pairs:
            cs_last = jnp.sum(jnp.where(rsel, cs[p], 0.0), axis=0, keepdims=True)
            xe = (xdt[p] * jnp.exp(cs_last - cs[p])).astype(BF16)
            st_scr[p] = st[p] * jnp.exp(cs_last) + _dot_tn(bm[group[p]], xe)
        return carry

    lax.fori_loop(0, n_chunks, chunk_body, 0)

    @pl.when(i == pl.num_programs(2) - 1)
    def _():
        sf_ref[...] = st_scr[...]


def _ssd_scan_call(xbc, dtbc, a_rep, s0, tb):
    b, t, _ = xbc.shape
    tb = min(tb, t)
    nb = t // tb
    blk = lambda dd, i: i + dd * (nb - 1 - 2 * i)
    st = pl.BlockSpec((None, None, N_PAIRS, SSD_STATE, PAIR), lambda dd, bi, i: (dd, bi, 0, 0, 0))
    tok2 = pl.BlockSpec((None, None, tb, SSD_WIDTH), lambda dd, bi, i: (dd, bi, blk(dd, i), 0))
    return pl.pallas_call(
        functools.partial(_ssd_scan_kernel, n_chunks=tb // SCAN_CHUNK),
        name="ssd_scan",
        grid=(2, b, nb),
        in_specs=[pl.BlockSpec((None, tb, SSD_XBC), lambda dd, bi, i: (bi, blk(dd, i), 0)),
                  tok2,
                  pl.BlockSpec((None, 1, SSD_WIDTH), lambda dd, bi, i: (dd, 0, 0)),
                  st],
        out_specs=[tok2, st],
        out_shape=[jax.ShapeDtypeStruct((2, b, t, SSD_WIDTH), F32),
                   jax.ShapeDtypeStruct((2, b, N_PAIRS, SSD_STATE, PAIR), F32)],
        scratch_shapes=[pltpu.VMEM((N_PAIRS, SSD_STATE, PAIR), F32)],
        compiler_params=_cparams(("parallel", "parallel", "arbitrary")),
    )(xbc, dtbc, a_rep, s0)


def _even_finish_kernel(yrf_ref, yrb_ref, bonus_ref, gate_ref, ysd_ref, xs_ref, z_ref, x_ref, m_ref,
                        lnw_ref, lnb_ref, dsk_ref, nw_ref, j_ref, wo_ref, o_ref):
    jm = j_ref[...]
    y = yrf_ref[...] + yrb_ref[...]
    inv_n = 1.0 / RWKV_HEAD_DIM
    mean = _dot01(y, jm) * inv_n
    yc = y - mean
    var = _dot01(yc * yc, jm) * inv_n
    y = yc * lax.rsqrt(var + RWKV_GN_EPS) * lnw_ref[...] + lnb_ref[...]
    y_rk = (y + bonus_ref[...].astype(F32)) * gate_ref[...].astype(F32)
    s = ysd_ref[0] + ysd_ref[1] + dsk_ref[...] * xs_ref[...].astype(F32)
    s = s * _silu(z_ref[...])
    s = s * lax.rsqrt(jnp.mean(s * s, axis=-1, keepdims=True) + NORM_EPS) * nw_ref[...]
    out = _dot(y_rk.astype(BF16), wo_ref[0:RWKV_WIDTH, :]) + \
        _dot(s.astype(BF16), wo_ref[RWKV_WIDTH:RWKV_WIDTH + SSD_WIDTH, :])
    o_ref[...] = x_ref[...] + m_ref[2:3, :] * out


def _even_finish_call(yrk, bonus, gate, ysd, xbc, proj, x, mods, p, tm):
    b, t, d = x.shape
    tm = min(tm, t)
    w = RWKV_WIDTH
    per_batch = mods.shape[0] > 1
    full = lambda shape: pl.BlockSpec(shape, lambda bi, i: (0,) * len(shape))
    tok = pl.BlockSpec((None, tm, w), lambda bi, i: (bi, i, 0))
    tok2 = pl.BlockSpec((2, None, tm, w), lambda bi, i: (0, bi, i, 0))
    return pl.pallas_call(
        _even_finish_kernel,
        name="even_finish",
        grid=(b, t // tm),
        in_specs=[tok, tok, tok, tok, tok2,
                  pl.BlockSpec((None, tm, SSD_WIDTH), lambda bi, i: (bi, i, 0)),
                  pl.BlockSpec((None, tm, SSD_WIDTH), lambda bi, i: (bi, i, EV_Z_OFF // SSD_WIDTH)),
                  pl.BlockSpec((None, tm, d), lambda bi, i: (bi, i, 0)),
                  pl.BlockSpec((None, N_MOD, d), (lambda bi, i: (bi, 0, 0)) if per_batch
                               else (lambda bi, i: (0, 0, 0))),
                  full((1, w)), full((1, w)), full((1, w)), full((1, w)), full((w, w)),
                  full((2 * w, d))],
        out_specs=pl.BlockSpec((None, tm, d), lambda bi, i: (bi, i, 0)),
        out_shape=jax.ShapeDtypeStruct((b, t, d), F32),
        compiler_params=_cparams(("parallel", "parallel")),
    )(yrk[0], yrk[1], bonus, gate, ysd, xbc, proj, x, mods, p['ln_w'], p['ln_b'], p['d_skip'], p['norm_w'],
      p['head_sum'], p['w_out'])


def _ret_scan_kernel(*refs, n_chunks, on_grid):
    if on_grid:
        q_ref, k_ref, v_ref, cos_ref, sin_ref, lg_ref, s0_ref, y_ref, sf_ref, st_scr, dec_scr = refs
    else:
        q_ref, k_ref, v_ref, lg_ref, s0_ref, y_ref, sf_ref, st_scr, dec_scr = refs
    d = pl.program_id(0)
    i = pl.program_id(2)
    c = SCAN_CHUNK
    dk, dv = RET_QK_DIM, RET_V_DIM
    heads = range(RET_HEADS)
    lg_all = lg_ref[...]

    @pl.when(i == 0)
    def _():
        st_scr[...] = s0_ref[...]
        _, before_eq = _order_masks(d, c)
        row = lax.broadcasted_iota(jnp.int32, (c, c), 0)
        col = lax.broadcasted_iota(jnp.int32, (c, c), 1)
        rel = jnp.abs(row - col).astype(F32)
        pos = (row + d * (c - 1 - 2 * row)).astype(F32)
        for h in heads:
            lg = lg_all[:, h * dk:h * dk + 1]
            dec_scr[h, 0] = jnp.where(before_eq, jnp.exp(rel * lg), 0.0)
            dec_scr[h, 1] = jnp.exp((pos + 1.0) * lg)
            dec_scr[h, 2] = jnp.exp((c - 1.0 - pos) * lg) * (dk ** -0.5)

    def chunk_body(j, carry):
        cj = jnp.where(d == 0, j, n_chunks - 1 - j)
        rows = pl.ds(pl.multiple_of(cj * c, c), c)
        qs, ks, vs = [], [], []
        for h in heads:
            ql = slice(h * dk, (h + 1) * dk)
            q = q_ref[rows, ql].astype(F32)
            k = k_ref[rows, ql].astype(F32)
            if on_grid:
                cs_, sn_ = cos_ref[rows, :], sin_ref[rows, :]
                q = q * cs_ + pltpu.roll(q, dk // 2, axis=1) * sn_
                k = k * cs_ + pltpu.roll(k, dk // 2, axis=1) * sn_
            qs.append(q)
            ks.append(k)
            vs.append(v_ref[rows, h * dv:(h + 1) * dv])
        qk = [_dot_nt(qs[h].astype(BF16), (ks[h] * (dk ** -0.5)).astype(BF16)) for h in heads]
        scores = [(qk[h] * dec_scr[h, 0]).astype(BF16) for h in heads]
        st = [st_scr[h] for h in heads]
        y_st = [_dot((qs[h] * dec_scr[h, 1]).astype(BF16), st[h].astype(BF16)) for h in heads]
        for h in heads:
            y_ref[rows, h * dv:(h + 1) * dv] = (_dot(scores[h], vs[h]) + y_st[h]).astype(y_ref.dtype)
        for h in heads:
            lg = lg_all[:, h * dk:h * dk + 1]
            st_scr[h] = st[h] * jnp.exp(c * lg) + _dot_tn((ks[h] * dec_scr[h, 2]).astype(BF16), vs[h])
        return carry

    lax.fori_loop(0, n_chunks, chunk_body, 0)

    @pl.when(i == pl.num_programs(2) - 1)
    def _():
        sf_ref[...] = st_scr[...]


def _ret_scan_call(proj, lg_rep, s0, tb, rope):
    b, t, _ = proj.shape
    tb = min(tb, t)
    nb = t // tb
    on_grid = rope is not None
    blk = lambda dd, i: i + dd * (nb - 1 - 2 * i)
    st = pl.BlockSpec((None, None, RET_HEADS, RET_QK_DIM, RET_V_DIM), lambda dd, bi, i: (dd, bi, 0, 0, 0))
    in_specs = [pl.BlockSpec((None, tb, RET_QK), lambda dd, bi, i: (bi, blk(dd, i), 0)),
                pl.BlockSpec((None, tb, RET_QK), lambda dd, bi, i: (bi, blk(dd, i), 1)),
                pl.BlockSpec((None, tb, RET_V), lambda dd, bi, i: (bi, blk(dd, i), 2 * RET_QK // RET_V))]
    args = [proj, proj, proj]
    if on_grid:
        tab = pl.BlockSpec((tb, RET_QK_DIM), lambda dd, bi, i: (blk(dd, i), 0))
        in_specs += [tab, tab]
        args += list(rope)
    in_specs += [pl.BlockSpec((None, 1, RET_QK), lambda dd, bi, i: (dd, 0, 0)), st]
    args += [lg_rep, s0]
    return pl.pallas_call(
        functools.partial(_ret_scan_kernel, n_chunks=tb // SCAN_CHUNK, on_grid=on_grid),
        name="ret_scan",
        grid=(2, b, nb),
        in_specs=in_specs,
        out_specs=[pl.BlockSpec((None, None, tb, RET_V), lambda dd, bi, i: (dd, bi, blk(dd, i), 0)), st],
        out_shape=[jax.ShapeDtypeStruct((2, b, t, RET_V), BF16),
                   jax.ShapeDtypeStruct((2, b, RET_HEADS, RET_QK_DIM, RET_V_DIM), F32)],
        scratch_shapes=[pltpu.VMEM((RET_HEADS, RET_QK_DIM, RET_V_DIM), F32),
                        pltpu.VMEM((RET_HEADS, 3, SCAN_CHUNK, RET_QK_DIM), F32)],
        compiler_params=_cparams(("parallel", "parallel", "arbitrary")),
    )(*args)


def _odd_finish_kernel(y_ref, g_ref, x_ref, m_ref, wo_ref, o_ref):
    y = y_ref[0].astype(F32) + y_ref[1].astype(F32)
    dv = RET_V_DIM
    parts = []
    for h in range(RET_HEADS):
        yh = y[:, h * dv:(h + 1) * dv]
        parts.append(yh * lax.rsqrt(jnp.mean(yh * yh, axis=-1, keepdims=True) + NORM_EPS))
    yn = jnp.concatenate(parts, axis=1)
    act = (_silu(g_ref[...].astype(F32)) * yn).astype(BF16)
    o_ref[...] = x_ref[...] + m_ref[2:3, :] * _dot(act, wo_ref[...])


def _odd_finish_call(y, proj, x, mods, w_out, tm):
    b, t, d = x.shape
    tm = min(tm, t)
    return pl.pallas_call(
        _odd_finish_kernel,
        name="odd_finish",
        grid=(b, t // tm),
        in_specs=[pl.BlockSpec((2, None, tm, RET_V), lambda bi, i: (0, bi, i, 0)),
                  pl.BlockSpec((None, tm, RET_V), lambda bi, i: (bi, i, (2 * RET_QK + RET_V) // RET_V)),
                  pl.BlockSpec((None, tm, d), lambda bi, i: (bi, i, 0)),
                  pl.BlockSpec((None, N_MOD, d), lambda bi, i: (bi, 0, 0)),
                  pl.BlockSpec((RET_V, d), lambda bi, i: (0, 0))],
        out_specs=pl.BlockSpec((None, tm, d), lambda bi, i: (bi, i, 0)),
        out_shape=jax.ShapeDtypeStruct((b, t, d), F32),
        compiler_params=_cparams(("parallel", "parallel")),
    )(y, proj, x, mods, w_out)


def _ffn_kernel(*refs, on_grid, final_norm):
    if on_grid:
        x_ref, xp_ref, xn_ref, g_ref, m_ref, wu_ref, cw_ref, cb_ref, wd_ref = refs[:9]
        rest = refs[9:]
    else:
        x_ref, g_ref, m_ref, wu_ref, cw_ref, cb_ref, wd_ref = refs[:7]
        rest = refs[7:]
    if final_norm:
        fg_ref, o_ref, *act_scrs = rest
    else:
        o_ref, *act_scrs = rest
    i = pl.program_id(1)
    n_tiles = pl.num_programs(1)
    tm = x_ref.shape[0]

    def norm_mod(xv):
        hv = xv * lax.rsqrt(jnp.mean(xv * xv, axis=-1, keepdims=True) + NORM_EPS) * g_ref[...]
        return (hv * (1.0 + m_ref[4:5, :]) + m_ref[3:4, :]).astype(BF16)

    x = x_ref[...]
    h = norm_mod(x)
    row = lax.broadcasted_iota(jnp.int32, (tm, 1), 0)
    if on_grid:
        col = row % GRID_W
        ok_left = col > 0
        ok_right = col < GRID_W - 1
        zero = jnp.zeros((GRID_W, x.shape[1]), BF16)
        h_ext = jnp.concatenate([jnp.where(i > 0, norm_mod(xp_ref[...]), zero), h,
                                 jnp.where(i < n_tiles - 1, norm_mod(xn_ref[...]), zero)], axis=0)
    else:
        ok_left = row > 0
        ok_right = row < tm - 1
    n_chunks = D_FF // FFN_COL_CHUNK

    def up_proj(j):
        cols = slice(j * FFN_COL_CHUNK, (j + 1) * FFN_COL_CHUNK)
        vcols = slice(D_FF + j * FFN_COL_CHUNK, D_FF + (j + 1) * FFN_COL_CHUNK)
        gate = _dot(h_ext if on_grid else h, wu_ref[:, cols])
        return gate, _dot(h, wu_ref[:, vcols])

    out = None
    nxt = up_proj(0)
    for j in range(n_chunks):
        cols = slice(j * FFN_COL_CHUNK, (j + 1) * FFN_COL_CHUNK)
        gate, val = nxt
        if j + 1 < n_chunks:
            nxt = up_proj(j + 1)
        if on_grid:
            rows3 = [gate[dr * GRID_W:dr * GRID_W + tm] for dr in range(3)]
            taps = [rows3[0] * cw_ref[dc:dc + 1, cols] + rows3[1] * cw_ref[3 + dc:4 + dc, cols]
                    + rows3[2] * cw_ref[6 + dc:7 + dc, cols] for dc in range(3)]
        else:
            taps = [gate * cw_ref[3 + dc:4 + dc, cols] for dc in range(3)]
        acc = cb_ref[:, cols] + taps[1] + jnp.where(ok_left, pltpu.roll(taps[0], 1, axis=0), 0.0) \
            + jnp.where(ok_right, pltpu.roll(taps[2], tm - 1, axis=0), 0.0)
        grp, slot = divmod(j, FFN_DOWN_GROUP)
        act_scr = act_scrs[grp]
        act_scr[:, slot * FFN_COL_CHUNK:(slot + 1) * FFN_COL_CHUNK] = (_gelu_tanh(acc) * val).astype(BF16)
        if slot + 1 == FFN_DOWN_GROUP or j + 1 == n_chunks:
            width = (slot + 1) * FFN_COL_CHUNK
            k0 = grp * FFN_DOWN_GROUP * FFN_COL_CHUNK
            part = _dot(act_scr[:, 0:width], wd_ref[k0:k0 + width, :])
            out = part if out is None else out + part
    out = x + m_ref[5:6, :] * out
    if final_norm:
        out = out * lax.rsqrt(jnp.mean(out * out, axis=-1, keepdims=True) + NORM_EPS) * fg_ref[...]
    o_ref[...] = out


def _ffn_call(x, norm_g, mods, w_up, conv_w9, conv_b, w_down, *, tm, on_grid, final_g=None):
    b, t, d = x.shape
    tm = min(tm, t)
    n_tiles = t // tm
    per_batch = mods.shape[0] > 1
    full = lambda shape: pl.BlockSpec(shape, lambda bi, i: (0,) * len(shape))
    in_specs = [pl.BlockSpec((None, tm, d), lambda bi, i: (bi, i, 0))]
    args = [x]
    if on_grid:
        r = tm // GRID_W
        last = t // GRID_W - 1
        in_specs += [pl.BlockSpec((None, GRID_W, d), lambda bi, i: (bi, jnp.maximum(i * r - 1, 0), 0)),
                     pl.BlockSpec((None, GRID_W, d), lambda bi, i: (bi, jnp.minimum((i + 1) * r, last), 0))]
        args += [x, x]
    else:
        assert n_tiles == 1
    in_specs += [full((1, d)),
                 pl.BlockSpec((None, N_MOD, d), (lambda bi, i: (bi, 0, 0)) if per_batch
                              else (lambda bi, i: (0, 0, 0))),
                 full((d, 2 * D_FF)), full((9, D_FF)), full((1, D_FF)), full((D_FF, d))]
    args += [norm_g, mods, w_up, conv_w9, conv_b, w_down]
    if final_g is not None:
        in_specs.append(full((1, d)))
        args.append(final_g)
    return pl.pallas_call(
        functools.partial(_ffn_kernel, on_grid=on_grid, final_norm=final_g is not None),
        name="conv_ffn",
        grid=(b, n_tiles),
        in_specs=in_specs,
        out_specs=pl.BlockSpec((None, tm, d), lambda bi, i: (bi, i, 0)),
        out_shape=jax.ShapeDtypeStruct((b, t, d), F32),
        scratch_shapes=[pltpu.VMEM((tm, FFN_DOWN_GROUP * FFN_COL_CHUNK), BF16)
                        for _ in range(-(-D_FF // (FFN_DOWN_GROUP * FFN_COL_CHUNK)))],
        compiler_params=_cparams(("parallel", "parallel")),
    )(*args)


def _block_diag2(a, b):
    za = jnp.zeros((a.shape[0], b.shape[1]), a.dtype)
    zb = jnp.zeros((b.shape[0], a.shape[1]), a.dtype)
    return jnp.concatenate([jnp.concatenate([a, za], axis=1), jnp.concatenate([zb, b], axis=1)], axis=0)


def _pad_cols(a, n):
    return jnp.pad(a, ((0, 0), (0, n - a.shape[1])))


def _even_params(j, ev_w_in, ev_mu_prev, ev_mu_next, rk_w0_f, rk_w0_b, rk_w2_f, rk_w2_b, rk_a0_f,
                 rk_a0_b, rk_a2_f, rk_a2_b, rk_g2, rk_k_k, rk_k_a, rk_r_k, rk_ln_w, rk_ln_b,
                 ssd_conv_w, ssd_conv_b, ssd_dt_bias_f, ssd_dt_bias_b, ssd_a_log_f, ssd_a_log_b,
                 ssd_d, ssd_norm_w, ev_w_out):
    w_in = ev_w_in[j]
    rw = w_in[:, :RWKV_COLS]
    z = w_in[:, RWKV_COLS:RWKV_COLS + SSD_WIDTH]
    xbc = w_in[:, RWKV_COLS + SSD_WIDTH:RWKV_COLS + SSD_WIDTH + SSD_XBC]
    dts = w_in[:, RWKV_COLS + SSD_WIDTH + SSD_XBC:]
    w_packed = jnp.concatenate([_pad_cols(jnp.concatenate([rw, dts], axis=1), EV_RW_BLOCK), xbc, z], axis=1)
    head = jnp.arange(RWKV_WIDTH) // RWKV_HEAD_DIM
    head_sum = (head[:, None] == head[None, :]).astype(BF16)
    lane = jnp.arange(V7X_LANES)[:, None]
    tgt = jnp.arange(2 * SSD_WIDTH)[None, :]
    dt_expand = (lane == (tgt // SSD_WIDTH) * SSD_HEADS + (tgt % SSD_WIDTH) // SSD_HEAD_DIM).astype(BF16)
    rep = lambda a: jnp.repeat(a, SSD_HEAD_DIM)[None, :]
    row = lambda a: a[None, :]
    return {
        'w_in': w_packed.astype(BF16),
        'mu_prev': _pad_cols(row(ev_mu_prev[j]), EV_RW_BLOCK),
        'mu_next': _pad_cols(row(ev_mu_next[j]), EV_RW_BLOCK),
        'w0': row(jnp.concatenate([rk_w0_f[j], rk_w0_b[j]])),
        'w2': _block_diag2(rk_w2_f[j], rk_w2_b[j]).astype(BF16),
        'a0': row(jnp.concatenate([rk_a0_f[j], rk_a0_b[j]])),
        'a2': _block_diag2(rk_a2_f[j], rk_a2_b[j]).astype(BF16),
        'g2': rk_g2[j].astype(BF16),
        'k_k': row(rk_k_k[j]), 'k_a': row(rk_k_a[j]), 'r_k': row(rk_r_k[j].reshape(-1)),
        'ln_w': row(rk_ln_w[j]), 'ln_b': row(rk_ln_b[j]),
        'head_sum': head_sum,
        'conv_w': ssd_conv_w[j], 'conv_b': row(ssd_conv_b[j]),
        'dt_bias': jnp.concatenate([rep(ssd_dt_bias_f[j]), rep(ssd_dt_bias_b[j])], axis=1),
        'dt_expand': dt_expand,
        'a_rep': jnp.stack([rep(-jnp.exp(ssd_a_log_f[j])), rep(-jnp.exp(ssd_a_log_b[j]))]),
        'd_skip': rep(ssd_d[j]),
        'norm_w': row(ssd_norm_w[j]),
        'w_out': ev_w_out[j].astype(BF16),
    }


def _rope_tables(t):
    n = RET_QK_DIM // 4
    pos = jnp.arange(t)
    row = (pos // GRID_W).astype(F32)
    col = (pos % GRID_W).astype(F32)
    inv = ROPE_BASE ** (-jnp.arange(n, dtype=F32) / n)
    ang = jnp.concatenate([row[:, None] * inv, col[:, None] * inv], axis=-1)
    cos, sin = jnp.cos(ang), jnp.sin(ang)
    return jnp.concatenate([cos, cos], axis=-1), jnp.concatenate([-sin, sin], axis=-1)


def _conv_ffn(x, mods, norm_g, w_up, conv_w9, conv_b, w_down, *, on_grid, final_g=None):
    return _ffn_call(x, norm_g, mods, w_up, conv_w9, conv_b, w_down, tm=512, on_grid=on_grid,
                     final_g=final_g)


def _even_layer(x, ctx, mods_x, mods_c, norm_g, p):
    b = x.shape[0]

    def features(h, mods):
        proj = _nm_call(h, norm_g, mods, p['w_in'], shift_row=0, tm=512, tn=512)
        feat = _rwkv_feat_call(proj, p, 256)
        xbc, dtbc = _ssd_feat_call(proj, p, 256)
        return proj, feat, xbc, dtbc

    proj_c, feat_c, xbc_c, dt_c = features(ctx, mods_c)
    proj_x, feat_x, xbc_x, dt_x = features(x, mods_x)
    s0 = jnp.zeros((2, b, N_PAIRS, PAIR, PAIR), F32)
    *yrk_c, s_ctx = _rwkv_scan_call(feat_c, s0, 256)
    *yrk_x, _ = _rwkv_scan_call(feat_x, s_ctx, 256)
    h0 = jnp.zeros((2, b, N_PAIRS, SSD_STATE, PAIR), F32)
    ysd_c, h_ctx = _ssd_scan_call(xbc_c, dt_c, p['a_rep'], h0, 256)
    ysd_x, _ = _ssd_scan_call(xbc_x, dt_x, p['a_rep'], h_ctx, 256)
    x = _even_finish_call(yrk_x, feat_x[3], feat_x[4], ysd_x, xbc_x, proj_x, x, mods_x, p, 256)
    ctx = _even_finish_call(yrk_c, feat_c[3], feat_c[4], ysd_c, xbc_c, proj_c, ctx, mods_c, p, 256)
    return x, ctx


def _odd_layer(x, ctx, mods_x, mods_c, norm_g, w_in, lg_rep, w_out):
    b, t, _ = x.shape
    proj_c = _nm_call(ctx, norm_g, mods_c, w_in, shift_row=0, tm=256, tn=512, out_dtype=BF16)
    proj_x = _nm_call(x, norm_g, mods_x, w_in, shift_row=0, tm=512, tn=512, out_dtype=BF16)
    s0 = jnp.zeros((2, b, RET_HEADS, RET_QK_DIM, RET_V_DIM), F32)
    _, s_ctx = _ret_scan_call(proj_c, lg_rep, s0, 256, None)
    y, _ = _ret_scan_call(proj_x, lg_rep, s_ctx, 256, _rope_tables(t))
    return _odd_finish_call(y, proj_x, x, mods_x, w_out, 256)


def kernel(x, c, ctx, c_ctx, mod_w, mod_b, norm1_g, norm2_g, ffn_w_up, ffn_conv_w, ffn_conv_b, ffn_w_down, ev_w_in, ev_mu_prev, ev_mu_next, rk_w0_f, rk_w0_b, rk_w2_f, rk_w2_b, rk_a0_f, rk_a0_b, rk_a2_f, rk_a2_b, rk_g2, rk_k_k, rk_k_a, rk_r_k, rk_ln_w, rk_ln_b, ssd_conv_w, ssd_conv_b, ssd_dt_bias_f, ssd_dt_bias_b, ssd_a_log_f, ssd_a_log_b, ssd_d, ssd_norm_w, ev_w_out, ret_w_in, ret_log2_f, ret_log2_b, ret_w_out, final_norm_g):
    b, t, d = x.shape
    depth = mod_w.shape[0]
    rows = -(-(b + 1) // V7X_SUBLANES) * V7X_SUBLANES
    cond = jnp.concatenate([c, c_ctx[None, :], jnp.zeros((rows - b - 1, d), F32)], axis=0)
    mods = _mod_call(cond, mod_w, mod_b).reshape(depth, rows, N_MOD, d)
    for i in range(depth):
        need_ctx = i < depth - 1
        mods_x = mods[i, :b]
        mods_c = mods[i, b:b + 1]
        j = i // 2
        g1 = norm1_g[i][None, :]
        if i % 2 == 0:
            p = _even_params(j, ev_w_in, ev_mu_prev, ev_mu_next, rk_w0_f, rk_w0_b, rk_w2_f, rk_w2_b,
                             rk_a0_f, rk_a0_b, rk_a2_f, rk_a2_b, rk_g2, rk_k_k, rk_k_a, rk_r_k,
                             rk_ln_w, rk_ln_b, ssd_conv_w, ssd_conv_b, ssd_dt_bias_f, ssd_dt_bias_b,
                             ssd_a_log_f, ssd_a_log_b, ssd_d, ssd_norm_w, ev_w_out)
            x, ctx_mixed = _even_layer(x, ctx, mods_x, mods_c, g1, p)
        else:
            lg = jnp.stack([jnp.log1p(-jnp.exp2(-ret_log2_f[j])), jnp.log1p(-jnp.exp2(-ret_log2_b[j]))])
            lg_rep = jnp.repeat(lg, RET_QK_DIM, axis=-1)[:, None, :]
            x = _odd_layer(x, ctx, mods_x, mods_c, g1, ret_w_in[j].astype(BF16), lg_rep,
                           ret_w_out[j].astype(BF16))
            ctx_mixed = None
        g2 = norm2_g[i][None, :]
        w_up = ffn_w_up[i].astype(BF16)
        w_down = ffn_w_down[i].astype(BF16)
        conv_w9 = ffn_conv_w[i].reshape(9, D_FF)
        conv_b = ffn_conv_b[i][None, :]
        last = i == depth - 1
        x = _conv_ffn(x, mods_x, g2, w_up, conv_w9, conv_b, w_down, on_grid=True,
                      final_g=final_norm_g[None, :] if last else None)
        if need_ctx:
            ctx = _conv_ffn(ctx_mixed, mods_c, g2, w_up, conv_w9, conv_b, w_down, on_grid=False)
    return x
```

```python
import functools
import math

import jax
import jax.numpy as jnp
from jax import lax
from jax.experimental import pallas as pl
from jax.experimental.pallas import tpu as pltpu

F32 = jnp.float32
BF16 = jnp.bfloat16

D_MODEL = 1024
GRID_W = 64
N_MOD = 6
NORM_EPS = 1e-6
RWKV_HEADS = 8
RWKV_HEAD_DIM = 64
RWKV_WIDTH = RWKV_HEADS * RWKV_HEAD_DIM
DECAY_LORA = 64
ICLR_LORA = 64
GATE_LORA = 128
RWKV_GN_EPS = 64e-5
RWKV_COLS = 3 * RWKV_WIDTH + 2 * DECAY_LORA + 2 * ICLR_LORA + GATE_LORA
SSD_HEADS = 8
SSD_HEAD_DIM = 64
SSD_WIDTH = SSD_HEADS * SSD_HEAD_DIM
SSD_GROUPS = 2
SSD_STATE = 128
SSD_XBC = SSD_WIDTH + 2 * SSD_GROUPS * SSD_STATE
RET_HEADS = 8
RET_QK_DIM = 128
RET_V_DIM = 256
RET_QK = RET_HEADS * RET_QK_DIM
RET_V = RET_HEADS * RET_V_DIM
ROPE_BASE = 10000.0
D_FF = 2816

V7X_LANES = 128
V7X_SUBLANES = 8
V7X_VMEM_LIMIT_BYTES = 56 * 1024 * 1024

RWKV_CHUNK = 64
SCAN_CHUNK = 128
FFN_COL_CHUNK = 256
FFN_DOWN_GROUP = 4
PAIR = 2 * RWKV_HEAD_DIM
N_PAIRS = RWKV_HEADS // 2
EV_RW_BLOCK = 2048
EV_DT_OFF = RWKV_COLS
EV_XBC_OFF = EV_RW_BLOCK
EV_Z_OFF = EV_RW_BLOCK + SSD_XBC
EV_COLS = EV_Z_OFF + SSD_WIDTH


def _cparams(sem):
    return pltpu.CompilerParams(dimension_semantics=sem, vmem_limit_bytes=V7X_VMEM_LIMIT_BYTES)


def _split3(x):
    hi = x.astype(BF16)
    r1 = x - hi.astype(F32)
    mid = r1.astype(BF16)
    lo = (r1 - mid.astype(F32)).astype(BF16)
    return hi, mid, lo


def _dot(a, b):
    return jnp.dot(a, b, preferred_element_type=F32)


def _dot_nt(a, b):
    return lax.dot_general(a, b, (((1,), (1,)), ((), ())), preferred_element_type=F32)


def _dot_tn(a, b):
    return lax.dot_general(a, b, (((0,), (0,)), ((), ())), preferred_element_type=F32)


def _dot01(x, m01):
    hi, mid, lo = _split3(x)
    return _dot(hi, m01) + _dot(mid, m01) + _dot(lo, m01)


def _dot01_left(m01, x):
    hi, mid, lo = _split3(x)
    return _dot(m01, hi) + _dot(m01, mid) + _dot(m01, lo)


def _sigmoid(x):
    return 1.0 / (1.0 + jnp.exp(-x))


def _silu(x):
    return x * _sigmoid(x)


def _softplus(x):
    return jnp.maximum(x, 0.0) + jnp.log1p(jnp.exp(-jnp.abs(x)))


def _gelu_tanh(x):
    c = math.sqrt(2.0 / math.pi)
    half = 0.5 * x
    return half + half * jnp.tanh(x * (c + (0.044715 * c) * (x * x)))


def _order_masks(d, n):
    row = lax.broadcasted_iota(jnp.int32, (n, n), 0)
    col = lax.broadcasted_iota(jnp.int32, (n, n), 1)
    diff = (row - col) * (1 - 2 * d)
    return diff > 0, diff >= 0


def _mod_kernel(c_ref, w_ref, b_ref, o_ref):
    h = _silu(c_ref[...])
    hi, mid, lo = _split3(h)
    w = w_ref[...]
    wh = w.astype(BF16)
    wl = (w - wh.astype(F32)).astype(BF16)
    acc = _dot(hi, wh) + _dot(mid, wh) + _dot(hi, wl)
    o_ref[...] = acc + b_ref[...]


def _mod_call(cond, mod_w, mod_b):
    depth, d, n = mod_w.shape
    rows = cond.shape[0]
    tn = 1024
    return pl.pallas_call(
        _mod_kernel,
        name="adaln_mod",
        grid=(depth, n // tn),
        in_specs=[pl.BlockSpec((rows, d), lambda l, j: (0, 0)),
                  pl.BlockSpec((None, d, tn), lambda l, j: (l, 0, j)),
                  pl.BlockSpec((None, 1, tn), lambda l, j: (l, 0, j))],
        out_specs=pl.BlockSpec((None, rows, tn), lambda l, j: (l, 0, j)),
        out_shape=jax.ShapeDtypeStruct((depth, rows, n), F32),
        compiler_params=_cparams(("parallel", "parallel")),
    )(cond, mod_w, mod_b.reshape(depth, 1, n))


def _nm_kernel(*refs, shift_row, tn, qk_mode):
    if qk_mode == 'rope':
        x_ref, g_ref, m_ref, w_ref, cos_ref, sin_ref, o_ref = refs
    else:
        x_ref, g_ref, m_ref, w_ref, o_ref = refs
    x = x_ref[...]
    h = x * lax.rsqrt(jnp.mean(x * x, axis=-1, keepdims=True) + NORM_EPS) * g_ref[...]
    h = h * (1.0 + m_ref[shift_row + 1:shift_row + 2, :]) + m_ref[shift_row:shift_row + 1, :]
    h = h.astype(BF16)
    dk = RET_QK_DIM
    for j in range(w_ref.shape[1] // tn):
        cols = slice(j * tn, (j + 1) * tn)
        y = _dot(h, w_ref[:, cols])
        if qk_mode is not None and j * tn < 2 * RET_QK:
            scale = dk ** -0.5 if j * tn >= RET_QK else 1.0
            heads = []
            for hh in range(tn // dk):
                yh = y[:, hh * dk:(hh + 1) * dk]
                if qk_mode == 'rope':
                    yh = yh * cos_ref[...] + pltpu.roll(yh, dk // 2, axis=1) * sin_ref[...]
                heads.append(yh * scale if scale != 1.0 else yh)
            y = jnp.concatenate(heads, axis=1)
        o_ref[:, cols] = y.astype(o_ref.dtype)


def _nm_call(x, g, mods, w, *, shift_row, tm, tn, out_dtype=F32, qk_mode=None, rope=None):
    b, t, d = x.shape
    n = w.shape[1]
    tm = min(tm, t)
    per_batch = mods.shape[0] > 1
    in_specs = [pl.BlockSpec((None, tm, d), lambda bi, i: (bi, i, 0)),
                pl.BlockSpec((1, d), lambda bi, i: (0, 0)),
                pl.BlockSpec((None, N_MOD, d), (lambda bi, i: (bi, 0, 0)) if per_batch
                             else (lambda bi, i: (0, 0, 0))),
                pl.BlockSpec((d, n), lambda bi, i: (0, 0))]
    args = [x, g, mods, w]
    if qk_mode == 'rope':
        tab = pl.BlockSpec((tm, RET_QK_DIM), lambda bi, i: (i, 0))
        in_specs += [tab, tab]
        args += list(rope)
    return pl.pallas_call(
        functools.partial(_nm_kernel, shift_row=shift_row, tn=tn, qk_mode=qk_mode),
        name="norm_mod_matmul",
        grid=(b, t // tm),
        in_specs=in_specs,
        out_specs=pl.BlockSpec((None, tm, n), lambda bi, i: (bi, i, 0)),
        out_shape=jax.ShapeDtypeStruct((b, t, n), out_dtype),
        compiler_params=_cparams(("parallel", "parallel")),
    )(*args)


def _halo_specs(tm, width, col_block, n_row_tiles, t):
    r = tm // V7X_SUBLANES
    last = t // V7X_SUBLANES - 1
    return [pl.BlockSpec((None, tm, width), lambda bi, i: (bi, i, col_block)),
            pl.BlockSpec((None, V7X_SUBLANES, width),
                         lambda bi, i: (bi, jnp.maximum(i * r - 1, 0), col_block)),
            pl.BlockSpec((None, V7X_SUBLANES, width),
                         lambda bi, i: (bi, jnp.minimum((i + 1) * r, last), col_block))]


def _shifted(u, prev_blk, next_blk, i, n_tiles):
    tm = u.shape[0]
    row = lax.broadcasted_iota(jnp.int32, u.shape, 0)
    prev_row = jnp.where(i > 0, prev_blk[V7X_SUBLANES - 1:V7X_SUBLANES, :], 0.0)
    next_row = jnp.where(i < n_tiles - 1, next_blk[0:1, :], 0.0)
    prev = jnp.where(row == 0, prev_row, pltpu.roll(u, 1, axis=0))
    nxt = jnp.where(row == tm - 1, next_row, pltpu.roll(u, tm - 1, axis=0))
    return prev, nxt


def _rwkv_feat_kernel(u_ref, up_ref, un_ref, mup_ref, mun_ref, w0_ref, w2_ref, a0_ref, a2_ref,
                      g2_ref, kk_ref, ka_ref, rk_ref, j_ref,
                      r_ref, v_ref, kkn_ref, bonus_ref, gate_ref, ld_ref, kd_ref, bd_ref):
    i = pl.program_id(1)
    n_tiles = pl.num_programs(1)
    u = u_ref[...]
    prev, nxt = _shifted(u, up_ref[...], un_ref[...], i, n_tiles)
    rw = u + mup_ref[...] * (prev - u) + mun_ref[...] * (nxt - u)
    w = RWKV_WIDTH
    r = rw[:, 0:w]
    k = rw[:, w:2 * w]
    v = rw[:, 2 * w:3 * w]
    wd = rw[:, 3 * w:3 * w + 2 * DECAY_LORA]
    ad = rw[:, 3 * w + 2 * DECAY_LORA:3 * w + 2 * DECAY_LORA + 2 * ICLR_LORA]
    gd = rw[:, 3 * w + 2 * DECAY_LORA + 2 * ICLR_LORA:RWKV_COLS]
    jm = j_ref[...]

    kk = k * kk_ref[...]
    ss = _dot01(kk * kk, jm)
    kk = kk / jnp.maximum(jnp.sqrt(ss), 1e-12)
    r_ref[...] = r.astype(r_ref.dtype)
    v_ref[...] = v.astype(v_ref.dtype)
    kkn_ref[...] = kk.astype(kkn_ref.dtype)
    bonus_ref[...] = (_dot01(r * k * rk_ref[...], jm) * v).astype(bonus_ref.dtype)
    gate_ref[...] = _dot(_sigmoid(gd).astype(BF16), g2_ref[...]).astype(gate_ref.dtype)

    zw = _dot(jnp.tanh(wd).astype(BF16), w2_ref[...]) + w0_ref[...]
    za = _dot(ad.astype(BF16), a2_ref[...]) + a0_ref[...]
    for di in range(2):
        ld_ref[di] = -math.exp(-0.5) * _sigmoid(zw[:, di * w:(di + 1) * w])
        iclr = _sigmoid(za[:, di * w:(di + 1) * w])
        kd_ref[di] = (k * (1.0 + (iclr - 1.0) * ka_ref[...])).astype(kd_ref.dtype)
        bd_ref[di] = (kk * iclr).astype(bd_ref.dtype)


def _rwkv_feat_call(proj, p, tm):
    b, t, _ = proj.shape
    tm = min(tm, t)
    n_tiles = t // tm
    w = RWKV_WIDTH
    full = lambda shape: pl.BlockSpec(shape, lambda bi, i: (0,) * len(shape))
    tok = pl.BlockSpec((None, tm, w), lambda bi, i: (bi, i, 0))
    tok2 = pl.BlockSpec((2, None, tm, w), lambda bi, i: (0, bi, i, 0))
    sd = jax.ShapeDtypeStruct((b, t, w), BF16)
    sd2 = jax.ShapeDtypeStruct((2, b, t, w), BF16)
    ld2 = jax.ShapeDtypeStruct((2, b, t, w), F32)
    return pl.pallas_call(
        _rwkv_feat_kernel,
        name="rwkv_feat",
        grid=(b, n_tiles),
        in_specs=_halo_specs(tm, EV_RW_BLOCK, 0, n_tiles, t) + [
            full((1, EV_RW_BLOCK)), full((1, EV_RW_BLOCK)),
            full((1, 2 * w)), full((2 * DECAY_LORA, 2 * w)),
            full((1, 2 * w)), full((2 * ICLR_LORA, 2 * w)),
            full((GATE_LORA, w)), full((1, w)), full((1, w)), full((1, w)), full((w, w))],
        out_specs=[tok, tok, tok, tok, tok, tok2, tok2, tok2],
        out_shape=[sd, sd, sd, sd, sd, ld2, sd2, sd2],
        compiler_params=_cparams(("parallel", "parallel")),
    )(proj, proj, proj, p['mu_prev'], p['mu_next'], p['w0'], p['w2'], p['a0'], p['a2'],
      p['g2'], p['k_k'], p['k_a'], p['r_k'], p['head_sum'])


def _rwkv_scan_kernel(rf_ref, vf_ref, kkf_ref, ldf_ref, kdf_ref, bdf_ref,
                      rb_ref, vb_ref, kkb_ref, ldb_ref, kdb_ref, bdb_ref, s0_ref,
                      yf_ref, yb_ref, sf_ref, st_scr, *, n_chunks):
    i = pl.program_id(1)
    c = RWKV_CHUNK

    @pl.when(i == 0)
    def _():
        st_scr[...] = s0_ref[...]

    row2 = lax.broadcasted_iota(jnp.int32, (2 * c, 2 * c), 0)
    col2 = lax.broadcasted_iota(jnp.int32, (2 * c, 2 * c), 1)
    same_head = (row2 // c) == (col2 // c)
    eye_bd = (row2 == col2).astype(F32)
    rowc = lax.broadcasted_iota(jnp.int32, (c, 2 * c), 0)
    colc = lax.broadcasted_iota(jnp.int32, (c, 2 * c), 1) % c
    m0 = lax.broadcasted_iota(jnp.int32, (c, PAIR), 1) < RWKV_HEAD_DIM
    row1 = lax.broadcasted_iota(jnp.int32, (c, 1), 0)
    incl01, strict_bd, incl_wide, rsel = [], [], [], []
    for d in range(2):
        sgn = 1 - 2 * d
        incl01.append(_order_masks(d, c)[1].astype(BF16))
        strict_bd.append(jnp.logical_and(same_head, (row2 % c - col2 % c) * sgn > 0))
        incl_wide.append((rowc - colc) * sgn >= 0)
        rsel.append(row1 == (c - 1 if d == 0 else 0))
    refs = ((rf_ref, vf_ref, kkf_ref, ldf_ref, kdf_ref, bdf_ref, yf_ref),
            (rb_ref, vb_ref, kkb_ref, ldb_ref, kdb_ref, bdb_ref, yb_ref))
    chains = [(d, p) for d in range(2) for p in range(N_PAIRS)]
    lanes = [slice(p * PAIR, (p + 1) * PAIR) for p in range(N_PAIRS)]

    def stack(x):
        return jnp.concatenate([jnp.where(m0, x, 0.0), jnp.where(m0, 0.0, x)], axis=0)

    def chunk_body(j, carry):
        rows = (pl.ds(pl.multiple_of(j * c, c), c),
                pl.ds(pl.multiple_of((n_chunks - 1 - j) * c, c), c))
        cs_all = [_dot01_left(incl01[d], refs[d][3][rows[d], :]) for d in range(2)]
        a_s, r_t, b_s, k_s, v_s, b_end, k_end, g_tot = [], [], [], [], [], [], [], []
        for d, p in chains:
            r_ref, v_ref, kk_ref, ld_ref, kd_ref, bd_ref, _ = refs[d]
            rw, ln = rows[d], lanes[p]
            cs = cs_all[d][:, ln]
            cs_last = jnp.sum(jnp.where(rsel[d], cs, 0.0), axis=0, keepdims=True)
            g_neg = jnp.exp(-cs)
            g_end = jnp.exp(cs_last - cs)
            g_tot.append(jnp.exp(cs_last))
            kdv = kd_ref[rw, ln].astype(F32)
            bdv = bd_ref[rw, ln].astype(F32)
            a_s.append(stack(-kk_ref[rw, ln].astype(F32) * jnp.exp(cs - ld_ref[rw, ln])).astype(BF16))
            r_t.append((r_ref[rw, ln].astype(F32) * jnp.exp(cs)).astype(BF16))
            b_s.append(stack(bdv * g_neg).astype(BF16))
            k_s.append(stack(kdv * g_neg).astype(BF16))
            v_s.append(stack(v_ref[rw, ln].astype(F32)).astype(BF16))
            b_end.append(stack(bdv * g_end).astype(BF16))
            k_end.append(stack(kdv * g_end).astype(BF16))
        n = range(len(chains))
        dirs = [d for d, _ in chains]
        gram = [_dot_nt(jnp.concatenate([a_s[q], r_t[q]], axis=0),
                        jnp.concatenate([b_s[q], k_s[q]], axis=0)) for q in n]
        a_ab = [jnp.where(strict_bd[dirs[q]], gram[q][0:2 * c, 0:2 * c], 0.0) for q in n]
        a_ak = [jnp.where(strict_bd[dirs[q]], gram[q][0:2 * c, 2 * c:4 * c], 0.0).astype(BF16) for q in n]
        p_rb = [jnp.where(incl_wide[dirs[q]], gram[q][2 * c:3 * c, 0:2 * c], 0.0).astype(BF16) for q in n]
        p_rk = [jnp.where(incl_wide[dirs[q]], gram[q][2 * c:3 * c, 2 * c:4 * c], 0.0).astype(BF16) for q in n]
        minv = [eye_bd + a_ab[q] for q in n]
        pwb = [a_ab[q].astype(BF16) for q in n]
        pwb = [_dot(pwb[q], pwb[q]).astype(BF16) for q in n]
        levels = int(math.log2(c))
        for k in range(2, levels):
            both = [_dot(pwb[q], jnp.concatenate([pwb[q], minv[q].astype(BF16)], axis=1)) for q in n]
            minv = [minv[q] + both[q][:, 2 * c:4 * c] for q in n]
            pwb = [both[q][:, 0:2 * c].astype(BF16) for q in n]
        minv = [minv[q] + _dot(pwb[q], minv[q].astype(BF16)) for q in n]
        akv = [_dot(a_ak[q], v_s[q]).astype(BF16) for q in n]
        eff = [_dot(minv[q].astype(BF16), jnp.concatenate([a_s[q], akv[q]], axis=1)) for q in n]
        bk_end = [jnp.concatenate([b_end[q], k_end[q]], axis=0) for q in n]
        p_both = [jnp.concatenate([p_rb[q], p_rk[q]], axis=1) for q in n]
        st = [st_scr[d, p] for d, p in chains]
        fs = [_dot_nt(jnp.concatenate([eff[q][:, 0:PAIR].astype(BF16), r_t[q]], axis=0),
                      st[q].astype(BF16)) for q in n]
        uv = [jnp.concatenate([(fs[q][0:2 * c] + eff[q][:, PAIR:2 * PAIR]).astype(BF16), v_s[q]], axis=0)
              for q in n]
        for q, (d, p) in enumerate(chains):
            st_scr[d, p] = st[q] * g_tot[q] + _dot_tn(uv[q], bk_end[q])
        for q, (d, p) in enumerate(chains):
            refs[d][6][rows[d], lanes[p]] = fs[q][2 * c:3 * c] + _dot(p_both[q], uv[q])
        return carry

    lax.fori_loop(0, n_chunks, chunk_body, 0, unroll=2)

    @pl.when(i == pl.num_programs(1) - 1)
    def _():
        sf_ref[...] = st_scr[...]


def _rwkv_scan_call(feat, s0, tb):
    r, v, kk, _, _, ld, kd, bd = feat
    b, t, w = r.shape
    tb = min(tb, t)
    nb = t // tb
    tok_f = pl.BlockSpec((None, tb, w), lambda bi, i: (bi, i, 0))
    tok_b = pl.BlockSpec((None, tb, w), lambda bi, i: (bi, nb - 1 - i, 0))
    dir_f = pl.BlockSpec((None, None, tb, w), lambda bi, i: (0, bi, i, 0))
    dir_b = pl.BlockSpec((None, None, tb, w), lambda bi, i: (1, bi, nb - 1 - i, 0))
    st = pl.BlockSpec((2, None, N_PAIRS, PAIR, PAIR), lambda bi, i: (0, bi, 0, 0, 0))
    y_sd = jax.ShapeDtypeStruct((b, t, w), F32)
    return pl.pallas_call(
        functools.partial(_rwkv_scan_kernel, n_chunks=tb // RWKV_CHUNK),
        name="rwkv_scan",
        grid=(b, nb),
        in_specs=[tok_f, tok_f, tok_f, dir_f, dir_f, dir_f, tok_b, tok_b, tok_b, dir_b, dir_b, dir_b, st],
        out_specs=[tok_f, tok_b, st],
        out_shape=[y_sd, y_sd, jax.ShapeDtypeStruct((2, b, N_PAIRS, PAIR, PAIR), F32)],
        scratch_shapes=[pltpu.VMEM((2, N_PAIRS, PAIR, PAIR), F32)],
        compiler_params=_cparams(("parallel", "arbitrary")),
    )(r, v, kk, ld, kd, bd, r, v, kk, ld, kd, bd, s0)


def _ssd_feat_kernel(x_ref, xp_ref, xn_ref, dt_ref, cw_ref, cb_ref, dtb_ref, e_ref,
                     xbc_ref, dtbc_ref):
    i = pl.program_id(1)
    n_tiles = pl.num_programs(1)
    x = x_ref[...]
    prev, nxt = _shifted(x, xp_ref[...], xn_ref[...], i, n_tiles)
    y = prev * cw_ref[0:1, :] + x * cw_ref[1:2, :] + nxt * cw_ref[2:3, :] + cb_ref[...]
    xbc_ref[...] = _silu(y).astype(xbc_ref.dtype)
    dt_rep = _dot01(dt_ref[...], e_ref[...])
    dt = _softplus(dt_rep + dtb_ref[...])
    dtbc_ref[0] = dt[:, 0:SSD_WIDTH]
    dtbc_ref[1] = dt[:, SSD_WIDTH:2 * SSD_WIDTH]


def _ssd_feat_call(proj, p, tm):
    b, t, _ = proj.shape
    tm = min(tm, t)
    n_tiles = t // tm
    full = lambda shape: pl.BlockSpec(shape, lambda bi, i: (0,) * len(shape))
    return pl.pallas_call(
        _ssd_feat_kernel,
        name="ssd_feat",
        grid=(b, n_tiles),
        in_specs=_halo_specs(tm, SSD_XBC, EV_XBC_OFF // SSD_XBC, n_tiles, t) + [
            pl.BlockSpec((None, tm, V7X_LANES), lambda bi, i: (bi, i, EV_DT_OFF // V7X_LANES)),
            full((3, SSD_XBC)), full((1, SSD_XBC)), full((1, 2 * SSD_WIDTH)),
            full((V7X_LANES, 2 * SSD_WIDTH))],
        out_specs=[pl.BlockSpec((None, tm, SSD_XBC), lambda bi, i: (bi, i, 0)),
                   pl.BlockSpec((2, None, tm, SSD_WIDTH), lambda bi, i: (0, bi, i, 0))],
        out_shape=[jax.ShapeDtypeStruct((b, t, SSD_XBC), BF16),
                   jax.ShapeDtypeStruct((2, b, t, SSD_WIDTH), F32)],
        compiler_params=_cparams(("parallel", "parallel")),
    )(proj, proj, proj, proj, p['conv_w'], p['conv_b'], p['dt_bias'], p['dt_expand'])


def _ssd_scan_kernel(xbc_ref, dt_ref, a_ref, s0_ref, y_ref, sf_ref, st_scr, *, n_chunks):
    d = pl.program_id(0)
    i = pl.program_id(2)
    c = SCAN_CHUNK
    hd = SSD_HEAD_DIM

    @pl.when(i == 0)
    def _():
        st_scr[...] = s0_ref[...]

    _, before_eq = _order_masks(d, c)
    incl01 = before_eq.astype(BF16)
    last_row = jnp.where(d == 0, c - 1, 0)
    rsel = lax.broadcasted_iota(jnp.int32, (c, 1), 0) == last_row
    lane = lax.broadcasted_iota(jnp.int32, (c, PAIR), 1)
    m0 = lane < hd
    a_row = a_ref[...]

    def chunk_body(j, carry):
        cj = jnp.where(d == 0, j, n_chunks - 1 - j)
        rows = pl.ds(pl.multiple_of(cj * c, c), c)
        dt = dt_ref[rows, :]
        cs_all = _dot01_left(incl01, dt * a_row)
        pairs = range(N_PAIRS)
        group = [p // (N_PAIRS // SSD_GROUPS) for p in pairs]
        lanes = [slice(p * PAIR, (p + 1) * PAIR) for p in pairs]
        bm = [xbc_ref[rows, SSD_WIDTH + g * SSD_STATE:SSD_WIDTH + (g + 1) * SSD_STATE]
              for g in range(SSD_GROUPS)]
        cm = [xbc_ref[rows, SSD_WIDTH + (SSD_GROUPS + g) * SSD_STATE:
                      SSD_WIDTH + (SSD_GROUPS + g + 1) * SSD_STATE] for g in range(SSD_GROUPS)]
        cb = [_dot_nt(cm[g], bm[g]) for g in range(SSD_GROUPS)]
        st = [st_scr[p] for p in pairs]
        y_st = [_dot(cm[group[p]], st[p].astype(BF16)) for p in pairs]
        cs = [cs_all[:, lanes[p]] for p in pairs]
        xdt = [xbc_ref[rows, lanes[p]].astype(F32) * dt[:, lanes[p]] for p in pairs]
        probs = []
        for p in pairs:
            cs_t = cs[p].T
            both = []
            for hh in range(2):
                col = cs[p][:, hh * hd:hh * hd + 1]
                rowv = cs_t[hh * hd:hh * hd + 1, :]
                dec = jnp.exp(jnp.where(before_eq, col - rowv, -jnp.inf))
                both.append((cb[group[p]] * dec).astype(BF16))
            probs.append(jnp.concatenate(both, axis=1))
        xs2 = [jnp.concatenate([jnp.where(m0, xdt[p], 0.0), jnp.where(m0, 0.0, xdt[p])],
                               axis=0).astype(BF16) for p in pairs]
        y_in = [_dot(probs[p], xs2[p]) for p in pairs]
        for p in pairs:
            y_ref[rows, lanes[p]] = y_in[p] + jnp.exp(cs[p]) * y_st[p]
        for p in pairs:
            cs_last = jnp.sum(jnp.where(rsel, cs[p], 0.0), axis=0, keepdims=True)
            xe = (xdt[p] * jnp.exp(cs_last - cs[p])).astype(BF16)
            st_scr[p] = st[p] * jnp.exp(cs_last) + _dot_tn(bm[group[p]], xe)
        return carry

    lax.fori_loop(0, n_chunks, chunk_body, 0, unroll=2)

    @pl.when(i == pl.num_programs(2) - 1)
    def _():
        sf_ref[...] = st_scr[...]


def _ssd_scan_call(xbc, dtbc, a_rep, s0, tb):
    b, t, _ = xbc.shape
    tb = min(tb, t)
    nb = t // tb
    blk = lambda dd, i: i + dd * (nb - 1 - 2 * i)
    st = pl.BlockSpec((None, None, N_PAIRS, SSD_STATE, PAIR), lambda dd, bi, i: (dd, bi, 0, 0, 0))
    tok2 = pl.BlockSpec((None, None, tb, SSD_WIDTH), lambda dd, bi, i: (dd, bi, blk(dd, i), 0))
    return pl.pallas_call(
        functools.partial(_ssd_scan_kernel, n_chunks=tb // SCAN_CHUNK),
        name="ssd_scan",
        grid=(2, b, nb),
        in_specs=[pl.BlockSpec((None, tb, SSD_XBC), lambda dd, bi, i: (bi, blk(dd, i), 0)),
                  tok2,
                  pl.BlockSpec((None, 1, SSD_WIDTH), lambda dd, bi, i: (dd, 0, 0)),
                  st],
        out_specs=[tok2, st],
        out_shape=[jax.ShapeDtypeStruct((2, b, t, SSD_WIDTH), F32),
                   jax.ShapeDtypeStruct((2, b, N_PAIRS, SSD_STATE, PAIR), F32)],
        scratch_shapes=[pltpu.VMEM((N_PAIRS, SSD_STATE, PAIR), F32)],
        compiler_params=_cparams(("parallel", "parallel", "arbitrary")),
    )(xbc, dtbc, a_rep, s0)


def _even_finish_kernel(yrf_ref, yrb_ref, bonus_ref, gate_ref, ysd_ref, xs_ref, z_ref, x_ref, m_ref,
                        lnw_ref, lnb_ref, dsk_ref, nw_ref, j_ref, wo_ref, o_ref):
    jm = j_ref[...]
    y = yrf_ref[...] + yrb_ref[...]
    inv_n = 1.0 / RWKV_HEAD_DIM
    mean = _dot01(y, jm) * inv_n
    yc = y - mean
    var = _dot01(yc * yc, jm) * inv_n
    y = yc * lax.rsqrt(var + RWKV_GN_EPS) * lnw_ref[...] + lnb_ref[...]
    y_rk = (y + bonus_ref[...].astype(F32)) * gate_ref[...].astype(F32)
    s = ysd_ref[0] + ysd_ref[1] + dsk_ref[...] * xs_ref[...].astype(F32)
    s = s * _silu(z_ref[...])
    s = s * lax.rsqrt(jnp.mean(s * s, axis=-1, keepdims=True) + NORM_EPS) * nw_ref[...]
    out = _dot(y_rk.astype(BF16), wo_ref[0:RWKV_WIDTH, :]) + \
        _dot(s.astype(BF16), wo_ref[RWKV_WIDTH:RWKV_WIDTH + SSD_WIDTH, :])
    o_ref[...] = x_ref[...] + m_ref[2:3, :] * out


def _even_finish_call(yrk, bonus, gate, ysd, xbc, proj, x, mods, p, tm):
    b, t, d = x.shape
    tm = min(tm, t)
    w = RWKV_WIDTH
    per_batch = mods.shape[0] > 1
    full = lambda shape: pl.BlockSpec(shape, lambda bi, i: (0,) * len(shape))
    tok = pl.BlockSpec((None, tm, w), lambda bi, i: (bi, i, 0))
    tok2 = pl.BlockSpec((2, None, tm, w), lambda bi, i: (0, bi, i, 0))
    return pl.pallas_call(
        _even_finish_kernel,
        name="even_finish",
        grid=(b, t // tm),
        in_specs=[tok, tok, tok, tok, tok2,
                  pl.BlockSpec((None, tm, SSD_WIDTH), lambda bi, i: (bi, i, 0)),
                  pl.BlockSpec((None, tm, SSD_WIDTH), lambda bi, i: (bi, i, EV_Z_OFF // SSD_WIDTH)),
                  pl.BlockSpec((None, tm, d), lambda bi, i: (bi, i, 0)),
                  pl.BlockSpec((None, N_MOD, d), (lambda bi, i: (bi, 0, 0)) if per_batch
                               else (lambda bi, i: (0, 0, 0))),
                  full((1, w)), full((1, w)), full((1, w)), full((1, w)), full((w, w)),
                  full((2 * w, d))],
        out_specs=pl.BlockSpec((None, tm, d), lambda bi, i: (bi, i, 0)),
        out_shape=jax.ShapeDtypeStruct((b, t, d), F32),
        compiler_params=_cparams(("parallel", "parallel")),
    )(yrk[0], yrk[1], bonus, gate, ysd, xbc, proj, x, mods, p['ln_w'], p['ln_b'], p['d_skip'], p['norm_w'],
      p['head_sum'], p['w_out'])


def _ret_scan_kernel(q_ref, k_ref, v_ref, lg_ref, s0_ref, y_ref, sf_ref, st_scr, dec_scr, sc_scr,
                     *, n_chunks):
    d = pl.program_id(0)
    i = pl.program_id(2)
    c = SCAN_CHUNK
    dk, dv = RET_QK_DIM, RET_V_DIM
    heads = range(RET_HEADS)
    lg_all = lg_ref[...]

    @pl.when(i == 0)
    def _():
        st_scr[...] = s0_ref[...]
        _, before_eq = _order_masks(d, c)
        row = lax.broadcasted_iota(jnp.int32, (c, c), 0)
        col = lax.broadcasted_iota(jnp.int32, (c, c), 1)
        rel = jnp.abs(row - col).astype(F32)
        pos = (row + d * (c - 1 - 2 * row)).astype(F32)
        for h in heads:
            lg = lg_all[:, h * dk:h * dk + 1]
            dec_scr[h] = jnp.where(before_eq, jnp.exp(rel * lg), 0.0)
            sc_scr[h, 0] = jnp.exp((pos + 1.0) * lg).astype(BF16)
            sc_scr[h, 1] = jnp.exp((c - 1.0 - pos) * lg).astype(BF16)

    def chunk_body(j, carry):
        cj = jnp.where(d == 0, j, n_chunks - 1 - j)
        rows = pl.ds(pl.multiple_of(cj * c, c), c)
        qs = [q_ref[rows, h * dk:(h + 1) * dk] for h in heads]
        ks = [k_ref[rows, h * dk:(h + 1) * dk] for h in heads]
        vs = [v_ref[rows, h * dv:(h + 1) * dv] for h in heads]
        qk = [_dot_nt(qs[h], ks[h]) for h in heads]
        scores = [(qk[h] * dec_scr[h]).astype(BF16) for h in heads]
        st = [st_scr[h] for h in heads]
        y_st = [_dot(qs[h] * sc_scr[h, 0], st[h].astype(BF16)) for h in heads]
        for h in heads:
            y_ref[rows, h * dv:(h + 1) * dv] = (_dot(scores[h], vs[h]) + y_st[h]).astype(y_ref.dtype)
        for h in heads:
            lg = lg_all[:, h * dk:h * dk + 1]
            st_scr[h] = st[h] * jnp.exp(c * lg) + _dot_tn(ks[h] * sc_scr[h, 1], vs[h])
        return carry

    lax.fori_loop(0, n_chunks, chunk_body, 0, unroll=2)

    @pl.when(i == pl.num_programs(2) - 1)
    def _():
        sf_ref[...] = st_scr[...]


def _ret_scan_call(proj, lg_rep, s0, tb):
    b, t, _ = proj.shape
    tb = min(tb, t)
    nb = t // tb
    blk = lambda dd, i: i + dd * (nb - 1 - 2 * i)
    st = pl.BlockSpec((None, None, RET_HEADS, RET_QK_DIM, RET_V_DIM), lambda dd, bi, i: (dd, bi, 0, 0, 0))
    return pl.pallas_call(
        functools.partial(_ret_scan_kernel, n_chunks=tb // SCAN_CHUNK),
        name="ret_scan",
        grid=(2, b, nb),
        in_specs=[pl.BlockSpec((None, tb, RET_QK), lambda dd, bi, i: (bi, blk(dd, i), 0)),
                  pl.BlockSpec((None, tb, RET_QK), lambda dd, bi, i: (bi, blk(dd, i), 1)),
                  pl.BlockSpec((None, tb, RET_V), lambda dd, bi, i: (bi, blk(dd, i), 2 * RET_QK // RET_V)),
                  pl.BlockSpec((None, 1, RET_QK), lambda dd, bi, i: (dd, 0, 0)), st],
        out_specs=[pl.BlockSpec((None, None, tb, RET_V), lambda dd, bi, i: (dd, bi, blk(dd, i), 0)), st],
        out_shape=[jax.ShapeDtypeStruct((2, b, t, RET_V), BF16),
                   jax.ShapeDtypeStruct((2, b, RET_HEADS, RET_QK_DIM, RET_V_DIM), F32)],
        scratch_shapes=[pltpu.VMEM((RET_HEADS, RET_QK_DIM, RET_V_DIM), F32),
                        pltpu.VMEM((RET_HEADS, SCAN_CHUNK, SCAN_CHUNK), F32),
                        pltpu.VMEM((RET_HEADS, 2, SCAN_CHUNK, RET_QK_DIM), BF16)],
        compiler_params=_cparams(("parallel", "parallel", "arbitrary")),
    )(proj, proj, proj, lg_rep, s0)


def _odd_finish_kernel(y_ref, g_ref, x_ref, m_ref, wo_ref, o_ref):
    y = y_ref[0].astype(F32) + y_ref[1].astype(F32)
    dv = RET_V_DIM
    parts = []
    for h in range(RET_HEADS):
        yh = y[:, h * dv:(h + 1) * dv]
        parts.append(yh * lax.rsqrt(jnp.mean(yh * yh, axis=-1, keepdims=True) + NORM_EPS))
    yn = jnp.concatenate(parts, axis=1)
    act = (_silu(g_ref[...].astype(F32)) * yn).astype(BF16)
    o_ref[...] = x_ref[...] + m_ref[2:3, :] * _dot(act, wo_ref[...])


def _odd_finish_call(y, proj, x, mods, w_out, tm):
    b, t, d = x.shape
    tm = min(tm, t)
    return pl.pallas_call(
        _odd_finish_kernel,
        name="odd_finish",
        grid=(b, t // tm),
        in_specs=[pl.BlockSpec((2, None, tm, RET_V), lambda bi, i: (0, bi, i, 0)),
                  pl.BlockSpec((None, tm, RET_V), lambda bi, i: (bi, i, (2 * RET_QK + RET_V) // RET_V)),
                  pl.BlockSpec((None, tm, d), lambda bi, i: (bi, i, 0)),
                  pl.BlockSpec((None, N_MOD, d), lambda bi, i: (bi, 0, 0)),
                  pl.BlockSpec((RET_V, d), lambda bi, i: (0, 0))],
        out_specs=pl.BlockSpec((None, tm, d), lambda bi, i: (bi, i, 0)),
        out_shape=jax.ShapeDtypeStruct((b, t, d), F32),
        compiler_params=_cparams(("parallel", "parallel")),
    )(y, proj, x, mods, w_out)


def _ffn_kernel(*refs, on_grid, final_norm):
    if on_grid:
        x_ref, xp_ref, xn_ref, g_ref, m_ref, wu_ref, cw_ref, cb_ref, wd_ref = refs[:9]
        rest = refs[9:]
    else:
        x_ref, g_ref, m_ref, wu_ref, cw_ref, cb_ref, wd_ref = refs[:7]
        rest = refs[7:]
    if final_norm:
        fg_ref, o_ref, *act_scrs = rest
    else:
        o_ref, *act_scrs = rest
    i = pl.program_id(1)
    n_tiles = pl.num_programs(1)
    tm = x_ref.shape[0]

    def norm_mod(xv):
        hv = xv * lax.rsqrt(jnp.mean(xv * xv, axis=-1, keepdims=True) + NORM_EPS) * g_ref[...]
        return (hv * (1.0 + m_ref[4:5, :]) + m_ref[3:4, :]).astype(BF16)

    x = x_ref[...]
    h = norm_mod(x)
    row = lax.broadcasted_iota(jnp.int32, (tm, 1), 0)
    if on_grid:
        col = row % GRID_W
        ok_left = col > 0
        ok_right = col < GRID_W - 1
        zero = jnp.zeros((GRID_W, x.shape[1]), BF16)
        h_ext = jnp.concatenate([jnp.where(i > 0, norm_mod(xp_ref[...]), zero), h,
                                 jnp.where(i < n_tiles - 1, norm_mod(xn_ref[...]), zero)], axis=0)
    else:
        ok_left = row > 0
        ok_right = row < tm - 1
    n_chunks = D_FF // FFN_COL_CHUNK

    def up_proj(j):
        cols = slice(j * FFN_COL_CHUNK, (j + 1) * FFN_COL_CHUNK)
        vcols = slice(D_FF + j * FFN_COL_CHUNK, D_FF + (j + 1) * FFN_COL_CHUNK)
        gate = _dot(h_ext if on_grid else h, wu_ref[:, cols])
        return gate, _dot(h, wu_ref[:, vcols])

    out = None
    nxt = up_proj(0)
    for j in range(n_chunks):
        cols = slice(j * FFN_COL_CHUNK, (j + 1) * FFN_COL_CHUNK)
        gate, val = nxt
        if j + 1 < n_chunks:
            nxt = up_proj(j + 1)
        if on_grid:
            rows3 = [gate[dr * GRID_W:dr * GRID_W + tm] for dr in range(3)]
            taps = [rows3[0] * cw_ref[dc:dc + 1, cols] + rows3[1] * cw_ref[3 + dc:4 + dc, cols]
                    + rows3[2] * cw_ref[6 + dc:7 + dc, cols] for dc in range(3)]
        else:
            taps = [gate * cw_ref[3 + dc:4 + dc, cols] for dc in range(3)]
        acc = cb_ref[:, cols] + taps[1] + jnp.where(ok_left, pltpu.roll(taps[0], 1, axis=0), 0.0) \
            + jnp.where(ok_right, pltpu.roll(taps[2], tm - 1, axis=0), 0.0)
        grp, slot = divmod(j, FFN_DOWN_GROUP)
        act_scr = act_scrs[grp]
        act_scr[:, slot * FFN_COL_CHUNK:(slot + 1) * FFN_COL_CHUNK] = (_gelu_tanh(acc) * val).astype(BF16)
        if slot + 1 == FFN_DOWN_GROUP or j + 1 == n_chunks:
            width = (slot + 1) * FFN_COL_CHUNK
            k0 = grp * FFN_DOWN_GROUP * FFN_COL_CHUNK
            part = _dot(act_scr[:, 0:width], wd_ref[k0:k0 + width, :])
            out = part if out is None else out + part
    out = x + m_ref[5:6, :] * out
    if final_norm:
        out = out * lax.rsqrt(jnp.mean(out * out, axis=-1, keepdims=True) + NORM_EPS) * fg_ref[...]
    o_ref[...] = out


def _ffn_call(x, norm_g, mods, w_up, conv_w9, conv_b, w_down, *, tm, on_grid, final_g=None):
    b, t, d = x.shape
    tm = min(tm, t)
    n_tiles = t // tm
    per_batch = mods.shape[0] > 1
    full = lambda shape: pl.BlockSpec(shape, lambda bi, i: (0,) * len(shape))
    in_specs = [pl.BlockSpec((None, tm, d), lambda bi, i: (bi, i, 0))]
    args = [x]
    if on_grid:
        r = tm // GRID_W
        last = t // GRID_W - 1
        in_specs += [pl.BlockSpec((None, GRID_W, d), lambda bi, i: (bi, jnp.maximum(i * r - 1, 0), 0)),
                     pl.BlockSpec((None, GRID_W, d), lambda bi, i: (bi, jnp.minimum((i + 1) * r, last), 0))]
        args += [x, x]
    else:
        assert n_tiles == 1
    in_specs += [full((1, d)),
                 pl.BlockSpec((None, N_MOD, d), (lambda bi, i: (bi, 0, 0)) if per_batch
                              else (lambda bi, i: (0, 0, 0))),
                 full((d, 2 * D_FF)), full((9, D_FF)), full((1, D_FF)), full((D_FF, d))]
    args += [norm_g, mods, w_up, conv_w9, conv_b, w_down]
    if final_g is not None:
        in_specs.append(full((1, d)))
        args.append(final_g)
    return pl.pallas_call(
        functools.partial(_ffn_kernel, on_grid=on_grid, final_norm=final_g is not None),
        name="conv_ffn",
        grid=(b, n_tiles),
        in_specs=in_specs,
        out_specs=pl.BlockSpec((None, tm, d), lambda bi, i: (bi, i, 0)),
        out_shape=jax.ShapeDtypeStruct((b, t, d), F32),
        scratch_shapes=[pltpu.VMEM((tm, FFN_DOWN_GROUP * FFN_COL_CHUNK), BF16)
                        for _ in range(-(-D_FF // (FFN_DOWN_GROUP * FFN_COL_CHUNK)))],
        compiler_params=_cparams(("parallel", "parallel")),
    )(*args)


def _block_diag2(a, b):
    za = jnp.zeros((a.shape[0], b.shape[1]), a.dtype)
    zb = jnp.zeros((b.shape[0], a.shape[1]), a.dtype)
    return jnp.concatenate([jnp.concatenate([a, za], axis=1), jnp.concatenate([zb, b], axis=1)], axis=0)


def _pad_cols(a, n):
    return jnp.pad(a, ((0, 0), (0, n - a.shape[1])))


def _even_params(j, ev_w_in, ev_mu_prev, ev_mu_next, rk_w0_f, rk_w0_b, rk_w2_f, rk_w2_b, rk_a0_f,
                 rk_a0_b, rk_a2_f, rk_a2_b, rk_g2, rk_k_k, rk_k_a, rk_r_k, rk_ln_w, rk_ln_b,
                 ssd_conv_w, ssd_conv_b, ssd_dt_bias_f, ssd_dt_bias_b, ssd_a_log_f, ssd_a_log_b,
                 ssd_d, ssd_norm_w, ev_w_out):
    w_in = ev_w_in[j]
    rw = w_in[:, :RWKV_COLS]
    z = w_in[:, RWKV_COLS:RWKV_COLS + SSD_WIDTH]
    xbc = w_in[:, RWKV_COLS + SSD_WIDTH:RWKV_COLS + SSD_WIDTH + SSD_XBC]
    dts = w_in[:, RWKV_COLS + SSD_WIDTH + SSD_XBC:]
    w_packed = jnp.concatenate([_pad_cols(jnp.concatenate([rw, dts], axis=1), EV_RW_BLOCK), xbc, z], axis=1)
    head = jnp.arange(RWKV_WIDTH) // RWKV_HEAD_DIM
    head_sum = (head[:, None] == head[None, :]).astype(BF16)
    lane = jnp.arange(V7X_LANES)[:, None]
    tgt = jnp.arange(2 * SSD_WIDTH)[None, :]
    dt_expand = (lane == (tgt // SSD_WIDTH) * SSD_HEADS + (tgt % SSD_WIDTH) // SSD_HEAD_DIM).astype(BF16)
    rep = lambda a: jnp.repeat(a, SSD_HEAD_DIM)[None, :]
    row = lambda a: a[None, :]
    return {
        'w_in': w_packed.astype(BF16),
        'mu_prev': _pad_cols(row(ev_mu_prev[j]), EV_RW_BLOCK),
        'mu_next': _pad_cols(row(ev_mu_next[j]), EV_RW_BLOCK),
        'w0': row(jnp.concatenate([rk_w0_f[j], rk_w0_b[j]])),
        'w2': _block_diag2(rk_w2_f[j], rk_w2_b[j]).astype(BF16),
        'a0': row(jnp.concatenate([rk_a0_f[j], rk_a0_b[j]])),
        'a2': _block_diag2(rk_a2_f[j], rk_a2_b[j]).astype(BF16),
        'g2': rk_g2[j].astype(BF16),
        'k_k': row(rk_k_k[j]), 'k_a': row(rk_k_a[j]), 'r_k': row(rk_r_k[j].reshape(-1)),
        'ln_w': row(rk_ln_w[j]), 'ln_b': row(rk_ln_b[j]),
        'head_sum': head_sum,
        'conv_w': ssd_conv_w[j], 'conv_b': row(ssd_conv_b[j]),
        'dt_bias': jnp.concatenate([rep(ssd_dt_bias_f[j]), rep(ssd_dt_bias_b[j])], axis=1),
        'dt_expand': dt_expand,
        'a_rep': jnp.stack([rep(-jnp.exp(ssd_a_log_f[j])), rep(-jnp.exp(ssd_a_log_b[j]))]),
        'd_skip': rep(ssd_d[j]),
        'norm_w': row(ssd_norm_w[j]),
        'w_out': ev_w_out[j].astype(BF16),
    }


def _rope_tables(t):
    n = RET_QK_DIM // 4
    pos = jnp.arange(t)
    row = (pos // GRID_W).astype(F32)
    col = (pos % GRID_W).astype(F32)
    inv = ROPE_BASE ** (-jnp.arange(n, dtype=F32) / n)
    ang = jnp.concatenate([row[:, None] * inv, col[:, None] * inv], axis=-1)
    cos, sin = jnp.cos(ang), jnp.sin(ang)
    return jnp.concatenate([cos, cos], axis=-1), jnp.concatenate([-sin, sin], axis=-1)


def _conv_ffn(x, mods, norm_g, w_up, conv_w9, conv_b, w_down, *, on_grid, final_g=None):
    return _ffn_call(x, norm_g, mods, w_up, conv_w9, conv_b, w_down, tm=512, on_grid=on_grid,
                     final_g=final_g)


def _even_layer(x, ctx, mods_x, mods_c, norm_g, p):
    b = x.shape[0]

    def features(h, mods):
        proj = _nm_call(h, norm_g, mods, p['w_in'], shift_row=0, tm=512, tn=512)
        feat = _rwkv_feat_call(proj, p, 256)
        xbc, dtbc = _ssd_feat_call(proj, p, 256)
        return proj, feat, xbc, dtbc

    proj_c, feat_c, xbc_c, dt_c = features(ctx, mods_c)
    proj_x, feat_x, xbc_x, dt_x = features(x, mods_x)
    s0 = jnp.zeros((2, b, N_PAIRS, PAIR, PAIR), F32)
    *yrk_c, s_ctx = _rwkv_scan_call(feat_c, s0, 256)
    *yrk_x, _ = _rwkv_scan_call(feat_x, s_ctx, 512)
    h0 = jnp.zeros((2, b, N_PAIRS, SSD_STATE, PAIR), F32)
    ysd_c, h_ctx = _ssd_scan_call(xbc_c, dt_c, p['a_rep'], h0, 256)
    ysd_x, _ = _ssd_scan_call(xbc_x, dt_x, p['a_rep'], h_ctx, 512)
    x = _even_finish_call(yrk_x, feat_x[3], feat_x[4], ysd_x, xbc_x, proj_x, x, mods_x, p, 256)
    ctx = _even_finish_call(yrk_c, feat_c[3], feat_c[4], ysd_c, xbc_c, proj_c, ctx, mods_c, p, 256)
    return x, ctx


def _odd_layer(x, ctx, mods_x, mods_c, norm_g, w_in, lg_rep, w_out):
    b, t, _ = x.shape
    proj_c = _nm_call(ctx, norm_g, mods_c, w_in, shift_row=0, tm=256, tn=512, out_dtype=BF16,
                      qk_mode='scale')
    proj_x = _nm_call(x, norm_g, mods_x, w_in, shift_row=0, tm=512, tn=512, out_dtype=BF16,
                      qk_mode='rope', rope=_rope_tables(t))
    s0 = jnp.zeros((2, b, RET_HEADS, RET_QK_DIM, RET_V_DIM), F32)
    _, s_ctx = _ret_scan_call(proj_c, lg_rep, s0, 256)
    y, _ = _ret_scan_call(proj_x, lg_rep, s_ctx, 512)
    return _odd_finish_call(y, proj_x, x, mods_x, w_out, 256)


def kernel(x, c, ctx, c_ctx, mod_w, mod_b, norm1_g, norm2_g, ffn_w_up, ffn_conv_w, ffn_conv_b, ffn_w_down, ev_w_in, ev_mu_prev, ev_mu_next, rk_w0_f, rk_w0_b, rk_w2_f, rk_w2_b, rk_a0_f, rk_a0_b, rk_a2_f, rk_a2_b, rk_g2, rk_k_k, rk_k_a, rk_r_k, rk_ln_w, rk_ln_b, ssd_conv_w, ssd_conv_b, ssd_dt_bias_f, ssd_dt_bias_b, ssd_a_log_f, ssd_a_log_b, ssd_d, ssd_norm_w, ev_w_out, ret_w_in, ret_log2_f, ret_log2_b, ret_w_out, final_norm_g):
    b, t, d = x.shape
    depth = mod_w.shape[0]
    rows = -(-(b + 1) // V7X_SUBLANES) * V7X_SUBLANES
    cond = jnp.concatenate([c, c_ctx[None, :], jnp.zeros((rows - b - 1, d), F32)], axis=0)
    mods = _mod_call(cond, mod_w, mod_b).reshape(depth, rows, N_MOD, d)
    for i in range(depth):
        need_ctx = i < depth - 1
        mods_x = mods[i, :b]
        mods_c = mods[i, b:b + 1]
        j = i // 2
        g1 = norm1_g[i][None, :]
        if i % 2 == 0:
            p = _even_params(j, ev_w_in, ev_mu_prev, ev_mu_next, rk_w0_f, rk_w0_b, rk_w2_f, rk_w2_b,
                             rk_a0_f, rk_a0_b, rk_a2_f, rk_a2_b, rk_g2, rk_k_k, rk_k_a, rk_r_k,
                             rk_ln_w, rk_ln_b, ssd_conv_w, ssd_conv_b, ssd_dt_bias_f, ssd_dt_bias_b,
                             ssd_a_log_f, ssd_a_log_b, ssd_d, ssd_norm_w, ev_w_out)
            x, ctx_mixed = _even_layer(x, ctx, mods_x, mods_c, g1, p)
        else:
            lg = jnp.stack([jnp.log1p(-jnp.exp2(-ret_log2_f[j])), jnp.log1p(-jnp.exp2(-ret_log2_b[j]))])
            lg_rep = jnp.repeat(lg, RET_QK_DIM, axis=-1)[:, None, :]
            x = _odd_layer(x, ctx, mods_x, mods_c, g1, ret_w_in[j].astype(BF16), lg_rep,
                           ret_w_out[j].astype(BF16))
            ctx_mixed = None
        g2 = norm2_g[i][None, :]
        w_up = ffn_w_up[i].astype(BF16)
        w_down = ffn_w_down[i].astype(BF16)
        conv_w9 = ffn_conv_w[i].reshape(9, D_FF)
        conv_b = ffn_conv_b[i][None, :]
        last = i == depth - 1
        x = _conv_ffn(x, mods_x, g2, w_up, conv_w9, conv_b, w_down, on_grid=True,
                      final_g=final_norm_g[None, :] if last else None)
        if need_ctx:
            ctx = _conv_ffn(ctx_mixed, mods_c, g2, w_up, conv_w9, conv_b, w_down, on_grid=False)
    return x
```

```python
import functools
import math

import jax
import jax.numpy as jnp
from jax import lax
from jax.experimental import pallas as pl
from jax.experimental.pallas import tpu as pltpu

F32 = jnp.float32
BF16 = jnp.bfloat16

D_MODEL = 1024
GRID_W = 64
N_MOD = 6
NORM_EPS = 1e-6
RWKV_HEADS = 8
RWKV_HEAD_DIM = 64
RWKV_WIDTH = RWKV_HEADS * RWKV_HEAD_DIM
DECAY_LORA = 64
ICLR_LORA = 64
GATE_LORA = 128
RWKV_GN_EPS = 64e-5
RWKV_COLS = 3 * RWKV_WIDTH + 2 * DECAY_LORA + 2 * ICLR_LORA + GATE_LORA
SSD_HEADS = 8
SSD_HEAD_DIM = 64
SSD_WIDTH = SSD_HEADS * SSD_HEAD_DIM
SSD_GROUPS = 2
SSD_STATE = 128
SSD_XBC = SSD_WIDTH + 2 * SSD_GROUPS * SSD_STATE
RET_HEADS = 8
RET_QK_DIM = 128
RET_V_DIM = 256
RET_QK = RET_HEADS * RET_QK_DIM
RET_V = RET_HEADS * RET_V_DIM
ROPE_BASE = 10000.0
D_FF = 2816

V7X_LANES = 128
V7X_SUBLANES = 8
V7X_MXU_DIM = 256
V7X_VMEM_LIMIT_BYTES = 56 * 1024 * 1024

RWKV_CHUNK = 64
SCAN_CHUNK = 128
FFN_COL_CHUNK = 256
FFN_DOWN_GROUP = 4
PAIR = 2 * RWKV_HEAD_DIM
N_PAIRS = RWKV_HEADS // 2
EV_RW_BLOCK = 2048
EV_DT_OFF = RWKV_COLS
EV_XBC_OFF = EV_RW_BLOCK
EV_Z_OFF = EV_RW_BLOCK + SSD_XBC
EV_COLS = EV_Z_OFF + SSD_WIDTH


def _cparams(sem):
    return pltpu.CompilerParams(dimension_semantics=sem, vmem_limit_bytes=V7X_VMEM_LIMIT_BYTES)


def _split3(x):
    hi = x.astype(BF16)
    r1 = x - hi.astype(F32)
    mid = r1.astype(BF16)
    lo = (r1 - mid.astype(F32)).astype(BF16)
    return hi, mid, lo


def _dot(a, b):
    return jnp.dot(a, b, preferred_element_type=F32)


def _dot_nt(a, b):
    return lax.dot_general(a, b, (((1,), (1,)), ((), ())), preferred_element_type=F32)


def _dot_tn(a, b):
    return lax.dot_general(a, b, (((0,), (0,)), ((), ())), preferred_element_type=F32)


def _dot01(x, m01):
    hi, mid, lo = _split3(x)
    return _dot(hi, m01) + _dot(mid, m01) + _dot(lo, m01)


def _head_sum(x, j01):
    hi = x.astype(BF16)
    lo = (x - hi.astype(F32)).astype(BF16)
    n = j01.shape[0]
    return jnp.concatenate([_dot(hi[:, g * n:(g + 1) * n], j01) + _dot(lo[:, g * n:(g + 1) * n], j01)
                            for g in range(x.shape[1] // n)], axis=1)


def _dot01_left(m01, x):
    hi, mid, lo = _split3(x)
    return _dot(m01, hi) + _dot(m01, mid) + _dot(m01, lo)


def _sigmoid(x):
    return 1.0 / (1.0 + jnp.exp(-x))


def _silu(x):
    return x * _sigmoid(x)


def _softplus(x):
    return jnp.maximum(x, 0.0) + jnp.log1p(jnp.exp(-jnp.abs(x)))


def _gelu_tanh(x):
    c = math.sqrt(2.0 / math.pi)
    half = 0.5 * x
    return half + half * jnp.tanh(x * (c + (0.044715 * c) * (x * x)))


def _order_masks(d, n):
    row = lax.broadcasted_iota(jnp.int32, (n, n), 0)
    col = lax.broadcasted_iota(jnp.int32, (n, n), 1)
    diff = (row - col) * (1 - 2 * d)
    return diff > 0, diff >= 0


def _mod_kernel(c_ref, w_ref, b_ref, o_ref):
    h = _silu(c_ref[...])
    hi, mid, lo = _split3(h)
    w = w_ref[...]
    wh = w.astype(BF16)
    wl = (w - wh.astype(F32)).astype(BF16)
    acc = _dot(hi, wh) + _dot(mid, wh) + _dot(hi, wl)
    o_ref[...] = acc + b_ref[...]


def _mod_call(cond, mod_w, mod_b):
    depth, d, n = mod_w.shape
    rows = cond.shape[0]
    tn = 1024
    return pl.pallas_call(
        _mod_kernel,
        name="adaln_mod",
        grid=(depth, n // tn),
        in_specs=[pl.BlockSpec((rows, d), lambda l, j: (0, 0)),
                  pl.BlockSpec((None, d, tn), lambda l, j: (l, 0, j)),
                  pl.BlockSpec((None, 1, tn), lambda l, j: (l, 0, j))],
        out_specs=pl.BlockSpec((None, rows, tn), lambda l, j: (l, 0, j)),
        out_shape=jax.ShapeDtypeStruct((depth, rows, n), F32),
        compiler_params=_cparams(("parallel", "parallel")),
    )(cond, mod_w, mod_b.reshape(depth, 1, n))


def _nm_kernel(*refs, shift_row, tn, qk_mode):
    if qk_mode == 'rope':
        x_ref, g_ref, m_ref, w_ref, cos_ref, sin_ref, o_ref = refs
    else:
        x_ref, g_ref, m_ref, w_ref, o_ref = refs
    x = x_ref[...]
    h = x * lax.rsqrt(jnp.mean(x * x, axis=-1, keepdims=True) + NORM_EPS) * g_ref[...]
    h = h * (1.0 + m_ref[shift_row + 1:shift_row + 2, :]) + m_ref[shift_row:shift_row + 1, :]
    h = h.astype(BF16)
    dk = RET_QK_DIM
    for j in range(w_ref.shape[1] // tn):
        cols = slice(j * tn, (j + 1) * tn)
        y = _dot(h, w_ref[:, cols])
        if qk_mode is not None and j * tn < 2 * RET_QK:
            scale = dk ** -0.5 if j * tn >= RET_QK else 1.0
            heads = []
            for hh in range(tn // dk):
                yh = y[:, hh * dk:(hh + 1) * dk]
                if qk_mode == 'rope':
                    yh = yh * cos_ref[...] + pltpu.roll(yh, dk // 2, axis=1) * sin_ref[...]
                heads.append(yh * scale if scale != 1.0 else yh)
            y = jnp.concatenate(heads, axis=1)
        o_ref[:, cols] = y.astype(o_ref.dtype)


def _nm_call(x, g, mods, w, *, shift_row, tm, tn, out_dtype=F32, qk_mode=None, rope=None):
    b, t, d = x.shape
    n = w.shape[1]
    tm = min(tm, t)
    per_batch = mods.shape[0] > 1
    in_specs = [pl.BlockSpec((None, tm, d), lambda bi, i: (bi, i, 0)),
                pl.BlockSpec((1, d), lambda bi, i: (0, 0)),
                pl.BlockSpec((None, N_MOD, d), (lambda bi, i: (bi, 0, 0)) if per_batch
                             else (lambda bi, i: (0, 0, 0))),
                pl.BlockSpec((d, n), lambda bi, i: (0, 0))]
    args = [x, g, mods, w]
    if qk_mode == 'rope':
        tab = pl.BlockSpec((tm, RET_QK_DIM), lambda bi, i: (i, 0))
        in_specs += [tab, tab]
        args += list(rope)
    return pl.pallas_call(
        functools.partial(_nm_kernel, shift_row=shift_row, tn=tn, qk_mode=qk_mode),
        name="norm_mod_matmul",
        grid=(b, t // tm),
        in_specs=in_specs,
        out_specs=pl.BlockSpec((None, tm, n), lambda bi, i: (bi, i, 0)),
        out_shape=jax.ShapeDtypeStruct((b, t, n), out_dtype),
        compiler_params=_cparams(("parallel", "parallel")),
    )(*args)


def _even_feat_kernel(x_ref, xp_ref, xn_ref, g_ref, m_ref, w_ref,
                      mus_ref, mup_ref, mun_ref, w0_ref, w2_ref, a0_ref, a2_ref,
                      g2_ref, kk_ref, ka_ref, rk_ref, j_ref, cw_ref, cb_ref, dtb_ref, e_ref,
                      r_ref, v_ref, kkn_ref, bonus_ref, gate_ref, ld_ref, kd_ref, bd_ref,
                      xbc_ref, dtbc_ref, z_ref):
    i = pl.program_id(1)
    n_tiles = pl.num_programs(1)
    tm = x_ref.shape[0]
    halo = V7X_SUBLANES
    ext = tm + 2 * halo

    def norm_mod(xv):
        hv = xv * lax.rsqrt(jnp.mean(xv * xv, axis=-1, keepdims=True) + NORM_EPS) * g_ref[...]
        return (hv * (1.0 + m_ref[1:2, :]) + m_ref[0:1, :]).astype(BF16)

    h = norm_mod(x_ref[...])
    zero = jnp.zeros((halo, x_ref.shape[1]), BF16)
    h_ext = jnp.concatenate([jnp.where(i > 0, norm_mod(xp_ref[...]), zero), h,
                             jnp.where(i < n_tiles - 1, norm_mod(xn_ref[...]), zero)], axis=0)

    def proj3(cols):
        ye = _dot(h_ext, w_ref[:, cols])
        return (ye[halo:halo + tm], pltpu.roll(ye, 1, axis=0)[halo:halo + tm],
                pltpu.roll(ye, ext - 1, axis=0)[halo:halo + tm])

    w = RWKV_WIDTH
    rw = []
    for j in range(EV_RW_BLOCK // w):
        cols = slice(j * w, (j + 1) * w)
        cur, prev, nxt = proj3(cols)
        rw.append(cur * mus_ref[:, cols] + prev * mup_ref[:, cols] + nxt * mun_ref[:, cols])
    r, k, v, lora = rw
    wd = lora[:, 0:2 * DECAY_LORA]
    ad = lora[:, 2 * DECAY_LORA:2 * DECAY_LORA + 2 * ICLR_LORA]
    gd = lora[:, 2 * DECAY_LORA + 2 * ICLR_LORA:2 * DECAY_LORA + 2 * ICLR_LORA + GATE_LORA]
    dt_raw = lora[:, EV_DT_OFF - 3 * w:EV_RW_BLOCK - 3 * w]
    jm = j_ref[...]

    for j in range(SSD_XBC // w):
        cols = slice(j * w, (j + 1) * w)
        cur, prev, nxt = proj3(slice(EV_XBC_OFF + j * w, EV_XBC_OFF + (j + 1) * w))
        y = prev * cw_ref[0:1, cols] + cur * cw_ref[1:2, cols] + nxt * cw_ref[2:3, cols] + cb_ref[:, cols]
        xbc_ref[:, cols] = _silu(y).astype(xbc_ref.dtype)
    z_ref[...] = _dot(h, w_ref[:, EV_Z_OFF:EV_Z_OFF + SSD_WIDTH]).astype(z_ref.dtype)
    dt = _softplus(_dot01(dt_raw, e_ref[...]) + dtb_ref[...])
    dtbc_ref[0] = dt[:, 0:SSD_WIDTH]
    dtbc_ref[1] = dt[:, SSD_WIDTH:2 * SSD_WIDTH]

    kk = k * kk_ref[...]
    ss = _head_sum(kk * kk, jm)
    kk = kk / jnp.maximum(jnp.sqrt(ss), 1e-12)
    r_ref[...] = r.astype(r_ref.dtype)
    v_ref[...] = v.astype(v_ref.dtype)
    kkn_ref[...] = kk.astype(kkn_ref.dtype)
    bonus_ref[...] = (_head_sum(r * k * rk_ref[...], jm) * v).astype(bonus_ref.dtype)
    gate_ref[...] = _dot(_sigmoid(gd).astype(BF16), g2_ref[...]).astype(gate_ref.dtype)

    zw = _dot(jnp.tanh(wd).astype(BF16), w2_ref[...]) + w0_ref[...]
    za = _dot(ad.astype(BF16), a2_ref[...]) + a0_ref[...]
    for di in range(2):
        ld_ref[di] = -math.exp(-0.5) * _sigmoid(zw[:, di * w:(di + 1) * w])
        iclr = _sigmoid(za[:, di * w:(di + 1) * w])
        kd_ref[di] = (k * (1.0 + (iclr - 1.0) * ka_ref[...])).astype(kd_ref.dtype)
        bd_ref[di] = (kk * iclr).astype(bd_ref.dtype)


def _even_feat_call(x, norm_g, mods, p, tm):
    b, t, d = x.shape
    tm = min(tm, t)
    n_tiles = t // tm
    w = RWKV_WIDTH
    per_batch = mods.shape[0] > 1
    r8 = tm // V7X_SUBLANES
    last = t // V7X_SUBLANES - 1
    full = lambda shape: pl.BlockSpec(shape, lambda bi, i: (0,) * len(shape))
    tok = pl.BlockSpec((None, tm, w), lambda bi, i: (bi, i, 0))
    tok2 = pl.BlockSpec((2, None, tm, w), lambda bi, i: (0, bi, i, 0))
    sd = jax.ShapeDtypeStruct((b, t, w), BF16)
    sd2 = jax.ShapeDtypeStruct((2, b, t, w), BF16)
    f2 = jax.ShapeDtypeStruct((2, b, t, w), F32)
    return pl.pallas_call(
        _even_feat_kernel,
        name="even_feat",
        grid=(b, n_tiles),
        in_specs=[pl.BlockSpec((None, tm, d), lambda bi, i: (bi, i, 0)),
                  pl.BlockSpec((None, V7X_SUBLANES, d), lambda bi, i: (bi, jnp.maximum(i * r8 - 1, 0), 0)),
                  pl.BlockSpec((None, V7X_SUBLANES, d), lambda bi, i: (bi, jnp.minimum((i + 1) * r8, last), 0)),
                  full((1, d)),
                  pl.BlockSpec((None, N_MOD, d), (lambda bi, i: (bi, 0, 0)) if per_batch
                               else (lambda bi, i: (0, 0, 0))),
                  full((d, EV_COLS)),
                  full((1, EV_RW_BLOCK)), full((1, EV_RW_BLOCK)), full((1, EV_RW_BLOCK)),
                  full((1, 2 * w)), full((2 * DECAY_LORA, 2 * w)),
                  full((1, 2 * w)), full((2 * ICLR_LORA, 2 * w)),
                  full((GATE_LORA, w)), full((1, w)), full((1, w)), full((1, w)),
                  full((V7X_MXU_DIM, V7X_MXU_DIM)),
                  full((3, SSD_XBC)), full((1, SSD_XBC)), full((1, 2 * SSD_WIDTH)),
                  full((V7X_LANES, 2 * SSD_WIDTH))],
        out_specs=[tok, tok, tok, tok, tok, tok2, tok2, tok2,
                   pl.BlockSpec((None, tm, SSD_XBC), lambda bi, i: (bi, i, 0)), tok2, tok],
        out_shape=[sd, sd, sd, sd, sd, f2, sd2, sd2,
                   jax.ShapeDtypeStruct((b, t, SSD_XBC), BF16), f2, sd],
        compiler_params=_cparams(("parallel", "parallel")),
    )(x, x, x, norm_g, mods, p['w_in'], p['mu_self'], p['mu_prev'], p['mu_next'], p['w0'], p['w2'],
      p['a0'], p['a2'], p['g2'], p['k_k'], p['k_a'], p['r_k'], p['head_sum'],
      p['conv_w'], p['conv_b'], p['dt_bias'], p['dt_expand'])


def _rwkv_scan_kernel(rf_ref, vf_ref, kkf_ref, ldf_ref, kdf_ref, bdf_ref,
                      rb_ref, vb_ref, kkb_ref, ldb_ref, kdb_ref, bdb_ref, s0_ref,
                      yf_ref, yb_ref, sf_ref, st_scr, *, n_chunks):
    i = pl.program_id(1)
    c = RWKV_CHUNK

    @pl.when(i == 0)
    def _():
        st_scr[...] = s0_ref[...]

    row2 = lax.broadcasted_iota(jnp.int32, (2 * c, 2 * c), 0)
    col2 = lax.broadcasted_iota(jnp.int32, (2 * c, 2 * c), 1)
    same_head = (row2 // c) == (col2 // c)
    eye_bd = (row2 == col2).astype(F32)
    rowc = lax.broadcasted_iota(jnp.int32, (c, 2 * c), 0)
    colc = lax.broadcasted_iota(jnp.int32, (c, 2 * c), 1) % c
    m0 = lax.broadcasted_iota(jnp.int32, (c, PAIR), 1) < RWKV_HEAD_DIM
    row1 = lax.broadcasted_iota(jnp.int32, (c, 1), 0)
    incl01, strict_bd, incl_wide, rsel = [], [], [], []
    for d in range(2):
        sgn = 1 - 2 * d
        incl01.append(_order_masks(d, c)[1].astype(BF16))
        strict_bd.append(jnp.logical_and(same_head, (row2 % c - col2 % c) * sgn > 0))
        incl_wide.append((rowc - colc) * sgn >= 0)
        rsel.append(row1 == (c - 1 if d == 0 else 0))
    refs = ((rf_ref, vf_ref, kkf_ref, ldf_ref, kdf_ref, bdf_ref, yf_ref),
            (rb_ref, vb_ref, kkb_ref, ldb_ref, kdb_ref, bdb_ref, yb_ref))
    chains = [(d, p) for d in range(2) for p in range(N_PAIRS)]
    lanes = [slice(p * PAIR, (p + 1) * PAIR) for p in range(N_PAIRS)]

    def stack(x):
        return jnp.concatenate([jnp.where(m0, x, 0.0), jnp.where(m0, 0.0, x)], axis=0)

    def chunk_body(j, carry):
        rows = (pl.ds(pl.multiple_of(j * c, c), c),
                pl.ds(pl.multiple_of((n_chunks - 1 - j) * c, c), c))
        cs_all = [_dot01_left(incl01[d], refs[d][3][rows[d], :]) for d in range(2)]
        a_s, r_t, b_s, k_s, v_s, b_end, k_end, g_tot = [], [], [], [], [], [], [], []
        for d, p in chains:
            r_ref, v_ref, kk_ref, ld_ref, kd_ref, bd_ref, _ = refs[d]
            rw, ln = rows[d], lanes[p]
            cs = cs_all[d][:, ln]
            cs_last = jnp.sum(jnp.where(rsel[d], cs, 0.0), axis=0, keepdims=True)
            g_neg = jnp.exp(-cs)
            g_end = jnp.exp(cs_last - cs)
            g_tot.append(jnp.exp(cs_last))
            kdv = kd_ref[rw, ln].astype(F32)
            bdv = bd_ref[rw, ln].astype(F32)
            a_s.append(stack(-kk_ref[rw, ln].astype(F32) * jnp.exp(cs - ld_ref[rw, ln])).astype(BF16))
            r_t.append((r_ref[rw, ln].astype(F32) * jnp.exp(cs)).astype(BF16))
            b_s.append(stack(bdv * g_neg).astype(BF16))
            k_s.append(stack(kdv * g_neg).astype(BF16))
            v_s.append(stack(v_ref[rw, ln].astype(F32)).astype(BF16))
            b_end.append(stack(bdv * g_end).astype(BF16))
            k_end.append(stack(kdv * g_end).astype(BF16))
        n = range(len(chains))
        dirs = [d for d, _ in chains]
        gram = [_dot_nt(jnp.concatenate([a_s[q], r_t[q]], axis=0),
                        jnp.concatenate([b_s[q], k_s[q]], axis=0)) for q in n]
        a_ab = [jnp.where(strict_bd[dirs[q]], gram[q][0:2 * c, 0:2 * c], 0.0) for q in n]
        a_ak = [jnp.where(strict_bd[dirs[q]], gram[q][0:2 * c, 2 * c:4 * c], 0.0).astype(BF16) for q in n]
        p_rb = [jnp.where(incl_wide[dirs[q]], gram[q][2 * c:3 * c, 0:2 * c], 0.0).astype(BF16) for q in n]
        p_rk = [jnp.where(incl_wide[dirs[q]], gram[q][2 * c:3 * c, 2 * c:4 * c], 0.0).astype(BF16) for q in n]
        minv = [eye_bd + a_ab[q] for q in n]
        pwb = [a_ab[q].astype(BF16) for q in n]
        pwb = [_dot(pwb[q], pwb[q]).astype(BF16) for q in n]
        levels = int(math.log2(c))
        for k in range(2, levels):
            both = [_dot(pwb[q], jnp.concatenate([pwb[q], minv[q].astype(BF16)], axis=1)) for q in n]
            minv = [minv[q] + both[q][:, 2 * c:4 * c] for q in n]
            pwb = [both[q][:, 0:2 * c].astype(BF16) for q in n]
        minv = [minv[q] + _dot(pwb[q], minv[q].astype(BF16)) for q in n]
        akv = [_dot(a_ak[q], v_s[q]).astype(BF16) for q in n]
        eff = [_dot(minv[q].astype(BF16), jnp.concatenate([a_s[q], akv[q]], axis=1)) for q in n]
        bk_end = [jnp.concatenate([b_end[q], k_end[q]], axis=0) for q in n]
        p_both = [jnp.concatenate([p_rb[q], p_rk[q]], axis=1) for q in n]
        st = [st_scr[d, p] for d, p in chains]
        fs = [_dot_nt(jnp.concatenate([eff[q][:, 0:PAIR].astype(BF16), r_t[q]], axis=0),
                      st[q].astype(BF16)) for q in n]
        uv = [jnp.concatenate([(fs[q][0:2 * c] + eff[q][:, PAIR:2 * PAIR]).astype(BF16), v_s[q]], axis=0)
              for q in n]
        for q, (d, p) in enumerate(chains):
            st_scr[d, p] = st[q] * g_tot[q] + _dot_tn(uv[q], bk_end[q])
        for q, (d, p) in enumerate(chains):
            refs[d][6][rows[d], lanes[p]] = fs[q][2 * c:3 * c] + _dot(p_both[q], uv[q])
        return carry

    lax.fori_loop(0, n_chunks, chunk_body, 0, unroll=2)

    @pl.when(i == pl.num_programs(1) - 1)
    def _():
        sf_ref[...] = st_scr[...]


def _rwkv_scan_call(feat, s0, tb):
    r, v, kk, _, _, ld, kd, bd = feat
    b, t, w = r.shape
    tb = min(tb, t)
    nb = t // tb
    tok_f = pl.BlockSpec((None, tb, w), lambda bi, i: (bi, i, 0))
    tok_b = pl.BlockSpec((None, tb, w), lambda bi, i: (bi, nb - 1 - i, 0))
    dir_f = pl.BlockSpec((None, None, tb, w), lambda bi, i: (0, bi, i, 0))
    dir_b = pl.BlockSpec((None, None, tb, w), lambda bi, i: (1, bi, nb - 1 - i, 0))
    st = pl.BlockSpec((2, None, N_PAIRS, PAIR, PAIR), lambda bi, i: (0, bi, 0, 0, 0))
    y_sd = jax.ShapeDtypeStruct((b, t, w), F32)
    return pl.pallas_call(
        functools.partial(_rwkv_scan_kernel, n_chunks=tb // RWKV_CHUNK),
        name="rwkv_scan",
        grid=(b, nb),
        in_specs=[tok_f, tok_f, tok_f, dir_f, dir_f, dir_f, tok_b, tok_b, tok_b, dir_b, dir_b, dir_b, st],
        out_specs=[tok_f, tok_b, st],
        out_shape=[y_sd, y_sd, jax.ShapeDtypeStruct((2, b, N_PAIRS, PAIR, PAIR), F32)],
        scratch_shapes=[pltpu.VMEM((2, N_PAIRS, PAIR, PAIR), F32)],
        compiler_params=_cparams(("parallel", "arbitrary")),
    )(r, v, kk, ld, kd, bd, r, v, kk, ld, kd, bd, s0)


def _ssd_scan_kernel(xbc_ref, dt_ref, a_ref, s0_ref, y_ref, sf_ref, st_scr, *, n_chunks):
    d = pl.program_id(0)
    i = pl.program_id(2)
    c = SCAN_CHUNK
    hd = SSD_HEAD_DIM

    @pl.when(i == 0)
    def _():
        st_scr[...] = s0_ref[...]

    _, before_eq = _order_masks(d, c)
    incl01 = before_eq.astype(BF16)
    last_row = jnp.where(d == 0, c - 1, 0)
    rsel = lax.broadcasted_iota(jnp.int32, (c, 1), 0) == last_row
    lane = lax.broadcasted_iota(jnp.int32, (c, PAIR), 1)
    m0 = lane < hd
    a_row = a_ref[...]

    def chunk_body(j, carry):
        cj = jnp.where(d == 0, j, n_chunks - 1 - j)
        rows = pl.ds(pl.multiple_of(cj * c, c), c)
        dt = dt_ref[rows, :]
        cs_all = _dot01_left(incl01, dt * a_row)
        pairs = range(N_PAIRS)
        group = [p // (N_PAIRS // SSD_GROUPS) for p in pairs]
        lanes = [slice(p * PAIR, (p + 1) * PAIR) for p in pairs]
        bm = [xbc_ref[rows, SSD_WIDTH + g * SSD_STATE:SSD_WIDTH + (g + 1) * SSD_STATE]
              for g in range(SSD_GROUPS)]
        cm = [xbc_ref[rows, SSD_WIDTH + (SSD_GROUPS + g) * SSD_STATE:
                      SSD_WIDTH + (SSD_GROUPS + g + 1) * SSD_STATE] for g in range(SSD_GROUPS)]
        cb = [_dot_nt(cm[g], bm[g]) for g in range(SSD_GROUPS)]
        st = [st_scr[p] for p in pairs]
        y_st = [_dot(cm[group[p]], st[p].astype(BF16)) for p in pairs]
        cs = [cs_all[:, lanes[p]] for p in pairs]
        xdt = [xbc_ref[rows, lanes[p]].astype(F32) * dt[:, lanes[p]] for p in pairs]
        probs = []
        for p in pairs:
            cs_t = cs[p].T
            both = []
            for hh in range(2):
                col = cs[p][:, hh * hd:hh * hd + 1]
                rowv = cs_t[hh * hd:hh * hd + 1, :]
                dec = jnp.exp(jnp.where(before_eq, col - rowv, -jnp.inf))
                both.append((cb[group[p]] * dec).astype(BF16))
            probs.append(jnp.concatenate(both, axis=1))
        xs2 = [jnp.concatenate([jnp.where(m0, xdt[p], 0.0), jnp.where(m0, 0.0, xdt[p])],
                               axis=0).astype(BF16) for p in pairs]
        y_in = [_dot(probs[p], xs2[p]) for p in pairs]
        for p in pairs:
            y_ref[rows, lanes[p]] = y_in[p] + jnp.exp(cs[p]) * y_st[p]
        for p in pairs:
            cs_last = jnp.sum(jnp.where(rsel, cs[p], 0.0), axis=0, keepdims=True)
            xe = (xdt[p] * jnp.exp(cs_last - cs[p])).astype(BF16)
            st_scr[p] = st[p] * jnp.exp(cs_last) + _dot_tn(bm[group[p]], xe)
        return carry

    lax.fori_loop(0, n_chunks, chunk_body, 0, unroll=2)

    @pl.when(i == pl.num_programs(2) - 1)
    def _():
        sf_ref[...] = st_scr[...]


def _ssd_scan_call(xbc, dtbc, a_rep, s0, tb):
    b, t, _ = xbc.shape
    tb = min(tb, t)
    nb = t // tb
    blk = lambda dd, i: i + dd * (nb - 1 - 2 * i)
    st = pl.BlockSpec((None, None, N_PAIRS, SSD_STATE, PAIR), lambda dd, bi, i: (dd, bi, 0, 0, 0))
    tok2 = pl.BlockSpec((None, None, tb, SSD_WIDTH), lambda dd, bi, i: (dd, bi, blk(dd, i), 0))
    return pl.pallas_call(
        functools.partial(_ssd_scan_kernel, n_chunks=tb // SCAN_CHUNK),
        name="ssd_scan",
        grid=(2, b, nb),
        in_specs=[pl.BlockSpec((None, tb, SSD_XBC), lambda dd, bi, i: (bi, blk(dd, i), 0)),
                  tok2,
                  pl.BlockSpec((None, 1, SSD_WIDTH), lambda dd, bi, i: (dd, 0, 0)),
                  st],
        out_specs=[tok2, st],
        out_shape=[jax.ShapeDtypeStruct((2, b, t, SSD_WIDTH), F32),
                   jax.ShapeDtypeStruct((2, b, N_PAIRS, SSD_STATE, PAIR), F32)],
        scratch_shapes=[pltpu.VMEM((N_PAIRS, SSD_STATE, PAIR), F32)],
        compiler_params=_cparams(("parallel", "parallel", "arbitrary")),
    )(xbc, dtbc, a_rep, s0)


def _even_finish_kernel(yrf_ref, yrb_ref, bonus_ref, gate_ref, ysd_ref, xs_ref, z_ref, x_ref, m_ref,
                        lnw_ref, lnb_ref, dsk_ref, nw_ref, j_ref, wo_ref, o_ref):
    jm = j_ref[...]
    y = yrf_ref[...] + yrb_ref[...]
    inv_n = 1.0 / RWKV_HEAD_DIM
    mean = _head_sum(y, jm) * inv_n
    yc = y - mean
    var = _head_sum(yc * yc, jm) * inv_n
    y = yc * lax.rsqrt(var + RWKV_GN_EPS) * lnw_ref[...] + lnb_ref[...]
    y_rk = (y + bonus_ref[...].astype(F32)) * gate_ref[...].astype(F32)
    s = ysd_ref[0] + ysd_ref[1] + dsk_ref[...] * xs_ref[...].astype(F32)
    s = s * _silu(z_ref[...].astype(F32))
    s = s * lax.rsqrt(jnp.mean(s * s, axis=-1, keepdims=True) + NORM_EPS) * nw_ref[...]
    out = _dot(y_rk.astype(BF16), wo_ref[0:RWKV_WIDTH, :]) + \
        _dot(s.astype(BF16), wo_ref[RWKV_WIDTH:RWKV_WIDTH + SSD_WIDTH, :])
    o_ref[...] = x_ref[...] + m_ref[2:3, :] * out


def _even_finish_call(yrk, bonus, gate, ysd, xbc, proj, x, mods, p, tm):
    b, t, d = x.shape
    tm = min(tm, t)
    w = RWKV_WIDTH
    per_batch = mods.shape[0] > 1
    full = lambda shape: pl.BlockSpec(shape, lambda bi, i: (0,) * len(shape))
    tok = pl.BlockSpec((None, tm, w), lambda bi, i: (bi, i, 0))
    tok2 = pl.BlockSpec((2, None, tm, w), lambda bi, i: (0, bi, i, 0))
    return pl.pallas_call(
        _even_finish_kernel,
        name="even_finish",
        grid=(b, t // tm),
        in_specs=[tok, tok, tok, tok, tok2,
                  pl.BlockSpec((None, tm, SSD_WIDTH), lambda bi, i: (bi, i, 0)),
                  pl.BlockSpec((None, tm, SSD_WIDTH), lambda bi, i: (bi, i, 0)),
                  pl.BlockSpec((None, tm, d), lambda bi, i: (bi, i, 0)),
                  pl.BlockSpec((None, N_MOD, d), (lambda bi, i: (bi, 0, 0)) if per_batch
                               else (lambda bi, i: (0, 0, 0))),
                  full((1, w)), full((1, w)), full((1, w)), full((1, w)),
                  full((V7X_MXU_DIM, V7X_MXU_DIM)),
                  full((2 * w, d))],
        out_specs=pl.BlockSpec((None, tm, d), lambda bi, i: (bi, i, 0)),
        out_shape=jax.ShapeDtypeStruct((b, t, d), F32),
        compiler_params=_cparams(("parallel", "parallel")),
    )(yrk[0], yrk[1], bonus, gate, ysd, xbc, proj, x, mods, p['ln_w'], p['ln_b'], p['d_skip'], p['norm_w'],
      p['head_sum'], p['w_out'])


def _ret_scan_kernel(q_ref, k_ref, v_ref, lg_ref, s0_ref, y_ref, sf_ref, st_scr, dec_scr, sc_scr,
                     *, n_chunks):
    d = pl.program_id(0)
    i = pl.program_id(2)
    c = SCAN_CHUNK
    dk, dv = RET_QK_DIM, RET_V_DIM
    heads = range(RET_HEADS)
    lg_all = lg_ref[...]

    @pl.when(i == 0)
    def _():
        st_scr[...] = s0_ref[...]
        _, before_eq = _order_masks(d, c)
        row = lax.broadcasted_iota(jnp.int32, (c, c), 0)
        col = lax.broadcasted_iota(jnp.int32, (c, c), 1)
        rel = jnp.abs(row - col).astype(F32)
        pos = (row + d * (c - 1 - 2 * row)).astype(F32)
        for h in heads:
            lg = lg_all[:, h * dk:h * dk + 1]
            dec_scr[h] = jnp.where(before_eq, jnp.exp(rel * lg), 0.0)
            sc_scr[h, 0] = jnp.exp((pos + 1.0) * lg).astype(BF16)
            sc_scr[h, 1] = jnp.exp((c - 1.0 - pos) * lg).astype(BF16)

    def chunk_body(j, carry):
        cj = jnp.where(d == 0, j, n_chunks - 1 - j)
        rows = pl.ds(pl.multiple_of(cj * c, c), c)
        qs = [q_ref[rows, h * dk:(h + 1) * dk] for h in heads]
        ks = [k_ref[rows, h * dk:(h + 1) * dk] for h in heads]
        vs = [v_ref[rows, h * dv:(h + 1) * dv] for h in heads]
        qk = [_dot_nt(qs[h], ks[h]) for h in heads]
        scores = [(qk[h] * dec_scr[h]).astype(BF16) for h in heads]
        st = [st_scr[h] for h in heads]
        y_st = [_dot(qs[h] * sc_scr[h, 0], st[h].astype(BF16)) for h in heads]
        for h in heads:
            y_ref[rows, h * dv:(h + 1) * dv] = (_dot(scores[h], vs[h]) + y_st[h]).astype(y_ref.dtype)
        for h in heads:
            lg = lg_all[:, h * dk:h * dk + 1]
            st_scr[h] = st[h] * jnp.exp(c * lg) + _dot_tn(ks[h] * sc_scr[h, 1], vs[h])
        return carry

    lax.fori_loop(0, n_chunks, chunk_body, 0, unroll=2)

    @pl.when(i == pl.num_programs(2) - 1)
    def _():
        sf_ref[...] = st_scr[...]


def _ret_scan_call(proj, lg_rep, s0, tb):
    b, t, _ = proj.shape
    tb = min(tb, t)
    nb = t // tb
    blk = lambda dd, i: i + dd * (nb - 1 - 2 * i)
    st = pl.BlockSpec((None, None, RET_HEADS, RET_QK_DIM, RET_V_DIM), lambda dd, bi, i: (dd, bi, 0, 0, 0))
    return pl.pallas_call(
        functools.partial(_ret_scan_kernel, n_chunks=tb // SCAN_CHUNK),
        name="ret_scan",
        grid=(2, b, nb),
        in_specs=[pl.BlockSpec((None, tb, RET_QK), lambda dd, bi, i: (bi, blk(dd, i), 0)),
                  pl.BlockSpec((None, tb, RET_QK), lambda dd, bi, i: (bi, blk(dd, i), 1)),
                  pl.BlockSpec((None, tb, RET_V), lambda dd, bi, i: (bi, blk(dd, i), 2 * RET_QK // RET_V)),
                  pl.BlockSpec((None, 1, RET_QK), lambda dd, bi, i: (dd, 0, 0)), st],
        out_specs=[pl.BlockSpec((None, None, tb, RET_V), lambda dd, bi, i: (dd, bi, blk(dd, i), 0)), st],
        out_shape=[jax.ShapeDtypeStruct((2, b, t, RET_V), BF16),
                   jax.ShapeDtypeStruct((2, b, RET_HEADS, RET_QK_DIM, RET_V_DIM), F32)],
        scratch_shapes=[pltpu.VMEM((RET_HEADS, RET_QK_DIM, RET_V_DIM), F32),
                        pltpu.VMEM((RET_HEADS, SCAN_CHUNK, SCAN_CHUNK), F32),
                        pltpu.VMEM((RET_HEADS, 2, SCAN_CHUNK, RET_QK_DIM), BF16)],
        compiler_params=_cparams(("parallel", "parallel", "arbitrary")),
    )(proj, proj, proj, lg_rep, s0)


def _odd_finish_kernel(y_ref, g_ref, x_ref, m_ref, wo_ref, o_ref):
    y = y_ref[0].astype(F32) + y_ref[1].astype(F32)
    dv = RET_V_DIM
    parts = []
    for h in range(RET_HEADS):
        yh = y[:, h * dv:(h + 1) * dv]
        parts.append(yh * lax.rsqrt(jnp.mean(yh * yh, axis=-1, keepdims=True) + NORM_EPS))
    yn = jnp.concatenate(parts, axis=1)
    act = (_silu(g_ref[...].astype(F32)) * yn).astype(BF16)
    o_ref[...] = x_ref[...] + m_ref[2:3, :] * _dot(act, wo_ref[...])


def _odd_finish_call(y, proj, x, mods, w_out, tm):
    b, t, d = x.shape
    tm = min(tm, t)
    return pl.pallas_call(
        _odd_finish_kernel,
        name="odd_finish",
        grid=(b, t // tm),
        in_specs=[pl.BlockSpec((2, None, tm, RET_V), lambda bi, i: (0, bi, i, 0)),
                  pl.BlockSpec((None, tm, RET_V), lambda bi, i: (bi, i, (2 * RET_QK + RET_V) // RET_V)),
                  pl.BlockSpec((None, tm, d), lambda bi, i: (bi, i, 0)),
                  pl.BlockSpec((None, N_MOD, d), lambda bi, i: (bi, 0, 0)),
                  pl.BlockSpec((RET_V, d), lambda bi, i: (0, 0))],
        out_specs=pl.BlockSpec((None, tm, d), lambda bi, i: (bi, i, 0)),
        out_shape=jax.ShapeDtypeStruct((b, t, d), F32),
        compiler_params=_cparams(("parallel", "parallel")),
    )(y, proj, x, mods, w_out)


def _ffn_kernel(*refs, on_grid, final_norm):
    if on_grid:
        x_ref, xp_ref, xn_ref, g_ref, m_ref, wu_ref, cw_ref, cb_ref, wd_ref = refs[:9]
        rest = refs[9:]
    else:
        x_ref, g_ref, m_ref, wu_ref, cw_ref, cb_ref, wd_ref = refs[:7]
        rest = refs[7:]
    if final_norm:
        fg_ref, o_ref, *act_scrs = rest
    else:
        o_ref, *act_scrs = rest
    i = pl.program_id(1)
    n_tiles = pl.num_programs(1)
    tm = x_ref.shape[0]

    def norm_mod(xv):
        hv = xv * lax.rsqrt(jnp.mean(xv * xv, axis=-1, keepdims=True) + NORM_EPS) * g_ref[...]
        return (hv * (1.0 + m_ref[4:5, :]) + m_ref[3:4, :]).astype(BF16)

    x = x_ref[...]
    h = norm_mod(x)
    row = lax.broadcasted_iota(jnp.int32, (tm, 1), 0)
    if on_grid:
        col = row % GRID_W
        ok_left = col > 0
        ok_right = col < GRID_W - 1
        zero = jnp.zeros((GRID_W, x.shape[1]), BF16)
        h_ext = jnp.concatenate([jnp.where(i > 0, norm_mod(xp_ref[...]), zero), h,
                                 jnp.where(i < n_tiles - 1, norm_mod(xn_ref[...]), zero)], axis=0)
    else:
        ok_left = row > 0
        ok_right = row < tm - 1
    n_chunks = D_FF // FFN_COL_CHUNK

    def up_proj(j):
        cols = slice(j * FFN_COL_CHUNK, (j + 1) * FFN_COL_CHUNK)
        vcols = slice(D_FF + j * FFN_COL_CHUNK, D_FF + (j + 1) * FFN_COL_CHUNK)
        gate = _dot(h_ext if on_grid else h, wu_ref[:, cols])
        return gate, _dot(h, wu_ref[:, vcols])

    out = None
    nxt = up_proj(0)
    for j in range(n_chunks):
        cols = slice(j * FFN_COL_CHUNK, (j + 1) * FFN_COL_CHUNK)
        gate, val = nxt
        if j + 1 < n_chunks:
            nxt = up_proj(j + 1)
        if on_grid:
            rows3 = [gate[dr * GRID_W:dr * GRID_W + tm] for dr in range(3)]
            taps = [rows3[0] * cw_ref[dc:dc + 1, cols] + rows3[1] * cw_ref[3 + dc:4 + dc, cols]
                    + rows3[2] * cw_ref[6 + dc:7 + dc, cols] for dc in range(3)]
        else:
            taps = [gate * cw_ref[3 + dc:4 + dc, cols] for dc in range(3)]
        acc = cb_ref[:, cols] + taps[1] + jnp.where(ok_left, pltpu.roll(taps[0], 1, axis=0), 0.0) \
            + jnp.where(ok_right, pltpu.roll(taps[2], tm - 1, axis=0), 0.0)
        grp, slot = divmod(j, FFN_DOWN_GROUP)
        act_scr = act_scrs[grp]
        act_scr[:, slot * FFN_COL_CHUNK:(slot + 1) * FFN_COL_CHUNK] = (_gelu_tanh(acc) * val).astype(BF16)
        if slot + 1 == FFN_DOWN_GROUP or j + 1 == n_chunks:
            width = (slot + 1) * FFN_COL_CHUNK
            k0 = grp * FFN_DOWN_GROUP * FFN_COL_CHUNK
            part = _dot(act_scr[:, 0:width], wd_ref[k0:k0 + width, :])
            out = part if out is None else out + part
    out = x + m_ref[5:6, :] * out
    if final_norm:
        out = out * lax.rsqrt(jnp.mean(out * out, axis=-1, keepdims=True) + NORM_EPS) * fg_ref[...]
    o_ref[...] = out


def _ffn_call(x, norm_g, mods, w_up, conv_w9, conv_b, w_down, *, tm, on_grid, final_g=None):
    b, t, d = x.shape
    tm = min(tm, t)
    n_tiles = t // tm
    per_batch = mods.shape[0] > 1
    full = lambda shape: pl.BlockSpec(shape, lambda bi, i: (0,) * len(shape))
    in_specs = [pl.BlockSpec((None, tm, d), lambda bi, i: (bi, i, 0))]
    args = [x]
    if on_grid:
        r = tm // GRID_W
        last = t // GRID_W - 1
        in_specs += [pl.BlockSpec((None, GRID_W, d), lambda bi, i: (bi, jnp.maximum(i * r - 1, 0), 0)),
                     pl.BlockSpec((None, GRID_W, d), lambda bi, i: (bi, jnp.minimum((i + 1) * r, last), 0))]
        args += [x, x]
    else:
        assert n_tiles == 1
    in_specs += [full((1, d)),
                 pl.BlockSpec((None, N_MOD, d), (lambda bi, i: (bi, 0, 0)) if per_batch
                              else (lambda bi, i: (0, 0, 0))),
                 full((d, 2 * D_FF)), full((9, D_FF)), full((1, D_FF)), full((D_FF, d))]
    args += [norm_g, mods, w_up, conv_w9, conv_b, w_down]
    if final_g is not None:
        in_specs.append(full((1, d)))
        args.append(final_g)
    return pl.pallas_call(
        functools.partial(_ffn_kernel, on_grid=on_grid, final_norm=final_g is not None),
        name="conv_ffn",
        grid=(b, n_tiles),
        in_specs=in_specs,
        out_specs=pl.BlockSpec((None, tm, d), lambda bi, i: (bi, i, 0)),
        out_shape=jax.ShapeDtypeStruct((b, t, d), F32),
        scratch_shapes=[pltpu.VMEM((tm, FFN_DOWN_GROUP * FFN_COL_CHUNK), BF16)
                        for _ in range(-(-D_FF // (FFN_DOWN_GROUP * FFN_COL_CHUNK)))],
        compiler_params=_cparams(("parallel", "parallel")),
    )(*args)


def _block_diag2(a, b):
    za = jnp.zeros((a.shape[0], b.shape[1]), a.dtype)
    zb = jnp.zeros((b.shape[0], a.shape[1]), a.dtype)
    return jnp.concatenate([jnp.concatenate([a, za], axis=1), jnp.concatenate([zb, b], axis=1)], axis=0)


def _pad_cols(a, n, fill=0.0):
    return jnp.pad(a, ((0, 0), (0, n - a.shape[1])), constant_values=fill)


def _even_params(j, ev_w_in, ev_mu_prev, ev_mu_next, rk_w0_f, rk_w0_b, rk_w2_f, rk_w2_b, rk_a0_f,
                 rk_a0_b, rk_a2_f, rk_a2_b, rk_g2, rk_k_k, rk_k_a, rk_r_k, rk_ln_w, rk_ln_b,
                 ssd_conv_w, ssd_conv_b, ssd_dt_bias_f, ssd_dt_bias_b, ssd_a_log_f, ssd_a_log_b,
                 ssd_d, ssd_norm_w, ev_w_out):
    w_in = ev_w_in[j]
    rw = w_in[:, :RWKV_COLS]
    z = w_in[:, RWKV_COLS:RWKV_COLS + SSD_WIDTH]
    xbc = w_in[:, RWKV_COLS + SSD_WIDTH:RWKV_COLS + SSD_WIDTH + SSD_XBC]
    dts = w_in[:, RWKV_COLS + SSD_WIDTH + SSD_XBC:]
    w_packed = jnp.concatenate([_pad_cols(jnp.concatenate([rw, dts], axis=1), EV_RW_BLOCK), xbc, z], axis=1)
    head = jnp.arange(V7X_MXU_DIM) // RWKV_HEAD_DIM
    head_sum = (head[:, None] == head[None, :]).astype(BF16)
    lane = jnp.arange(V7X_LANES)[:, None]
    tgt = jnp.arange(2 * SSD_WIDTH)[None, :]
    dt_expand = (lane == (tgt // SSD_WIDTH) * SSD_HEADS + (tgt % SSD_WIDTH) // SSD_HEAD_DIM).astype(BF16)
    rep = lambda a: jnp.repeat(a, SSD_HEAD_DIM)[None, :]
    row = lambda a: a[None, :]
    return {
        'w_in': w_packed.astype(BF16),
        'mu_self': _pad_cols(row(1.0 - ev_mu_prev[j] - ev_mu_next[j]), EV_RW_BLOCK, 1.0),
        'mu_prev': _pad_cols(row(ev_mu_prev[j]), EV_RW_BLOCK),
        'mu_next': _pad_cols(row(ev_mu_next[j]), EV_RW_BLOCK),
        'w0': row(jnp.concatenate([rk_w0_f[j], rk_w0_b[j]])),
        'w2': _block_diag2(rk_w2_f[j], rk_w2_b[j]).astype(BF16),
        'a0': row(jnp.concatenate([rk_a0_f[j], rk_a0_b[j]])),
        'a2': _block_diag2(rk_a2_f[j], rk_a2_b[j]).astype(BF16),
        'g2': rk_g2[j].astype(BF16),
        'k_k': row(rk_k_k[j]), 'k_a': row(rk_k_a[j]), 'r_k': row(rk_r_k[j].reshape(-1)),
        'ln_w': row(rk_ln_w[j]), 'ln_b': row(rk_ln_b[j]),
        'head_sum': head_sum,
        'conv_w': ssd_conv_w[j], 'conv_b': row(ssd_conv_b[j]),
        'dt_bias': jnp.concatenate([rep(ssd_dt_bias_f[j]), rep(ssd_dt_bias_b[j])], axis=1),
        'dt_expand': dt_expand,
        'a_rep': jnp.stack([rep(-jnp.exp(ssd_a_log_f[j])), rep(-jnp.exp(ssd_a_log_b[j]))]),
        'd_skip': rep(ssd_d[j]),
        'norm_w': row(ssd_norm_w[j]),
        'w_out': ev_w_out[j].astype(BF16),
    }


def _rope_tables(t):
    n = RET_QK_DIM // 4
    pos = jnp.arange(t)
    row = (pos // GRID_W).astype(F32)
    col = (pos % GRID_W).astype(F32)
    inv = ROPE_BASE ** (-jnp.arange(n, dtype=F32) / n)
    ang = jnp.concatenate([row[:, None] * inv, col[:, None] * inv], axis=-1)
    cos, sin = jnp.cos(ang), jnp.sin(ang)
    return jnp.concatenate([cos, cos], axis=-1), jnp.concatenate([-sin, sin], axis=-1)


def _conv_ffn(x, mods, norm_g, w_up, conv_w9, conv_b, w_down, *, on_grid, final_g=None):
    return _ffn_call(x, norm_g, mods, w_up, conv_w9, conv_b, w_down, tm=512, on_grid=on_grid,
                     final_g=final_g)


def _even_layer(x, ctx, mods_x, mods_c, norm_g, p):
    b = x.shape[0]

    def features(h, mods):
        *feat, xbc, dtbc, z = _even_feat_call(h, norm_g, mods, p, 256)
        return z, feat, xbc, dtbc

    proj_c, feat_c, xbc_c, dt_c = features(ctx, mods_c)
    proj_x, feat_x, xbc_x, dt_x = features(x, mods_x)
    s0 = jnp.zeros((2, b, N_PAIRS, PAIR, PAIR), F32)
    *yrk_c, s_ctx = _rwkv_scan_call(feat_c, s0, 256)
    *yrk_x, _ = _rwkv_scan_call(feat_x, s_ctx, 512)
    h0 = jnp.zeros((2, b, N_PAIRS, SSD_STATE, PAIR), F32)
    ysd_c, h_ctx = _ssd_scan_call(xbc_c, dt_c, p['a_rep'], h0, 256)
    ysd_x, _ = _ssd_scan_call(xbc_x, dt_x, p['a_rep'], h_ctx, 512)
    x = _even_finish_call(yrk_x, feat_x[3], feat_x[4], ysd_x, xbc_x, proj_x, x, mods_x, p, 512)
    ctx = _even_finish_call(yrk_c, feat_c[3], feat_c[4], ysd_c, xbc_c, proj_c, ctx, mods_c, p, 256)
    return x, ctx


def _odd_layer(x, ctx, mods_x, mods_c, norm_g, w_in, lg_rep, w_out):
    b, t, _ = x.shape
    proj_c = _nm_call(ctx, norm_g, mods_c, w_in, shift_row=0, tm=256, tn=512, out_dtype=BF16,
                      qk_mode='scale')
    proj_x = _nm_call(x, norm_g, mods_x, w_in, shift_row=0, tm=512, tn=512, out_dtype=BF16,
                      qk_mode='rope', rope=_rope_tables(t))
    s0 = jnp.zeros((2, b, RET_HEADS, RET_QK_DIM, RET_V_DIM), F32)
    _, s_ctx = _ret_scan_call(proj_c, lg_rep, s0, 256)
    y, _ = _ret_scan_call(proj_x, lg_rep, s_ctx, 512)
    return _odd_finish_call(y, proj_x, x, mods_x, w_out, 512)


def kernel(x, c, ctx, c_ctx, mod_w, mod_b, norm1_g, norm2_g, ffn_w_up, ffn_conv_w, ffn_conv_b, ffn_w_down, ev_w_in, ev_mu_prev, ev_mu_next, rk_w0_f, rk_w0_b, rk_w2_f, rk_w2_b, rk_a0_f, rk_a0_b, rk_a2_f, rk_a2_b, rk_g2, rk_k_k, rk_k_a, rk_r_k, rk_ln_w, rk_ln_b, ssd_conv_w, ssd_conv_b, ssd_dt_bias_f, ssd_dt_bias_b, ssd_a_log_f, ssd_a_log_b, ssd_d, ssd_norm_w, ev_w_out, ret_w_in, ret_log2_f, ret_log2_b, ret_w_out, final_norm_g):
    b, t, d = x.shape
    depth = mod_w.shape[0]
    rows = -(-(b + 1) // V7X_SUBLANES) * V7X_SUBLANES
    cond = jnp.concatenate([c, c_ctx[None, :], jnp.zeros((rows - b - 1, d), F32)], axis=0)
    mods = _mod_call(cond, mod_w, mod_b).reshape(depth, rows, N_MOD, d)
    for i in range(depth):
        need_ctx = i < depth - 1
        mods_x = mods[i, :b]
        mods_c = mods[i, b:b + 1]
        j = i // 2
        g1 = norm1_g[i][None, :]
        if i % 2 == 0:
            p = _even_params(j, ev_w_in, ev_mu_prev, ev_mu_next, rk_w0_f, rk_w0_b, rk_w2_f, rk_w2_b,
                             rk_a0_f, rk_a0_b, rk_a2_f, rk_a2_b, rk_g2, rk_k_k, rk_k_a, rk_r_k,
                             rk_ln_w, rk_ln_b, ssd_conv_w, ssd_conv_b, ssd_dt_bias_f, ssd_dt_bias_b,
                             ssd_a_log_f, ssd_a_log_b, ssd_d, ssd_norm_w, ev_w_out)
            x, ctx_mixed = _even_layer(x, ctx, mods_x, mods_c, g1, p)
        else:
            lg = jnp.stack([jnp.log1p(-jnp.exp2(-ret_log2_f[j])), jnp.log1p(-jnp.exp2(-ret_log2_b[j]))])
            lg_rep = jnp.repeat(lg, RET_QK_DIM, axis=-1)[:, None, :]
            x = _odd_layer(x, ctx, mods_x, mods_c, g1, ret_w_in[j].astype(BF16), lg_rep,
                           ret_w_out[j].astype(BF16))
            ctx_mixed = None
        g2 = norm2_g[i][None, :]
        w_up = ffn_w_up[i].astype(BF16)
        w_down = ffn_w_down[i].astype(BF16)
        conv_w9 = ffn_conv_w[i].reshape(9, D_FF)
        conv_b = ffn_conv_b[i][None, :]
        last = i == depth - 1
        x = _conv_ffn(x, mods_x, g2, w_up, conv_w9, conv_b, w_down, on_grid=True,
                      final_g=final_norm_g[None, :] if last else None)
        if need_ctx:
            ctx = _conv_ffn(ctx_mixed, mods_c, g2, w_up, conv_w9, conv_b, w_down, on_grid=False)
    return x
```

```python
import functools
import math

import jax
import jax.numpy as jnp
from jax import lax
from jax.experimental import pallas as pl
from jax.experimental.pallas import tpu as pltpu

F32 = jnp.float32
BF16 = jnp.bfloat16

D_MODEL = 1024
GRID_W = 64
N_MOD = 6
NORM_EPS = 1e-6
RWKV_HEADS = 8
RWKV_HEAD_DIM = 64
RWKV_WIDTH = RWKV_HEADS * RWKV_HEAD_DIM
DECAY_LORA = 64
ICLR_LORA = 64
GATE_LORA = 128
RWKV_GN_EPS = 64e-5
RWKV_COLS = 3 * RWKV_WIDTH + 2 * DECAY_LORA + 2 * ICLR_LORA + GATE_LORA
SSD_HEADS = 8
SSD_HEAD_DIM = 64
SSD_WIDTH = SSD_HEADS * SSD_HEAD_DIM
SSD_GROUPS = 2
SSD_STATE = 128
SSD_XBC = SSD_WIDTH + 2 * SSD_GROUPS * SSD_STATE
RET_HEADS = 8
RET_QK_DIM = 128
RET_V_DIM = 256
RET_QK = RET_HEADS * RET_QK_DIM
RET_V = RET_HEADS * RET_V_DIM
ROPE_BASE = 10000.0
D_FF = 2816

V7X_LANES = 128
V7X_SUBLANES = 8
V7X_MXU_DIM = 256
V7X_VMEM_LIMIT_BYTES = 56 * 1024 * 1024

RWKV_CHUNK = 64
SCAN_CHUNK = 128
FFN_COL_CHUNK = 256
FFN_DOWN_GROUP = 4
PAIR = 2 * RWKV_HEAD_DIM
N_PAIRS = RWKV_HEADS // 2
EV_RW_BLOCK = 2048
EV_DT_OFF = RWKV_COLS
EV_XBC_OFF = EV_RW_BLOCK
EV_Z_OFF = EV_RW_BLOCK + SSD_XBC
EV_COLS = EV_Z_OFF + SSD_WIDTH


def _cparams(sem):
    return pltpu.CompilerParams(dimension_semantics=sem, vmem_limit_bytes=V7X_VMEM_LIMIT_BYTES)


def _split3(x):
    hi = x.astype(BF16)
    r1 = x - hi.astype(F32)
    mid = r1.astype(BF16)
    lo = (r1 - mid.astype(F32)).astype(BF16)
    return hi, mid, lo


def _dot(a, b):
    return jnp.dot(a, b, preferred_element_type=F32)


def _dot_nt(a, b):
    return lax.dot_general(a, b, (((1,), (1,)), ((), ())), preferred_element_type=F32)


def _dot_tn(a, b):
    return lax.dot_general(a, b, (((0,), (0,)), ((), ())), preferred_element_type=F32)


def _dot01(x, m01):
    hi, mid, lo = _split3(x)
    return _dot(hi, m01) + _dot(mid, m01) + _dot(lo, m01)


def _head_sum(x, j01):
    hi = x.astype(BF16)
    lo = (x - hi.astype(F32)).astype(BF16)
    n = j01.shape[0]
    return jnp.concatenate([_dot(hi[:, g * n:(g + 1) * n], j01) + _dot(lo[:, g * n:(g + 1) * n], j01)
                            for g in range(x.shape[1] // n)], axis=1)


def _dot01_left(m01, x):
    hi, mid, lo = _split3(x)
    return _dot(m01, hi) + _dot(m01, mid) + _dot(m01, lo)


def _sigmoid(x):
    return 1.0 / (1.0 + jnp.exp(-x))


def _silu(x):
    return x * _sigmoid(x)


def _softplus(x):
    return jnp.maximum(x, 0.0) + jnp.log1p(jnp.exp(-jnp.abs(x)))


def _gelu_tanh(x):
    c = math.sqrt(2.0 / math.pi)
    half = 0.5 * x
    return half + half * jnp.tanh(x * (c + (0.044715 * c) * (x * x)))


def _order_masks(d, n):
    row = lax.broadcasted_iota(jnp.int32, (n, n), 0)
    col = lax.broadcasted_iota(jnp.int32, (n, n), 1)
    diff = (row - col) * (1 - 2 * d)
    return diff > 0, diff >= 0


def _mod_kernel(c_ref, w_ref, b_ref, o_ref):
    h = _silu(c_ref[...])
    hi, mid, lo = _split3(h)
    w = w_ref[...]
    wh = w.astype(BF16)
    wl = (w - wh.astype(F32)).astype(BF16)
    acc = _dot(hi, wh) + _dot(mid, wh) + _dot(hi, wl)
    o_ref[...] = acc + b_ref[...]


def _mod_call(cond, mod_w, mod_b):
    depth, d, n = mod_w.shape
    rows = cond.shape[0]
    tn = 1024
    return pl.pallas_call(
        _mod_kernel,
        name="adaln_mod",
        grid=(depth, n // tn),
        in_specs=[pl.BlockSpec((rows, d), lambda l, j: (0, 0)),
                  pl.BlockSpec((None, d, tn), lambda l, j: (l, 0, j)),
                  pl.BlockSpec((None, 1, tn), lambda l, j: (l, 0, j))],
        out_specs=pl.BlockSpec((None, rows, tn), lambda l, j: (l, 0, j)),
        out_shape=jax.ShapeDtypeStruct((depth, rows, n), F32),
        compiler_params=_cparams(("parallel", "parallel")),
    )(cond, mod_w, mod_b.reshape(depth, 1, n))


def _nm_kernel(*refs, shift_row, tn, qk_mode):
    if qk_mode == 'rope':
        x_ref, g_ref, m_ref, w_ref, cos_ref, sin_ref, o_ref = refs
    else:
        x_ref, g_ref, m_ref, w_ref, o_ref = refs
    x = x_ref[...]
    h = x * lax.rsqrt(jnp.mean(x * x, axis=-1, keepdims=True) + NORM_EPS) * g_ref[...]
    h = h * (1.0 + m_ref[shift_row + 1:shift_row + 2, :]) + m_ref[shift_row:shift_row + 1, :]
    h = h.astype(BF16)
    dk = RET_QK_DIM
    for j in range(w_ref.shape[1] // tn):
        cols = slice(j * tn, (j + 1) * tn)
        y = _dot(h, w_ref[:, cols])
        if qk_mode is not None and j * tn < 2 * RET_QK:
            scale = dk ** -0.5 if j * tn >= RET_QK else 1.0
            heads = []
            for hh in range(tn // dk):
                yh = y[:, hh * dk:(hh + 1) * dk]
                if qk_mode == 'rope':
                    yh = yh * cos_ref[...] + pltpu.roll(yh, dk // 2, axis=1) * sin_ref[...]
                heads.append(yh * scale if scale != 1.0 else yh)
            y = jnp.concatenate(heads, axis=1)
        o_ref[:, cols] = y.astype(o_ref.dtype)


def _nm_call(x, g, mods, w, *, shift_row, tm, tn, out_dtype=F32, qk_mode=None, rope=None):
    b, t, d = x.shape
    n = w.shape[1]
    tm = min(tm, t)
    per_batch = mods.shape[0] > 1
    in_specs = [pl.BlockSpec((None, tm, d), lambda bi, i: (bi, i, 0)),
                pl.BlockSpec((1, d), lambda bi, i: (0, 0)),
                pl.BlockSpec((None, N_MOD, d), (lambda bi, i: (bi, 0, 0)) if per_batch
                             else (lambda bi, i: (0, 0, 0))),
                pl.BlockSpec((d, n), lambda bi, i: (0, 0))]
    args = [x, g, mods, w]
    if qk_mode == 'rope':
        tab = pl.BlockSpec((tm, RET_QK_DIM), lambda bi, i: (i, 0))
        in_specs += [tab, tab]
        args += list(rope)
    return pl.pallas_call(
        functools.partial(_nm_kernel, shift_row=shift_row, tn=tn, qk_mode=qk_mode),
        name="norm_mod_matmul",
        grid=(b, t // tm),
        in_specs=in_specs,
        out_specs=pl.BlockSpec((None, tm, n), lambda bi, i: (bi, i, 0)),
        out_shape=jax.ShapeDtypeStruct((b, t, n), out_dtype),
        compiler_params=_cparams(("parallel", "parallel")),
    )(*args)


def _even_feat_kernel(x_ref, xp_ref, xn_ref, g_ref, m_ref, w_ref,
                      mus_ref, mup_ref, mun_ref, w0_ref, w2_ref, a0_ref, a2_ref,
                      g2_ref, kk_ref, ka_ref, rk_ref, j_ref, cw_ref, cb_ref, dtb_ref, e_ref,
                      r_ref, v_ref, kkn_ref, bonus_ref, gate_ref, ld_ref, kd_ref, bd_ref,
                      xbc_ref, dtbc_ref, z_ref):
    i = pl.program_id(1)
    n_tiles = pl.num_programs(1)
    tm = x_ref.shape[0]
    halo = V7X_SUBLANES
    ext = tm + 2 * halo

    def norm_mod(xv):
        hv = xv * lax.rsqrt(jnp.mean(xv * xv, axis=-1, keepdims=True) + NORM_EPS) * g_ref[...]
        return (hv * (1.0 + m_ref[1:2, :]) + m_ref[0:1, :]).astype(BF16)

    h = norm_mod(x_ref[...])
    zero = jnp.zeros((halo, x_ref.shape[1]), BF16)
    h_ext = jnp.concatenate([jnp.where(i > 0, norm_mod(xp_ref[...]), zero), h,
                             jnp.where(i < n_tiles - 1, norm_mod(xn_ref[...]), zero)], axis=0)

    def proj3(cols):
        ye = _dot(h_ext, w_ref[:, cols])
        return (ye[halo:halo + tm], pltpu.roll(ye, 1, axis=0)[halo:halo + tm],
                pltpu.roll(ye, ext - 1, axis=0)[halo:halo + tm])

    w = RWKV_WIDTH
    rw = []
    for j in range(EV_RW_BLOCK // w):
        cols = slice(j * w, (j + 1) * w)
        cur, prev, nxt = proj3(cols)
        rw.append(cur * mus_ref[:, cols] + prev * mup_ref[:, cols] + nxt * mun_ref[:, cols])
    r, k, v, lora = rw
    wd = lora[:, 0:2 * DECAY_LORA]
    ad = lora[:, 2 * DECAY_LORA:2 * DECAY_LORA + 2 * ICLR_LORA]
    gd = lora[:, 2 * DECAY_LORA + 2 * ICLR_LORA:2 * DECAY_LORA + 2 * ICLR_LORA + GATE_LORA]
    dt_raw = lora[:, EV_DT_OFF - 3 * w:EV_RW_BLOCK - 3 * w]
    jm = j_ref[...]

    for j in range(SSD_XBC // w):
        cols = slice(j * w, (j + 1) * w)
        cur, prev, nxt = proj3(slice(EV_XBC_OFF + j * w, EV_XBC_OFF + (j + 1) * w))
        y = prev * cw_ref[0:1, cols] + cur * cw_ref[1:2, cols] + nxt * cw_ref[2:3, cols] + cb_ref[:, cols]
        xbc_ref[:, cols] = _silu(y).astype(xbc_ref.dtype)
    z_ref[...] = _dot(h, w_ref[:, EV_Z_OFF:EV_Z_OFF + SSD_WIDTH]).astype(z_ref.dtype)
    dt = _dot01(_softplus(dt_raw + dtb_ref[...]), e_ref[...])
    dtbc_ref[0] = dt[:, 0:SSD_WIDTH]
    dtbc_ref[1] = dt[:, SSD_WIDTH:2 * SSD_WIDTH]

    kk = k * kk_ref[...]
    ss = _head_sum(kk * kk, jm)
    kk = kk / jnp.maximum(jnp.sqrt(ss), 1e-12)
    r_ref[...] = r.astype(r_ref.dtype)
    v_ref[...] = v.astype(v_ref.dtype)
    kkn_ref[...] = kk.astype(kkn_ref.dtype)
    bonus_ref[...] = (_head_sum(r * k * rk_ref[...], jm) * v).astype(bonus_ref.dtype)
    gate_ref[...] = _dot(_sigmoid(gd).astype(BF16), g2_ref[...]).astype(gate_ref.dtype)

    zw = _dot(jnp.tanh(wd).astype(BF16), w2_ref[...]) + w0_ref[...]
    za = _dot(ad.astype(BF16), a2_ref[...]) + a0_ref[...]
    for di in range(2):
        ld_ref[di] = -math.exp(-0.5) * _sigmoid(zw[:, di * w:(di + 1) * w])
        iclr = _sigmoid(za[:, di * w:(di + 1) * w])
        kd_ref[di] = (k * (1.0 + (iclr - 1.0) * ka_ref[...])).astype(kd_ref.dtype)
        bd_ref[di] = (kk * iclr).astype(bd_ref.dtype)


def _even_feat_call(x, norm_g, mods, p, tm):
    b, t, d = x.shape
    tm = min(tm, t)
    n_tiles = t // tm
    w = RWKV_WIDTH
    per_batch = mods.shape[0] > 1
    r8 = tm // V7X_SUBLANES
    last = t // V7X_SUBLANES - 1
    full = lambda shape: pl.BlockSpec(shape, lambda bi, i: (0,) * len(shape))
    tok = pl.BlockSpec((None, tm, w), lambda bi, i: (bi, i, 0))
    tok2 = pl.BlockSpec((2, None, tm, w), lambda bi, i: (0, bi, i, 0))
    sd = jax.ShapeDtypeStruct((b, t, w), BF16)
    sd2 = jax.ShapeDtypeStruct((2, b, t, w), BF16)
    f2 = jax.ShapeDtypeStruct((2, b, t, w), F32)
    return pl.pallas_call(
        _even_feat_kernel,
        name="even_feat",
        grid=(b, n_tiles),
        in_specs=[pl.BlockSpec((None, tm, d), lambda bi, i: (bi, i, 0)),
                  pl.BlockSpec((None, V7X_SUBLANES, d), lambda bi, i: (bi, jnp.maximum(i * r8 - 1, 0), 0)),
                  pl.BlockSpec((None, V7X_SUBLANES, d), lambda bi, i: (bi, jnp.minimum((i + 1) * r8, last), 0)),
                  full((1, d)),
                  pl.BlockSpec((None, N_MOD, d), (lambda bi, i: (bi, 0, 0)) if per_batch
                               else (lambda bi, i: (0, 0, 0))),
                  full((d, EV_COLS)),
                  full((1, EV_RW_BLOCK)), full((1, EV_RW_BLOCK)), full((1, EV_RW_BLOCK)),
                  full((1, 2 * w)), full((2 * DECAY_LORA, 2 * w)),
                  full((1, 2 * w)), full((2 * ICLR_LORA, 2 * w)),
                  full((GATE_LORA, w)), full((1, w)), full((1, w)), full((1, w)),
                  full((V7X_MXU_DIM, V7X_MXU_DIM)),
                  full((3, SSD_XBC)), full((1, SSD_XBC)), full((1, V7X_LANES)),
                  full((V7X_LANES, 2 * SSD_WIDTH))],
        out_specs=[tok, tok, tok, tok, tok, tok2, tok2, tok2,
                   pl.BlockSpec((None, tm, SSD_XBC), lambda bi, i: (bi, i, 0)), tok2, tok],
        out_shape=[sd, sd, sd, sd, sd, f2, sd2, sd2,
                   jax.ShapeDtypeStruct((b, t, SSD_XBC), BF16), f2, sd],
        compiler_params=_cparams(("parallel", "parallel")),
    )(x, x, x, norm_g, mods, p['w_in'], p['mu_self'], p['mu_prev'], p['mu_next'], p['w0'], p['w2'],
      p['a0'], p['a2'], p['g2'], p['k_k'], p['k_a'], p['r_k'], p['head_sum'],
      p['conv_w'], p['conv_b'], p['dt_bias'], p['dt_expand'])


def _rwkv_scan_kernel(rf_ref, vf_ref, kkf_ref, ldf_ref, kdf_ref, bdf_ref,
                      rb_ref, vb_ref, kkb_ref, ldb_ref, kdb_ref, bdb_ref, s0_ref,
                      yf_ref, yb_ref, sf_ref, st_scr, *, n_chunks):
    i = pl.program_id(1)
    c = RWKV_CHUNK

    @pl.when(i == 0)
    def _():
        st_scr[...] = s0_ref[...]

    row2 = lax.broadcasted_iota(jnp.int32, (2 * c, 2 * c), 0)
    col2 = lax.broadcasted_iota(jnp.int32, (2 * c, 2 * c), 1)
    same_head = (row2 // c) == (col2 // c)
    eye_bd = (row2 == col2).astype(F32)
    rowc = lax.broadcasted_iota(jnp.int32, (c, 2 * c), 0)
    colc = lax.broadcasted_iota(jnp.int32, (c, 2 * c), 1) % c
    m0 = lax.broadcasted_iota(jnp.int32, (c, PAIR), 1) < RWKV_HEAD_DIM
    row1 = lax.broadcasted_iota(jnp.int32, (c, 1), 0)
    incl01, strict_bd, incl_wide, rsel = [], [], [], []
    for d in range(2):
        sgn = 1 - 2 * d
        incl01.append(_order_masks(d, c)[1].astype(BF16))
        strict_bd.append(jnp.logical_and(same_head, (row2 % c - col2 % c) * sgn > 0))
        incl_wide.append((rowc - colc) * sgn >= 0)
        rsel.append(row1 == (c - 1 if d == 0 else 0))
    refs = ((rf_ref, vf_ref, kkf_ref, ldf_ref, kdf_ref, bdf_ref, yf_ref),
            (rb_ref, vb_ref, kkb_ref, ldb_ref, kdb_ref, bdb_ref, yb_ref))
    chains = [(d, p) for d in range(2) for p in range(N_PAIRS)]
    lanes = [slice(p * PAIR, (p + 1) * PAIR) for p in range(N_PAIRS)]

    def stack(x):
        return jnp.concatenate([jnp.where(m0, x, 0.0), jnp.where(m0, 0.0, x)], axis=0)

    def chunk_body(j, carry):
        rows = (pl.ds(pl.multiple_of(j * c, c), c),
                pl.ds(pl.multiple_of((n_chunks - 1 - j) * c, c), c))
        cs_all = [_dot01_left(incl01[d], refs[d][3][rows[d], :]) for d in range(2)]
        a_s, r_t, b_s, k_s, v_s, b_end, k_end, g_tot = [], [], [], [], [], [], [], []
        for d, p in chains:
            r_ref, v_ref, kk_ref, ld_ref, kd_ref, bd_ref, _ = refs[d]
            rw, ln = rows[d], lanes[p]
            cs = cs_all[d][:, ln]
            cs_last = jnp.sum(jnp.where(rsel[d], cs, 0.0), axis=0, keepdims=True)
            g_neg = jnp.exp(-cs)
            g_end = jnp.exp(cs_last - cs)
            g_tot.append(jnp.exp(cs_last))
            kdv = kd_ref[rw, ln].astype(F32)
            bdv = bd_ref[rw, ln].astype(F32)
            a_s.append(stack(-kk_ref[rw, ln].astype(F32) * jnp.exp(cs - ld_ref[rw, ln])).astype(BF16))
            r_t.append((r_ref[rw, ln].astype(F32) * jnp.exp(cs)).astype(BF16))
            b_s.append(stack(bdv * g_neg).astype(BF16))
            k_s.append(stack(kdv * g_neg).astype(BF16))
            v_s.append(stack(v_ref[rw, ln].astype(F32)).astype(BF16))
            b_end.append(stack(bdv * g_end).astype(BF16))
            k_end.append(stack(kdv * g_end).astype(BF16))
        n = range(len(chains))
        dirs = [d for d, _ in chains]
        gram = [_dot_nt(jnp.concatenate([a_s[q], r_t[q]], axis=0),
                        jnp.concatenate([b_s[q], k_s[q]], axis=0)) for q in n]
        a_ab = [jnp.where(strict_bd[dirs[q]], gram[q][0:2 * c, 0:2 * c], 0.0) for q in n]
        a_ak = [jnp.where(strict_bd[dirs[q]], gram[q][0:2 * c, 2 * c:4 * c], 0.0).astype(BF16) for q in n]
        p_rb = [jnp.where(incl_wide[dirs[q]], gram[q][2 * c:3 * c, 0:2 * c], 0.0).astype(BF16) for q in n]
        p_rk = [jnp.where(incl_wide[dirs[q]], gram[q][2 * c:3 * c, 2 * c:4 * c], 0.0).astype(BF16) for q in n]
        minv = [eye_bd + a_ab[q] for q in n]
        pwb = [a_ab[q].astype(BF16) for q in n]
        pwb = [_dot(pwb[q], pwb[q]).astype(BF16) for q in n]
        levels = int(math.log2(c))
        for k in range(2, levels):
            both = [_dot(pwb[q], jnp.concatenate([pwb[q], minv[q].astype(BF16)], axis=1)) for q in n]
            minv = [minv[q] + both[q][:, 2 * c:4 * c] for q in n]
            pwb = [both[q][:, 0:2 * c].astype(BF16) for q in n]
        minv = [minv[q] + _dot(pwb[q], minv[q].astype(BF16)) for q in n]
        akv = [_dot(a_ak[q], v_s[q]).astype(BF16) for q in n]
        eff = [_dot(minv[q].astype(BF16), jnp.concatenate([a_s[q], akv[q]], axis=1)) for q in n]
        bk_end = [jnp.concatenate([b_end[q], k_end[q]], axis=0) for q in n]
        p_both = [jnp.concatenate([p_rb[q], p_rk[q]], axis=1) for q in n]
        st = [st_scr[d, p] for d, p in chains]
        fs = [_dot_nt(jnp.concatenate([eff[q][:, 0:PAIR].astype(BF16), r_t[q]], axis=0),
                      st[q].astype(BF16)) for q in n]
        uv = [jnp.concatenate([(fs[q][0:2 * c] + eff[q][:, PAIR:2 * PAIR]).astype(BF16), v_s[q]], axis=0)
              for q in n]
        for q, (d, p) in enumerate(chains):
            st_scr[d, p] = st[q] * g_tot[q] + _dot_tn(uv[q], bk_end[q])
        for q, (d, p) in enumerate(chains):
            refs[d][6][rows[d], lanes[p]] = fs[q][2 * c:3 * c] + _dot(p_both[q], uv[q])
        return carry

    lax.fori_loop(0, n_chunks, chunk_body, 0, unroll=2)

    @pl.when(i == pl.num_programs(1) - 1)
    def _():
        sf_ref[...] = st_scr[...]


def _rwkv_scan_call(feat, s0, tb):
    r, v, kk, _, _, ld, kd, bd = feat
    b, t, w = r.shape
    tb = min(tb, t)
    nb = t // tb
    tok_f = pl.BlockSpec((None, tb, w), lambda bi, i: (bi, i, 0))
    tok_b = pl.BlockSpec((None, tb, w), lambda bi, i: (bi, nb - 1 - i, 0))
    dir_f = pl.BlockSpec((None, None, tb, w), lambda bi, i: (0, bi, i, 0))
    dir_b = pl.BlockSpec((None, None, tb, w), lambda bi, i: (1, bi, nb - 1 - i, 0))
    st = pl.BlockSpec((2, None, N_PAIRS, PAIR, PAIR), lambda bi, i: (0, bi, 0, 0, 0))
    y_sd = jax.ShapeDtypeStruct((b, t, w), F32)
    return pl.pallas_call(
        functools.partial(_rwkv_scan_kernel, n_chunks=tb // RWKV_CHUNK),
        name="rwkv_scan",
        grid=(b, nb),
        in_specs=[tok_f, tok_f, tok_f, dir_f, dir_f, dir_f, tok_b, tok_b, tok_b, dir_b, dir_b, dir_b, st],
        out_specs=[tok_f, tok_b, st],
        out_shape=[y_sd, y_sd, jax.ShapeDtypeStruct((2, b, N_PAIRS, PAIR, PAIR), F32)],
        scratch_shapes=[pltpu.VMEM((2, N_PAIRS, PAIR, PAIR), F32)],
        compiler_params=_cparams(("parallel", "arbitrary")),
    )(r, v, kk, ld, kd, bd, r, v, kk, ld, kd, bd, s0)


def _ssd_scan_kernel(xf_ref, dtf_ref, xb_ref, dtb_ref, a_ref, s0_ref, yf_ref, yb_ref, sf_ref, st_scr,
                     *, n_chunks):
    i = pl.program_id(1)
    c = SCAN_CHUNK
    hd = SSD_HEAD_DIM

    @pl.when(i == 0)
    def _():
        st_scr[...] = s0_ref[...]

    row1 = lax.broadcasted_iota(jnp.int32, (c, 1), 0)
    m0 = lax.broadcasted_iota(jnp.int32, (c, PAIR), 1) < hd
    before_eq = [_order_masks(d, c)[1] for d in range(2)]
    incl01 = [before_eq[d].astype(BF16) for d in range(2)]
    rsel = [row1 == (c - 1 if d == 0 else 0) for d in range(2)]
    refs = ((xf_ref, dtf_ref, yf_ref), (xb_ref, dtb_ref, yb_ref))
    chains = [(d, p) for d in range(2) for p in range(N_PAIRS)]
    n = range(len(chains))
    group = [p // (N_PAIRS // SSD_GROUPS) for _, p in chains]
    lanes = [slice(p * PAIR, (p + 1) * PAIR) for _, p in chains]

    def chunk_body(j, carry):
        rows = (pl.ds(pl.multiple_of(j * c, c), c),
                pl.ds(pl.multiple_of((n_chunks - 1 - j) * c, c), c))
        dt = [refs[d][1][rows[d], :] for d in range(2)]
        cs_all = [_dot01_left(incl01[d], dt[d] * a_ref[d]) for d in range(2)]
        bm = [[refs[d][0][rows[d], SSD_WIDTH + g * SSD_STATE:SSD_WIDTH + (g + 1) * SSD_STATE]
               for g in range(SSD_GROUPS)] for d in range(2)]
        cm = [[refs[d][0][rows[d], SSD_WIDTH + (SSD_GROUPS + g) * SSD_STATE:
                          SSD_WIDTH + (SSD_GROUPS + g + 1) * SSD_STATE]
               for g in range(SSD_GROUPS)] for d in range(2)]
        cb = [[_dot_nt(cm[d][g], bm[d][g]) for g in range(SSD_GROUPS)] for d in range(2)]
        st = [st_scr[d, p] for d, p in chains]
        y_st = [_dot(cm[chains[q][0]][group[q]], st[q].astype(BF16)) for q in n]
        cs = [cs_all[chains[q][0]][:, lanes[q]] for q in n]
        xdt = [refs[chains[q][0]][0][rows[chains[q][0]], lanes[q]].astype(F32) * dt[chains[q][0]][:, lanes[q]]
               for q in n]
        probs = []
        for q in n:
            d = chains[q][0]
            cs_t = cs[q].T
            both = []
            for hh in range(2):
                col = cs[q][:, hh * hd:hh * hd + 1]
                rowv = cs_t[hh * hd:hh * hd + 1, :]
                dec = jnp.exp(jnp.where(before_eq[d], col - rowv, -jnp.inf))
                both.append((cb[d][group[q]] * dec).astype(BF16))
            probs.append(jnp.concatenate(both, axis=1))
        xs2 = [jnp.concatenate([jnp.where(m0, xdt[q], 0.0), jnp.where(m0, 0.0, xdt[q])],
                               axis=0).astype(BF16) for q in n]
        y_in = [_dot(probs[q], xs2[q]) for q in n]
        for q in n:
            d = chains[q][0]
            refs[d][2][rows[d], lanes[q]] = y_in[q] + jnp.exp(cs[q]) * y_st[q]
        for q, (d, p) in enumerate(chains):
            cs_last = jnp.sum(jnp.where(rsel[d], cs[q], 0.0), axis=0, keepdims=True)
            xe = (xdt[q] * jnp.exp(cs_last - cs[q])).astype(BF16)
            st_scr[d, p] = st[q] * jnp.exp(cs_last) + _dot_tn(bm[d][group[q]], xe)
        return carry

    lax.fori_loop(0, n_chunks, chunk_body, 0)

    @pl.when(i == pl.num_programs(1) - 1)
    def _():
        sf_ref[...] = st_scr[...]


def _ssd_scan_call(xbc, dtbc, a_rep, s0, tb):
    b, t, _ = xbc.shape
    tb = min(tb, t)
    nb = t // tb
    st = pl.BlockSpec((2, None, N_PAIRS, SSD_STATE, PAIR), lambda bi, i: (0, bi, 0, 0, 0))
    y_sd = jax.ShapeDtypeStruct((b, t, SSD_WIDTH), F32)
    return pl.pallas_call(
        functools.partial(_ssd_scan_kernel, n_chunks=tb // SCAN_CHUNK),
        name="ssd_scan",
        grid=(b, nb),
        in_specs=[pl.BlockSpec((None, tb, SSD_XBC), lambda bi, i: (bi, i, 0)),
                  pl.BlockSpec((None, None, tb, SSD_WIDTH), lambda bi, i: (0, bi, i, 0)),
                  pl.BlockSpec((None, tb, SSD_XBC), lambda bi, i: (bi, nb - 1 - i, 0)),
                  pl.BlockSpec((None, None, tb, SSD_WIDTH), lambda bi, i: (1, bi, nb - 1 - i, 0)),
                  pl.BlockSpec((2, 1, SSD_WIDTH), lambda bi, i: (0, 0, 0)),
                  st],
        out_specs=[pl.BlockSpec((None, tb, SSD_WIDTH), lambda bi, i: (bi, i, 0)),
                   pl.BlockSpec((None, tb, SSD_WIDTH), lambda bi, i: (bi, nb - 1 - i, 0)), st],
        out_shape=[y_sd, y_sd, jax.ShapeDtypeStruct((2, b, N_PAIRS, SSD_STATE, PAIR), F32)],
        scratch_shapes=[pltpu.VMEM((2, N_PAIRS, SSD_STATE, PAIR), F32)],
        compiler_params=_cparams(("parallel", "arbitrary")),
    )(xbc, dtbc, xbc, dtbc, a_rep, s0)


def _even_finish_kernel(yrf_ref, yrb_ref, bonus_ref, gate_ref, ysf_ref, ysb_ref, xs_ref, z_ref, x_ref,
                        m_ref, lnw_ref, lnb_ref, dsk_ref, nw_ref, j_ref, wo_ref, o_ref):
    jm = j_ref[...]
    y = yrf_ref[...] + yrb_ref[...]
    inv_n = 1.0 / RWKV_HEAD_DIM
    mean = _head_sum(y, jm) * inv_n
    yc = y - mean
    var = _head_sum(yc * yc, jm) * inv_n
    y = yc * lax.rsqrt(var + RWKV_GN_EPS) * lnw_ref[...] + lnb_ref[...]
    y_rk = (y + bonus_ref[...].astype(F32)) * gate_ref[...].astype(F32)
    s = ysf_ref[...] + ysb_ref[...] + dsk_ref[...] * xs_ref[...].astype(F32)
    s = s * _silu(z_ref[...].astype(F32))
    s = s * lax.rsqrt(jnp.mean(s * s, axis=-1, keepdims=True) + NORM_EPS) * nw_ref[...]
    out = _dot(y_rk.astype(BF16), wo_ref[0:RWKV_WIDTH, :]) + \
        _dot(s.astype(BF16), wo_ref[RWKV_WIDTH:RWKV_WIDTH + SSD_WIDTH, :])
    o_ref[...] = x_ref[...] + m_ref[2:3, :] * out


def _even_finish_call(yrk, bonus, gate, ysd, xbc, proj, x, mods, p, tm):
    b, t, d = x.shape
    tm = min(tm, t)
    w = RWKV_WIDTH
    per_batch = mods.shape[0] > 1
    full = lambda shape: pl.BlockSpec(shape, lambda bi, i: (0,) * len(shape))
    tok = pl.BlockSpec((None, tm, w), lambda bi, i: (bi, i, 0))
    return pl.pallas_call(
        _even_finish_kernel,
        name="even_finish",
        grid=(b, t // tm),
        in_specs=[tok, tok, tok, tok, tok, tok,
                  pl.BlockSpec((None, tm, SSD_WIDTH), lambda bi, i: (bi, i, 0)),
                  pl.BlockSpec((None, tm, SSD_WIDTH), lambda bi, i: (bi, i, 0)),
                  pl.BlockSpec((None, tm, d), lambda bi, i: (bi, i, 0)),
                  pl.BlockSpec((None, N_MOD, d), (lambda bi, i: (bi, 0, 0)) if per_batch
                               else (lambda bi, i: (0, 0, 0))),
                  full((1, w)), full((1, w)), full((1, w)), full((1, w)),
                  full((V7X_MXU_DIM, V7X_MXU_DIM)),
                  full((2 * w, d))],
        out_specs=pl.BlockSpec((None, tm, d), lambda bi, i: (bi, i, 0)),
        out_shape=jax.ShapeDtypeStruct((b, t, d), F32),
        compiler_params=_cparams(("parallel", "parallel")),
    )(yrk[0], yrk[1], bonus, gate, ysd[0], ysd[1], xbc, proj, x, mods, p['ln_w'], p['ln_b'], p['d_skip'],
      p['norm_w'],
      p['head_sum'], p['w_out'])


def _ret_scan_kernel(q_ref, k_ref, v_ref, lg_ref, s0_ref, y_ref, sf_ref, st_scr, dec_scr, sc_scr,
                     *, n_chunks):
    d = pl.program_id(0)
    i = pl.program_id(2)
    c = SCAN_CHUNK
    dk, dv = RET_QK_DIM, RET_V_DIM
    heads = range(RET_HEADS)
    lg_all = lg_ref[...]

    @pl.when(i == 0)
    def _():
        st_scr[...] = s0_ref[...]
        _, before_eq = _order_masks(d, c)
        row = lax.broadcasted_iota(jnp.int32, (c, c), 0)
        col = lax.broadcasted_iota(jnp.int32, (c, c), 1)
        rel = jnp.abs(row - col).astype(F32)
        pos = (row + d * (c - 1 - 2 * row)).astype(F32)
        for h in heads:
            lg = lg_all[:, h * dk:h * dk + 1]
            dec_scr[h] = jnp.where(before_eq, jnp.exp(rel * lg), 0.0)
            sc_scr[h, 0] = jnp.exp((pos + 1.0) * lg).astype(BF16)
            sc_scr[h, 1] = jnp.exp((c - 1.0 - pos) * lg).astype(BF16)

    def chunk_body(j, carry):
        cj = jnp.where(d == 0, j, n_chunks - 1 - j)
        rows = pl.ds(pl.multiple_of(cj * c, c), c)
        qs = [q_ref[rows, h * dk:(h + 1) * dk] for h in heads]
        ks = [k_ref[rows, h * dk:(h + 1) * dk] for h in heads]
        vs = [v_ref[rows, h * dv:(h + 1) * dv] for h in heads]
        qk = [_dot_nt(qs[h], ks[h]) for h in heads]
        scores = [(qk[h] * dec_scr[h]).astype(BF16) for h in heads]
        st = [st_scr[h] for h in heads]
        y_st = [_dot(qs[h] * sc_scr[h, 0], st[h].astype(BF16)) for h in heads]
        for h in heads:
            y_ref[rows, h * dv:(h + 1) * dv] = (_dot(scores[h], vs[h]) + y_st[h]).astype(y_ref.dtype)
        for h in heads:
            lg = lg_all[:, h * dk:h * dk + 1]
            st_scr[h] = st[h] * jnp.exp(c * lg) + _dot_tn(ks[h] * sc_scr[h, 1], vs[h])
        return carry

    lax.fori_loop(0, n_chunks, chunk_body, 0, unroll=2)

    @pl.when(i == pl.num_programs(2) - 1)
    def _():
        sf_ref[...] = st_scr[...]


def _ret_scan_call(proj, lg_rep, s0, tb):
    b, t, _ = proj.shape
    tb = min(tb, t)
    nb = t // tb
    blk = lambda dd, i: i + dd * (nb - 1 - 2 * i)
    st = pl.BlockSpec((None, None, RET_HEADS, RET_QK_DIM, RET_V_DIM), lambda dd, bi, i: (dd, bi, 0, 0, 0))
    return pl.pallas_call(
        functools.partial(_ret_scan_kernel, n_chunks=tb // SCAN_CHUNK),
        name="ret_scan",
        grid=(2, b, nb),
        in_specs=[pl.BlockSpec((None, tb, RET_QK), lambda dd, bi, i: (bi, blk(dd, i), 0)),
                  pl.BlockSpec((None, tb, RET_QK), lambda dd, bi, i: (bi, blk(dd, i), 1)),
                  pl.BlockSpec((None, tb, RET_V), lambda dd, bi, i: (bi, blk(dd, i), 2 * RET_QK // RET_V)),
                  pl.BlockSpec((None, 1, RET_QK), lambda dd, bi, i: (dd, 0, 0)), st],
        out_specs=[pl.BlockSpec((None, None, tb, RET_V), lambda dd, bi, i: (dd, bi, blk(dd, i), 0)), st],
        out_shape=[jax.ShapeDtypeStruct((2, b, t, RET_V), BF16),
                   jax.ShapeDtypeStruct((2, b, RET_HEADS, RET_QK_DIM, RET_V_DIM), F32)],
        scratch_shapes=[pltpu.VMEM((RET_HEADS, RET_QK_DIM, RET_V_DIM), F32),
                        pltpu.VMEM((RET_HEADS, SCAN_CHUNK, SCAN_CHUNK), F32),
                        pltpu.VMEM((RET_HEADS, 2, SCAN_CHUNK, RET_QK_DIM), BF16)],
        compiler_params=_cparams(("parallel", "parallel", "arbitrary")),
    )(proj, proj, proj, lg_rep, s0)


def _odd_finish_kernel(y_ref, g_ref, x_ref, m_ref, wo_ref, o_ref):
    y = (y_ref[0] + y_ref[1]).astype(F32)
    dv = RET_V_DIM
    parts = []
    for h in range(RET_HEADS):
        yh = y[:, h * dv:(h + 1) * dv]
        parts.append(yh * lax.rsqrt(jnp.mean(yh * yh, axis=-1, keepdims=True) + NORM_EPS))
    yn = jnp.concatenate(parts, axis=1)
    act = (_silu(g_ref[...].astype(F32)) * yn).astype(BF16)
    o_ref[...] = x_ref[...] + m_ref[2:3, :] * _dot(act, wo_ref[...])


def _odd_finish_call(y, proj, x, mods, w_out, tm):
    b, t, d = x.shape
    tm = min(tm, t)
    return pl.pallas_call(
        _odd_finish_kernel,
        name="odd_finish",
        grid=(b, t // tm),
        in_specs=[pl.BlockSpec((2, None, tm, RET_V), lambda bi, i: (0, bi, i, 0)),
                  pl.BlockSpec((None, tm, RET_V), lambda bi, i: (bi, i, (2 * RET_QK + RET_V) // RET_V)),
                  pl.BlockSpec((None, tm, d), lambda bi, i: (bi, i, 0)),
                  pl.BlockSpec((None, N_MOD, d), lambda bi, i: (bi, 0, 0)),
                  pl.BlockSpec((RET_V, d), lambda bi, i: (0, 0))],
        out_specs=pl.BlockSpec((None, tm, d), lambda bi, i: (bi, i, 0)),
        out_shape=jax.ShapeDtypeStruct((b, t, d), F32),
        compiler_params=_cparams(("parallel", "parallel")),
    )(y, proj, x, mods, w_out)


def _ffn_kernel(*refs, on_grid, final_norm):
    if on_grid:
        x_ref, xn_ref, g_ref, m_ref, wu_ref, cw_ref, cb_ref, wd_ref = refs[:8]
        rest = refs[8:]
    else:
        x_ref, g_ref, m_ref, wu_ref, cw_ref, cb_ref, wd_ref = refs[:7]
        rest = refs[7:]
    if final_norm:
        fg_ref, o_ref, *scrs = rest
    else:
        o_ref, *scrs = rest
    gate_scr, val_scr, *scrs = scrs
    if on_grid:
        gtop_scr, *act_scrs = scrs
    else:
        act_scrs = scrs
    i = pl.program_id(1)
    n_tiles = pl.num_programs(1)
    tm = x_ref.shape[0]

    def norm_mod(xv):
        hv = xv * lax.rsqrt(jnp.mean(xv * xv, axis=-1, keepdims=True) + NORM_EPS) * g_ref[...]
        return (hv * (1.0 + m_ref[4:5, :]) + m_ref[3:4, :]).astype(BF16)

    x = x_ref[...]
    h = norm_mod(x)
    row = lax.broadcasted_iota(jnp.int32, (tm, 1), 0)
    if on_grid:
        col = row % GRID_W
        ok_left = col > 0
        ok_right = col < GRID_W - 1
        zero = jnp.zeros((GRID_W, x.shape[1]), BF16)
        h_ext = jnp.concatenate([h, jnp.where(i < n_tiles - 1, norm_mod(xn_ref[...]), zero)], axis=0)

        @pl.when(i == 0)
        def _():
            gtop_scr[...] = jnp.zeros_like(gtop_scr)
    else:
        ok_left = row > 0
        ok_right = row < tm - 1
    n_chunks = D_FF // FFN_COL_CHUNK

    def up_proj(j):
        cols = slice(j * FFN_COL_CHUNK, (j + 1) * FFN_COL_CHUNK)
        vcols = slice(D_FF + j * FFN_COL_CHUNK, D_FF + (j + 1) * FFN_COL_CHUNK)
        if on_grid:
            gate_scr[j % 2, GRID_W:, :] = _dot(h_ext, wu_ref[:, cols])
        else:
            gate_scr[j % 2] = _dot(h, wu_ref[:, cols])
        val_scr[j % 2] = _dot(h, wu_ref[:, vcols])

    out = None
    up_proj(0)
    for j in range(n_chunks):
        cols = slice(j * FFN_COL_CHUNK, (j + 1) * FFN_COL_CHUNK)
        buf = j % 2
        if j + 1 < n_chunks:
            up_proj(j + 1)
        grp, slot = divmod(j, FFN_DOWN_GROUP)
        act_scr = act_scrs[grp]
        acols = slice(slot * FFN_COL_CHUNK, (slot + 1) * FFN_COL_CHUNK)
        if on_grid:
            gate_scr[buf, 0:GRID_W, :] = gtop_scr[:, cols]
            gtop_scr[:, cols] = gate_scr[buf, tm:tm + GRID_W, :]
            rows3 = [gate_scr[buf, dr * GRID_W:dr * GRID_W + tm, :] for dr in range(3)]
            taps = [rows3[0] * cw_ref[dc:dc + 1, cols] + rows3[1] * cw_ref[3 + dc:4 + dc, cols]
                    + rows3[2] * cw_ref[6 + dc:7 + dc, cols] for dc in range(3)]
        else:
            gate = gate_scr[buf]
            taps = [gate * cw_ref[3 + dc:4 + dc, cols] for dc in range(3)]
        acc = cb_ref[:, cols] + taps[1] + jnp.where(ok_left, pltpu.roll(taps[0], 1, axis=0), 0.0) \
            + jnp.where(ok_right, pltpu.roll(taps[2], tm - 1, axis=0), 0.0)
        act_scr[:, acols] = (_gelu_tanh(acc) * val_scr[buf]).astype(BF16)
        if slot + 1 == FFN_DOWN_GROUP or j + 1 == n_chunks:
            width = (slot + 1) * FFN_COL_CHUNK
            k0 = grp * FFN_DOWN_GROUP * FFN_COL_CHUNK
            part = _dot(act_scr[:, 0:width], wd_ref[k0:k0 + width, :])
            out = part if out is None else out + part
    out = x + m_ref[5:6, :] * out
    if final_norm:
        out = out * lax.rsqrt(jnp.mean(out * out, axis=-1, keepdims=True) + NORM_EPS) * fg_ref[...]
    o_ref[...] = out


def _ffn_call(x, norm_g, mods, w_up, conv_w9, conv_b, w_down, *, tm, on_grid, final_g=None):
    b, t, d = x.shape
    tm = min(tm, t)
    n_tiles = t // tm
    per_batch = mods.shape[0] > 1
    full = lambda shape: pl.BlockSpec(shape, lambda bi, i: (0,) * len(shape))
    in_specs = [pl.BlockSpec((None, tm, d), lambda bi, i: (bi, i, 0))]
    args = [x]
    scratch = [pltpu.VMEM((tm, FFN_DOWN_GROUP * FFN_COL_CHUNK), BF16)
               for _ in range(-(-D_FF // (FFN_DOWN_GROUP * FFN_COL_CHUNK)))]
    if on_grid:
        r = tm // GRID_W
        last = t // GRID_W - 1
        in_specs.append(pl.BlockSpec((None, GRID_W, d), lambda bi, i: (bi, jnp.minimum((i + 1) * r, last), 0)))
        args.append(x)
        scratch.insert(0, pltpu.VMEM((GRID_W, D_FF), F32))
    else:
        assert n_tiles == 1
    gate_rows = tm + 2 * GRID_W if on_grid else tm
    scratch = [pltpu.VMEM((2, gate_rows, FFN_COL_CHUNK), F32),
               pltpu.VMEM((2, tm, FFN_COL_CHUNK), F32)] + scratch
    in_specs += [full((1, d)),
                 pl.BlockSpec((None, N_MOD, d), (lambda bi, i: (bi, 0, 0)) if per_batch
                              else (lambda bi, i: (0, 0, 0))),
                 full((d, 2 * D_FF)), full((9, D_FF)), full((1, D_FF)), full((D_FF, d))]
    args += [norm_g, mods, w_up, conv_w9, conv_b, w_down]
    if final_g is not None:
        in_specs.append(full((1, d)))
        args.append(final_g)
    return pl.pallas_call(
        functools.partial(_ffn_kernel, on_grid=on_grid, final_norm=final_g is not None),
        name="conv_ffn",
        grid=(b, n_tiles),
        in_specs=in_specs,
        out_specs=pl.BlockSpec((None, tm, d), lambda bi, i: (bi, i, 0)),
        out_shape=jax.ShapeDtypeStruct((b, t, d), F32),
        scratch_shapes=scratch,
        compiler_params=_cparams(("parallel", "arbitrary")),
    )(*args)


def _block_diag2(a, b):
    za = jnp.zeros((a.shape[0], b.shape[1]), a.dtype)
    zb = jnp.zeros((b.shape[0], a.shape[1]), a.dtype)
    return jnp.concatenate([jnp.concatenate([a, za], axis=1), jnp.concatenate([zb, b], axis=1)], axis=0)


def _pad_cols(a, n, fill=0.0):
    return jnp.pad(a, ((0, 0), (0, n - a.shape[1])), constant_values=fill)


def _even_params(j, ev_w_in, ev_mu_prev, ev_mu_next, rk_w0_f, rk_w0_b, rk_w2_f, rk_w2_b, rk_a0_f,
                 rk_a0_b, rk_a2_f, rk_a2_b, rk_g2, rk_k_k, rk_k_a, rk_r_k, rk_ln_w, rk_ln_b,
                 ssd_conv_w, ssd_conv_b, ssd_dt_bias_f, ssd_dt_bias_b, ssd_a_log_f, ssd_a_log_b,
                 ssd_d, ssd_norm_w, ev_w_out):
    w_in = ev_w_in[j]
    rw = w_in[:, :RWKV_COLS]
    z = w_in[:, RWKV_COLS:RWKV_COLS + SSD_WIDTH]
    xbc = w_in[:, RWKV_COLS + SSD_WIDTH:RWKV_COLS + SSD_WIDTH + SSD_XBC]
    dts = w_in[:, RWKV_COLS + SSD_WIDTH + SSD_XBC:]
    w_packed = jnp.concatenate([_pad_cols(jnp.concatenate([rw, dts], axis=1), EV_RW_BLOCK), xbc, z], axis=1)
    head = jnp.arange(V7X_MXU_DIM) // RWKV_HEAD_DIM
    head_sum = (head[:, None] == head[None, :]).astype(BF16)
    lane = jnp.arange(V7X_LANES)[:, None]
    tgt = jnp.arange(2 * SSD_WIDTH)[None, :]
    dt_expand = (lane == (tgt // SSD_WIDTH) * SSD_HEADS + (tgt % SSD_WIDTH) // SSD_HEAD_DIM).astype(BF16)
    rep = lambda a: jnp.repeat(a, SSD_HEAD_DIM)[None, :]
    row = lambda a: a[None, :]
    return {
        'w_in': w_packed.astype(BF16),
        'mu_self': _pad_cols(row(1.0 - ev_mu_prev[j] - ev_mu_next[j]), EV_RW_BLOCK, 1.0),
        'mu_prev': _pad_cols(row(ev_mu_prev[j]), EV_RW_BLOCK),
        'mu_next': _pad_cols(row(ev_mu_next[j]), EV_RW_BLOCK),
        'w0': row(jnp.concatenate([rk_w0_f[j], rk_w0_b[j]])),
        'w2': _block_diag2(rk_w2_f[j], rk_w2_b[j]).astype(BF16),
        'a0': row(jnp.concatenate([rk_a0_f[j], rk_a0_b[j]])),
        'a2': _block_diag2(rk_a2_f[j], rk_a2_b[j]).astype(BF16),
        'g2': rk_g2[j].astype(BF16),
        'k_k': row(rk_k_k[j]), 'k_a': row(rk_k_a[j]), 'r_k': row(rk_r_k[j].reshape(-1)),
        'ln_w': row(rk_ln_w[j]), 'ln_b': row(rk_ln_b[j]),
        'head_sum': head_sum,
        'conv_w': ssd_conv_w[j], 'conv_b': row(ssd_conv_b[j]),
        'dt_bias': _pad_cols(row(jnp.concatenate([ssd_dt_bias_f[j], ssd_dt_bias_b[j]])), V7X_LANES),
        'dt_expand': dt_expand,
        'a_rep': jnp.stack([rep(-jnp.exp(ssd_a_log_f[j])), rep(-jnp.exp(ssd_a_log_b[j]))]),
        'd_skip': rep(ssd_d[j]),
        'norm_w': row(ssd_norm_w[j]),
        'w_out': ev_w_out[j].astype(BF16),
    }


def _rope_tables(t):
    n = RET_QK_DIM // 4
    pos = jnp.arange(t)
    row = (pos // GRID_W).astype(F32)
    col = (pos % GRID_W).astype(F32)
    inv = ROPE_BASE ** (-jnp.arange(n, dtype=F32) / n)
    ang = jnp.concatenate([row[:, None] * inv, col[:, None] * inv], axis=-1)
    cos, sin = jnp.cos(ang), jnp.sin(ang)
    return jnp.concatenate([cos, cos], axis=-1), jnp.concatenate([-sin, sin], axis=-1)


def _conv_ffn(x, mods, norm_g, w_up, conv_w9, conv_b, w_down, *, on_grid, final_g=None):
    return _ffn_call(x, norm_g, mods, w_up, conv_w9, conv_b, w_down, tm=512, on_grid=on_grid,
                     final_g=final_g)


def _even_layer(x, ctx, mods_x, mods_c, norm_g, p):
    b = x.shape[0]

    def features(h, mods):
        *feat, xbc, dtbc, z = _even_feat_call(h, norm_g, mods, p, 256)
        return z, feat, xbc, dtbc

    proj_c, feat_c, xbc_c, dt_c = features(ctx, mods_c)
    proj_x, feat_x, xbc_x, dt_x = features(x, mods_x)
    s0 = jnp.zeros((2, b, N_PAIRS, PAIR, PAIR), F32)
    *yrk_c, s_ctx = _rwkv_scan_call(feat_c, s0, 256)
    *yrk_x, _ = _rwkv_scan_call(feat_x, s_ctx, 512)
    h0 = jnp.zeros((2, b, N_PAIRS, SSD_STATE, PAIR), F32)
    *ysd_c, h_ctx = _ssd_scan_call(xbc_c, dt_c, p['a_rep'], h0, 256)
    *ysd_x, _ = _ssd_scan_call(xbc_x, dt_x, p['a_rep'], h_ctx, 512)
    x = _even_finish_call(yrk_x, feat_x[3], feat_x[4], ysd_x, xbc_x, proj_x, x, mods_x, p, 512)
    ctx = _even_finish_call(yrk_c, feat_c[3], feat_c[4], ysd_c, xbc_c, proj_c, ctx, mods_c, p, 256)
    return x, ctx


def _odd_layer(x, ctx, mods_x, mods_c, norm_g, w_in, lg_rep, w_out):
    b, t, _ = x.shape
    proj_c = _nm_call(ctx, norm_g, mods_c, w_in, shift_row=0, tm=256, tn=512, out_dtype=BF16,
                      qk_mode='scale')
    proj_x = _nm_call(x, norm_g, mods_x, w_in, shift_row=0, tm=512, tn=512, out_dtype=BF16,
                      qk_mode='rope', rope=_rope_tables(t))
    s0 = jnp.zeros((2, b, RET_HEADS, RET_QK_DIM, RET_V_DIM), F32)
    _, s_ctx = _ret_scan_call(proj_c, lg_rep, s0, 256)
    y, _ = _ret_scan_call(proj_x, lg_rep, s_ctx, 512)
    return _odd_finish_call(y, proj_x, x, mods_x, w_out, 512)


def kernel(x, c, ctx, c_ctx, mod_w, mod_b, norm1_g, norm2_g, ffn_w_up, ffn_conv_w, ffn_conv_b, ffn_w_down, ev_w_in, ev_mu_prev, ev_mu_next, rk_w0_f, rk_w0_b, rk_w2_f, rk_w2_b, rk_a0_f, rk_a0_b, rk_a2_f, rk_a2_b, rk_g2, rk_k_k, rk_k_a, rk_r_k, rk_ln_w, rk_ln_b, ssd_conv_w, ssd_conv_b, ssd_dt_bias_f, ssd_dt_bias_b, ssd_a_log_f, ssd_a_log_b, ssd_d, ssd_norm_w, ev_w_out, ret_w_in, ret_log2_f, ret_log2_b, ret_w_out, final_norm_g):
    b, t, d = x.shape
    depth = mod_w.shape[0]
    rows = -(-(b + 1) // V7X_SUBLANES) * V7X_SUBLANES
    cond = jnp.concatenate([c, c_ctx[None, :], jnp.zeros((rows - b - 1, d), F32)], axis=0)
    mods = _mod_call(cond, mod_w, mod_b).reshape(depth, rows, N_MOD, d)
    for i in range(depth):
        need_ctx = i < depth - 1
        mods_x = mods[i, :b]
        mods_c = mods[i, b:b + 1]
        j = i // 2
        g1 = norm1_g[i][None, :]
        if i % 2 == 0:
            p = _even_params(j, ev_w_in, ev_mu_prev, ev_mu_next, rk_w0_f, rk_w0_b, rk_w2_f, rk_w2_b,
                             rk_a0_f, rk_a0_b, rk_a2_f, rk_a2_b, rk_g2, rk_k_k, rk_k_a, rk_r_k,
                             rk_ln_w, rk_ln_b, ssd_conv_w, ssd_conv_b, ssd_dt_bias_f, ssd_dt_bias_b,
                             ssd_a_log_f, ssd_a_log_b, ssd_d, ssd_norm_w, ev_w_out)
            x, ctx_mixed = _even_layer(x, ctx, mods_x, mods_c, g1, p)
        else:
            lg = jnp.stack([jnp.log1p(-jnp.exp2(-ret_log2_f[j])), jnp.log1p(-jnp.exp2(-ret_log2_b[j]))])
            lg_rep = jnp.repeat(lg, RET_QK_DIM, axis=-1)[:, None, :]
            x = _odd_layer(x, ctx, mods_x, mods_c, g1, ret_w_in[j].astype(BF16), lg_rep,
                           ret_w_out[j].astype(BF16))
            ctx_mixed = None
        g2 = norm2_g[i][None, :]
        w_up = ffn_w_up[i].astype(BF16)
        w_down = ffn_w_down[i].astype(BF16)
        conv_w9 = ffn_conv_w[i].reshape(9, D_FF)
        conv_b = ffn_conv_b[i][None, :]
        last = i == depth - 1
        x = _conv_ffn(x, mods_x, g2, w_up, conv_w9, conv_b, w_down, on_grid=True,
                      final_g=final_norm_g[None, :] if last else None)
        if need_ctx:
            ctx = _conv_ffn(ctx_mixed, mods_c, g2, w_up, conv_w9, conv_b, w_down, on_grid=False)
    return x
```

```python
import functools
import math

import jax
import jax.numpy as jnp
from jax import lax
from jax.experimental import pallas as pl
from jax.experimental.pallas import tpu as pltpu

F32 = jnp.float32
BF16 = jnp.bfloat16

D_MODEL = 1024
GRID_W = 64
N_MOD = 6
NORM_EPS = 1e-6
RWKV_HEADS = 8
RWKV_HEAD_DIM = 64
RWKV_WIDTH = RWKV_HEADS * RWKV_HEAD_DIM
DECAY_LORA = 64
ICLR_LORA = 64
GATE_LORA = 128
RWKV_GN_EPS = 64e-5
RWKV_COLS = 3 * RWKV_WIDTH + 2 * DECAY_LORA + 2 * ICLR_LORA + GATE_LORA
SSD_HEADS = 8
SSD_HEAD_DIM = 64
SSD_WIDTH = SSD_HEADS * SSD_HEAD_DIM
SSD_GROUPS = 2
SSD_STATE = 128
SSD_XBC = SSD_WIDTH + 2 * SSD_GROUPS * SSD_STATE
RET_HEADS = 8
RET_QK_DIM = 128
RET_V_DIM = 256
RET_QK = RET_HEADS * RET_QK_DIM
RET_V = RET_HEADS * RET_V_DIM
ROPE_BASE = 10000.0
D_FF = 2816

V7X_LANES = 128
V7X_SUBLANES = 8
V7X_MXU_DIM = 256
V7X_VMEM_LIMIT_BYTES = 56 * 1024 * 1024

RWKV_CHUNK = 64
RWKV_CHUNKS_PER_STEP = 2
SCAN_CHUNK = 128
FFN_COL_CHUNK = 256
FFN_DOWN_GROUP = 4
PAIR = 2 * RWKV_HEAD_DIM
N_PAIRS = RWKV_HEADS // 2
EV_RW_BLOCK = 2048
EV_DT_OFF = RWKV_COLS
EV_XBC_OFF = EV_RW_BLOCK
EV_Z_OFF = EV_RW_BLOCK + SSD_XBC
EV_COLS = EV_Z_OFF + SSD_WIDTH


def _cparams(sem):
    return pltpu.CompilerParams(dimension_semantics=sem, vmem_limit_bytes=V7X_VMEM_LIMIT_BYTES)


def _split3(x):
    hi = x.astype(BF16)
    r1 = x - hi.astype(F32)
    mid = r1.astype(BF16)
    lo = (r1 - mid.astype(F32)).astype(BF16)
    return hi, mid, lo


def _dot(a, b):
    return jnp.dot(a, b, preferred_element_type=F32)


def _dot_nt(a, b):
    return lax.dot_general(a, b, (((1,), (1,)), ((), ())), preferred_element_type=F32)


def _dot_tn(a, b):
    return lax.dot_general(a, b, (((0,), (0,)), ((), ())), preferred_element_type=F32)


def _dot01(x, m01):
    hi, mid, lo = _split3(x)
    return _dot(hi, m01) + _dot(mid, m01) + _dot(lo, m01)


def _head_sum(x, j01):
    hi = x.astype(BF16)
    lo = (x - hi.astype(F32)).astype(BF16)
    n = j01.shape[0]
    return jnp.concatenate([_dot(hi[:, g * n:(g + 1) * n], j01) + _dot(lo[:, g * n:(g + 1) * n], j01)
                            for g in range(x.shape[1] // n)], axis=1)


def _dot01_left(m01, x):
    hi, mid, lo = _split3(x)
    return _dot(m01, hi) + _dot(m01, mid) + _dot(m01, lo)


def _sigmoid(x):
    return 1.0 / (1.0 + jnp.exp(-x))


def _silu(x):
    return x * _sigmoid(x)


def _softplus(x):
    return jnp.maximum(x, 0.0) + jnp.log1p(jnp.exp(-jnp.abs(x)))


def _gelu_tanh(x):
    c = math.sqrt(2.0 / math.pi)
    half = 0.5 * x
    return half + half * jnp.tanh(x * (c + (0.044715 * c) * (x * x)))


def _order_masks(d, n):
    row = lax.broadcasted_iota(jnp.int32, (n, n), 0)
    col = lax.broadcasted_iota(jnp.int32, (n, n), 1)
    diff = (row - col) * (1 - 2 * d)
    return diff > 0, diff >= 0


def _mod_kernel(c_ref, w_ref, b_ref, o_ref):
    h = _silu(c_ref[...])
    hi, mid, lo = _split3(h)
    w = w_ref[...]
    wh = w.astype(BF16)
    wl = (w - wh.astype(F32)).astype(BF16)
    acc = _dot(hi, wh) + _dot(mid, wh) + _dot(hi, wl)
    o_ref[...] = acc + b_ref[...]


def _mod_call(cond, mod_w, mod_b):
    depth, d, n = mod_w.shape
    rows = cond.shape[0]
    tn = 1024
    return pl.pallas_call(
        _mod_kernel,
        name="adaln_mod",
        grid=(depth, n // tn),
        in_specs=[pl.BlockSpec((rows, d), lambda l, j: (0, 0)),
                  pl.BlockSpec((None, d, tn), lambda l, j: (l, 0, j)),
                  pl.BlockSpec((None, 1, tn), lambda l, j: (l, 0, j))],
        out_specs=pl.BlockSpec((None, rows, tn), lambda l, j: (l, 0, j)),
        out_shape=jax.ShapeDtypeStruct((depth, rows, n), F32),
        compiler_params=_cparams(("parallel", "parallel")),
    )(cond, mod_w, mod_b.reshape(depth, 1, n))


def _nm_kernel(*refs, shift_row, tn, qk_mode):
    if qk_mode == 'rope':
        x_ref, g_ref, m_ref, w_ref, cos_ref, sin_ref, o_ref = refs
    else:
        x_ref, g_ref, m_ref, w_ref, o_ref = refs
    x = x_ref[...]
    h = x * lax.rsqrt(jnp.mean(x * x, axis=-1, keepdims=True) + NORM_EPS) * g_ref[...]
    h = h * (1.0 + m_ref[shift_row + 1:shift_row + 2, :]) + m_ref[shift_row:shift_row + 1, :]
    h = h.astype(BF16)
    dk = RET_QK_DIM
    for j in range(w_ref.shape[1] // tn):
        cols = slice(j * tn, (j + 1) * tn)
        y = _dot(h, w_ref[:, cols])
        if qk_mode is not None and j * tn < 2 * RET_QK:
            scale = dk ** -0.5 if j * tn >= RET_QK else 1.0
            heads = []
            for hh in range(tn // dk):
                yh = y[:, hh * dk:(hh + 1) * dk]
                if qk_mode == 'rope':
                    yh = yh * cos_ref[...] + pltpu.roll(yh, dk // 2, axis=1) * sin_ref[...]
                heads.append(yh * scale if scale != 1.0 else yh)
            y = jnp.concatenate(heads, axis=1)
        o_ref[:, cols] = y.astype(o_ref.dtype)


def _nm_call(x, g, mods, w, *, shift_row, tm, tn, out_dtype=F32, qk_mode=None, rope=None):
    b, t, d = x.shape
    n = w.shape[1]
    tm = min(tm, t)
    per_batch = mods.shape[0] > 1
    in_specs = [pl.BlockSpec((None, tm, d), lambda bi, i: (bi, i, 0)),
                pl.BlockSpec((1, d), lambda bi, i: (0, 0)),
                pl.BlockSpec((None, N_MOD, d), (lambda bi, i: (bi, 0, 0)) if per_batch
                             else (lambda bi, i: (0, 0, 0))),
                pl.BlockSpec((d, n), lambda bi, i: (0, 0))]
    args = [x, g, mods, w]
    if qk_mode == 'rope':
        tab = pl.BlockSpec((tm, RET_QK_DIM), lambda bi, i: (i, 0))
        in_specs += [tab, tab]
        args += list(rope)
    return pl.pallas_call(
        functools.partial(_nm_kernel, shift_row=shift_row, tn=tn, qk_mode=qk_mode),
        name="norm_mod_matmul",
        grid=(b, t // tm),
        in_specs=in_specs,
        out_specs=pl.BlockSpec((None, tm, n), lambda bi, i: (bi, i, 0)),
        out_shape=jax.ShapeDtypeStruct((b, t, n), out_dtype),
        compiler_params=_cparams(("parallel", "parallel")),
    )(*args)


def _even_feat_kernel(x_ref, xp_ref, xn_ref, g_ref, m_ref, w_ref,
                      mus_ref, mup_ref, mun_ref, w0_ref, w2_ref, a0_ref, a2_ref,
                      g2_ref, kk_ref, ka_ref, rk_ref, j_ref, cw_ref, cb_ref, dtb_ref, e_ref,
                      r_ref, v_ref, kkn_ref, bonus_ref, gate_ref, ld_ref, kd_ref, bd_ref,
                      xbc_ref, dtbc_ref, z_ref):
    i = pl.program_id(1)
    n_tiles = pl.num_programs(1)
    tm = x_ref.shape[0]
    halo = V7X_SUBLANES
    ext = tm + 2 * halo

    def norm_mod(xv):
        hv = xv * lax.rsqrt(jnp.mean(xv * xv, axis=-1, keepdims=True) + NORM_EPS) * g_ref[...]
        return (hv * (1.0 + m_ref[1:2, :]) + m_ref[0:1, :]).astype(BF16)

    h = norm_mod(x_ref[...])
    zero = jnp.zeros((halo, x_ref.shape[1]), BF16)
    h_ext = jnp.concatenate([jnp.where(i > 0, norm_mod(xp_ref[...]), zero), h,
                             jnp.where(i < n_tiles - 1, norm_mod(xn_ref[...]), zero)], axis=0)

    def proj3(cols):
        ye = _dot(h_ext, w_ref[:, cols])
        return (ye[halo:halo + tm], pltpu.roll(ye, 1, axis=0)[halo:halo + tm],
                pltpu.roll(ye, ext - 1, axis=0)[halo:halo + tm])

    w = RWKV_WIDTH
    rw = []
    for j in range(EV_RW_BLOCK // w):
        cols = slice(j * w, (j + 1) * w)
        cur, prev, nxt = proj3(cols)
        rw.append(cur * mus_ref[:, cols] + prev * mup_ref[:, cols] + nxt * mun_ref[:, cols])
    r, k, v, lora = rw
    wd = lora[:, 0:2 * DECAY_LORA]
    ad = lora[:, 2 * DECAY_LORA:2 * DECAY_LORA + 2 * ICLR_LORA]
    gd = lora[:, 2 * DECAY_LORA + 2 * ICLR_LORA:2 * DECAY_LORA + 2 * ICLR_LORA + GATE_LORA]
    dt_raw = lora[:, EV_DT_OFF - 3 * w:EV_RW_BLOCK - 3 * w]
    jm = j_ref[...]

    for j in range(SSD_XBC // w):
        cols = slice(j * w, (j + 1) * w)
        cur, prev, nxt = proj3(slice(EV_XBC_OFF + j * w, EV_XBC_OFF + (j + 1) * w))
        y = prev * cw_ref[0:1, cols] + cur * cw_ref[1:2, cols] + nxt * cw_ref[2:3, cols] + cb_ref[:, cols]
        xbc_ref[:, cols] = _silu(y).astype(xbc_ref.dtype)
    z_ref[...] = _dot(h, w_ref[:, EV_Z_OFF:EV_Z_OFF + SSD_WIDTH]).astype(z_ref.dtype)
    dt = _dot01(_softplus(dt_raw + dtb_ref[...]), e_ref[...])
    dtbc_ref[0] = dt[:, 0:SSD_WIDTH]
    dtbc_ref[1] = dt[:, SSD_WIDTH:2 * SSD_WIDTH]

    kk = k * kk_ref[...]
    ss = _head_sum(kk * kk, jm)
    kk = kk / jnp.maximum(jnp.sqrt(ss), 1e-12)
    r_ref[...] = r.astype(r_ref.dtype)
    v_ref[...] = v.astype(v_ref.dtype)
    kkn_ref[...] = kk.astype(kkn_ref.dtype)
    bonus_ref[...] = (_head_sum(r * k * rk_ref[...], jm) * v).astype(bonus_ref.dtype)
    gate_ref[...] = _dot(_sigmoid(gd).astype(BF16), g2_ref[...]).astype(gate_ref.dtype)

    zw = _dot(jnp.tanh(wd).astype(BF16), w2_ref[...]) + w0_ref[...]
    za = _dot(ad.astype(BF16), a2_ref[...]) + a0_ref[...]
    for di in range(2):
        ld_ref[di] = -math.exp(-0.5) * _sigmoid(zw[:, di * w:(di + 1) * w])
        iclr = _sigmoid(za[:, di * w:(di + 1) * w])
        kd_ref[di] = (k * (1.0 + (iclr - 1.0) * ka_ref[...])).astype(kd_ref.dtype)
        bd_ref[di] = (kk * iclr).astype(bd_ref.dtype)


def _even_feat_call(x, norm_g, mods, p, tm):
    b, t, d = x.shape
    tm = min(tm, t)
    n_tiles = t // tm
    w = RWKV_WIDTH
    per_batch = mods.shape[0] > 1
    r8 = tm // V7X_SUBLANES
    last = t // V7X_SUBLANES - 1
    full = lambda shape: pl.BlockSpec(shape, lambda bi, i: (0,) * len(shape))
    tok = pl.BlockSpec((None, tm, w), lambda bi, i: (bi, i, 0))
    tok2 = pl.BlockSpec((2, None, tm, w), lambda bi, i: (0, bi, i, 0))
    sd = jax.ShapeDtypeStruct((b, t, w), BF16)
    sd2 = jax.ShapeDtypeStruct((2, b, t, w), BF16)
    f2 = jax.ShapeDtypeStruct((2, b, t, w), F32)
    return pl.pallas_call(
        _even_feat_kernel,
        name="even_feat",
        grid=(b, n_tiles),
        in_specs=[pl.BlockSpec((None, tm, d), lambda bi, i: (bi, i, 0)),
                  pl.BlockSpec((None, V7X_SUBLANES, d), lambda bi, i: (bi, jnp.maximum(i * r8 - 1, 0), 0)),
                  pl.BlockSpec((None, V7X_SUBLANES, d), lambda bi, i: (bi, jnp.minimum((i + 1) * r8, last), 0)),
                  full((1, d)),
                  pl.BlockSpec((None, N_MOD, d), (lambda bi, i: (bi, 0, 0)) if per_batch
                               else (lambda bi, i: (0, 0, 0))),
                  full((d, EV_COLS)),
                  full((1, EV_RW_BLOCK)), full((1, EV_RW_BLOCK)), full((1, EV_RW_BLOCK)),
                  full((1, 2 * w)), full((2 * DECAY_LORA, 2 * w)),
                  full((1, 2 * w)), full((2 * ICLR_LORA, 2 * w)),
                  full((GATE_LORA, w)), full((1, w)), full((1, w)), full((1, w)),
                  full((V7X_MXU_DIM, V7X_MXU_DIM)),
                  full((3, SSD_XBC)), full((1, SSD_XBC)), full((1, V7X_LANES)),
                  full((V7X_LANES, 2 * SSD_WIDTH))],
        out_specs=[tok, tok, tok, tok, tok, tok2, tok2, tok2,
                   pl.BlockSpec((None, tm, SSD_XBC), lambda bi, i: (bi, i, 0)), tok2, tok],
        out_shape=[sd, sd, sd, sd, sd, f2, sd2, sd2,
                   jax.ShapeDtypeStruct((b, t, SSD_XBC), BF16), f2, sd],
        compiler_params=_cparams(("parallel", "parallel")),
    )(x, x, x, norm_g, mods, p['w_in'], p['mu_self'], p['mu_prev'], p['mu_next'], p['w0'], p['w2'],
      p['a0'], p['a2'], p['g2'], p['k_k'], p['k_a'], p['r_k'], p['head_sum'],
      p['conv_w'], p['conv_b'], p['dt_bias'], p['dt_expand'])


def _rwkv_scan_kernel(rf_ref, vf_ref, kkf_ref, ldf_ref, kdf_ref, bdf_ref,
                      rb_ref, vb_ref, kkb_ref, ldb_ref, kdb_ref, bdb_ref, s0_ref,
                      yf_ref, yb_ref, sf_ref, st_scr, *, n_chunks):
    i = pl.program_id(1)
    c = RWKV_CHUNK

    @pl.when(i == 0)
    def _():
        st_scr[...] = s0_ref[...]

    rowc = lax.broadcasted_iota(jnp.int32, (c, 2 * c), 0)
    colc = lax.broadcasted_iota(jnp.int32, (c, 2 * c), 1) % c
    eye_wide = (rowc == colc).astype(F32)
    m0 = lax.broadcasted_iota(jnp.int32, (c, PAIR), 1) < RWKV_HEAD_DIM
    row1 = lax.broadcasted_iota(jnp.int32, (c, 1), 0)
    incl01, strict_wide, incl_wide, rsel = [], [], [], []
    for d in range(2):
        sgn = 1 - 2 * d
        incl01.append(_order_masks(d, c)[1].astype(BF16))
        strict_wide.append((rowc - colc) * sgn > 0)
        incl_wide.append((rowc - colc) * sgn >= 0)
        rsel.append(row1 == (c - 1 if d == 0 else 0))
    refs = ((rf_ref, vf_ref, kkf_ref, ldf_ref, kdf_ref, bdf_ref, yf_ref),
            (rb_ref, vb_ref, kkb_ref, ldb_ref, kdb_ref, bdb_ref, yb_ref))
    chains = [(d, p) for d in range(2) for p in range(N_PAIRS)]
    lanes = [slice(p * PAIR, (p + 1) * PAIR) for p in range(N_PAIRS)]

    def stack(x):
        return jnp.concatenate([jnp.where(m0, x, 0.0), jnp.where(m0, 0.0, x)], axis=0)

    sub = range(RWKV_CHUNKS_PER_STEP)
    items = [(u, d, p) for u in sub for d, p in chains]

    def chunk_body(j, carry):
        def rows_of(u, d):
            cj = j * len(sub) + u
            return pl.ds(pl.multiple_of((cj if d == 0 else n_chunks - 1 - cj) * c, c), c)

        rows = {(u, d): rows_of(u, d) for u in sub for d in range(2)}
        cs_all = {ud: _dot01_left(incl01[ud[1]], refs[ud[1]][3][rows[ud], :]) for ud in rows}
        a_s, r_t, b_s, k_s, v_s, b_end, k_end, g_tot = [], [], [], [], [], [], [], []
        for u, d, p in items:
            r_ref, v_ref, kk_ref, ld_ref, kd_ref, bd_ref, _ = refs[d]
            rw, ln = rows[u, d], lanes[p]
            cs = cs_all[u, d][:, ln]
            cs_last = jnp.sum(jnp.where(rsel[d], cs, 0.0), axis=0, keepdims=True)
            g_neg = jnp.exp(-cs)
            g_end = jnp.exp(cs_last - cs)
            g_tot.append(jnp.exp(cs_last))
            kdv = kd_ref[rw, ln].astype(F32)
            bdv = bd_ref[rw, ln].astype(F32)
            a_w = (-kk_ref[rw, ln].astype(F32) * jnp.exp(cs - ld_ref[rw, ln]))
            a_s.append((a_w.astype(BF16), stack(a_w).astype(BF16)))
            r_t.append((r_ref[rw, ln].astype(F32) * jnp.exp(cs)).astype(BF16))
            b_s.append(stack(bdv * g_neg).astype(BF16))
            k_s.append(stack(kdv * g_neg).astype(BF16))
            v_s.append(stack(v_ref[rw, ln].astype(F32)).astype(BF16))
            b_end.append(stack(bdv * g_end).astype(BF16))
            k_end.append(stack(kdv * g_end).astype(BF16))
        n = range(len(items))
        dirs = [d for _, d, _ in items]
        gram = [_dot_nt(jnp.concatenate([a_s[q][0], r_t[q]], axis=0),
                        jnp.concatenate([b_s[q], k_s[q]], axis=0)) for q in n]
        a_ab = [jnp.where(strict_wide[dirs[q]], gram[q][0:c, 0:2 * c], 0.0) for q in n]
        a_ak = [jnp.where(strict_wide[dirs[q]], gram[q][0:c, 2 * c:4 * c], 0.0).astype(BF16) for q in n]
        p_rb = [jnp.where(incl_wide[dirs[q]], gram[q][c:2 * c, 0:2 * c], 0.0).astype(BF16) for q in n]
        p_rk = [jnp.where(incl_wide[dirs[q]], gram[q][c:2 * c, 2 * c:4 * c], 0.0).astype(BF16) for q in n]
        minv = [eye_wide + a_ab[q] for q in n]
        pw = [a_ab[q] for q in n]
        pw = [_dot(pw[q].astype(BF16), stack(pw[q]).astype(BF16)) for q in n]
        levels = int(math.log2(c))
        for k in range(2, levels):
            both = [_dot(pw[q].astype(BF16),
                         jnp.concatenate([stack(pw[q]), stack(minv[q])], axis=1).astype(BF16)) for q in n]
            minv = [minv[q] + both[q][:, 2 * c:4 * c] for q in n]
            pw = [both[q][:, 0:2 * c] for q in n]
        minv = [minv[q] + _dot(pw[q].astype(BF16), stack(minv[q]).astype(BF16)) for q in n]
        akv = [_dot(a_ak[q], v_s[q]) for q in n]
        eff = [_dot(minv[q].astype(BF16),
                    jnp.concatenate([a_s[q][1], stack(akv[q]).astype(BF16)], axis=1)) for q in n]
        bk_end = [jnp.concatenate([b_end[q], k_end[q]], axis=0) for q in n]
        p_both = [jnp.concatenate([p_rb[q], p_rk[q]], axis=1) for q in n]
        st = [st_scr[d, p] for d, p in chains]
        nc = range(len(chains))
        for u in sub:
            q0 = u * len(chains)
            fs = [_dot_nt(jnp.concatenate([eff[q0 + m][:, 0:PAIR].astype(BF16), r_t[q0 + m]], axis=0),
                          st[m].astype(BF16)) for m in nc]
            uv = [jnp.concatenate([stack(fs[m][0:c] + eff[q0 + m][:, PAIR:2 * PAIR]).astype(BF16),
                                   v_s[q0 + m]], axis=0) for m in nc]
            st = [st[m] * g_tot[q0 + m] + _dot_tn(uv[m], bk_end[q0 + m]) for m in nc]
            for m, (d, p) in enumerate(chains):
                refs[d][6][rows[u, d], lanes[p]] = fs[m][c:2 * c] + _dot(p_both[q0 + m], uv[m])
        for m, (d, p) in enumerate(chains):
            st_scr[d, p] = st[m]
        return carry

    lax.fori_loop(0, n_chunks // RWKV_CHUNKS_PER_STEP, chunk_body, 0)

    @pl.when(i == pl.num_programs(1) - 1)
    def _():
        sf_ref[...] = st_scr[...]


def _rwkv_scan_call(feat, s0, tb):
    r, v, kk, _, _, ld, kd, bd = feat
    b, t, w = r.shape
    tb = min(tb, t)
    nb = t // tb
    tok_f = pl.BlockSpec((None, tb, w), lambda bi, i: (bi, i, 0))
    tok_b = pl.BlockSpec((None, tb, w), lambda bi, i: (bi, nb - 1 - i, 0))
    dir_f = pl.BlockSpec((None, None, tb, w), lambda bi, i: (0, bi, i, 0))
    dir_b = pl.BlockSpec((None, None, tb, w), lambda bi, i: (1, bi, nb - 1 - i, 0))
    st = pl.BlockSpec((2, None, N_PAIRS, PAIR, PAIR), lambda bi, i: (0, bi, 0, 0, 0))
    y_sd = jax.ShapeDtypeStruct((b, t, w), F32)
    return pl.pallas_call(
        functools.partial(_rwkv_scan_kernel, n_chunks=tb // RWKV_CHUNK),
        name="rwkv_scan",
        grid=(b, nb),
        in_specs=[tok_f, tok_f, tok_f, dir_f, dir_f, dir_f, tok_b, tok_b, tok_b, dir_b, dir_b, dir_b, st],
        out_specs=[tok_f, tok_b, st],
        out_shape=[y_sd, y_sd, jax.ShapeDtypeStruct((2, b, N_PAIRS, PAIR, PAIR), F32)],
        scratch_shapes=[pltpu.VMEM((2, N_PAIRS, PAIR, PAIR), F32)],
        compiler_params=_cparams(("parallel", "arbitrary")),
    )(r, v, kk, ld, kd, bd, r, v, kk, ld, kd, bd, s0)


def _ssd_scan_kernel(xf_ref, dtf_ref, xb_ref, dtb_ref, a_ref, s0_ref, yf_ref, yb_ref, sf_ref, st_scr,
                     *, n_chunks):
    i = pl.program_id(1)
    c = SCAN_CHUNK
    hd = SSD_HEAD_DIM

    @pl.when(i == 0)
    def _():
        st_scr[...] = s0_ref[...]

    row1 = lax.broadcasted_iota(jnp.int32, (c, 1), 0)
    m0 = lax.broadcasted_iota(jnp.int32, (c, PAIR), 1) < hd
    before_eq = [_order_masks(d, c)[1] for d in range(2)]
    incl01 = [before_eq[d].astype(BF16) for d in range(2)]
    rsel = [row1 == (c - 1 if d == 0 else 0) for d in range(2)]
    refs = ((xf_ref, dtf_ref, yf_ref), (xb_ref, dtb_ref, yb_ref))
    chains = [(d, p) for d in range(2) for p in range(N_PAIRS)]
    n = range(len(chains))
    group = [p // (N_PAIRS // SSD_GROUPS) for _, p in chains]
    lanes = [slice(p * PAIR, (p + 1) * PAIR) for _, p in chains]

    def chunk_body(j, carry):
        rows = (pl.ds(pl.multiple_of(j * c, c), c),
                pl.ds(pl.multiple_of((n_chunks - 1 - j) * c, c), c))
        dt = [refs[d][1][rows[d], :] for d in range(2)]
        cs_all = [_dot01_left(incl01[d], dt[d] * a_ref[d]) for d in range(2)]
        bm = [[refs[d][0][rows[d], SSD_WIDTH + g * SSD_STATE:SSD_WIDTH + (g + 1) * SSD_STATE]
               for g in range(SSD_GROUPS)] for d in range(2)]
        cm = [[refs[d][0][rows[d], SSD_WIDTH + (SSD_GROUPS + g) * SSD_STATE:
                          SSD_WIDTH + (SSD_GROUPS + g + 1) * SSD_STATE]
               for g in range(SSD_GROUPS)] for d in range(2)]
        cb = [[_dot_nt(cm[d][g], bm[d][g]) for g in range(SSD_GROUPS)] for d in range(2)]
        st = [st_scr[d, p] for d, p in chains]
        y_st = [_dot(cm[chains[q][0]][group[q]], st[q].astype(BF16)) for q in n]
        cs = [cs_all[chains[q][0]][:, lanes[q]] for q in n]
        xdt = [refs[chains[q][0]][0][rows[chains[q][0]], lanes[q]].astype(F32) * dt[chains[q][0]][:, lanes[q]]
               for q in n]
        probs = []
        for q in n:
            d = chains[q][0]
            cs_t = cs[q].T
            both = []
            for hh in range(2):
                col = cs[q][:, hh * hd:hh * hd + 1]
                rowv = cs_t[hh * hd:hh * hd + 1, :]
                dec = jnp.exp(jnp.where(before_eq[d], col - rowv, -jnp.inf))
                both.append((cb[d][group[q]] * dec).astype(BF16))
            probs.append(jnp.concatenate(both, axis=1))
        xs2 = [jnp.concatenate([jnp.where(m0, xdt[q], 0.0), jnp.where(m0, 0.0, xdt[q])],
                               axis=0).astype(BF16) for q in n]
        y_in = [_dot(probs[q], xs2[q]) for q in n]
        for q in n:
            d = chains[q][0]
            refs[d][2][rows[d], lanes[q]] = y_in[q] + jnp.exp(cs[q]) * y_st[q]
        for q, (d, p) in enumerate(chains):
            cs_last = jnp.sum(jnp.where(rsel[d], cs[q], 0.0), axis=0, keepdims=True)
            xe = (xdt[q] * jnp.exp(cs_last - cs[q])).astype(BF16)
            st_scr[d, p] = st[q] * jnp.exp(cs_last) + _dot_tn(bm[d][group[q]], xe)
        return carry

    lax.fori_loop(0, n_chunks, chunk_body, 0)

    @pl.when(i == pl.num_programs(1) - 1)
    def _():
        sf_ref[...] = st_scr[...]


def _ssd_scan_call(xbc, dtbc, a_rep, s0, tb):
    b, t, _ = xbc.shape
    tb = min(tb, t)
    nb = t // tb
    st = pl.BlockSpec((2, None, N_PAIRS, SSD_STATE, PAIR), lambda bi, i: (0, bi, 0, 0, 0))
    y_sd = jax.ShapeDtypeStruct((b, t, SSD_WIDTH), F32)
    return pl.pallas_call(
        functools.partial(_ssd_scan_kernel, n_chunks=tb // SCAN_CHUNK),
        name="ssd_scan",
        grid=(b, nb),
        in_specs=[pl.BlockSpec((None, tb, SSD_XBC), lambda bi, i: (bi, i, 0)),
                  pl.BlockSpec((None, None, tb, SSD_WIDTH), lambda bi, i: (0, bi, i, 0)),
                  pl.BlockSpec((None, tb, SSD_XBC), lambda bi, i: (bi, nb - 1 - i, 0)),
                  pl.BlockSpec((None, None, tb, SSD_WIDTH), lambda bi, i: (1, bi, nb - 1 - i, 0)),
                  pl.BlockSpec((2, 1, SSD_WIDTH), lambda bi, i: (0, 0, 0)),
                  st],
        out_specs=[pl.BlockSpec((None, tb, SSD_WIDTH), lambda bi, i: (bi, i, 0)),
                   pl.BlockSpec((None, tb, SSD_WIDTH), lambda bi, i: (bi, nb - 1 - i, 0)), st],
        out_shape=[y_sd, y_sd, jax.ShapeDtypeStruct((2, b, N_PAIRS, SSD_STATE, PAIR), F32)],
        scratch_shapes=[pltpu.VMEM((2, N_PAIRS, SSD_STATE, PAIR), F32)],
        compiler_params=_cparams(("parallel", "arbitrary")),
    )(xbc, dtbc, xbc, dtbc, a_rep, s0)


def _even_finish_kernel(yrf_ref, yrb_ref, bonus_ref, gate_ref, ysf_ref, ysb_ref, xs_ref, z_ref, x_ref,
                        m_ref, lnw_ref, lnb_ref, dsk_ref, nw_ref, j_ref, wo_ref, o_ref):
    jm = j_ref[...]
    y = yrf_ref[...] + yrb_ref[...]
    inv_n = 1.0 / RWKV_HEAD_DIM
    mean = _head_sum(y, jm) * inv_n
    yc = y - mean
    var = _head_sum(yc * yc, jm) * inv_n
    y = yc * lax.rsqrt(var + RWKV_GN_EPS) * lnw_ref[...] + lnb_ref[...]
    y_rk = (y + bonus_ref[...].astype(F32)) * gate_ref[...].astype(F32)
    s = ysf_ref[...] + ysb_ref[...] + dsk_ref[...] * xs_ref[...].astype(F32)
    s = s * _silu(z_ref[...].astype(F32))
    s = s * lax.rsqrt(jnp.mean(s * s, axis=-1, keepdims=True) + NORM_EPS) * nw_ref[...]
    out = _dot(y_rk.astype(BF16), wo_ref[0:RWKV_WIDTH, :]) + \
        _dot(s.astype(BF16), wo_ref[RWKV_WIDTH:RWKV_WIDTH + SSD_WIDTH, :])
    o_ref[...] = x_ref[...] + m_ref[2:3, :] * out


def _even_finish_call(yrk, bonus, gate, ysd, xbc, proj, x, mods, p, tm):
    b, t, d = x.shape
    tm = min(tm, t)
    w = RWKV_WIDTH
    per_batch = mods.shape[0] > 1
    full = lambda shape: pl.BlockSpec(shape, lambda bi, i: (0,) * len(shape))
    tok = pl.BlockSpec((None, tm, w), lambda bi, i: (bi, i, 0))
    return pl.pallas_call(
        _even_finish_kernel,
        name="even_finish",
        grid=(b, t // tm),
        in_specs=[tok, tok, tok, tok, tok, tok,
                  pl.BlockSpec((None, tm, SSD_WIDTH), lambda bi, i: (bi, i, 0)),
                  pl.BlockSpec((None, tm, SSD_WIDTH), lambda bi, i: (bi, i, 0)),
                  pl.BlockSpec((None, tm, d), lambda bi, i: (bi, i, 0)),
                  pl.BlockSpec((None, N_MOD, d), (lambda bi, i: (bi, 0, 0)) if per_batch
                               else (lambda bi, i: (0, 0, 0))),
                  full((1, w)), full((1, w)), full((1, w)), full((1, w)),
                  full((V7X_MXU_DIM, V7X_MXU_DIM)),
                  full((2 * w, d))],
        out_specs=pl.BlockSpec((None, tm, d), lambda bi, i: (bi, i, 0)),
        out_shape=jax.ShapeDtypeStruct((b, t, d), F32),
        compiler_params=_cparams(("parallel", "parallel")),
    )(yrk[0], yrk[1], bonus, gate, ysd[0], ysd[1], xbc, proj, x, mods, p['ln_w'], p['ln_b'], p['d_skip'],
      p['norm_w'],
      p['head_sum'], p['w_out'])


def _ret_scan_kernel(q_ref, k_ref, v_ref, lg_ref, s0_ref, y_ref, sf_ref, st_scr, dec_scr, sc_scr,
                     *, n_chunks):
    d = pl.program_id(0)
    i = pl.program_id(2)
    c = SCAN_CHUNK
    dk, dv = RET_QK_DIM, RET_V_DIM
    heads = range(RET_HEADS)
    lg_all = lg_ref[...]

    @pl.when(i == 0)
    def _():
        st_scr[...] = s0_ref[...]
        _, before_eq = _order_masks(d, c)
        row = lax.broadcasted_iota(jnp.int32, (c, c), 0)
        col = lax.broadcasted_iota(jnp.int32, (c, c), 1)
        rel = jnp.abs(row - col).astype(F32)
        pos = (row + d * (c - 1 - 2 * row)).astype(F32)
        for h in heads:
            lg = lg_all[:, h * dk:h * dk + 1]
            dec_scr[h] = jnp.where(before_eq, jnp.exp(rel * lg), 0.0)
            sc_scr[h, 0] = jnp.exp((pos + 1.0) * lg).astype(BF16)
            sc_scr[h, 1] = jnp.exp((c - 1.0 - pos) * lg).astype(BF16)

    def chunk_body(j, carry):
        cj = jnp.where(d == 0, j, n_chunks - 1 - j)
        rows = pl.ds(pl.multiple_of(cj * c, c), c)
        qs = [q_ref[rows, h * dk:(h + 1) * dk] for h in heads]
        ks = [k_ref[rows, h * dk:(h + 1) * dk] for h in heads]
        vs = [v_ref[rows, h * dv:(h + 1) * dv] for h in heads]
        qk = [_dot_nt(qs[h], ks[h]) for h in heads]
        scores = [(qk[h] * dec_scr[h]).astype(BF16) for h in heads]
        st = [st_scr[h] for h in heads]
        y_st = [_dot(qs[h] * sc_scr[h, 0], st[h].astype(BF16)) for h in heads]
        for h in heads:
            y_ref[rows, h * dv:(h + 1) * dv] = (_dot(scores[h], vs[h]) + y_st[h]).astype(y_ref.dtype)
        for h in heads:
            lg = lg_all[:, h * dk:h * dk + 1]
            st_scr[h] = st[h] * jnp.exp(c * lg) + _dot_tn(ks[h] * sc_scr[h, 1], vs[h])
        return carry

    lax.fori_loop(0, n_chunks, chunk_body, 0, unroll=2)

    @pl.when(i == pl.num_programs(2) - 1)
    def _():
        sf_ref[...] = st_scr[...]


def _ret_scan_call(proj, lg_rep, s0, tb):
    b, t, _ = proj.shape
    tb = min(tb, t)
    nb = t // tb
    blk = lambda dd, i: i + dd * (nb - 1 - 2 * i)
    st = pl.BlockSpec((None, None, RET_HEADS, RET_QK_DIM, RET_V_DIM), lambda dd, bi, i: (dd, bi, 0, 0, 0))
    return pl.pallas_call(
        functools.partial(_ret_scan_kernel, n_chunks=tb // SCAN_CHUNK),
        name="ret_scan",
        grid=(2, b, nb),
        in_specs=[pl.BlockSpec((None, tb, RET_QK), lambda dd, bi, i: (bi, blk(dd, i), 0)),
                  pl.BlockSpec((None, tb, RET_QK), lambda dd, bi, i: (bi, blk(dd, i), 1)),
                  pl.BlockSpec((None, tb, RET_V), lambda dd, bi, i: (bi, blk(dd, i), 2 * RET_QK // RET_V)),
                  pl.BlockSpec((None, 1, RET_QK), lambda dd, bi, i: (dd, 0, 0)), st],
        out_specs=[pl.BlockSpec((None, None, tb, RET_V), lambda dd, bi, i: (dd, bi, blk(dd, i), 0)), st],
        out_shape=[jax.ShapeDtypeStruct((2, b, t, RET_V), BF16),
                   jax.ShapeDtypeStruct((2, b, RET_HEADS, RET_QK_DIM, RET_V_DIM), F32)],
        scratch_shapes=[pltpu.VMEM((RET_HEADS, RET_QK_DIM, RET_V_DIM), F32),
                        pltpu.VMEM((RET_HEADS, SCAN_CHUNK, SCAN_CHUNK), F32),
                        pltpu.VMEM((RET_HEADS, 2, SCAN_CHUNK, RET_QK_DIM), BF16)],
        compiler_params=_cparams(("parallel", "parallel", "arbitrary")),
    )(proj, proj, proj, lg_rep, s0)


def _odd_finish_kernel(y_ref, g_ref, x_ref, m_ref, wo_ref, o_ref):
    y = (y_ref[0] + y_ref[1]).astype(F32)
    dv = RET_V_DIM
    parts = []
    for h in range(RET_HEADS):
        yh = y[:, h * dv:(h + 1) * dv]
        parts.append(yh * lax.rsqrt(jnp.mean(yh * yh, axis=-1, keepdims=True) + NORM_EPS))
    yn = jnp.concatenate(parts, axis=1)
    act = (_silu(g_ref[...].astype(F32)) * yn).astype(BF16)
    o_ref[...] = x_ref[...] + m_ref[2:3, :] * _dot(act, wo_ref[...])


def _odd_finish_call(y, proj, x, mods, w_out, tm):
    b, t, d = x.shape
    tm = min(tm, t)
    return pl.pallas_call(
        _odd_finish_kernel,
        name="odd_finish",
        grid=(b, t // tm),
        in_specs=[pl.BlockSpec((2, None, tm, RET_V), lambda bi, i: (0, bi, i, 0)),
                  pl.BlockSpec((None, tm, RET_V), lambda bi, i: (bi, i, (2 * RET_QK + RET_V) // RET_V)),
                  pl.BlockSpec((None, tm, d), lambda bi, i: (bi, i, 0)),
                  pl.BlockSpec((None, N_MOD, d), lambda bi, i: (bi, 0, 0)),
                  pl.BlockSpec((RET_V, d), lambda bi, i: (0, 0))],
        out_specs=pl.BlockSpec((None, tm, d), lambda bi, i: (bi, i, 0)),
        out_shape=jax.ShapeDtypeStruct((b, t, d), F32),
        compiler_params=_cparams(("parallel", "parallel")),
    )(y, proj, x, mods, w_out)


def _ffn_kernel(*refs, on_grid, final_norm):
    if on_grid:
        x_ref, xn_ref, g_ref, m_ref, wu_ref, cw_ref, cb_ref, wd_ref = refs[:8]
        rest = refs[8:]
    else:
        x_ref, g_ref, m_ref, wu_ref, cw_ref, cb_ref, wd_ref = refs[:7]
        rest = refs[7:]
    if final_norm:
        fg_ref, o_ref, *scrs = rest
    else:
        o_ref, *scrs = rest
    gate_scr, val_scr, *scrs = scrs
    if on_grid:
        gtop_scr, *act_scrs = scrs
    else:
        act_scrs = scrs
    i = pl.program_id(1)
    n_tiles = pl.num_programs(1)
    tm = x_ref.shape[0]

    def norm_mod(xv):
        hv = xv * lax.rsqrt(jnp.mean(xv * xv, axis=-1, keepdims=True) + NORM_EPS) * g_ref[...]
        return (hv * (1.0 + m_ref[4:5, :]) + m_ref[3:4, :]).astype(BF16)

    x = x_ref[...]
    h = norm_mod(x)
    row = lax.broadcasted_iota(jnp.int32, (tm, 1), 0)
    if on_grid:
        col = row % GRID_W
        ok_left = col > 0
        ok_right = col < GRID_W - 1
        zero = jnp.zeros((GRID_W, x.shape[1]), BF16)
        h_ext = jnp.concatenate([h, jnp.where(i < n_tiles - 1, norm_mod(xn_ref[...]), zero)], axis=0)

        @pl.when(i == 0)
        def _():
            gtop_scr[...] = jnp.zeros_like(gtop_scr)
    else:
        ok_left = row > 0
        ok_right = row < tm - 1
    n_chunks = D_FF // FFN_COL_CHUNK

    def up_proj(j):
        cols = slice(j * FFN_COL_CHUNK, (j + 1) * FFN_COL_CHUNK)
        vcols = slice(D_FF + j * FFN_COL_CHUNK, D_FF + (j + 1) * FFN_COL_CHUNK)
        if on_grid:
            gate_scr[j % 2, GRID_W:, :] = _dot(h_ext, wu_ref[:, cols])
        else:
            gate_scr[j % 2] = _dot(h, wu_ref[:, cols])
        val_scr[j % 2] = _dot(h, wu_ref[:, vcols])

    out = None
    up_proj(0)
    for j in range(n_chunks):
        cols = slice(j * FFN_COL_CHUNK, (j + 1) * FFN_COL_CHUNK)
        buf = j % 2
        if j + 1 < n_chunks:
            up_proj(j + 1)
        grp, slot = divmod(j, FFN_DOWN_GROUP)
        act_scr = act_scrs[grp]
        acols = slice(slot * FFN_COL_CHUNK, (slot + 1) * FFN_COL_CHUNK)
        if on_grid:
            gate_scr[buf, 0:GRID_W, :] = gtop_scr[:, cols]
            gtop_scr[:, cols] = gate_scr[buf, tm:tm + GRID_W, :]
            rows3 = [gate_scr[buf, dr * GRID_W:dr * GRID_W + tm, :] for dr in range(3)]
            taps = [rows3[0] * cw_ref[dc:dc + 1, cols] + rows3[1] * cw_ref[3 + dc:4 + dc, cols]
                    + rows3[2] * cw_ref[6 + dc:7 + dc, cols] for dc in range(3)]
        else:
            gate = gate_scr[buf]
            taps = [gate * cw_ref[3 + dc:4 + dc, cols] for dc in range(3)]
        acc = cb_ref[:, cols] + taps[1] + jnp.where(ok_left, pltpu.roll(taps[0], 1, axis=0), 0.0) \
            + jnp.where(ok_right, pltpu.roll(taps[2], tm - 1, axis=0), 0.0)
        act_scr[:, acols] = (_gelu_tanh(acc) * val_scr[buf]).astype(BF16)
        if slot + 1 == FFN_DOWN_GROUP or j + 1 == n_chunks:
            width = (slot + 1) * FFN_COL_CHUNK
            k0 = grp * FFN_DOWN_GROUP * FFN_COL_CHUNK
            part = _dot(act_scr[:, 0:width], wd_ref[k0:k0 + width, :])
            out = part if out is None else out + part
    out = x + m_ref[5:6, :] * out
    if final_norm:
        out = out * lax.rsqrt(jnp.mean(out * out, axis=-1, keepdims=True) + NORM_EPS) * fg_ref[...]
    o_ref[...] = out


def _ffn_call(x, norm_g, mods, w_up, conv_w9, conv_b, w_down, *, tm, on_grid, final_g=None):
    b, t, d = x.shape
    tm = min(tm, t)
    n_tiles = t // tm
    per_batch = mods.shape[0] > 1
    full = lambda shape: pl.BlockSpec(shape, lambda bi, i: (0,) * len(shape))
    in_specs = [pl.BlockSpec((None, tm, d), lambda bi, i: (bi, i, 0))]
    args = [x]
    scratch = [pltpu.VMEM((tm, FFN_DOWN_GROUP * FFN_COL_CHUNK), BF16)
               for _ in range(-(-D_FF // (FFN_DOWN_GROUP * FFN_COL_CHUNK)))]
    if on_grid:
        r = tm // GRID_W
        last = t // GRID_W - 1
        in_specs.append(pl.BlockSpec((None, GRID_W, d), lambda bi, i: (bi, jnp.minimum((i + 1) * r, last), 0)))
        args.append(x)
        scratch.insert(0, pltpu.VMEM((GRID_W, D_FF), F32))
    else:
        assert n_tiles == 1
    gate_rows = tm + 2 * GRID_W if on_grid else tm
    scratch = [pltpu.VMEM((2, gate_rows, FFN_COL_CHUNK), F32),
               pltpu.VMEM((2, tm, FFN_COL_CHUNK), F32)] + scratch
    in_specs += [full((1, d)),
                 pl.BlockSpec((None, N_MOD, d), (lambda bi, i: (bi, 0, 0)) if per_batch
                              else (lambda bi, i: (0, 0, 0))),
                 full((d, 2 * D_FF)), full((9, D_FF)), full((1, D_FF)), full((D_FF, d))]
    args += [norm_g, mods, w_up, conv_w9, conv_b, w_down]
    if final_g is not None:
        in_specs.append(full((1, d)))
        args.append(final_g)
    return pl.pallas_call(
        functools.partial(_ffn_kernel, on_grid=on_grid, final_norm=final_g is not None),
        name="conv_ffn",
        grid=(b, n_tiles),
        in_specs=in_specs,
        out_specs=pl.BlockSpec((None, tm, d), lambda bi, i: (bi, i, 0)),
        out_shape=jax.ShapeDtypeStruct((b, t, d), F32),
        scratch_shapes=scratch,
        compiler_params=_cparams(("parallel", "arbitrary")),
    )(*args)


def _block_diag2(a, b):
    za = jnp.zeros((a.shape[0], b.shape[1]), a.dtype)
    zb = jnp.zeros((b.shape[0], a.shape[1]), a.dtype)
    return jnp.concatenate([jnp.concatenate([a, za], axis=1), jnp.concatenate([zb, b], axis=1)], axis=0)


def _pad_cols(a, n, fill=0.0):
    return jnp.pad(a, ((0, 0), (0, n - a.shape[1])), constant_values=fill)


def _even_params(j, ev_w_in, ev_mu_prev, ev_mu_next, rk_w0_f, rk_w0_b, rk_w2_f, rk_w2_b, rk_a0_f,
                 rk_a0_b, rk_a2_f, rk_a2_b, rk_g2, rk_k_k, rk_k_a, rk_r_k, rk_ln_w, rk_ln_b,
                 ssd_conv_w, ssd_conv_b, ssd_dt_bias_f, ssd_dt_bias_b, ssd_a_log_f, ssd_a_log_b,
                 ssd_d, ssd_norm_w, ev_w_out):
    w_in = ev_w_in[j]
    rw = w_in[:, :RWKV_COLS]
    z = w_in[:, RWKV_COLS:RWKV_COLS + SSD_WIDTH]
    xbc = w_in[:, RWKV_COLS + SSD_WIDTH:RWKV_COLS + SSD_WIDTH + SSD_XBC]
    dts = w_in[:, RWKV_COLS + SSD_WIDTH + SSD_XBC:]
    w_packed = jnp.concatenate([_pad_cols(jnp.concatenate([rw, dts], axis=1), EV_RW_BLOCK), xbc, z], axis=1)
    head = jnp.arange(V7X_MXU_DIM) // RWKV_HEAD_DIM
    head_sum = (head[:, None] == head[None, :]).astype(BF16)
    lane = jnp.arange(V7X_LANES)[:, None]
    tgt = jnp.arange(2 * SSD_WIDTH)[None, :]
    dt_expand = (lane == (tgt // SSD_WIDTH) * SSD_HEADS + (tgt % SSD_WIDTH) // SSD_HEAD_DIM).astype(BF16)
    rep = lambda a: jnp.repeat(a, SSD_HEAD_DIM)[None, :]
    row = lambda a: a[None, :]
    return {
        'w_in': w_packed.astype(BF16),
        'mu_self': _pad_cols(row(1.0 - ev_mu_prev[j] - ev_mu_next[j]), EV_RW_BLOCK, 1.0),
        'mu_prev': _pad_cols(row(ev_mu_prev[j]), EV_RW_BLOCK),
        'mu_next': _pad_cols(row(ev_mu_next[j]), EV_RW_BLOCK),
        'w0': row(jnp.concatenate([rk_w0_f[j], rk_w0_b[j]])),
        'w2': _block_diag2(rk_w2_f[j], rk_w2_b[j]).astype(BF16),
        'a0': row(jnp.concatenate([rk_a0_f[j], rk_a0_b[j]])),
        'a2': _block_diag2(rk_a2_f[j], rk_a2_b[j]).astype(BF16),
        'g2': rk_g2[j].astype(BF16),
        'k_k': row(rk_k_k[j]), 'k_a': row(rk_k_a[j]), 'r_k': row(rk_r_k[j].reshape(-1)),
        'ln_w': row(rk_ln_w[j]), 'ln_b': row(rk_ln_b[j]),
        'head_sum': head_sum,
        'conv_w': ssd_conv_w[j], 'conv_b': row(ssd_conv_b[j]),
        'dt_bias': _pad_cols(row(jnp.concatenate([ssd_dt_bias_f[j], ssd_dt_bias_b[j]])), V7X_LANES),
        'dt_expand': dt_expand,
        'a_rep': jnp.stack([rep(-jnp.exp(ssd_a_log_f[j])), rep(-jnp.exp(ssd_a_log_b[j]))]),
        'd_skip': rep(ssd_d[j]),
        'norm_w': row(ssd_norm_w[j]),
        'w_out': ev_w_out[j].astype(BF16),
    }


def _rope_tables(t):
    n = RET_QK_DIM // 4
    pos = jnp.arange(t)
    row = (pos // GRID_W).astype(F32)
    col = (pos % GRID_W).astype(F32)
    inv = ROPE_BASE ** (-jnp.arange(n, dtype=F32) / n)
    ang = jnp.concatenate([row[:, None] * inv, col[:, None] * inv], axis=-1)
    cos, sin = jnp.cos(ang), jnp.sin(ang)
    return jnp.concatenate([cos, cos], axis=-1), jnp.concatenate([-sin, sin], axis=-1)


def _conv_ffn(x, mods, norm_g, w_up, conv_w9, conv_b, w_down, *, on_grid, final_g=None):
    return _ffn_call(x, norm_g, mods, w_up, conv_w9, conv_b, w_down, tm=512, on_grid=on_grid,
                     final_g=final_g)


def _even_layer(x, ctx, mods_x, mods_c, norm_g, p):
    b = x.shape[0]

    def features(h, mods):
        *feat, xbc, dtbc, z = _even_feat_call(h, norm_g, mods, p, 256)
        return z, feat, xbc, dtbc

    proj_c, feat_c, xbc_c, dt_c = features(ctx, mods_c)
    proj_x, feat_x, xbc_x, dt_x = features(x, mods_x)
    s0 = jnp.zeros((2, b, N_PAIRS, PAIR, PAIR), F32)
    *yrk_c, s_ctx = _rwkv_scan_call(feat_c, s0, 256)
    *yrk_x, _ = _rwkv_scan_call(feat_x, s_ctx, 512)
    h0 = jnp.zeros((2, b, N_PAIRS, SSD_STATE, PAIR), F32)
    *ysd_c, h_ctx = _ssd_scan_call(xbc_c, dt_c, p['a_rep'], h0, 256)
    *ysd_x, _ = _ssd_scan_call(xbc_x, dt_x, p['a_rep'], h_ctx, 512)
    x = _even_finish_call(yrk_x, feat_x[3], feat_x[4], ysd_x, xbc_x, proj_x, x, mods_x, p, 512)
    ctx = _even_finish_call(yrk_c, feat_c[3], feat_c[4], ysd_c, xbc_c, proj_c, ctx, mods_c, p, 256)
    return x, ctx


def _odd_layer(x, ctx, mods_x, mods_c, norm_g, w_in, lg_rep, w_out):
    b, t, _ = x.shape
    proj_c = _nm_call(ctx, norm_g, mods_c, w_in, shift_row=0, tm=256, tn=512, out_dtype=BF16,
                      qk_mode='scale')
    proj_x = _nm_call(x, norm_g, mods_x, w_in, shift_row=0, tm=512, tn=512, out_dtype=BF16,
                      qk_mode='rope', rope=_rope_tables(t))
    s0 = jnp.zeros((2, b, RET_HEADS, RET_QK_DIM, RET_V_DIM), F32)
    _, s_ctx = _ret_scan_call(proj_c, lg_rep, s0, 256)
    y, _ = _ret_scan_call(proj_x, lg_rep, s_ctx, 512)
    return _odd_finish_call(y, proj_x, x, mods_x, w_out, 512)


def kernel(x, c, ctx, c_ctx, mod_w, mod_b, norm1_g, norm2_g, ffn_w_up, ffn_conv_w, ffn_conv_b, ffn_w_down, ev_w_in, ev_mu_prev, ev_mu_next, rk_w0_f, rk_w0_b, rk_w2_f, rk_w2_b, rk_a0_f, rk_a0_b, rk_a2_f, rk_a2_b, rk_g2, rk_k_k, rk_k_a, rk_r_k, rk_ln_w, rk_ln_b, ssd_conv_w, ssd_conv_b, ssd_dt_bias_f, ssd_dt_bias_b, ssd_a_log_f, ssd_a_log_b, ssd_d, ssd_norm_w, ev_w_out, ret_w_in, ret_log2_f, ret_log2_b, ret_w_out, final_norm_g):
    b, t, d = x.shape
    depth = mod_w.shape[0]
    rows = -(-(b + 1) // V7X_SUBLANES) * V7X_SUBLANES
    cond = jnp.concatenate([c, c_ctx[None, :], jnp.zeros((rows - b - 1, d), F32)], axis=0)
    mods = _mod_call(cond, mod_w, mod_b).reshape(depth, rows, N_MOD, d)
    for i in range(depth):
        need_ctx = i < depth - 1
        mods_x = mods[i, :b]
        mods_c = mods[i, b:b + 1]
        j = i // 2
        g1 = norm1_g[i][None, :]
        if i % 2 == 0:
            p = _even_params(j, ev_w_in, ev_mu_prev, ev_mu_next, rk_w0_f, rk_w0_b, rk_w2_f, rk_w2_b,
                             rk_a0_f, rk_a0_b, rk_a2_f, rk_a2_b, rk_g2, rk_k_k, rk_k_a, rk_r_k,
                             rk_ln_w, rk_ln_b, ssd_conv_w, ssd_conv_b, ssd_dt_bias_f, ssd_dt_bias_b,
                             ssd_a_log_f, ssd_a_log_b, ssd_d, ssd_norm_w, ev_w_out)
            x, ctx_mixed = _even_layer(x, ctx, mods_x, mods_c, g1, p)
        else:
            lg = jnp.stack([jnp.log1p(-jnp.exp2(-ret_log2_f[j])), jnp.log1p(-jnp.exp2(-ret_log2_b[j]))])
            lg_rep = jnp.repeat(lg, RET_QK_DIM, axis=-1)[:, None, :]
            x = _odd_layer(x, ctx, mods_x, mods_c, g1, ret_w_in[j].astype(BF16), lg_rep,
                           ret_w_out[j].astype(BF16))
            ctx_mixed = None
        g2 = norm2_g[i][None, :]
        w_up = ffn_w_up[i].astype(BF16)
        w_down = ffn_w_down[i].astype(BF16)
        conv_w9 = ffn_conv_w[i].reshape(9, D_FF)
        conv_b = ffn_conv_b[i][None, :]
        last = i == depth - 1
        x = _conv_ffn(x, mods_x, g2, w_up, conv_w9, conv_b, w_down, on_grid=True,
                      final_g=final_norm_g[None, :] if last else None)
        if need_ctx:
            ctx = _conv_ffn(ctx_mixed, mods_c, g2, w_up, conv_w9, conv_b, w_down, on_grid=False)
    return x
```

```python
import functools
import math

import jax
import jax.numpy as jnp
from jax import lax
from jax.experimental import pallas as pl
from jax.experimental.pallas import tpu as pltpu

F32 = jnp.float32
BF16 = jnp.bfloat16

D_MODEL = 1024
GRID_W = 64
N_MOD = 6
NORM_EPS = 1e-6
RWKV_HEADS = 8
RWKV_HEAD_DIM = 64
RWKV_WIDTH = RWKV_HEADS * RWKV_HEAD_DIM
DECAY_LORA = 64
ICLR_LORA = 64
GATE_LORA = 128
RWKV_GN_EPS = 64e-5
RWKV_COLS = 3 * RWKV_WIDTH + 2 * DECAY_LORA + 2 * ICLR_LORA + GATE_LORA
SSD_HEADS = 8
SSD_HEAD_DIM = 64
SSD_WIDTH = SSD_HEADS * SSD_HEAD_DIM
SSD_GROUPS = 2
SSD_STATE = 128
SSD_XBC = SSD_WIDTH + 2 * SSD_GROUPS * SSD_STATE
RET_HEADS = 8
RET_QK_DIM = 128
RET_V_DIM = 256
RET_QK = RET_HEADS * RET_QK_DIM
RET_V = RET_HEADS * RET_V_DIM
ROPE_BASE = 10000.0
D_FF = 2816

V7X_LANES = 128
V7X_SUBLANES = 8
V7X_MXU_DIM = 256
V7X_VMEM_LIMIT_BYTES = 56 * 1024 * 1024

RWKV_CHUNK = 64
RWKV_CHUNKS_PER_STEP = 2
SCAN_CHUNK = 128
FFN_COL_CHUNK = 256
FFN_DOWN_GROUP = 4
PAIR = 2 * RWKV_HEAD_DIM
N_PAIRS = RWKV_HEADS // 2
EV_RW_BLOCK = 2048
EV_DT_OFF = RWKV_COLS
EV_XBC_OFF = EV_RW_BLOCK
EV_Z_OFF = EV_RW_BLOCK + SSD_XBC
EV_COLS = EV_Z_OFF + SSD_WIDTH


def _cparams(sem):
    return pltpu.CompilerParams(dimension_semantics=sem, vmem_limit_bytes=V7X_VMEM_LIMIT_BYTES)


def _split3(x):
    hi = x.astype(BF16)
    r1 = x - hi.astype(F32)
    mid = r1.astype(BF16)
    lo = (r1 - mid.astype(F32)).astype(BF16)
    return hi, mid, lo


def _dot(a, b):
    return jnp.dot(a, b, preferred_element_type=F32)


def _dot_nt(a, b):
    return lax.dot_general(a, b, (((1,), (1,)), ((), ())), preferred_element_type=F32)


def _dot_tn(a, b):
    return lax.dot_general(a, b, (((0,), (0,)), ((), ())), preferred_element_type=F32)


def _dot01(x, m01):
    hi, mid, lo = _split3(x)
    return _dot(hi, m01) + _dot(mid, m01) + _dot(lo, m01)


def _head_sum(x, j01):
    hi = x.astype(BF16)
    lo = (x - hi.astype(F32)).astype(BF16)
    n = j01.shape[0]
    return jnp.concatenate([_dot(hi[:, g * n:(g + 1) * n], j01) + _dot(lo[:, g * n:(g + 1) * n], j01)
                            for g in range(x.shape[1] // n)], axis=1)


def _dot01_left(m01, x):
    hi, mid, lo = _split3(x)
    return _dot(m01, hi) + _dot(m01, mid) + _dot(m01, lo)


def _sigmoid(x):
    return 1.0 / (1.0 + jnp.exp(-x))


def _silu(x):
    return x * _sigmoid(x)


def _softplus(x):
    return jnp.maximum(x, 0.0) + jnp.log1p(jnp.exp(-jnp.abs(x)))


def _gelu_tanh(x):
    c = math.sqrt(2.0 / math.pi)
    half = 0.5 * x
    return half + half * jnp.tanh(x * (c + (0.044715 * c) * (x * x)))


def _order_masks(d, n):
    row = lax.broadcasted_iota(jnp.int32, (n, n), 0)
    col = lax.broadcasted_iota(jnp.int32, (n, n), 1)
    diff = (row - col) * (1 - 2 * d)
    return diff > 0, diff >= 0


def _mod_kernel(c_ref, w_ref, b_ref, o_ref):
    h = _silu(c_ref[...])
    hi, mid, lo = _split3(h)
    w = w_ref[...]
    wh = w.astype(BF16)
    wl = (w - wh.astype(F32)).astype(BF16)
    acc = _dot(hi, wh) + _dot(mid, wh) + _dot(hi, wl)
    o_ref[...] = acc + b_ref[...]


def _mod_call(cond, mod_w, mod_b):
    depth, d, n = mod_w.shape
    rows = cond.shape[0]
    tn = 1024
    return pl.pallas_call(
        _mod_kernel,
        name="adaln_mod",
        grid=(depth, n // tn),
        in_specs=[pl.BlockSpec((rows, d), lambda l, j: (0, 0)),
                  pl.BlockSpec((None, d, tn), lambda l, j: (l, 0, j)),
                  pl.BlockSpec((None, 1, tn), lambda l, j: (l, 0, j))],
        out_specs=pl.BlockSpec((None, rows, tn), lambda l, j: (l, 0, j)),
        out_shape=jax.ShapeDtypeStruct((depth, rows, n), F32),
        compiler_params=_cparams(("parallel", "parallel")),
    )(cond, mod_w, mod_b.reshape(depth, 1, n))


def _nm_kernel(*refs, shift_row, tn, qk_mode):
    if qk_mode == 'rope':
        x_ref, g_ref, m_ref, w_ref, cos_ref, sin_ref, o_ref = refs
    else:
        x_ref, g_ref, m_ref, w_ref, o_ref = refs
    x = x_ref[...]
    h = x * lax.rsqrt(jnp.mean(x * x, axis=-1, keepdims=True) + NORM_EPS) * g_ref[...]
    h = h * (1.0 + m_ref[shift_row + 1:shift_row + 2, :]) + m_ref[shift_row:shift_row + 1, :]
    h = h.astype(BF16)
    dk = RET_QK_DIM
    for j in range(w_ref.shape[1] // tn):
        cols = slice(j * tn, (j + 1) * tn)
        y = _dot(h, w_ref[:, cols])
        if qk_mode is not None and j * tn < 2 * RET_QK:
            scale = dk ** -0.5 if j * tn >= RET_QK else 1.0
            heads = []
            for hh in range(tn // dk):
                yh = y[:, hh * dk:(hh + 1) * dk]
                if qk_mode == 'rope':
                    yh = yh * cos_ref[...] + pltpu.roll(yh, dk // 2, axis=1) * sin_ref[...]
                heads.append(yh * scale if scale != 1.0 else yh)
            y = jnp.concatenate(heads, axis=1)
        o_ref[:, cols] = y.astype(o_ref.dtype)


def _nm_call(x, g, mods, w, *, shift_row, tm, tn, out_dtype=F32, qk_mode=None, rope=None):
    b, t, d = x.shape
    n = w.shape[1]
    tm = min(tm, t)
    per_batch = mods.shape[0] > 1
    in_specs = [pl.BlockSpec((None, tm, d), lambda bi, i: (bi, i, 0)),
                pl.BlockSpec((1, d), lambda bi, i: (0, 0)),
                pl.BlockSpec((None, N_MOD, d), (lambda bi, i: (bi, 0, 0)) if per_batch
                             else (lambda bi, i: (0, 0, 0))),
                pl.BlockSpec((d, n), lambda bi, i: (0, 0))]
    args = [x, g, mods, w]
    if qk_mode == 'rope':
        tab = pl.BlockSpec((tm, RET_QK_DIM), lambda bi, i: (i, 0))
        in_specs += [tab, tab]
        args += list(rope)
    return pl.pallas_call(
        functools.partial(_nm_kernel, shift_row=shift_row, tn=tn, qk_mode=qk_mode),
        name="norm_mod_matmul",
        grid=(b, t // tm),
        in_specs=in_specs,
        out_specs=pl.BlockSpec((None, tm, n), lambda bi, i: (bi, i, 0)),
        out_shape=jax.ShapeDtypeStruct((b, t, n), out_dtype),
        compiler_params=_cparams(("parallel", "parallel")),
    )(*args)


def _even_feat_kernel(x_ref, xp_ref, xn_ref, g_ref, m_ref, w_ref,
                      mus_ref, mup_ref, mun_ref, w0_ref, w2_ref, a0_ref, a2_ref,
                      g2_ref, kk_ref, ka_ref, rk_ref, j_ref, cw_ref, cb_ref, dtb_ref, e_ref,
                      r_ref, v_ref, kkn_ref, bonus_ref, gate_ref, ld_ref, kd_ref, bd_ref,
                      xbc_ref, dtbc_ref, z_ref):
    i = pl.program_id(1)
    n_tiles = pl.num_programs(1)
    tm = x_ref.shape[0]
    halo = V7X_SUBLANES
    ext = tm + 2 * halo

    def norm_mod(xv):
        hv = xv * lax.rsqrt(jnp.mean(xv * xv, axis=-1, keepdims=True) + NORM_EPS) * g_ref[...]
        return (hv * (1.0 + m_ref[1:2, :]) + m_ref[0:1, :]).astype(BF16)

    h = norm_mod(x_ref[...])
    zero = jnp.zeros((halo, x_ref.shape[1]), BF16)
    h_ext = jnp.concatenate([jnp.where(i > 0, norm_mod(xp_ref[...]), zero), h,
                             jnp.where(i < n_tiles - 1, norm_mod(xn_ref[...]), zero)], axis=0)

    def proj3(cols):
        ye = _dot(h_ext, w_ref[:, cols])
        return (ye[halo:halo + tm], pltpu.roll(ye, 1, axis=0)[halo:halo + tm],
                pltpu.roll(ye, ext - 1, axis=0)[halo:halo + tm])

    w = RWKV_WIDTH
    rw = []
    for j in range(EV_RW_BLOCK // w):
        cols = slice(j * w, (j + 1) * w)
        cur, prev, nxt = proj3(cols)
        rw.append(cur * mus_ref[:, cols] + prev * mup_ref[:, cols] + nxt * mun_ref[:, cols])
    r, k, v, lora = rw
    wd = lora[:, 0:2 * DECAY_LORA]
    ad = lora[:, 2 * DECAY_LORA:2 * DECAY_LORA + 2 * ICLR_LORA]
    gd = lora[:, 2 * DECAY_LORA + 2 * ICLR_LORA:2 * DECAY_LORA + 2 * ICLR_LORA + GATE_LORA]
    dt_raw = lora[:, EV_DT_OFF - 3 * w:EV_RW_BLOCK - 3 * w]
    jm = j_ref[...]

    for j in range(SSD_XBC // w):
        cols = slice(j * w, (j + 1) * w)
        cur, prev, nxt = proj3(slice(EV_XBC_OFF + j * w, EV_XBC_OFF + (j + 1) * w))
        y = prev * cw_ref[0:1, cols] + cur * cw_ref[1:2, cols] + nxt * cw_ref[2:3, cols] + cb_ref[:, cols]
        xbc_ref[:, cols] = _silu(y).astype(xbc_ref.dtype)
    z_ref[...] = _dot(h, w_ref[:, EV_Z_OFF:EV_Z_OFF + SSD_WIDTH]).astype(z_ref.dtype)
    dt = _dot01(_softplus(dt_raw + dtb_ref[...]), e_ref[...])
    dtbc_ref[0] = dt[:, 0:SSD_WIDTH]
    dtbc_ref[1] = dt[:, SSD_WIDTH:2 * SSD_WIDTH]

    kk = k * kk_ref[...]
    ss = _head_sum(kk * kk, jm)
    kk = kk / jnp.maximum(jnp.sqrt(ss), 1e-12)
    r_ref[...] = r.astype(r_ref.dtype)
    v_ref[...] = v.astype(v_ref.dtype)
    kkn_ref[...] = kk.astype(kkn_ref.dtype)
    bonus_ref[...] = (_head_sum(r * k * rk_ref[...], jm) * v).astype(bonus_ref.dtype)
    gate_ref[...] = _dot(_sigmoid(gd).astype(BF16), g2_ref[...]).astype(gate_ref.dtype)

    zw = _dot(jnp.tanh(wd).astype(BF16), w2_ref[...]) + w0_ref[...]
    za = _dot(ad.astype(BF16), a2_ref[...]) + a0_ref[...]
    for di in range(2):
        ld_ref[di] = -math.exp(-0.5) * _sigmoid(zw[:, di * w:(di + 1) * w])
        iclr = _sigmoid(za[:, di * w:(di + 1) * w])
        kd_ref[di] = (k * (1.0 + (iclr - 1.0) * ka_ref[...])).astype(kd_ref.dtype)
        bd_ref[di] = (kk * iclr).astype(bd_ref.dtype)


def _even_feat_call(x, norm_g, mods, p, tm):
    b, t, d = x.shape
    tm = min(tm, t)
    n_tiles = t // tm
    w = RWKV_WIDTH
    per_batch = mods.shape[0] > 1
    r8 = tm // V7X_SUBLANES
    last = t // V7X_SUBLANES - 1
    full = lambda shape: pl.BlockSpec(shape, lambda bi, i: (0,) * len(shape))
    tok = pl.BlockSpec((None, tm, w), lambda bi, i: (bi, i, 0))
    tok2 = pl.BlockSpec((2, None, tm, w), lambda bi, i: (0, bi, i, 0))
    sd = jax.ShapeDtypeStruct((b, t, w), BF16)
    sd2 = jax.ShapeDtypeStruct((2, b, t, w), BF16)
    f2 = jax.ShapeDtypeStruct((2, b, t, w), F32)
    return pl.pallas_call(
        _even_feat_kernel,
        name="even_feat",
        grid=(b, n_tiles),
        in_specs=[pl.BlockSpec((None, tm, d), lambda bi, i: (bi, i, 0)),
                  pl.BlockSpec((None, V7X_SUBLANES, d), lambda bi, i: (bi, jnp.maximum(i * r8 - 1, 0), 0)),
                  pl.BlockSpec((None, V7X_SUBLANES, d), lambda bi, i: (bi, jnp.minimum((i + 1) * r8, last), 0)),
                  full((1, d)),
                  pl.BlockSpec((None, N_MOD, d), (lambda bi, i: (bi, 0, 0)) if per_batch
                               else (lambda bi, i: (0, 0, 0))),
                  full((d, EV_COLS)),
                  full((1, EV_RW_BLOCK)), full((1, EV_RW_BLOCK)), full((1, EV_RW_BLOCK)),
                  full((1, 2 * w)), full((2 * DECAY_LORA, 2 * w)),
                  full((1, 2 * w)), full((2 * ICLR_LORA, 2 * w)),
                  full((GATE_LORA, w)), full((1, w)), full((1, w)), full((1, w)),
                  full((V7X_MXU_DIM, V7X_MXU_DIM)),
                  full((3, SSD_XBC)), full((1, SSD_XBC)), full((1, V7X_LANES)),
                  full((V7X_LANES, 2 * SSD_WIDTH))],
        out_specs=[tok, tok, tok, tok, tok, tok2, tok2, tok2,
                   pl.BlockSpec((None, tm, SSD_XBC), lambda bi, i: (bi, i, 0)), tok2, tok],
        out_shape=[sd, sd, sd, sd, sd, f2, sd2, sd2,
                   jax.ShapeDtypeStruct((b, t, SSD_XBC), BF16), f2, sd],
        compiler_params=_cparams(("parallel", "parallel")),
    )(x, x, x, norm_g, mods, p['w_in'], p['mu_self'], p['mu_prev'], p['mu_next'], p['w0'], p['w2'],
      p['a0'], p['a2'], p['g2'], p['k_k'], p['k_a'], p['r_k'], p['head_sum'],
      p['conv_w'], p['conv_b'], p['dt_bias'], p['dt_expand'])


def _rwkv_scan_body(rf_ref, vf_ref, kkf_ref, ldf_ref, kdf_ref, bdf_ref,
                    rb_ref, vb_ref, kkb_ref, ldb_ref, kdb_ref, bdb_ref, yf_ref, yb_ref, st_scr, n_chunks):
    c = RWKV_CHUNK
    rowc = lax.broadcasted_iota(jnp.int32, (c, 2 * c), 0)
    colc = lax.broadcasted_iota(jnp.int32, (c, 2 * c), 1) % c
    eye_wide = (rowc == colc).astype(F32)
    m0 = lax.broadcasted_iota(jnp.int32, (c, PAIR), 1) < RWKV_HEAD_DIM
    row1 = lax.broadcasted_iota(jnp.int32, (c, 1), 0)
    incl01, strict_wide, incl_wide, rsel = [], [], [], []
    for d in range(2):
        sgn = 1 - 2 * d
        incl01.append(_order_masks(d, c)[1].astype(BF16))
        strict_wide.append((rowc - colc) * sgn > 0)
        incl_wide.append((rowc - colc) * sgn >= 0)
        rsel.append(row1 == (c - 1 if d == 0 else 0))
    refs = ((rf_ref, vf_ref, kkf_ref, ldf_ref, kdf_ref, bdf_ref, yf_ref),
            (rb_ref, vb_ref, kkb_ref, ldb_ref, kdb_ref, bdb_ref, yb_ref))
    chains = [(d, p) for d in range(2) for p in range(N_PAIRS)]
    lanes = [slice(p * PAIR, (p + 1) * PAIR) for p in range(N_PAIRS)]

    def stack(x):
        return jnp.concatenate([jnp.where(m0, x, 0.0), jnp.where(m0, 0.0, x)], axis=0)

    sub = range(RWKV_CHUNKS_PER_STEP)
    items = [(u, d, p) for u in sub for d, p in chains]

    def stages(j):
        def rows_of(u, d):
            cj = j * len(sub) + u
            return pl.ds(pl.multiple_of((cj if d == 0 else n_chunks - 1 - cj) * c, c), c)

        rows = {(u, d): rows_of(u, d) for u in sub for d in range(2)}
        cs_all = {ud: _dot01_left(incl01[ud[1]], refs[ud[1]][3][rows[ud], :]) for ud in rows}
        yield
        a_s, r_t, b_s, k_s, v_s, b_end, k_end, g_tot = [], [], [], [], [], [], [], []
        for u, d, p in items:
            r_ref, v_ref, kk_ref, ld_ref, kd_ref, bd_ref, _ = refs[d]
            rw, ln = rows[u, d], lanes[p]
            cs = cs_all[u, d][:, ln]
            cs_last = jnp.sum(jnp.where(rsel[d], cs, 0.0), axis=0, keepdims=True)
            g_neg = jnp.exp(-cs)
            g_end = jnp.exp(cs_last - cs)
            g_tot.append(jnp.exp(cs_last))
            kdv = kd_ref[rw, ln].astype(F32)
            bdv = bd_ref[rw, ln].astype(F32)
            a_w = (-kk_ref[rw, ln].astype(F32) * jnp.exp(cs - ld_ref[rw, ln]))
            a_s.append((a_w.astype(BF16), stack(a_w).astype(BF16)))
            r_t.append((r_ref[rw, ln].astype(F32) * jnp.exp(cs)).astype(BF16))
            b_s.append(stack(bdv * g_neg).astype(BF16))
            k_s.append(stack(kdv * g_neg).astype(BF16))
            v_s.append(stack(v_ref[rw, ln].astype(F32)).astype(BF16))
            b_end.append(stack(bdv * g_end).astype(BF16))
            k_end.append(stack(kdv * g_end).astype(BF16))
            if p == N_PAIRS - 1:
                yield
        n = range(len(items))
        dirs = [d for _, d, _ in items]
        gram = [_dot_nt(jnp.concatenate([a_s[q][0], r_t[q]], axis=0),
                        jnp.concatenate([b_s[q], k_s[q]], axis=0)) for q in n]
        yield
        a_ab =[jnp.where(strict_wide[dirs[q]], gram[q][0:c, 0:2 * c], 0.0) for q in n]
        a_ak = [jnp.where(strict_wide[dirs[q]], gram[q][0:c, 2 * c:4 * c], 0.0).astype(BF16) for q in n]
        p_rb = [jnp.where(incl_wide[dirs[q]], gram[q][c:2 * c, 0:2 * c], 0.0).astype(BF16) for q in n]
        p_rk = [jnp.where(incl_wide[dirs[q]], gram[q][c:2 * c, 2 * c:4 * c], 0.0).astype(BF16) for q in n]
        minv = [eye_wide + a_ab[q] for q in n]
        pw = [a_ab[q] for q in n]
        pw = [_dot(pw[q].astype(BF16), stack(pw[q]).astype(BF16)) for q in n]
        yield
        levels = int(math.log2(c))
        for k in range(2, levels):
            both = [_dot(pw[q].astype(BF16),
                         jnp.concatenate([stack(pw[q]), stack(minv[q])], axis=1).astype(BF16)) for q in n]
            minv = [minv[q] + both[q][:, 2 * c:4 * c] for q in n]
            pw = [both[q][:, 0:2 * c] for q in n]
            yield
        minv = [minv[q] + _dot(pw[q].astype(BF16), stack(minv[q]).astype(BF16)) for q in n]
        akv = [_dot(a_ak[q], v_s[q]) for q in n]
        yield
        eff = [_dot(minv[q].astype(BF16),
                    jnp.concatenate([a_s[q][1], stack(akv[q]).astype(BF16)], axis=1)) for q in n]
        bk_end = [jnp.concatenate([b_end[q], k_end[q]], axis=0) for q in n]
        p_both = [jnp.concatenate([p_rb[q], p_rk[q]], axis=1) for q in n]
        yield
        st = [st_scr[d, p] for d, p in chains]
        nc = range(len(chains))
        for u in sub:
            q0 = u * len(chains)
            fs = [_dot_nt(jnp.concatenate([eff[q0 + m][:, 0:PAIR].astype(BF16), r_t[q0 + m]], axis=0),
                          st[m].astype(BF16)) for m in nc]
            uv = [jnp.concatenate([stack(fs[m][0:c] + eff[q0 + m][:, PAIR:2 * PAIR]).astype(BF16),
                                   v_s[q0 + m]], axis=0) for m in nc]
            st = [st[m] * g_tot[q0 + m] + _dot_tn(uv[m], bk_end[q0 + m]) for m in nc]
            for m, (d, p) in enumerate(chains):
                refs[d][6][rows[u, d], lanes[p]] = fs[m][c:2 * c] + _dot(p_both[q0 + m], uv[m])
            yield
        for m, (d, p) in enumerate(chains):
            st_scr[d, p] = st[m]

    return stages


def _ssd_scan_body(xf_ref, dtf_ref, xb_ref, dtb_ref, a_ref, yf_ref, yb_ref, st_scr, n_chunks):
    c = SCAN_CHUNK
    hd = SSD_HEAD_DIM
    row1 = lax.broadcasted_iota(jnp.int32, (c, 1), 0)
    m0 = lax.broadcasted_iota(jnp.int32, (c, PAIR), 1) < hd
    before_eq = [_order_masks(d, c)[1] for d in range(2)]
    incl01 = [before_eq[d].astype(BF16) for d in range(2)]
    rsel = [row1 == (c - 1 if d == 0 else 0) for d in range(2)]
    refs = ((xf_ref, dtf_ref, yf_ref), (xb_ref, dtb_ref, yb_ref))
    chains = [(d, p) for d in range(2) for p in range(N_PAIRS)]
    n = range(len(chains))
    group = [p // (N_PAIRS // SSD_GROUPS) for _, p in chains]
    lanes = [slice(p * PAIR, (p + 1) * PAIR) for _, p in chains]

    def stages(j):
        rows = (pl.ds(pl.multiple_of(j * c, c), c),
                pl.ds(pl.multiple_of((n_chunks - 1 - j) * c, c), c))
        dt = [refs[d][1][rows[d], :] for d in range(2)]
        cs_all = [_dot01_left(incl01[d], dt[d] * a_ref[d]) for d in range(2)]
        bm = [[refs[d][0][rows[d], SSD_WIDTH + g * SSD_STATE:SSD_WIDTH + (g + 1) * SSD_STATE]
               for g in range(SSD_GROUPS)] for d in range(2)]
        cm = [[refs[d][0][rows[d], SSD_WIDTH + (SSD_GROUPS + g) * SSD_STATE:
                          SSD_WIDTH + (SSD_GROUPS + g + 1) * SSD_STATE]
               for g in range(SSD_GROUPS)] for d in range(2)]
        cb = [[_dot_nt(cm[d][g], bm[d][g]) for g in range(SSD_GROUPS)] for d in range(2)]
        yield
        st = [st_scr[d, p] for d, p in chains]
        y_st = [_dot(cm[chains[q][0]][group[q]], st[q].astype(BF16)) for q in n]
        cs = [cs_all[chains[q][0]][:, lanes[q]] for q in n]
        xdt = [refs[chains[q][0]][0][rows[chains[q][0]], lanes[q]].astype(F32) * dt[chains[q][0]][:, lanes[q]]
               for q in n]
        yield
        probs = []
        for q in n:
            d = chains[q][0]
            cs_t = cs[q].T
            both = []
            for hh in range(2):
                col = cs[q][:, hh * hd:hh * hd + 1]
                rowv = cs_t[hh * hd:hh * hd + 1, :]
                dec = jnp.exp(jnp.where(before_eq[d], col - rowv, -jnp.inf))
                both.append((cb[d][group[q]] * dec).astype(BF16))
            probs.append(jnp.concatenate(both, axis=1))
            yield
        xs2 = [jnp.concatenate([jnp.where(m0, xdt[q], 0.0), jnp.where(m0, 0.0, xdt[q])],
                               axis=0).astype(BF16) for q in n]
        y_in = [_dot(probs[q], xs2[q]) for q in n]
        yield
        for q in n:
            d = chains[q][0]
            refs[d][2][rows[d], lanes[q]] = y_in[q] + jnp.exp(cs[q]) * y_st[q]
        yield
        for q, (d, p) in enumerate(chains):
            cs_last = jnp.sum(jnp.where(rsel[d], cs[q], 0.0), axis=0, keepdims=True)
            xe = (xdt[q] * jnp.exp(cs_last - cs[q])).astype(BF16)
            st_scr[d, p] = st[q] * jnp.exp(cs_last) + _dot_tn(bm[d][group[q]], xe)

    return stages


def _even_scan_kernel(rf_ref, vf_ref, kkf_ref, ldf_ref, kdf_ref, bdf_ref,
                      rb_ref, vb_ref, kkb_ref, ldb_ref, kdb_ref, bdb_ref,
                      xf_ref, dtf_ref, xb_ref, dtb_ref, a_ref, s0r_ref, s0s_ref,
                      yrf_ref, yrb_ref, ysf_ref, ysb_ref, sfr_ref, sfs_ref, str_scr, sts_scr, *, n_steps):
    i = pl.program_id(1)

    @pl.when(i == 0)
    def _():
        str_scr[...] = s0r_ref[...]
        sts_scr[...] = s0s_ref[...]

    rwkv_stages = _rwkv_scan_body(rf_ref, vf_ref, kkf_ref, ldf_ref, kdf_ref, bdf_ref,
                                  rb_ref, vb_ref, kkb_ref, ldb_ref, kdb_ref, bdb_ref,
                                  yrf_ref, yrb_ref, str_scr, n_steps * RWKV_CHUNKS_PER_STEP)
    ssd_stages = _ssd_scan_body(xf_ref, dtf_ref, xb_ref, dtb_ref, a_ref, ysf_ref, ysb_ref, sts_scr, n_steps)

    def step(j, carry):
        live = [rwkv_stages(j), ssd_stages(j)]
        while live:
            for gen in list(live):
                if next(gen, StopIteration) is StopIteration:
                    live.remove(gen)
        return carry

    lax.fori_loop(0, n_steps, step, 0)

    @pl.when(i == pl.num_programs(1) - 1)
    def _():
        sfr_ref[...] = str_scr[...]
        sfs_ref[...] = sts_scr[...]


def _even_scan_call(feat, xbc, dtbc, a_rep, s0_rk, s0_sd, tb):
    r, v, kk, _, _, ld, kd, bd = feat
    b, t, w = r.shape
    tb = min(tb, t)
    nb = t // tb
    assert SCAN_CHUNK == RWKV_CHUNKS_PER_STEP * RWKV_CHUNK
    tok_f = pl.BlockSpec((None, tb, w), lambda bi, i: (bi, i, 0))
    tok_b = pl.BlockSpec((None, tb, w), lambda bi, i: (bi, nb - 1 - i, 0))
    dir_f = pl.BlockSpec((None, None, tb, w), lambda bi, i: (0, bi, i, 0))
    dir_b = pl.BlockSpec((None, None, tb, w), lambda bi, i: (1, bi, nb - 1 - i, 0))
    xbc_f = pl.BlockSpec((None, tb, SSD_XBC), lambda bi, i: (bi, i, 0))
    xbc_b = pl.BlockSpec((None, tb, SSD_XBC), lambda bi, i: (bi, nb - 1 - i, 0))
    st_r = pl.BlockSpec((2, None, N_PAIRS, PAIR, PAIR), lambda bi, i: (0, bi, 0, 0, 0))
    st_s = pl.BlockSpec((2, None, N_PAIRS, SSD_STATE, PAIR), lambda bi, i: (0, bi, 0, 0, 0))
    y_sd = jax.ShapeDtypeStruct((b, t, w), F32)
    return pl.pallas_call(
        functools.partial(_even_scan_kernel, n_steps=tb // SCAN_CHUNK),
        name="even_scan",
        grid=(b, nb),
        in_specs=[tok_f, tok_f, tok_f, dir_f, dir_f, dir_f, tok_b, tok_b, tok_b, dir_b, dir_b, dir_b,
                  xbc_f, dir_f, xbc_b, dir_b,
                  pl.BlockSpec((2, 1, SSD_WIDTH), lambda bi, i: (0, 0, 0)), st_r, st_s],
        out_specs=[tok_f, tok_b, tok_f, tok_b, st_r, st_s],
        out_shape=[y_sd, y_sd, y_sd, y_sd,
                   jax.ShapeDtypeStruct((2, b, N_PAIRS, PAIR, PAIR), F32),
                   jax.ShapeDtypeStruct((2, b, N_PAIRS, SSD_STATE, PAIR), F32)],
        scratch_shapes=[pltpu.VMEM((2, N_PAIRS, PAIR, PAIR), F32),
                        pltpu.VMEM((2, N_PAIRS, SSD_STATE, PAIR), F32)],
        compiler_params=_cparams(("parallel", "arbitrary")),
    )(r, v, kk, ld, kd, bd, r, v, kk, ld, kd, bd, xbc, dtbc, xbc, dtbc, a_rep, s0_rk, s0_sd)


def _even_finish_kernel(yrf_ref, yrb_ref, bonus_ref, gate_ref, ysf_ref, ysb_ref, xs_ref, z_ref, x_ref,
                        m_ref, lnw_ref, lnb_ref, dsk_ref, nw_ref, j_ref, wo_ref, o_ref):
    jm = j_ref[...]
    y = yrf_ref[...] + yrb_ref[...]
    inv_n = 1.0 / RWKV_HEAD_DIM
    mean = _head_sum(y, jm) * inv_n
    yc = y - mean
    var = _head_sum(yc * yc, jm) * inv_n
    y = yc * lax.rsqrt(var + RWKV_GN_EPS) * lnw_ref[...] + lnb_ref[...]
    y_rk = (y + bonus_ref[...].astype(F32)) * gate_ref[...].astype(F32)
    s = ysf_ref[...] + ysb_ref[...] + dsk_ref[...] * xs_ref[...].astype(F32)
    s = s * _silu(z_ref[...].astype(F32))
    s = s * lax.rsqrt(jnp.mean(s * s, axis=-1, keepdims=True) + NORM_EPS) * nw_ref[...]
    out = _dot(y_rk.astype(BF16), wo_ref[0:RWKV_WIDTH, :]) + \
        _dot(s.astype(BF16), wo_ref[RWKV_WIDTH:RWKV_WIDTH + SSD_WIDTH, :])
    o_ref[...] = x_ref[...] + m_ref[2:3, :] * out


def _even_finish_call(yrk, bonus, gate, ysd, xbc, proj, x, mods, p, tm):
    b, t, d = x.shape
    tm = min(tm, t)
    w = RWKV_WIDTH
    per_batch = mods.shape[0] > 1
    full = lambda shape: pl.BlockSpec(shape, lambda bi, i: (0,) * len(shape))
    tok = pl.BlockSpec((None, tm, w), lambda bi, i: (bi, i, 0))
    return pl.pallas_call(
        _even_finish_kernel,
        name="even_finish",
        grid=(b, t // tm),
        in_specs=[tok, tok, tok, tok, tok, tok,
                  pl.BlockSpec((None, tm, SSD_WIDTH), lambda bi, i: (bi, i, 0)),
                  pl.BlockSpec((None, tm, SSD_WIDTH), lambda bi, i: (bi, i, 0)),
                  pl.BlockSpec((None, tm, d), lambda bi, i: (bi, i, 0)),
                  pl.BlockSpec((None, N_MOD, d), (lambda bi, i: (bi, 0, 0)) if per_batch
                               else (lambda bi, i: (0, 0, 0))),
                  full((1, w)), full((1, w)), full((1, w)), full((1, w)),
                  full((V7X_MXU_DIM, V7X_MXU_DIM)),
                  full((2 * w, d))],
        out_specs=pl.BlockSpec((None, tm, d), lambda bi, i: (bi, i, 0)),
        out_shape=jax.ShapeDtypeStruct((b, t, d), F32),
        compiler_params=_cparams(("parallel", "parallel")),
    )(yrk[0], yrk[1], bonus, gate, ysd[0], ysd[1], xbc, proj, x, mods, p['ln_w'], p['ln_b'], p['d_skip'],
      p['norm_w'],
      p['head_sum'], p['w_out'])


def _ret_scan_kernel(q_ref, k_ref, v_ref, lg_ref, s0_ref, y_ref, sf_ref, st_scr, dec_scr, sc_scr,
                     *, n_chunks):
    d = pl.program_id(0)
    i = pl.program_id(2)
    c = SCAN_CHUNK
    dk, dv = RET_QK_DIM, RET_V_DIM
    heads = range(RET_HEADS)
    lg_all = lg_ref[...]

    @pl.when(i == 0)
    def _():
        st_scr[...] = s0_ref[...]
        _, before_eq = _order_masks(d, c)
        row = lax.broadcasted_iota(jnp.int32, (c, c), 0)
        col = lax.broadcasted_iota(jnp.int32, (c, c), 1)
        rel = jnp.abs(row - col).astype(F32)
        pos = (row + d * (c - 1 - 2 * row)).astype(F32)
        for h in heads:
            lg = lg_all[:, h * dk:h * dk + 1]
            dec_scr[h] = jnp.where(before_eq, jnp.exp(rel * lg), 0.0)
            sc_scr[h, 0] = jnp.exp((pos + 1.0) * lg).astype(BF16)
            sc_scr[h, 1] = jnp.exp((c - 1.0 - pos) * lg).astype(BF16)

    def chunk_body(j, carry):
        cj = jnp.where(d == 0, j, n_chunks - 1 - j)
        rows = pl.ds(pl.multiple_of(cj * c, c), c)
        qs = [q_ref[rows, h * dk:(h + 1) * dk] for h in heads]
        ks = [k_ref[rows, h * dk:(h + 1) * dk] for h in heads]
        vs = [v_ref[rows, h * dv:(h + 1) * dv] for h in heads]
        qk = [_dot_nt(qs[h], ks[h]) for h in heads]
        scores = [(qk[h] * dec_scr[h]).astype(BF16) for h in heads]
        st = [st_scr[h] for h in heads]
        y_st = [_dot(qs[h] * sc_scr[h, 0], st[h].astype(BF16)) for h in heads]
        for h in heads:
            y_ref[rows, h * dv:(h + 1) * dv] = (_dot(scores[h], vs[h]) + y_st[h]).astype(y_ref.dtype)
        for h in heads:
            lg = lg_all[:, h * dk:h * dk + 1]
            st_scr[h] = st[h] * jnp.exp(c * lg) + _dot_tn(ks[h] * sc_scr[h, 1], vs[h])
        return carry

    lax.fori_loop(0, n_chunks, chunk_body, 0, unroll=2)

    @pl.when(i == pl.num_programs(2) - 1)
    def _():
        sf_ref[...] = st_scr[...]


def _ret_scan_call(proj, lg_rep, s0, tb):
    b, t, _ = proj.shape
    tb = min(tb, t)
    nb = t // tb
    blk = lambda dd, i: i + dd * (nb - 1 - 2 * i)
    st = pl.BlockSpec((None, None, RET_HEADS, RET_QK_DIM, RET_V_DIM), lambda dd, bi, i: (dd, bi, 0, 0, 0))
    return pl.pallas_call(
        functools.partial(_ret_scan_kernel, n_chunks=tb // SCAN_CHUNK),
        name="ret_scan",
        grid=(2, b, nb),
        in_specs=[pl.BlockSpec((None, tb, RET_QK), lambda dd, bi, i: (bi, blk(dd, i), 0)),
                  pl.BlockSpec((None, tb, RET_QK), lambda dd, bi, i: (bi, blk(dd, i), 1)),
                  pl.BlockSpec((None, tb, RET_V), lambda dd, bi, i: (bi, blk(dd, i), 2 * RET_QK // RET_V)),
                  pl.BlockSpec((None, 1, RET_QK), lambda dd, bi, i: (dd, 0, 0)), st],
        out_specs=[pl.BlockSpec((None, None, tb, RET_V), lambda dd, bi, i: (dd, bi, blk(dd, i), 0)), st],
        out_shape=[jax.ShapeDtypeStruct((2, b, t, RET_V), BF16),
                   jax.ShapeDtypeStruct((2, b, RET_HEADS, RET_QK_DIM, RET_V_DIM), F32)],
        scratch_shapes=[pltpu.VMEM((RET_HEADS, RET_QK_DIM, RET_V_DIM), F32),
                        pltpu.VMEM((RET_HEADS, SCAN_CHUNK, SCAN_CHUNK), F32),
                        pltpu.VMEM((RET_HEADS, 2, SCAN_CHUNK, RET_QK_DIM), BF16)],
        compiler_params=_cparams(("parallel", "parallel", "arbitrary")),
    )(proj, proj, proj, lg_rep, s0)


def _odd_finish_kernel(y_ref, g_ref, x_ref, m_ref, wo_ref, o_ref):
    y = (y_ref[0] + y_ref[1]).astype(F32)
    dv = RET_V_DIM
    parts = []
    for h in range(RET_HEADS):
        yh = y[:, h * dv:(h + 1) * dv]
        parts.append(yh * lax.rsqrt(jnp.mean(yh * yh, axis=-1, keepdims=True) + NORM_EPS))
    yn = jnp.concatenate(parts, axis=1)
    act = (_silu(g_ref[...].astype(F32)) * yn).astype(BF16)
    o_ref[...] = x_ref[...] + m_ref[2:3, :] * _dot(act, wo_ref[...])


def _odd_finish_call(y, proj, x, mods, w_out, tm):
    b, t, d = x.shape
    tm = min(tm, t)
    return pl.pallas_call(
        _odd_finish_kernel,
        name="odd_finish",
        grid=(b, t // tm),
        in_specs=[pl.BlockSpec((2, None, tm, RET_V), lambda bi, i: (0, bi, i, 0)),
                  pl.BlockSpec((None, tm, RET_V), lambda bi, i: (bi, i, (2 * RET_QK + RET_V) // RET_V)),
                  pl.BlockSpec((None, tm, d), lambda bi, i: (bi, i, 0)),
                  pl.BlockSpec((None, N_MOD, d), lambda bi, i: (bi, 0, 0)),
                  pl.BlockSpec((RET_V, d), lambda bi, i: (0, 0))],
        out_specs=pl.BlockSpec((None, tm, d), lambda bi, i: (bi, i, 0)),
        out_shape=jax.ShapeDtypeStruct((b, t, d), F32),
        compiler_params=_cparams(("parallel", "parallel")),
    )(y, proj, x, mods, w_out)


def _ffn_kernel(*refs, on_grid, final_norm):
    if on_grid:
        x_ref, xn_ref, g_ref, m_ref, wu_ref, cw_ref, cb_ref, wd_ref = refs[:8]
        rest = refs[8:]
    else:
        x_ref, g_ref, m_ref, wu_ref, cw_ref, cb_ref, wd_ref = refs[:7]
        rest = refs[7:]
    if final_norm:
        fg_ref, o_ref, *scrs = rest
    else:
        o_ref, *scrs = rest
    gate_scr, val_scr, *scrs = scrs
    if on_grid:
        gtop_scr, *act_scrs = scrs
    else:
        act_scrs = scrs
    i = pl.program_id(1)
    n_tiles = pl.num_programs(1)
    tm = x_ref.shape[0]

    def norm_mod(xv):
        hv = xv * lax.rsqrt(jnp.mean(xv * xv, axis=-1, keepdims=True) + NORM_EPS) * g_ref[...]
        return (hv * (1.0 + m_ref[4:5, :]) + m_ref[3:4, :]).astype(BF16)

    x = x_ref[...]
    h = norm_mod(x)
    row = lax.broadcasted_iota(jnp.int32, (tm, 1), 0)
    if on_grid:
        col = row % GRID_W
        ok_left = col > 0
        ok_right = col < GRID_W - 1
        zero = jnp.zeros((GRID_W, x.shape[1]), BF16)
        h_ext = jnp.concatenate([h, jnp.where(i < n_tiles - 1, norm_mod(xn_ref[...]), zero)], axis=0)

        @pl.when(i == 0)
        def _():
            gtop_scr[...] = jnp.zeros_like(gtop_scr)
    else:
        ok_left = row > 0
        ok_right = row < tm - 1
    n_chunks = D_FF // FFN_COL_CHUNK

    def up_proj(j):
        cols = slice(j * FFN_COL_CHUNK, (j + 1) * FFN_COL_CHUNK)
        vcols = slice(D_FF + j * FFN_COL_CHUNK, D_FF + (j + 1) * FFN_COL_CHUNK)
        if on_grid:
            gate_scr[j % 2, GRID_W:, :] = _dot(h_ext, wu_ref[:, cols])
        else:
            gate_scr[j % 2] = _dot(h, wu_ref[:, cols])
        val_scr[j % 2] = _dot(h, wu_ref[:, vcols])

    out = None
    up_proj(0)
    for j in range(n_chunks):
        cols = slice(j * FFN_COL_CHUNK, (j + 1) * FFN_COL_CHUNK)
        buf = j % 2
        if j + 1 < n_chunks:
            up_proj(j + 1)
        grp, slot = divmod(j, FFN_DOWN_GROUP)
        act_scr = act_scrs[grp]
        acols = slice(slot * FFN_COL_CHUNK, (slot + 1) * FFN_COL_CHUNK)
        if on_grid:
            gate_scr[buf, 0:GRID_W, :] = gtop_scr[:, cols]
            gtop_scr[:, cols] = gate_scr[buf, tm:tm + GRID_W, :]
            rows3 = [gate_scr[buf, dr * GRID_W:dr * GRID_W + tm, :] for dr in range(3)]
            taps = [rows3[0] * cw_ref[dc:dc + 1, cols] + rows3[1] * cw_ref[3 + dc:4 + dc, cols]
                    + rows3[2] * cw_ref[6 + dc:7 + dc, cols] for dc in range(3)]
        else:
            gate = gate_scr[buf]
            taps = [gate * cw_ref[3 + dc:4 + dc, cols] for dc in range(3)]
        acc = cb_ref[:, cols] + taps[1] + jnp.where(ok_left, pltpu.roll(taps[0], 1, axis=0), 0.0) \
            + jnp.where(ok_right, pltpu.roll(taps[2], tm - 1, axis=0), 0.0)
        act_scr[:, acols] = (_gelu_tanh(acc) * val_scr[buf]).astype(BF16)
        if slot + 1 == FFN_DOWN_GROUP or j + 1 == n_chunks:
            width = (slot + 1) * FFN_COL_CHUNK
            k0 = grp * FFN_DOWN_GROUP * FFN_COL_CHUNK
            part = _dot(act_scr[:, 0:width], wd_ref[k0:k0 + width, :])
            out = part if out is None else out + part
    out = x + m_ref[5:6, :] * out
    if final_norm:
        out = out * lax.rsqrt(jnp.mean(out * out, axis=-1, keepdims=True) + NORM_EPS) * fg_ref[...]
    o_ref[...] = out


def _ffn_call(x, norm_g, mods, w_up, conv_w9, conv_b, w_down, *, tm, on_grid, final_g=None):
    b, t, d = x.shape
    tm = min(tm, t)
    n_tiles = t // tm
    per_batch = mods.shape[0] > 1
    full = lambda shape: pl.BlockSpec(shape, lambda bi, i: (0,) * len(shape))
    in_specs = [pl.BlockSpec((None, tm, d), lambda bi, i: (bi, i, 0))]
    args = [x]
    scratch = [pltpu.VMEM((tm, FFN_DOWN_GROUP * FFN_COL_CHUNK), BF16)
               for _ in range(-(-D_FF // (FFN_DOWN_GROUP * FFN_COL_CHUNK)))]
    if on_grid:
        r = tm // GRID_W
        last = t // GRID_W - 1
        in_specs.append(pl.BlockSpec((None, GRID_W, d), lambda bi, i: (bi, jnp.minimum((i + 1) * r, last), 0)))
        args.append(x)
        scratch.insert(0, pltpu.VMEM((GRID_W, D_FF), F32))
    else:
        assert n_tiles == 1
    gate_rows = tm + 2 * GRID_W if on_grid else tm
    scratch = [pltpu.VMEM((2, gate_rows, FFN_COL_CHUNK), F32),
               pltpu.VMEM((2, tm, FFN_COL_CHUNK), F32)] + scratch
    in_specs += [full((1, d)),
                 pl.BlockSpec((None, N_MOD, d), (lambda bi, i: (bi, 0, 0)) if per_batch
                              else (lambda bi, i: (0, 0, 0))),
                 full((d, 2 * D_FF)), full((9, D_FF)), full((1, D_FF)), full((D_FF, d))]
    args += [norm_g, mods, w_up, conv_w9, conv_b, w_down]
    if final_g is not None:
        in_specs.append(full((1, d)))
        args.append(final_g)
    return pl.pallas_call(
        functools.partial(_ffn_kernel, on_grid=on_grid, final_norm=final_g is not None),
        name="conv_ffn",
        grid=(b, n_tiles),
        in_specs=in_specs,
        out_specs=pl.BlockSpec((None, tm, d), lambda bi, i: (bi, i, 0)),
        out_shape=jax.ShapeDtypeStruct((b, t, d), F32),
        scratch_shapes=scratch,
        compiler_params=_cparams(("parallel", "arbitrary")),
    )(*args)


def _block_diag2(a, b):
    za = jnp.zeros((a.shape[0], b.shape[1]), a.dtype)
    zb = jnp.zeros((b.shape[0], a.shape[1]), a.dtype)
    return jnp.concatenate([jnp.concatenate([a, za], axis=1), jnp.concatenate([zb, b], axis=1)], axis=0)


def _pad_cols(a, n, fill=0.0):
    return jnp.pad(a, ((0, 0), (0, n - a.shape[1])), constant_values=fill)


def _even_params(j, ev_w_in, ev_mu_prev, ev_mu_next, rk_w0_f, rk_w0_b, rk_w2_f, rk_w2_b, rk_a0_f,
                 rk_a0_b, rk_a2_f, rk_a2_b, rk_g2, rk_k_k, rk_k_a, rk_r_k, rk_ln_w, rk_ln_b,
                 ssd_conv_w, ssd_conv_b, ssd_dt_bias_f, ssd_dt_bias_b, ssd_a_log_f, ssd_a_log_b,
                 ssd_d, ssd_norm_w, ev_w_out):
    w_in = ev_w_in[j]
    rw = w_in[:, :RWKV_COLS]
    z = w_in[:, RWKV_COLS:RWKV_COLS + SSD_WIDTH]
    xbc = w_in[:, RWKV_COLS + SSD_WIDTH:RWKV_COLS + SSD_WIDTH + SSD_XBC]
    dts = w_in[:, RWKV_COLS + SSD_WIDTH + SSD_XBC:]
    w_packed = jnp.concatenate([_pad_cols(jnp.concatenate([rw, dts], axis=1), EV_RW_BLOCK), xbc, z], axis=1)
    head = jnp.arange(V7X_MXU_DIM) // RWKV_HEAD_DIM
    head_sum = (head[:, None] == head[None, :]).astype(BF16)
    lane = jnp.arange(V7X_LANES)[:, None]
    tgt = jnp.arange(2 * SSD_WIDTH)[None, :]
    dt_expand = (lane == (tgt // SSD_WIDTH) * SSD_HEADS + (tgt % SSD_WIDTH) // SSD_HEAD_DIM).astype(BF16)
    rep = lambda a: jnp.repeat(a, SSD_HEAD_DIM)[None, :]
    row = lambda a: a[None, :]
    return {
        'w_in': w_packed.astype(BF16),
        'mu_self': _pad_cols(row(1.0 - ev_mu_prev[j] - ev_mu_next[j]), EV_RW_BLOCK, 1.0),
        'mu_prev': _pad_cols(row(ev_mu_prev[j]), EV_RW_BLOCK),
        'mu_next': _pad_cols(row(ev_mu_next[j]), EV_RW_BLOCK),
        'w0': row(jnp.concatenate([rk_w0_f[j], rk_w0_b[j]])),
        'w2': _block_diag2(rk_w2_f[j], rk_w2_b[j]).astype(BF16),
        'a0': row(jnp.concatenate([rk_a0_f[j], rk_a0_b[j]])),
        'a2': _block_diag2(rk_a2_f[j], rk_a2_b[j]).astype(BF16),
        'g2': rk_g2[j].astype(BF16),
        'k_k': row(rk_k_k[j]), 'k_a': row(rk_k_a[j]), 'r_k': row(rk_r_k[j].reshape(-1)),
        'ln_w': row(rk_ln_w[j]), 'ln_b': row(rk_ln_b[j]),
        'head_sum': head_sum,
        'conv_w': ssd_conv_w[j], 'conv_b': row(ssd_conv_b[j]),
        'dt_bias': _pad_cols(row(jnp.concatenate([ssd_dt_bias_f[j], ssd_dt_bias_b[j]])), V7X_LANES),
        'dt_expand': dt_expand,
        'a_rep': jnp.stack([rep(-jnp.exp(ssd_a_log_f[j])), rep(-jnp.exp(ssd_a_log_b[j]))]),
        'd_skip': rep(ssd_d[j]),
        'norm_w': row(ssd_norm_w[j]),
        'w_out': ev_w_out[j].astype(BF16),
    }


def _rope_tables(t):
    n = RET_QK_DIM // 4
    pos = jnp.arange(t)
    row = (pos // GRID_W).astype(F32)
    col = (pos % GRID_W).astype(F32)
    inv = ROPE_BASE ** (-jnp.arange(n, dtype=F32) / n)
    ang = jnp.concatenate([row[:, None] * inv, col[:, None] * inv], axis=-1)
    cos, sin = jnp.cos(ang), jnp.sin(ang)
    return jnp.concatenate([cos, cos], axis=-1), jnp.concatenate([-sin, sin], axis=-1)


def _conv_ffn(x, mods, norm_g, w_up, conv_w9, conv_b, w_down, *, on_grid, final_g=None):
    return _ffn_call(x, norm_g, mods, w_up, conv_w9, conv_b, w_down, tm=512, on_grid=on_grid,
                     final_g=final_g)


def _even_layer(x, ctx, mods_x, mods_c, norm_g, p):
    b = x.shape[0]

    def features(h, mods):
        *feat, xbc, dtbc, z = _even_feat_call(h, norm_g, mods, p, 256)
        return z, feat, xbc, dtbc

    proj_c, feat_c, xbc_c, dt_c = features(ctx, mods_c)
    proj_x, feat_x, xbc_x, dt_x = features(x, mods_x)
    s0 = jnp.zeros((2, b, N_PAIRS, PAIR, PAIR), F32)
    h0 = jnp.zeros((2, b, N_PAIRS, SSD_STATE, PAIR), F32)
    *y_c, s_ctx, h_ctx = _even_scan_call(feat_c, xbc_c, dt_c, p['a_rep'], s0, h0, 256)
    *y_x, _, _ = _even_scan_call(feat_x, xbc_x, dt_x, p['a_rep'], s_ctx, h_ctx, 512)
    yrk_c, ysd_c = y_c[0:2], y_c[2:4]
    yrk_x, ysd_x = y_x[0:2], y_x[2:4]
    x = _even_finish_call(yrk_x, feat_x[3], feat_x[4], ysd_x, xbc_x, proj_x, x, mods_x, p, 512)
    ctx = _even_finish_call(yrk_c, feat_c[3], feat_c[4], ysd_c, xbc_c, proj_c, ctx, mods_c, p, 256)
    return x, ctx


def _odd_layer(x, ctx, mods_x, mods_c, norm_g, w_in, lg_rep, w_out):
    b, t, _ = x.shape
    proj_c = _nm_call(ctx, norm_g, mods_c, w_in, shift_row=0, tm=256, tn=512, out_dtype=BF16,
                      qk_mode='scale')
    proj_x = _nm_call(x, norm_g, mods_x, w_in, shift_row=0, tm=512, tn=512, out_dtype=BF16,
                      qk_mode='rope', rope=_rope_tables(t))
    s0 = jnp.zeros((2, b, RET_HEADS, RET_QK_DIM, RET_V_DIM), F32)
    _, s_ctx = _ret_scan_call(proj_c, lg_rep, s0, 256)
    y, _ = _ret_scan_call(proj_x, lg_rep, s_ctx, 512)
    return _odd_finish_call(y, proj_x, x, mods_x, w_out, 512)


def kernel(x, c, ctx, c_ctx, mod_w, mod_b, norm1_g, norm2_g, ffn_w_up, ffn_conv_w, ffn_conv_b, ffn_w_down, ev_w_in, ev_mu_prev, ev_mu_next, rk_w0_f, rk_w0_b, rk_w2_f, rk_w2_b, rk_a0_f, rk_a0_b, rk_a2_f, rk_a2_b, rk_g2, rk_k_k, rk_k_a, rk_r_k, rk_ln_w, rk_ln_b, ssd_conv_w, ssd_conv_b, ssd_dt_bias_f, ssd_dt_bias_b, ssd_a_log_f, ssd_a_log_b, ssd_d, ssd_norm_w, ev_w_out, ret_w_in, ret_log2_f, ret_log2_b, ret_w_out, final_norm_g):
    b, t, d = x.shape
    depth = mod_w.shape[0]
    rows = -(-(b + 1) // V7X_SUBLANES) * V7X_SUBLANES
    cond = jnp.concatenate([c, c_ctx[None, :], jnp.zeros((rows - b - 1, d), F32)], axis=0)
    mods = _mod_call(cond, mod_w, mod_b).reshape(depth, rows, N_MOD, d)
    for i in range(depth):
        need_ctx = i < depth - 1
        mods_x = mods[i, :b]
        mods_c = mods[i, b:b + 1]
        j = i // 2
        g1 = norm1_g[i][None, :]
        if i % 2 == 0:
            p = _even_params(j, ev_w_in, ev_mu_prev, ev_mu_next, rk_w0_f, rk_w0_b, rk_w2_f, rk_w2_b,
                             rk_a0_f, rk_a0_b, rk_a2_f, rk_a2_b, rk_g2, rk_k_k, rk_k_a, rk_r_k,
                             rk_ln_w, rk_ln_b, ssd_conv_w, ssd_conv_b, ssd_dt_bias_f, ssd_dt_bias_b,
                             ssd_a_log_f, ssd_a_log_b, ssd_d, ssd_norm_w, ev_w_out)
            x, ctx_mixed = _even_layer(x, ctx, mods_x, mods_c, g1, p)
        else:
            lg = jnp.stack([jnp.log1p(-jnp.exp2(-ret_log2_f[j])), jnp.log1p(-jnp.exp2(-ret_log2_b[j]))])
            lg_rep = jnp.repeat(lg, RET_QK_DIM, axis=-1)[:, None, :]
            x = _odd_layer(x, ctx, mods_x, mods_c, g1, ret_w_in[j].astype(BF16), lg_rep,
                           ret_w_out[j].astype(BF16))
            ctx_mixed = None
        g2 = norm2_g[i][None, :]
        w_up = ffn_w_up[i].astype(BF16)
        w_down = ffn_w_down[i].astype(BF16)
        conv_w9 = ffn_conv_w[i].reshape(9, D_FF)
        conv_b = ffn_conv_b[i][None, :]
        last = i == depth - 1
        x = _conv_ffn(x, mods_x, g2, w_up, conv_w9, conv_b, w_down, on_grid=True,
                      final_g=final_norm_g[None, :] if last else None)
        if need_ctx:
            ctx = _conv_ffn(ctx_mixed, mods_c, g2, w_up, conv_w9, conv_b, w_down, on_grid=False)
    return x
```

```python
import functools
import math

import jax
import jax.numpy as jnp
from jax import lax
from jax.experimental import pallas as pl
from jax.experimental.pallas import tpu as pltpu

F32 = jnp.float32
BF16 = jnp.bfloat16

D_MODEL = 1024
GRID_W = 64
N_MOD = 6
NORM_EPS = 1e-6
RWKV_HEADS = 8
RWKV_HEAD_DIM = 64
RWKV_WIDTH = RWKV_HEADS * RWKV_HEAD_DIM
DECAY_LORA = 64
ICLR_LORA = 64
GATE_LORA = 128
RWKV_GN_EPS = 64e-5
RWKV_COLS = 3 * RWKV_WIDTH + 2 * DECAY_LORA + 2 * ICLR_LORA + GATE_LORA
SSD_HEADS = 8
SSD_HEAD_DIM = 64
SSD_WIDTH = SSD_HEADS * SSD_HEAD_DIM
SSD_GROUPS = 2
SSD_STATE = 128
SSD_XBC = SSD_WIDTH + 2 * SSD_GROUPS * SSD_STATE
RET_HEADS = 8
RET_QK_DIM = 128
RET_V_DIM = 256
RET_QK = RET_HEADS * RET_QK_DIM
RET_V = RET_HEADS * RET_V_DIM
ROPE_BASE = 10000.0
D_FF = 2816

V7X_LANES = 128
V7X_SUBLANES = 8
V7X_MXU_DIM = 256
V7X_VMEM_LIMIT_BYTES = 56 * 1024 * 1024

RWKV_CHUNK = 64
RWKV_CHUNKS_PER_STEP = 2
SCAN_CHUNK = 128
FFN_COL_CHUNK = 256
FFN_DOWN_GROUP = 6
FFN_UP_BUFFERS = 2
PAIR = 2 * RWKV_HEAD_DIM
N_PAIRS = RWKV_HEADS // 2
EV_RW_BLOCK = 2048
EV_DT_OFF = RWKV_COLS
EV_XBC_OFF = EV_RW_BLOCK
EV_Z_OFF = EV_RW_BLOCK + SSD_XBC
EV_COLS = EV_Z_OFF + SSD_WIDTH


def _cparams(sem):
    return pltpu.CompilerParams(dimension_semantics=sem, vmem_limit_bytes=V7X_VMEM_LIMIT_BYTES)


def _split3(x):
    hi = x.astype(BF16)
    r1 = x - hi.astype(F32)
    mid = r1.astype(BF16)
    lo = (r1 - mid.astype(F32)).astype(BF16)
    return hi, mid, lo


def _dot(a, b):
    return jnp.dot(a, b, preferred_element_type=F32)


def _dot_nt(a, b):
    return lax.dot_general(a, b, (((1,), (1,)), ((), ())), preferred_element_type=F32)


def _dot_tn(a, b):
    return lax.dot_general(a, b, (((0,), (0,)), ((), ())), preferred_element_type=F32)


def _dot01(x, m01):
    hi, mid, lo = _split3(x)
    return _dot(hi, m01) + _dot(mid, m01) + _dot(lo, m01)


def _head_sum(x, j01):
    hi = x.astype(BF16)
    lo = (x - hi.astype(F32)).astype(BF16)
    n = j01.shape[0]
    return jnp.concatenate([_dot(hi[:, g * n:(g + 1) * n], j01) + _dot(lo[:, g * n:(g + 1) * n], j01)
                            for g in range(x.shape[1] // n)], axis=1)


def _dot01_left(m01, x):
    hi, mid, lo = _split3(x)
    return _dot(m01, hi) + _dot(m01, mid) + _dot(m01, lo)


def _sigmoid(x):
    return 1.0 / (1.0 + jnp.exp(-x))


def _silu(x):
    return x * _sigmoid(x)


def _softplus(x):
    return jnp.maximum(x, 0.0) + jnp.log1p(jnp.exp(-jnp.abs(x)))


def _gelu_tanh(x):
    c = math.sqrt(2.0 / math.pi)
    half = 0.5 * x
    return half + half * jnp.tanh(x * (c + (0.044715 * c) * (x * x)))


def _order_masks(d, n):
    row = lax.broadcasted_iota(jnp.int32, (n, n), 0)
    col = lax.broadcasted_iota(jnp.int32, (n, n), 1)
    diff = (row - col) * (1 - 2 * d)
    return diff > 0, diff >= 0


def _mod_kernel(c_ref, w_ref, b_ref, o_ref):
    h = _silu(c_ref[...])
    hi, mid, lo = _split3(h)
    w = w_ref[...]
    wh = w.astype(BF16)
    wl = (w - wh.astype(F32)).astype(BF16)
    acc = _dot(hi, wh) + _dot(mid, wh) + _dot(hi, wl)
    o_ref[...] = acc + b_ref[...]


def _mod_call(cond, mod_w, mod_b):
    depth, d, n = mod_w.shape
    rows = cond.shape[0]
    tn = 1024
    return pl.pallas_call(
        _mod_kernel,
        name="adaln_mod",
        grid=(depth, n // tn),
        in_specs=[pl.BlockSpec((rows, d), lambda l, j: (0, 0)),
                  pl.BlockSpec((None, d, tn), lambda l, j: (l, 0, j)),
                  pl.BlockSpec((None, 1, tn), lambda l, j: (l, 0, j))],
        out_specs=pl.BlockSpec((None, rows, tn), lambda l, j: (l, 0, j)),
        out_shape=jax.ShapeDtypeStruct((depth, rows, n), F32),
        compiler_params=_cparams(("parallel", "parallel")),
    )(cond, mod_w, mod_b.reshape(depth, 1, n))


def _nm_kernel(*refs, shift_row, tn, qk_mode):
    if qk_mode == 'rope':
        x_ref, g_ref, m_ref, w_ref, cos_ref, sin_ref, o_ref = refs
    else:
        x_ref, g_ref, m_ref, w_ref, o_ref = refs
    x = x_ref[...]
    h = x * lax.rsqrt(jnp.mean(x * x, axis=-1, keepdims=True) + NORM_EPS) * g_ref[...]
    h = h * (1.0 + m_ref[shift_row + 1:shift_row + 2, :]) + m_ref[shift_row:shift_row + 1, :]
    h = h.astype(BF16)
    dk = RET_QK_DIM
    for j in range(w_ref.shape[1] // tn):
        cols = slice(j * tn, (j + 1) * tn)
        y = _dot(h, w_ref[:, cols])
        if qk_mode is not None and j * tn < 2 * RET_QK:
            scale = dk ** -0.5 if j * tn >= RET_QK else 1.0
            heads = []
            for hh in range(tn // dk):
                yh = y[:, hh * dk:(hh + 1) * dk]
                if qk_mode == 'rope':
                    yh = yh * cos_ref[...] + pltpu.roll(yh, dk // 2, axis=1) * sin_ref[...]
                heads.append(yh * scale if scale != 1.0 else yh)
            y = jnp.concatenate(heads, axis=1)
        o_ref[:, cols] = y.astype(o_ref.dtype)


def _nm_call(x, g, mods, w, *, shift_row, tm, tn, out_dtype=F32, qk_mode=None, rope=None):
    b, t, d = x.shape
    n = w.shape[1]
    tm = min(tm, t)
    per_batch = mods.shape[0] > 1
    in_specs = [pl.BlockSpec((None, tm, d), lambda bi, i: (bi, i, 0)),
                pl.BlockSpec((1, d), lambda bi, i: (0, 0)),
                pl.BlockSpec((None, N_MOD, d), (lambda bi, i: (bi, 0, 0)) if per_batch
                             else (lambda bi, i: (0, 0, 0))),
                pl.BlockSpec((d, n), lambda bi, i: (0, 0))]
    args = [x, g, mods, w]
    if qk_mode == 'rope':
        tab = pl.BlockSpec((tm, RET_QK_DIM), lambda bi, i: (i, 0))
        in_specs += [tab, tab]
        args += list(rope)
    return pl.pallas_call(
        functools.partial(_nm_kernel, shift_row=shift_row, tn=tn, qk_mode=qk_mode),
        name="norm_mod_matmul",
        grid=(b, t // tm),
        in_specs=in_specs,
        out_specs=pl.BlockSpec((None, tm, n), lambda bi, i: (bi, i, 0)),
        out_shape=jax.ShapeDtypeStruct((b, t, n), out_dtype),
        compiler_params=_cparams(("parallel", "parallel")),
    )(*args)


def _even_feat_kernel(x_ref, xp_ref, xn_ref, g_ref, m_ref, w_ref,
                      mus_ref, mup_ref, mun_ref, w0_ref, w2_ref, a0_ref, a2_ref,
                      g2_ref, kk_ref, ka_ref, rk_ref, j_ref, cw_ref, cb_ref, dtb_ref, e_ref,
                      r_ref, v_ref, kkn_ref, bonus_ref, gate_ref, ld_ref, kd_ref, bd_ref,
                      xbc_ref, dtbc_ref, z_ref):
    i = pl.program_id(1)
    n_tiles = pl.num_programs(1)
    tm = x_ref.shape[0]
    halo = V7X_SUBLANES
    ext = tm + 2 * halo

    def norm_mod(xv):
        hv = xv * lax.rsqrt(jnp.mean(xv * xv, axis=-1, keepdims=True) + NORM_EPS) * g_ref[...]
        return (hv * (1.0 + m_ref[1:2, :]) + m_ref[0:1, :]).astype(BF16)

    h = norm_mod(x_ref[...])
    zero = jnp.zeros((halo, x_ref.shape[1]), BF16)
    h_ext = jnp.concatenate([jnp.where(i > 0, norm_mod(xp_ref[...]), zero), h,
                             jnp.where(i < n_tiles - 1, norm_mod(xn_ref[...]), zero)], axis=0)

    def proj3(cols):
        ye = _dot(h_ext, w_ref[:, cols])
        return (ye[halo:halo + tm], pltpu.roll(ye, 1, axis=0)[halo:halo + tm],
                pltpu.roll(ye, ext - 1, axis=0)[halo:halo + tm])

    w = RWKV_WIDTH
    rw = []
    for j in range(EV_RW_BLOCK // w):
        cols = slice(j * w, (j + 1) * w)
        cur, prev, nxt = proj3(cols)
        rw.append(cur * mus_ref[:, cols] + prev * mup_ref[:, cols] + nxt * mun_ref[:, cols])
    r, k, v, lora = rw
    wd = lora[:, 0:2 * DECAY_LORA]
    ad = lora[:, 2 * DECAY_LORA:2 * DECAY_LORA + 2 * ICLR_LORA]
    gd = lora[:, 2 * DECAY_LORA + 2 * ICLR_LORA:2 * DECAY_LORA + 2 * ICLR_LORA + GATE_LORA]
    dt_raw = lora[:, EV_DT_OFF - 3 * w:EV_RW_BLOCK - 3 * w]
    jm = j_ref[...]

    for j in range(SSD_XBC // w):
        cols = slice(j * w, (j + 1) * w)
        cur, prev, nxt = proj3(slice(EV_XBC_OFF + j * w, EV_XBC_OFF + (j + 1) * w))
        y = prev * cw_ref[0:1, cols] + cur * cw_ref[1:2, cols] + nxt * cw_ref[2:3, cols] + cb_ref[:, cols]
        xbc_ref[:, cols] = _silu(y).astype(xbc_ref.dtype)
    z_ref[...] = _dot(h, w_ref[:, EV_Z_OFF:EV_Z_OFF + SSD_WIDTH]).astype(z_ref.dtype)
    dt = _dot01(_softplus(dt_raw + dtb_ref[...]), e_ref[...])
    dtbc_ref[0] = dt[:, 0:SSD_WIDTH]
    dtbc_ref[1] = dt[:, SSD_WIDTH:2 * SSD_WIDTH]

    kk = k * kk_ref[...]
    ss = _head_sum(kk * kk, jm)
    kk = kk / jnp.maximum(jnp.sqrt(ss), 1e-12)
    r_ref[...] = r.astype(r_ref.dtype)
    v_ref[...] = v.astype(v_ref.dtype)
    kkn_ref[...] = kk.astype(kkn_ref.dtype)
    bonus_ref[...] = (_head_sum(r * k * rk_ref[...], jm) * v).astype(bonus_ref.dtype)
    gate_ref[...] = _dot(_sigmoid(gd).astype(BF16), g2_ref[...]).astype(gate_ref.dtype)

    zw = _dot(jnp.tanh(wd).astype(BF16), w2_ref[...]) + w0_ref[...]
    za = _dot(ad.astype(BF16), a2_ref[...]) + a0_ref[...]
    for di in range(2):
        ld_ref[di] = -math.exp(-0.5) * _sigmoid(zw[:, di * w:(di + 1) * w])
        iclr = _sigmoid(za[:, di * w:(di + 1) * w])
        kd_ref[di] = (k * (1.0 + (iclr - 1.0) * ka_ref[...])).astype(kd_ref.dtype)
        bd_ref[di] = (kk * iclr).astype(bd_ref.dtype)


def _even_feat_call(x, norm_g, mods, p, tm):
    b, t, d = x.shape
    tm = min(tm, t)
    n_tiles = t // tm
    w = RWKV_WIDTH
    per_batch = mods.shape[0] > 1
    r8 = tm // V7X_SUBLANES
    last = t // V7X_SUBLANES - 1
    full = lambda shape: pl.BlockSpec(shape, lambda bi, i: (0,) * len(shape))
    tok = pl.BlockSpec((None, tm, w), lambda bi, i: (bi, i, 0))
    tok2 = pl.BlockSpec((2, None, tm, w), lambda bi, i: (0, bi, i, 0))
    sd = jax.ShapeDtypeStruct((b, t, w), BF16)
    sd2 = jax.ShapeDtypeStruct((2, b, t, w), BF16)
    f2 = jax.ShapeDtypeStruct((2, b, t, w), F32)
    return pl.pallas_call(
        _even_feat_kernel,
        name="even_feat",
        grid=(b, n_tiles),
        in_specs=[pl.BlockSpec((None, tm, d), lambda bi, i: (bi, i, 0)),
                  pl.BlockSpec((None, V7X_SUBLANES, d), lambda bi, i: (bi, jnp.maximum(i * r8 - 1, 0), 0)),
                  pl.BlockSpec((None, V7X_SUBLANES, d), lambda bi, i: (bi, jnp.minimum((i + 1) * r8, last), 0)),
                  full((1, d)),
                  pl.BlockSpec((None, N_MOD, d), (lambda bi, i: (bi, 0, 0)) if per_batch
                               else (lambda bi, i: (0, 0, 0))),
                  full((d, EV_COLS)),
                  full((1, EV_RW_BLOCK)), full((1, EV_RW_BLOCK)), full((1, EV_RW_BLOCK)),
                  full((1, 2 * w)), full((2 * DECAY_LORA, 2 * w)),
                  full((1, 2 * w)), full((2 * ICLR_LORA, 2 * w)),
                  full((GATE_LORA, w)), full((1, w)), full((1, w)), full((1, w)),
                  full((V7X_MXU_DIM, V7X_MXU_DIM)),
                  full((3, SSD_XBC)), full((1, SSD_XBC)), full((1, V7X_LANES)),
                  full((V7X_LANES, 2 * SSD_WIDTH))],
        out_specs=[tok, tok, tok, tok, tok, tok2, tok2, tok2,
                   pl.BlockSpec((None, tm, SSD_XBC), lambda bi, i: (bi, i, 0)), tok2, tok],
        out_shape=[sd, sd, sd, sd, sd, f2, sd2, sd2,
                   jax.ShapeDtypeStruct((b, t, SSD_XBC), BF16), f2, sd],
        compiler_params=_cparams(("parallel", "parallel")),
    )(x, x, x, norm_g, mods, p['w_in'], p['mu_self'], p['mu_prev'], p['mu_next'], p['w0'], p['w2'],
      p['a0'], p['a2'], p['g2'], p['k_k'], p['k_a'], p['r_k'], p['head_sum'],
      p['conv_w'], p['conv_b'], p['dt_bias'], p['dt_expand'])


def _rwkv_scan_body(rf_ref, vf_ref, kkf_ref, ldf_ref, kdf_ref, bdf_ref,
                    rb_ref, vb_ref, kkb_ref, ldb_ref, kdb_ref, bdb_ref, yf_ref, yb_ref, st_scr, n_chunks):
    c = RWKV_CHUNK
    rowc = lax.broadcasted_iota(jnp.int32, (c, 2 * c), 0)
    colc = lax.broadcasted_iota(jnp.int32, (c, 2 * c), 1) % c
    eye_wide = (rowc == colc).astype(F32)
    m0 = lax.broadcasted_iota(jnp.int32, (c, PAIR), 1) < RWKV_HEAD_DIM
    row1 = lax.broadcasted_iota(jnp.int32, (c, 1), 0)
    incl01, strict_wide, incl_wide, rsel = [], [], [], []
    for d in range(2):
        sgn = 1 - 2 * d
        incl01.append(_order_masks(d, c)[1].astype(BF16))
        strict_wide.append((rowc - colc) * sgn > 0)
        incl_wide.append((rowc - colc) * sgn >= 0)
        rsel.append(row1 == (c - 1 if d == 0 else 0))
    refs = ((rf_ref, vf_ref, kkf_ref, ldf_ref, kdf_ref, bdf_ref, yf_ref),
            (rb_ref, vb_ref, kkb_ref, ldb_ref, kdb_ref, bdb_ref, yb_ref))
    chains = [(d, p) for d in range(2) for p in range(N_PAIRS)]
    lanes = [slice(p * PAIR, (p + 1) * PAIR) for p in range(N_PAIRS)]

    def stack(x):
        return jnp.concatenate([jnp.where(m0, x, 0.0), jnp.where(m0, 0.0, x)], axis=0)

    sub = range(RWKV_CHUNKS_PER_STEP)
    items = [(u, d, p) for u in sub for d, p in chains]

    def stages(j):
        def rows_of(u, d):
            cj = j * len(sub) + u
            return pl.ds(pl.multiple_of((cj if d == 0 else n_chunks - 1 - cj) * c, c), c)

        rows = {(u, d): rows_of(u, d) for u in sub for d in range(2)}
        cs_all = {ud: _dot01_left(incl01[ud[1]], refs[ud[1]][3][rows[ud], :]) for ud in rows}
        yield
        a_s, r_t, b_s, k_s, v_s, b_end, k_end, g_tot = [], [], [], [], [], [], [], []
        for u, d, p in items:
            r_ref, v_ref, kk_ref, ld_ref, kd_ref, bd_ref, _ = refs[d]
            rw, ln = rows[u, d], lanes[p]
            cs = cs_all[u, d][:, ln]
            cs_last = jnp.sum(jnp.where(rsel[d], cs, 0.0), axis=0, keepdims=True)
            g_neg = jnp.exp(-cs)
            g_end = jnp.exp(cs_last - cs)
            g_tot.append(jnp.exp(cs_last))
            kdv = kd_ref[rw, ln].astype(F32)
            bdv = bd_ref[rw, ln].astype(F32)
            a_w = (-kk_ref[rw, ln].astype(F32) * jnp.exp(cs - ld_ref[rw, ln]))
            a_s.append((a_w.astype(BF16), stack(a_w).astype(BF16)))
            r_t.append((r_ref[rw, ln].astype(F32) * jnp.exp(cs)).astype(BF16))
            b_s.append(stack(bdv * g_neg).astype(BF16))
            k_s.append(stack(kdv * g_neg).astype(BF16))
            v_s.append(stack(v_ref[rw, ln].astype(F32)).astype(BF16))
            b_end.append(stack(bdv * g_end).astype(BF16))
            k_end.append(stack(kdv * g_end).astype(BF16))
            if p == N_PAIRS - 1:
                yield
        n = range(len(items))
        dirs = [d for _, d, _ in items]
        gram = [_dot_nt(jnp.concatenate([a_s[q][0], r_t[q]], axis=0),
                        jnp.concatenate([b_s[q], k_s[q]], axis=0)) for q in n]
        yield
        a_ab =[jnp.where(strict_wide[dirs[q]], gram[q][0:c, 0:2 * c], 0.0) for q in n]
        a_ak = [jnp.where(strict_wide[dirs[q]], gram[q][0:c, 2 * c:4 * c], 0.0).astype(BF16) for q in n]
        p_rb = [jnp.where(incl_wide[dirs[q]], gram[q][c:2 * c, 0:2 * c], 0.0).astype(BF16) for q in n]
        p_rk = [jnp.where(incl_wide[dirs[q]], gram[q][c:2 * c, 2 * c:4 * c], 0.0).astype(BF16) for q in n]
        minv = [eye_wide + a_ab[q] for q in n]
        pw = [a_ab[q] for q in n]
        pw = [_dot(pw[q].astype(BF16), stack(pw[q]).astype(BF16)) for q in n]
        yield
        levels = int(math.log2(c))
        for k in range(2, levels):
            both = [_dot(pw[q].astype(BF16),
                         jnp.concatenate([stack(pw[q]), stack(minv[q])], axis=1).astype(BF16)) for q in n]
            minv = [minv[q] + both[q][:, 2 * c:4 * c] for q in n]
            pw = [both[q][:, 0:2 * c] for q in n]
            yield
        minv = [minv[q] + _dot(pw[q].astype(BF16), stack(minv[q]).astype(BF16)) for q in n]
        akv = [_dot(a_ak[q], v_s[q]) for q in n]
        yield
        eff = [_dot(minv[q].astype(BF16),
                    jnp.concatenate([a_s[q][1], stack(akv[q]).astype(BF16)], axis=1)) for q in n]
        bk_end = [jnp.concatenate([b_end[q], k_end[q]], axis=0) for q in n]
        p_both = [jnp.concatenate([p_rb[q], p_rk[q]], axis=1) for q in n]
        yield
        st = [st_scr[d, p] for d, p in chains]
        nc = range(len(chains))
        for u in sub:
            q0 = u * len(chains)
            fs = [_dot_nt(jnp.concatenate([eff[q0 + m][:, 0:PAIR].astype(BF16), r_t[q0 + m]], axis=0),
                          st[m].astype(BF16)) for m in nc]
            uv = [jnp.concatenate([stack(fs[m][0:c] + eff[q0 + m][:, PAIR:2 * PAIR]).astype(BF16),
                                   v_s[q0 + m]], axis=0) for m in nc]
            st = [st[m] * g_tot[q0 + m] + _dot_tn(uv[m], bk_end[q0 + m]) for m in nc]
            for m, (d, p) in enumerate(chains):
                refs[d][6][rows[u, d], lanes[p]] = fs[m][c:2 * c] + _dot(p_both[q0 + m], uv[m])
            yield
        for m, (d, p) in enumerate(chains):
            st_scr[d, p] = st[m]

    return stages


def _ssd_scan_body(xf_ref, dtf_ref, xb_ref, dtb_ref, a_ref, yf_ref, yb_ref, st_scr, n_chunks):
    c = SCAN_CHUNK
    hd = SSD_HEAD_DIM
    row1 = lax.broadcasted_iota(jnp.int32, (c, 1), 0)
    m0 = lax.broadcasted_iota(jnp.int32, (c, PAIR), 1) < hd
    before_eq = [_order_masks(d, c)[1] for d in range(2)]
    incl01 = [before_eq[d].astype(BF16) for d in range(2)]
    rsel = [row1 == (c - 1 if d == 0 else 0) for d in range(2)]
    refs = ((xf_ref, dtf_ref, yf_ref), (xb_ref, dtb_ref, yb_ref))
    chains = [(d, p) for d in range(2) for p in range(N_PAIRS)]
    n = range(len(chains))
    group = [p // (N_PAIRS // SSD_GROUPS) for _, p in chains]
    lanes = [slice(p * PAIR, (p + 1) * PAIR) for _, p in chains]

    def stages(j):
        rows = (pl.ds(pl.multiple_of(j * c, c), c),
                pl.ds(pl.multiple_of((n_chunks - 1 - j) * c, c), c))
        dt = [refs[d][1][rows[d], :] for d in range(2)]
        cs_all = [_dot01_left(incl01[d], dt[d] * a_ref[d]) for d in range(2)]
        bm = [[refs[d][0][rows[d], SSD_WIDTH + g * SSD_STATE:SSD_WIDTH + (g + 1) * SSD_STATE]
               for g in range(SSD_GROUPS)] for d in range(2)]
        cm = [[refs[d][0][rows[d], SSD_WIDTH + (SSD_GROUPS + g) * SSD_STATE:
                          SSD_WIDTH + (SSD_GROUPS + g + 1) * SSD_STATE]
               for g in range(SSD_GROUPS)] for d in range(2)]
        cb = [[_dot_nt(cm[d][g], bm[d][g]) for g in range(SSD_GROUPS)] for d in range(2)]
        yield
        st = [st_scr[d, p] for d, p in chains]
        y_st = [_dot(cm[chains[q][0]][group[q]], st[q].astype(BF16)) for q in n]
        cs = [cs_all[chains[q][0]][:, lanes[q]] for q in n]
        xdt = [refs[chains[q][0]][0][rows[chains[q][0]], lanes[q]].astype(F32) * dt[chains[q][0]][:, lanes[q]]
               for q in n]
        yield
        probs = []
        for q in n:
            d = chains[q][0]
            cs_t = cs[q].T
            both = []
            for hh in range(2):
                col = cs[q][:, hh * hd:hh * hd + 1]
                rowv = cs_t[hh * hd:hh * hd + 1, :]
                dec = jnp.exp(jnp.where(before_eq[d], col - rowv, -jnp.inf))
                both.append((cb[d][group[q]] * dec).astype(BF16))
            probs.append(jnp.concatenate(both, axis=1))
            yield
        xs2 = [jnp.concatenate([jnp.where(m0, xdt[q], 0.0), jnp.where(m0, 0.0, xdt[q])],
                               axis=0).astype(BF16) for q in n]
        y_in = [_dot(probs[q], xs2[q]) for q in n]
        yield
        for q in n:
            d = chains[q][0]
            refs[d][2][rows[d], lanes[q]] = y_in[q] + jnp.exp(cs[q]) * y_st[q]
        yield
        for q, (d, p) in enumerate(chains):
            cs_last = jnp.sum(jnp.where(rsel[d], cs[q], 0.0), axis=0, keepdims=True)
            xe = (xdt[q] * jnp.exp(cs_last - cs[q])).astype(BF16)
            st_scr[d, p] = st[q] * jnp.exp(cs_last) + _dot_tn(bm[d][group[q]], xe)

    return stages


def _even_scan_kernel(rf_ref, vf_ref, kkf_ref, ldf_ref, kdf_ref, bdf_ref,
                      rb_ref, vb_ref, kkb_ref, ldb_ref, kdb_ref, bdb_ref,
                      xf_ref, dtf_ref, xb_ref, dtb_ref, a_ref, s0r_ref, s0s_ref,
                      yrf_ref, yrb_ref, ysf_ref, ysb_ref, sfr_ref, sfs_ref, str_scr, sts_scr, *, n_steps):
    i = pl.program_id(1)

    @pl.when(i == 0)
    def _():
        str_scr[...] = s0r_ref[...]
        sts_scr[...] = s0s_ref[...]

    rwkv_stages = _rwkv_scan_body(rf_ref, vf_ref, kkf_ref, ldf_ref, kdf_ref, bdf_ref,
                                  rb_ref, vb_ref, kkb_ref, ldb_ref, kdb_ref, bdb_ref,
                                  yrf_ref, yrb_ref, str_scr, n_steps * RWKV_CHUNKS_PER_STEP)
    ssd_stages = _ssd_scan_body(xf_ref, dtf_ref, xb_ref, dtb_ref, a_ref, ysf_ref, ysb_ref, sts_scr, n_steps)

    def step(j, carry):
        live = [rwkv_stages(j), ssd_stages(j)]
        while live:
            for gen in list(live):
                if next(gen, StopIteration) is StopIteration:
                    live.remove(gen)
        return carry

    lax.fori_loop(0, n_steps, step, 0)

    @pl.when(i == pl.num_programs(1) - 1)
    def _():
        sfr_ref[...] = str_scr[...]
        sfs_ref[...] = sts_scr[...]


def _even_scan_call(feat, xbc, dtbc, a_rep, s0_rk, s0_sd, tb):
    r, v, kk, _, _, ld, kd, bd = feat
    b, t, w = r.shape
    tb = min(tb, t)
    nb = t // tb
    assert SCAN_CHUNK == RWKV_CHUNKS_PER_STEP * RWKV_CHUNK
    tok_f = pl.BlockSpec((None, tb, w), lambda bi, i: (bi, i, 0))
    tok_b = pl.BlockSpec((None, tb, w), lambda bi, i: (bi, nb - 1 - i, 0))
    dir_f = pl.BlockSpec((None, None, tb, w), lambda bi, i: (0, bi, i, 0))
    dir_b = pl.BlockSpec((None, None, tb, w), lambda bi, i: (1, bi, nb - 1 - i, 0))
    xbc_f = pl.BlockSpec((None, tb, SSD_XBC), lambda bi, i: (bi, i, 0))
    xbc_b = pl.BlockSpec((None, tb, SSD_XBC), lambda bi, i: (bi, nb - 1 - i, 0))
    st_r = pl.BlockSpec((2, None, N_PAIRS, PAIR, PAIR), lambda bi, i: (0, bi, 0, 0, 0))
    st_s = pl.BlockSpec((2, None, N_PAIRS, SSD_STATE, PAIR), lambda bi, i: (0, bi, 0, 0, 0))
    y_sd = jax.ShapeDtypeStruct((b, t, w), F32)
    return pl.pallas_call(
        functools.partial(_even_scan_kernel, n_steps=tb // SCAN_CHUNK),
        name="even_scan",
        grid=(b, nb),
        in_specs=[tok_f, tok_f, tok_f, dir_f, dir_f, dir_f, tok_b, tok_b, tok_b, dir_b, dir_b, dir_b,
                  xbc_f, dir_f, xbc_b, dir_b,
                  pl.BlockSpec((2, 1, SSD_WIDTH), lambda bi, i: (0, 0, 0)), st_r, st_s],
        out_specs=[tok_f, tok_b, tok_f, tok_b, st_r, st_s],
        out_shape=[y_sd, y_sd, y_sd, y_sd,
                   jax.ShapeDtypeStruct((2, b, N_PAIRS, PAIR, PAIR), F32),
                   jax.ShapeDtypeStruct((2, b, N_PAIRS, SSD_STATE, PAIR), F32)],
        scratch_shapes=[pltpu.VMEM((2, N_PAIRS, PAIR, PAIR), F32),
                        pltpu.VMEM((2, N_PAIRS, SSD_STATE, PAIR), F32)],
        compiler_params=_cparams(("parallel", "arbitrary")),
    )(r, v, kk, ld, kd, bd, r, v, kk, ld, kd, bd, xbc, dtbc, xbc, dtbc, a_rep, s0_rk, s0_sd)


def _even_finish_kernel(yrf_ref, yrb_ref, bonus_ref, gate_ref, ysf_ref, ysb_ref, xs_ref, z_ref, x_ref,
                        m_ref, lnw_ref, lnb_ref, dsk_ref, nw_ref, j_ref, wo_ref, o_ref):
    jm = j_ref[...]
    y = yrf_ref[...] + yrb_ref[...]
    inv_n = 1.0 / RWKV_HEAD_DIM
    mean = _head_sum(y, jm) * inv_n
    yc = y - mean
    var = _head_sum(yc * yc, jm) * inv_n
    y = yc * lax.rsqrt(var + RWKV_GN_EPS) * lnw_ref[...] + lnb_ref[...]
    y_rk = (y + bonus_ref[...].astype(F32)) * gate_ref[...].astype(F32)
    s = ysf_ref[...] + ysb_ref[...] + dsk_ref[...] * xs_ref[...].astype(F32)
    s = s * _silu(z_ref[...].astype(F32))
    s = s * lax.rsqrt(jnp.mean(s * s, axis=-1, keepdims=True) + NORM_EPS) * nw_ref[...]
    out = _dot(y_rk.astype(BF16), wo_ref[0:RWKV_WIDTH, :]) + \
        _dot(s.astype(BF16), wo_ref[RWKV_WIDTH:RWKV_WIDTH + SSD_WIDTH, :])
    o_ref[...] = x_ref[...] + m_ref[2:3, :] * out


def _even_finish_call(yrk, bonus, gate, ysd, xbc, proj, x, mods, p, tm):
    b, t, d = x.shape
    tm = min(tm, t)
    w = RWKV_WIDTH
    per_batch = mods.shape[0] > 1
    full = lambda shape: pl.BlockSpec(shape, lambda bi, i: (0,) * len(shape))
    tok = pl.BlockSpec((None, tm, w), lambda bi, i: (bi, i, 0))
    return pl.pallas_call(
        _even_finish_kernel,
        name="even_finish",
        grid=(b, t // tm),
        in_specs=[tok, tok, tok, tok, tok, tok,
                  pl.BlockSpec((None, tm, SSD_WIDTH), lambda bi, i: (bi, i, 0)),
                  pl.BlockSpec((None, tm, SSD_WIDTH), lambda bi, i: (bi, i, 0)),
                  pl.BlockSpec((None, tm, d), lambda bi, i: (bi, i, 0)),
                  pl.BlockSpec((None, N_MOD, d), (lambda bi, i: (bi, 0, 0)) if per_batch
                               else (lambda bi, i: (0, 0, 0))),
                  full((1, w)), full((1, w)), full((1, w)), full((1, w)),
                  full((V7X_MXU_DIM, V7X_MXU_DIM)),
                  full((2 * w, d))],
        out_specs=pl.BlockSpec((None, tm, d), lambda bi, i: (bi, i, 0)),
        out_shape=jax.ShapeDtypeStruct((b, t, d), F32),
        compiler_params=_cparams(("parallel", "parallel")),
    )(yrk[0], yrk[1], bonus, gate, ysd[0], ysd[1], xbc, proj, x, mods, p['ln_w'], p['ln_b'], p['d_skip'],
      p['norm_w'],
      p['head_sum'], p['w_out'])


def _ret_scan_kernel(q_ref, k_ref, v_ref, lg_ref, s0_ref, y_ref, sf_ref, st_scr, dec_scr, sc_scr,
                     *, n_chunks):
    d = pl.program_id(0)
    i = pl.program_id(2)
    c = SCAN_CHUNK
    dk, dv = RET_QK_DIM, RET_V_DIM
    heads = range(RET_HEADS)
    lg_all = lg_ref[...]

    @pl.when(i == 0)
    def _():
        st_scr[...] = s0_ref[...]
        _, before_eq = _order_masks(d, c)
        row = lax.broadcasted_iota(jnp.int32, (c, c), 0)
        col = lax.broadcasted_iota(jnp.int32, (c, c), 1)
        rel = jnp.abs(row - col).astype(F32)
        pos = (row + d * (c - 1 - 2 * row)).astype(F32)
        for h in heads:
            lg = lg_all[:, h * dk:h * dk + 1]
            dec_scr[h] = jnp.where(before_eq, jnp.exp(rel * lg), 0.0)
            sc_scr[h, 0] = jnp.exp((pos + 1.0) * lg).astype(BF16)
            sc_scr[h, 1] = jnp.exp((c - 1.0 - pos) * lg).astype(BF16)

    def chunk_body(j, carry):
        cj = jnp.where(d == 0, j, n_chunks - 1 - j)
        rows = pl.ds(pl.multiple_of(cj * c, c), c)
        qs = [q_ref[rows, h * dk:(h + 1) * dk] for h in heads]
        ks = [k_ref[rows, h * dk:(h + 1) * dk] for h in heads]
        vs = [v_ref[rows, h * dv:(h + 1) * dv] for h in heads]
        qk = [_dot_nt(qs[h], ks[h]) for h in heads]
        scores = [(qk[h] * dec_scr[h]).astype(BF16) for h in heads]
        st = [st_scr[h] for h in heads]
        y_st = [_dot(qs[h] * sc_scr[h, 0], st[h].astype(BF16)) for h in heads]
        for h in heads:
            y_ref[rows, h * dv:(h + 1) * dv] = (_dot(scores[h], vs[h]) + y_st[h]).astype(y_ref.dtype)
        for h in heads:
            lg = lg_all[:, h * dk:h * dk + 1]
            st_scr[h] = st[h] * jnp.exp(c * lg) + _dot_tn(ks[h] * sc_scr[h, 1], vs[h])
        return carry

    lax.fori_loop(0, n_chunks, chunk_body, 0, unroll=2)

    @pl.when(i == pl.num_programs(2) - 1)
    def _():
        sf_ref[...] = st_scr[...]


def _ret_scan_call(proj, lg_rep, s0, tb):
    b, t, _ = proj.shape
    tb = min(tb, t)
    nb = t // tb
    blk = lambda dd, i: i + dd * (nb - 1 - 2 * i)
    st = pl.BlockSpec((None, None, RET_HEADS, RET_QK_DIM, RET_V_DIM), lambda dd, bi, i: (dd, bi, 0, 0, 0))
    return pl.pallas_call(
        functools.partial(_ret_scan_kernel, n_chunks=tb // SCAN_CHUNK),
        name="ret_scan",
        grid=(2, b, nb),
        in_specs=[pl.BlockSpec((None, tb, RET_QK), lambda dd, bi, i: (bi, blk(dd, i), 0)),
                  pl.BlockSpec((None, tb, RET_QK), lambda dd, bi, i: (bi, blk(dd, i), 1)),
                  pl.BlockSpec((None, tb, RET_V), lambda dd, bi, i: (bi, blk(dd, i), 2 * RET_QK // RET_V)),
                  pl.BlockSpec((None, 1, RET_QK), lambda dd, bi, i: (dd, 0, 0)), st],
        out_specs=[pl.BlockSpec((None, None, tb, RET_V), lambda dd, bi, i: (dd, bi, blk(dd, i), 0)), st],
        out_shape=[jax.ShapeDtypeStruct((2, b, t, RET_V), BF16),
                   jax.ShapeDtypeStruct((2, b, RET_HEADS, RET_QK_DIM, RET_V_DIM), F32)],
        scratch_shapes=[pltpu.VMEM((RET_HEADS, RET_QK_DIM, RET_V_DIM), F32),
                        pltpu.VMEM((RET_HEADS, SCAN_CHUNK, SCAN_CHUNK), F32),
                        pltpu.VMEM((RET_HEADS, 2, SCAN_CHUNK, RET_QK_DIM), BF16)],
        compiler_params=_cparams(("parallel", "parallel", "arbitrary")),
    )(proj, proj, proj, lg_rep, s0)


def _odd_finish_kernel(y_ref, g_ref, x_ref, m_ref, wo_ref, o_ref):
    y = (y_ref[0] + y_ref[1]).astype(F32)
    dv = RET_V_DIM
    parts = []
    for h in range(RET_HEADS):
        yh = y[:, h * dv:(h + 1) * dv]
        parts.append(yh * lax.rsqrt(jnp.mean(yh * yh, axis=-1, keepdims=True) + NORM_EPS))
    yn = jnp.concatenate(parts, axis=1)
    act = (_silu(g_ref[...].astype(F32)) * yn).astype(BF16)
    o_ref[...] = x_ref[...] + m_ref[2:3, :] * _dot(act, wo_ref[...])


def _odd_finish_call(y, proj, x, mods, w_out, tm):
    b, t, d = x.shape
    tm = min(tm, t)
    return pl.pallas_call(
        _odd_finish_kernel,
        name="odd_finish",
        grid=(b, t // tm),
        in_specs=[pl.BlockSpec((2, None, tm, RET_V), lambda bi, i: (0, bi, i, 0)),
                  pl.BlockSpec((None, tm, RET_V), lambda bi, i: (bi, i, (2 * RET_QK + RET_V) // RET_V)),
                  pl.BlockSpec((None, tm, d), lambda bi, i: (bi, i, 0)),
                  pl.BlockSpec((None, N_MOD, d), lambda bi, i: (bi, 0, 0)),
                  pl.BlockSpec((RET_V, d), lambda bi, i: (0, 0))],
        out_specs=pl.BlockSpec((None, tm, d), lambda bi, i: (bi, i, 0)),
        out_shape=jax.ShapeDtypeStruct((b, t, d), F32),
        compiler_params=_cparams(("parallel", "parallel")),
    )(y, proj, x, mods, w_out)


def _ffn_kernel(*refs, on_grid, final_norm):
    if on_grid:
        x_ref, xn_ref, g_ref, m_ref, wu_ref, cw_ref, cb_ref, wd_ref = refs[:8]
        rest = refs[8:]
    else:
        x_ref, g_ref, m_ref, wu_ref, cw_ref, cb_ref, wd_ref = refs[:7]
        rest = refs[7:]
    if final_norm:
        fg_ref, o_ref, *scrs = rest
    else:
        o_ref, *scrs = rest
    gate_scr, val_scr, *scrs = scrs
    if on_grid:
        gtop_scr, *act_scrs = scrs
    else:
        act_scrs = scrs
    i = pl.program_id(1)
    n_tiles = pl.num_programs(1)
    tm = x_ref.shape[0]

    def norm_mod(xv):
        hv = xv * lax.rsqrt(jnp.mean(xv * xv, axis=-1, keepdims=True) + NORM_EPS) * g_ref[...]
        return (hv * (1.0 + m_ref[4:5, :]) + m_ref[3:4, :]).astype(BF16)

    x = x_ref[...]
    h = norm_mod(x)
    row = lax.broadcasted_iota(jnp.int32, (tm, 1), 0)
    if on_grid:
        col = row % GRID_W
        ok_left = col > 0
        ok_right = col < GRID_W - 1
        zero = jnp.zeros((GRID_W, x.shape[1]), BF16)
        h_ext = jnp.concatenate([h, jnp.where(i < n_tiles - 1, norm_mod(xn_ref[...]), zero)], axis=0)

        @pl.when(i == 0)
        def _():
            gtop_scr[...] = jnp.zeros_like(gtop_scr)
    else:
        ok_left = row > 0
        ok_right = row < tm - 1
    n_chunks = D_FF // FFN_COL_CHUNK

    def up_proj(j):
        cols = slice(j * FFN_COL_CHUNK, (j + 1) * FFN_COL_CHUNK)
        vcols = slice(D_FF + j * FFN_COL_CHUNK, D_FF + (j + 1) * FFN_COL_CHUNK)
        if on_grid:
            gate_scr[j % n_buf, GRID_W:, :] = _dot(h_ext, wu_ref[:, cols])
        else:
            gate_scr[j % n_buf] = _dot(h, wu_ref[:, cols])
        val_scr[j % n_buf] = _dot(h, wu_ref[:, vcols])

    out = None
    n_buf = gate_scr.shape[0]
    for j in range(n_buf - 1):
        up_proj(j)
    for j in range(n_chunks):
        cols = slice(j * FFN_COL_CHUNK, (j + 1) * FFN_COL_CHUNK)
        buf = j % n_buf
        if j + n_buf - 1 < n_chunks:
            up_proj(j + n_buf - 1)
        grp, slot = divmod(j, FFN_DOWN_GROUP)
        act_scr = act_scrs[grp]
        acols = slice(slot * FFN_COL_CHUNK, (slot + 1) * FFN_COL_CHUNK)
        if on_grid:
            gate_scr[buf, 0:GRID_W, :] = gtop_scr[:, cols]
            gtop_scr[:, cols] = gate_scr[buf, tm:tm + GRID_W, :]
            rows3 = [gate_scr[buf, dr * GRID_W:dr * GRID_W + tm, :] for dr in range(3)]
            taps = [rows3[0] * cw_ref[dc:dc + 1, cols] + rows3[1] * cw_ref[3 + dc:4 + dc, cols]
                    + rows3[2] * cw_ref[6 + dc:7 + dc, cols] for dc in range(3)]
        else:
            gate = gate_scr[buf]
            taps = [gate * cw_ref[3 + dc:4 + dc, cols] for dc in range(3)]
        acc = cb_ref[:, cols] + taps[1] + jnp.where(ok_left, pltpu.roll(taps[0], 1, axis=0), 0.0) \
            + jnp.where(ok_right, pltpu.roll(taps[2], tm - 1, axis=0), 0.0)
        act_scr[:, acols] = (_gelu_tanh(acc) * val_scr[buf]).astype(BF16)
        if slot + 1 == FFN_DOWN_GROUP or j + 1 == n_chunks:
            width = (slot + 1) * FFN_COL_CHUNK
            k0 = grp * FFN_DOWN_GROUP * FFN_COL_CHUNK
            part = _dot(act_scr[:, 0:width], wd_ref[k0:k0 + width, :])
            out = part if out is None else out + part
    out = x + m_ref[5:6, :] * out
    if final_norm:
        out = out * lax.rsqrt(jnp.mean(out * out, axis=-1, keepdims=True) + NORM_EPS) * fg_ref[...]
    o_ref[...] = out


def _ffn_call(x, norm_g, mods, w_up, conv_w9, conv_b, w_down, *, tm, on_grid, final_g=None):
    b, t, d = x.shape
    tm = min(tm, t)
    n_tiles = t // tm
    per_batch = mods.shape[0] > 1
    full = lambda shape: pl.BlockSpec(shape, lambda bi, i: (0,) * len(shape))
    in_specs = [pl.BlockSpec((None, tm, d), lambda bi, i: (bi, i, 0))]
    args = [x]
    scratch = [pltpu.VMEM((tm, FFN_DOWN_GROUP * FFN_COL_CHUNK), BF16)
               for _ in range(-(-D_FF // (FFN_DOWN_GROUP * FFN_COL_CHUNK)))]
    if on_grid:
        r = tm // GRID_W
        last = t // GRID_W - 1
        in_specs.append(pl.BlockSpec((None, GRID_W, d), lambda bi, i: (bi, jnp.minimum((i + 1) * r, last), 0)))
        args.append(x)
        scratch.insert(0, pltpu.VMEM((GRID_W, D_FF), F32))
    else:
        assert n_tiles == 1
    gate_rows = tm + 2 * GRID_W if on_grid else tm
    scratch = [pltpu.VMEM((FFN_UP_BUFFERS, gate_rows, FFN_COL_CHUNK), F32),
               pltpu.VMEM((FFN_UP_BUFFERS, tm, FFN_COL_CHUNK), F32)] + scratch
    in_specs += [full((1, d)),
                 pl.BlockSpec((None, N_MOD, d), (lambda bi, i: (bi, 0, 0)) if per_batch
                              else (lambda bi, i: (0, 0, 0))),
                 full((d, 2 * D_FF)), full((9, D_FF)), full((1, D_FF)), full((D_FF, d))]
    args += [norm_g, mods, w_up, conv_w9, conv_b, w_down]
    if final_g is not None:
        in_specs.append(full((1, d)))
        args.append(final_g)
    return pl.pallas_call(
        functools.partial(_ffn_kernel, on_grid=on_grid, final_norm=final_g is not None),
        name="conv_ffn",
        grid=(b, n_tiles),
        in_specs=in_specs,
        out_specs=pl.BlockSpec((None, tm, d), lambda bi, i: (bi, i, 0)),
        out_shape=jax.ShapeDtypeStruct((b, t, d), F32),
        scratch_shapes=scratch,
        compiler_params=_cparams(("parallel", "arbitrary")),
    )(*args)


def _block_diag2(a, b):
    za = jnp.zeros((a.shape[0], b.shape[1]), a.dtype)
    zb = jnp.zeros((b.shape[0], a.shape[1]), a.dtype)
    return jnp.concatenate([jnp.concatenate([a, za], axis=1), jnp.concatenate([zb, b], axis=1)], axis=0)


def _pad_cols(a, n, fill=0.0):
    return jnp.pad(a, ((0, 0), (0, n - a.shape[1])), constant_values=fill)


def _even_params(j, ev_w_in, ev_mu_prev, ev_mu_next, rk_w0_f, rk_w0_b, rk_w2_f, rk_w2_b, rk_a0_f,
                 rk_a0_b, rk_a2_f, rk_a2_b, rk_g2, rk_k_k, rk_k_a, rk_r_k, rk_ln_w, rk_ln_b,
                 ssd_conv_w, ssd_conv_b, ssd_dt_bias_f, ssd_dt_bias_b, ssd_a_log_f, ssd_a_log_b,
                 ssd_d, ssd_norm_w, ev_w_out):
    w_in = ev_w_in[j]
    rw = w_in[:, :RWKV_COLS]
    z = w_in[:, RWKV_COLS:RWKV_COLS + SSD_WIDTH]
    xbc = w_in[:, RWKV_COLS + SSD_WIDTH:RWKV_COLS + SSD_WIDTH + SSD_XBC]
    dts = w_in[:, RWKV_COLS + SSD_WIDTH + SSD_XBC:]
    w_packed = jnp.concatenate([_pad_cols(jnp.concatenate([rw, dts], axis=1), EV_RW_BLOCK), xbc, z], axis=1)
    head = jnp.arange(V7X_MXU_DIM) // RWKV_HEAD_DIM
    head_sum = (head[:, None] == head[None, :]).astype(BF16)
    lane = jnp.arange(V7X_LANES)[:, None]
    tgt = jnp.arange(2 * SSD_WIDTH)[None, :]
    dt_expand = (lane == (tgt // SSD_WIDTH) * SSD_HEADS + (tgt % SSD_WIDTH) // SSD_HEAD_DIM).astype(BF16)
    rep = lambda a: jnp.repeat(a, SSD_HEAD_DIM)[None, :]
    row = lambda a: a[None, :]
    return {
        'w_in': w_packed.astype(BF16),
        'mu_self': _pad_cols(row(1.0 - ev_mu_prev[j] - ev_mu_next[j]), EV_RW_BLOCK, 1.0),
        'mu_prev': _pad_cols(row(ev_mu_prev[j]), EV_RW_BLOCK),
        'mu_next': _pad_cols(row(ev_mu_next[j]), EV_RW_BLOCK),
        'w0': row(jnp.concatenate([rk_w0_f[j], rk_w0_b[j]])),
        'w2': _block_diag2(rk_w2_f[j], rk_w2_b[j]).astype(BF16),
        'a0': row(jnp.concatenate([rk_a0_f[j], rk_a0_b[j]])),
        'a2': _block_diag2(rk_a2_f[j], rk_a2_b[j]).astype(BF16),
        'g2': rk_g2[j].astype(BF16),
        'k_k': row(rk_k_k[j]), 'k_a': row(rk_k_a[j]), 'r_k': row(rk_r_k[j].reshape(-1)),
        'ln_w': row(rk_ln_w[j]), 'ln_b': row(rk_ln_b[j]),
        'head_sum': head_sum,
        'conv_w': ssd_conv_w[j], 'conv_b': row(ssd_conv_b[j]),
        'dt_bias': _pad_cols(row(jnp.concatenate([ssd_dt_bias_f[j], ssd_dt_bias_b[j]])), V7X_LANES),
        'dt_expand': dt_expand,
        'a_rep': jnp.stack([rep(-jnp.exp(ssd_a_log_f[j])), rep(-jnp.exp(ssd_a_log_b[j]))]),
        'd_skip': rep(ssd_d[j]),
        'norm_w': row(ssd_norm_w[j]),
        'w_out': ev_w_out[j].astype(BF16),
    }


def _rope_tables(t):
    n = RET_QK_DIM // 4
    pos = jnp.arange(t)
    row = (pos // GRID_W).astype(F32)
    col = (pos % GRID_W).astype(F32)
    inv = ROPE_BASE ** (-jnp.arange(n, dtype=F32) / n)
    ang = jnp.concatenate([row[:, None] * inv, col[:, None] * inv], axis=-1)
    cos, sin = jnp.cos(ang), jnp.sin(ang)
    return jnp.concatenate([cos, cos], axis=-1), jnp.concatenate([-sin, sin], axis=-1)


def _conv_ffn(x, mods, norm_g, w_up, conv_w9, conv_b, w_down, *, on_grid, final_g=None):
    return _ffn_call(x, norm_g, mods, w_up, conv_w9, conv_b, w_down, tm=512, on_grid=on_grid,
                     final_g=final_g)


def _even_layer(x, ctx, mods_x, mods_c, norm_g, p):
    b = x.shape[0]

    def features(h, mods):
        *feat, xbc, dtbc, z = _even_feat_call(h, norm_g, mods, p, 512)
        return z, feat, xbc, dtbc

    proj_c, feat_c, xbc_c, dt_c = features(ctx, mods_c)
    proj_x, feat_x, xbc_x, dt_x = features(x, mods_x)
    s0 = jnp.zeros((2, b, N_PAIRS, PAIR, PAIR), F32)
    h0 = jnp.zeros((2, b, N_PAIRS, SSD_STATE, PAIR), F32)
    *y_c, s_ctx, h_ctx = _even_scan_call(feat_c, xbc_c, dt_c, p['a_rep'], s0, h0, 256)
    *y_x, _, _ = _even_scan_call(feat_x, xbc_x, dt_x, p['a_rep'], s_ctx, h_ctx, 512)
    yrk_c, ysd_c = y_c[0:2], y_c[2:4]
    yrk_x, ysd_x = y_x[0:2], y_x[2:4]
    x = _even_finish_call(yrk_x, feat_x[3], feat_x[4], ysd_x, xbc_x, proj_x, x, mods_x, p, 512)
    ctx = _even_finish_call(yrk_c, feat_c[3], feat_c[4], ysd_c, xbc_c, proj_c, ctx, mods_c, p, 256)
    return x, ctx


def _odd_layer(x, ctx, mods_x, mods_c, norm_g, w_in, lg_rep, w_out):
    b, t, _ = x.shape
    proj_c = _nm_call(ctx, norm_g, mods_c, w_in, shift_row=0, tm=256, tn=512, out_dtype=BF16,
                      qk_mode='scale')
    proj_x = _nm_call(x, norm_g, mods_x, w_in, shift_row=0, tm=512, tn=512, out_dtype=BF16,
                      qk_mode='rope', rope=_rope_tables(t))
    s0 = jnp.zeros((2, b, RET_HEADS, RET_QK_DIM, RET_V_DIM), F32)
    _, s_ctx = _ret_scan_call(proj_c, lg_rep, s0, 256)
    y, _ = _ret_scan_call(proj_x, lg_rep, s_ctx, 512)
    return _odd_finish_call(y, proj_x, x, mods_x, w_out, 512)


def kernel(x, c, ctx, c_ctx, mod_w, mod_b, norm1_g, norm2_g, ffn_w_up, ffn_conv_w, ffn_conv_b, ffn_w_down, ev_w_in, ev_mu_prev, ev_mu_next, rk_w0_f, rk_w0_b, rk_w2_f, rk_w2_b, rk_a0_f, rk_a0_b, rk_a2_f, rk_a2_b, rk_g2, rk_k_k, rk_k_a, rk_r_k, rk_ln_w, rk_ln_b, ssd_conv_w, ssd_conv_b, ssd_dt_bias_f, ssd_dt_bias_b, ssd_a_log_f, ssd_a_log_b, ssd_d, ssd_norm_w, ev_w_out, ret_w_in, ret_log2_f, ret_log2_b, ret_w_out, final_norm_g):
    b, t, d = x.shape
    depth = mod_w.shape[0]
    rows = -(-(b + 1) // V7X_SUBLANES) * V7X_SUBLANES
    cond = jnp.concatenate([c, c_ctx[None, :], jnp.zeros((rows - b - 1, d), F32)], axis=0)
    mods = _mod_call(cond, mod_w, mod_b).reshape(depth, rows, N_MOD, d)
    for i in range(depth):
        need_ctx = i < depth - 1
        mods_x = mods[i, :b]
        mods_c = mods[i, b:b + 1]
        j = i // 2
        g1 = norm1_g[i][None, :]
        if i % 2 == 0:
            p = _even_params(j, ev_w_in, ev_mu_prev, ev_mu_next, rk_w0_f, rk_w0_b, rk_w2_f, rk_w2_b,
                             rk_a0_f, rk_a0_b, rk_a2_f, rk_a2_b, rk_g2, rk_k_k, rk_k_a, rk_r_k,
                             rk_ln_w, rk_ln_b, ssd_conv_w, ssd_conv_b, ssd_dt_bias_f, ssd_dt_bias_b,
                             ssd_a_log_f, ssd_a_log_b, ssd_d, ssd_norm_w, ev_w_out)
            x, ctx_mixed = _even_layer(x, ctx, mods_x, mods_c, g1, p)
        else:
            lg = jnp.stack([jnp.log1p(-jnp.exp2(-ret_log2_f[j])), jnp.log1p(-jnp.exp2(-ret_log2_b[j]))])
            lg_rep = jnp.repeat(lg, RET_QK_DIM, axis=-1)[:, None, :]
            x = _odd_layer(x, ctx, mods_x, mods_c, g1, ret_w_in[j].astype(BF16), lg_rep,
                           ret_w_out[j].astype(BF16))
            ctx_mixed = None
        g2 = norm2_g[i][None, :]
        w_up = ffn_w_up[i].astype(BF16)
        w_down = ffn_w_down[i].astype(BF16)
        conv_w9 = ffn_conv_w[i].reshape(9, D_FF)
        conv_b = ffn_conv_b[i][None, :]
        last = i == depth - 1
        x = _conv_ffn(x, mods_x, g2, w_up, conv_w9, conv_b, w_down, on_grid=True,
                      final_g=final_norm_g[None, :] if last else None)
        if need_ctx:
            ctx = _conv_ffn(ctx_mixed, mods_c, g2, w_up, conv_w9, conv_b, w_down, on_grid=False)
    return x
```

```python
import functools
import math

import jax
import jax.numpy as jnp
from jax import lax
from jax.experimental import pallas as pl
from jax.experimental.pallas import tpu as pltpu

F32 = jnp.float32
BF16 = jnp.bfloat16

D_MODEL = 1024
GRID_W = 64
N_MOD = 6
NORM_EPS = 1e-6
RWKV_HEADS = 8
RWKV_HEAD_DIM = 64
RWKV_WIDTH = RWKV_HEADS * RWKV_HEAD_DIM
DECAY_LORA = 64
ICLR_LORA = 64
GATE_LORA = 128
RWKV_GN_EPS = 64e-5
RWKV_COLS = 3 * RWKV_WIDTH + 2 * DECAY_LORA + 2 * ICLR_LORA + GATE_LORA
SSD_HEADS = 8
SSD_HEAD_DIM = 64
SSD_WIDTH = SSD_HEADS * SSD_HEAD_DIM
SSD_GROUPS = 2
SSD_STATE = 128
SSD_XBC = SSD_WIDTH + 2 * SSD_GROUPS * SSD_STATE
RET_HEADS = 8
RET_QK_DIM = 128
RET_V_DIM = 256
RET_QK = RET_HEADS * RET_QK_DIM
RET_V = RET_HEADS * RET_V_DIM
ROPE_BASE = 10000.0
D_FF = 2816

V7X_LANES = 128
V7X_SUBLANES = 8
V7X_MXU_DIM = 256
V7X_VMEM_LIMIT_BYTES = 56 * 1024 * 1024

RWKV_CHUNK = 64
RWKV_CHUNKS_PER_STEP = 2
SCAN_CHUNK = 128
FFN_COL_CHUNK = 256
FFN_DOWN_GROUP = 6
FFN_UP_BUFFERS = 2
PAIR = 2 * RWKV_HEAD_DIM
N_PAIRS = RWKV_HEADS // 2
EV_RW_BLOCK = 2048
EV_DT_OFF = RWKV_COLS
EV_XBC_OFF = EV_RW_BLOCK
EV_Z_OFF = EV_RW_BLOCK + SSD_XBC
EV_COLS = EV_Z_OFF + SSD_WIDTH


def _cparams(sem):
    return pltpu.CompilerParams(dimension_semantics=sem, vmem_limit_bytes=V7X_VMEM_LIMIT_BYTES)


def _split3(x):
    hi = x.astype(BF16)
    r1 = x - hi.astype(F32)
    mid = r1.astype(BF16)
    lo = (r1 - mid.astype(F32)).astype(BF16)
    return hi, mid, lo


def _dot(a, b):
    return jnp.dot(a, b, preferred_element_type=F32)


def _dot_nt(a, b):
    return lax.dot_general(a, b, (((1,), (1,)), ((), ())), preferred_element_type=F32)


def _dot_tn(a, b):
    return lax.dot_general(a, b, (((0,), (0,)), ((), ())), preferred_element_type=F32)


def _dot01(x, m01):
    hi, mid, lo = _split3(x)
    return _dot(hi, m01) + _dot(mid, m01) + _dot(lo, m01)


def _head_sum(x, j01):
    hi = x.astype(BF16)
    lo = (x - hi.astype(F32)).astype(BF16)
    n = j01.shape[0]
    return jnp.concatenate([_dot(hi[:, g * n:(g + 1) * n], j01) + _dot(lo[:, g * n:(g + 1) * n], j01)
                            for g in range(x.shape[1] // n)], axis=1)


def _dot01_left(m01, x):
    hi, mid, lo = _split3(x)
    return _dot(m01, hi) + _dot(m01, mid) + _dot(m01, lo)


def _sigmoid(x):
    return 0.5 * jnp.tanh(0.5 * x) + 0.5


def _silu(x):
    half = 0.5 * x
    return half + half * jnp.tanh(half)


def _softplus(x):
    return jnp.maximum(x, 0.0) + jnp.log1p(jnp.exp(-jnp.abs(x)))


def _gelu_tanh(x):
    c = math.sqrt(2.0 / math.pi)
    half = 0.5 * x
    return half + half * jnp.tanh(x * (c + (0.044715 * c) * (x * x)))


def _order_masks(d, n):
    row = lax.broadcasted_iota(jnp.int32, (n, n), 0)
    col = lax.broadcasted_iota(jnp.int32, (n, n), 1)
    diff = (row - col) * (1 - 2 * d)
    return diff > 0, diff >= 0


def _mod_kernel(c_ref, w_ref, b_ref, o_ref):
    h = _silu(c_ref[...])
    hi, mid, lo = _split3(h)
    w = w_ref[...]
    wh = w.astype(BF16)
    wl = (w - wh.astype(F32)).astype(BF16)
    acc = _dot(hi, wh) + _dot(mid, wh) + _dot(hi, wl)
    o_ref[...] = acc + b_ref[...]


def _mod_call(cond, mod_w, mod_b):
    depth, d, n = mod_w.shape
    rows = cond.shape[0]
    tn = 1024
    return pl.pallas_call(
        _mod_kernel,
        name="adaln_mod",
        grid=(depth, n // tn),
        in_specs=[pl.BlockSpec((rows, d), lambda l, j: (0, 0)),
                  pl.BlockSpec((None, d, tn), lambda l, j: (l, 0, j)),
                  pl.BlockSpec((None, 1, tn), lambda l, j: (l, 0, j))],
        out_specs=pl.BlockSpec((None, rows, tn), lambda l, j: (l, 0, j)),
        out_shape=jax.ShapeDtypeStruct((depth, rows, n), F32),
        compiler_params=_cparams(("parallel", "parallel")),
    )(cond, mod_w, mod_b.reshape(depth, 1, n))


def _nm_kernel(*refs, shift_row, tn, qk_mode):
    if qk_mode == 'rope':
        x_ref, g_ref, m_ref, w_ref, cos_ref, sin_ref, o_ref = refs
    else:
        x_ref, g_ref, m_ref, w_ref, o_ref = refs
    x = x_ref[...]
    h = x * lax.rsqrt(jnp.mean(x * x, axis=-1, keepdims=True) + NORM_EPS) * g_ref[...]
    h = h * (1.0 + m_ref[shift_row + 1:shift_row + 2, :]) + m_ref[shift_row:shift_row + 1, :]
    h = h.astype(BF16)
    dk = RET_QK_DIM
    for j in range(w_ref.shape[1] // tn):
        cols = slice(j * tn, (j + 1) * tn)
        y = _dot(h, w_ref[:, cols])
        if qk_mode is not None and j * tn < 2 * RET_QK:
            scale = dk ** -0.5 if j * tn >= RET_QK else 1.0
            heads = []
            for hh in range(tn // dk):
                yh = y[:, hh * dk:(hh + 1) * dk]
                if qk_mode == 'rope':
                    yh = yh * cos_ref[...] + pltpu.roll(yh, dk // 2, axis=1) * sin_ref[...]
                heads.append(yh * scale if scale != 1.0 else yh)
            y = jnp.concatenate(heads, axis=1)
        o_ref[:, cols] = y.astype(o_ref.dtype)


def _nm_call(x, g, mods, w, *, shift_row, tm, tn, out_dtype=F32, qk_mode=None, rope=None):
    b, t, d = x.shape
    n = w.shape[1]
    tm = min(tm, t)
    per_batch = mods.shape[0] > 1
    in_specs = [pl.BlockSpec((None, tm, d), lambda bi, i: (bi, i, 0)),
                pl.BlockSpec((1, d), lambda bi, i: (0, 0)),
                pl.BlockSpec((None, N_MOD, d), (lambda bi, i: (bi, 0, 0)) if per_batch
                             else (lambda bi, i: (0, 0, 0))),
                pl.BlockSpec((d, n), lambda bi, i: (0, 0))]
    args = [x, g, mods, w]
    if qk_mode == 'rope':
        tab = pl.BlockSpec((tm, RET_QK_DIM), lambda bi, i: (i, 0))
        in_specs += [tab, tab]
        args += list(rope)
    return pl.pallas_call(
        functools.partial(_nm_kernel, shift_row=shift_row, tn=tn, qk_mode=qk_mode),
        name="norm_mod_matmul",
        grid=(b, t // tm),
        in_specs=in_specs,
        out_specs=pl.BlockSpec((None, tm, n), lambda bi, i: (bi, i, 0)),
        out_shape=jax.ShapeDtypeStruct((b, t, n), out_dtype),
        compiler_params=_cparams(("parallel", "parallel")),
    )(*args)


def _even_feat_kernel(x_ref, xp_ref, xn_ref, g_ref, m_ref, w_ref,
                      mus_ref, mup_ref, mun_ref, w0_ref, w2_ref, a0_ref, a2_ref,
                      g2_ref, kk_ref, ka_ref, rk_ref, j_ref, cw_ref, cb_ref, dtb_ref, e_ref,
                      r_ref, v_ref, kkn_ref, bonus_ref, gate_ref, ld_ref, kd_ref, bd_ref,
                      xbc_ref, dtbc_ref, z_ref):
    i = pl.program_id(1)
    n_tiles = pl.num_programs(1)
    tm = x_ref.shape[0]
    halo = V7X_SUBLANES
    ext = tm + 2 * halo

    def norm_mod(xv):
        hv = xv * lax.rsqrt(jnp.mean(xv * xv, axis=-1, keepdims=True) + NORM_EPS) * g_ref[...]
        return (hv * (1.0 + m_ref[1:2, :]) + m_ref[0:1, :]).astype(BF16)

    h = norm_mod(x_ref[...])
    zero = jnp.zeros((halo, x_ref.shape[1]), BF16)
    h_ext = jnp.concatenate([jnp.where(i > 0, norm_mod(xp_ref[...]), zero), h,
                             jnp.where(i < n_tiles - 1, norm_mod(xn_ref[...]), zero)], axis=0)

    def proj3(cols):
        ye = _dot(h_ext, w_ref[:, cols])
        return (ye[halo:halo + tm], pltpu.roll(ye, 1, axis=0)[halo:halo + tm],
                pltpu.roll(ye, ext - 1, axis=0)[halo:halo + tm])

    w = RWKV_WIDTH
    rw = []
    for j in range(EV_RW_BLOCK // w):
        cols = slice(j * w, (j + 1) * w)
        cur, prev, nxt = proj3(cols)
        rw.append(cur * mus_ref[:, cols] + prev * mup_ref[:, cols] + nxt * mun_ref[:, cols])
    r, k, v, lora = rw
    wd = lora[:, 0:2 * DECAY_LORA]
    ad = lora[:, 2 * DECAY_LORA:2 * DECAY_LORA + 2 * ICLR_LORA]
    gd = lora[:, 2 * DECAY_LORA + 2 * ICLR_LORA:2 * DECAY_LORA + 2 * ICLR_LORA + GATE_LORA]
    dt_raw = lora[:, EV_DT_OFF - 3 * w:EV_RW_BLOCK - 3 * w]
    jm = j_ref[...]

    for j in range(SSD_XBC // w):
        cols = slice(j * w, (j + 1) * w)
        cur, prev, nxt = proj3(slice(EV_XBC_OFF + j * w, EV_XBC_OFF + (j + 1) * w))
        y = prev * cw_ref[0:1, cols] + cur * cw_ref[1:2, cols] + nxt * cw_ref[2:3, cols] + cb_ref[:, cols]
        xbc_ref[:, cols] = _silu(y).astype(xbc_ref.dtype)
    z_ref[...] = _dot(h, w_ref[:, EV_Z_OFF:EV_Z_OFF + SSD_WIDTH]).astype(z_ref.dtype)
    dt = _dot01(_softplus(dt_raw + dtb_ref[...]), e_ref[...])
    dtbc_ref[0] = dt[:, 0:SSD_WIDTH]
    dtbc_ref[1] = dt[:, SSD_WIDTH:2 * SSD_WIDTH]

    kk = k * kk_ref[...]
    ss = _head_sum(kk * kk, jm)
    kk = kk / jnp.maximum(jnp.sqrt(ss), 1e-12)
    r_ref[...] = r.astype(r_ref.dtype)
    v_ref[...] = v.astype(v_ref.dtype)
    kkn_ref[...] = kk.astype(kkn_ref.dtype)
    bonus_ref[...] = (_head_sum(r * k * rk_ref[...], jm) * v).astype(bonus_ref.dtype)
    gate_ref[...] = _dot(_sigmoid(gd).astype(BF16), g2_ref[...]).astype(gate_ref.dtype)

    zw = _dot(jnp.tanh(wd).astype(BF16), w2_ref[...]) + w0_ref[...]
    za = _dot(ad.astype(BF16), a2_ref[...]) + a0_ref[...]
    for di in range(2):
        ld_ref[di] = -math.exp(-0.5) * _sigmoid(zw[:, di * w:(di + 1) * w])
        iclr = _sigmoid(za[:, di * w:(di + 1) * w])
        kd_ref[di] = (k * (1.0 + (iclr - 1.0) * ka_ref[...])).astype(kd_ref.dtype)
        bd_ref[di] = (kk * iclr).astype(bd_ref.dtype)


def _even_feat_call(x, norm_g, mods, p, tm):
    b, t, d = x.shape
    tm = min(tm, t)
    n_tiles = t // tm
    w = RWKV_WIDTH
    per_batch = mods.shape[0] > 1
    r8 = tm // V7X_SUBLANES
    last = t // V7X_SUBLANES - 1
    full = lambda shape: pl.BlockSpec(shape, lambda bi, i: (0,) * len(shape))
    tok = pl.BlockSpec((None, tm, w), lambda bi, i: (bi, i, 0))
    tok2 = pl.BlockSpec((2, None, tm, w), lambda bi, i: (0, bi, i, 0))
    sd = jax.ShapeDtypeStruct((b, t, w), BF16)
    sd2 = jax.ShapeDtypeStruct((2, b, t, w), BF16)
    f2 = jax.ShapeDtypeStruct((2, b, t, w), F32)
    return pl.pallas_call(
        _even_feat_kernel,
        name="even_feat",
        grid=(b, n_tiles),
        in_specs=[pl.BlockSpec((None, tm, d), lambda bi, i: (bi, i, 0)),
                  pl.BlockSpec((None, V7X_SUBLANES, d), lambda bi, i: (bi, jnp.maximum(i * r8 - 1, 0), 0)),
                  pl.BlockSpec((None, V7X_SUBLANES, d), lambda bi, i: (bi, jnp.minimum((i + 1) * r8, last), 0)),
                  full((1, d)),
                  pl.BlockSpec((None, N_MOD, d), (lambda bi, i: (bi, 0, 0)) if per_batch
                               else (lambda bi, i: (0, 0, 0))),
                  full((d, EV_COLS)),
                  full((1, EV_RW_BLOCK)), full((1, EV_RW_BLOCK)), full((1, EV_RW_BLOCK)),
                  full((1, 2 * w)), full((2 * DECAY_LORA, 2 * w)),
                  full((1, 2 * w)), full((2 * ICLR_LORA, 2 * w)),
                  full((GATE_LORA, w)), full((1, w)), full((1, w)), full((1, w)),
                  full((V7X_MXU_DIM, V7X_MXU_DIM)),
                  full((3, SSD_XBC)), full((1, SSD_XBC)), full((1, V7X_LANES)),
                  full((V7X_LANES, 2 * SSD_WIDTH))],
        out_specs=[tok, tok, tok, tok, tok, tok2, tok2, tok2,
                   pl.BlockSpec((None, tm, SSD_XBC), lambda bi, i: (bi, i, 0)), tok2, tok],
        out_shape=[sd, sd, sd, sd, sd, f2, sd2, sd2,
                   jax.ShapeDtypeStruct((b, t, SSD_XBC), BF16), f2, sd],
        compiler_params=_cparams(("parallel", "parallel")),
    )(x, x, x, norm_g, mods, p['w_in'], p['mu_self'], p['mu_prev'], p['mu_next'], p['w0'], p['w2'],
      p['a0'], p['a2'], p['g2'], p['k_k'], p['k_a'], p['r_k'], p['head_sum'],
      p['conv_w'], p['conv_b'], p['dt_bias'], p['dt_expand'])


def _rwkv_scan_body(rf_ref, vf_ref, kkf_ref, ldf_ref, kdf_ref, bdf_ref,
                    rb_ref, vb_ref, kkb_ref, ldb_ref, kdb_ref, bdb_ref, yf_ref, yb_ref, st_scr, n_chunks):
    c = RWKV_CHUNK
    rowc = lax.broadcasted_iota(jnp.int32, (c, 2 * c), 0)
    colc = lax.broadcasted_iota(jnp.int32, (c, 2 * c), 1) % c
    eye_wide = (rowc == colc).astype(F32)
    m0 = lax.broadcasted_iota(jnp.int32, (c, PAIR), 1) < RWKV_HEAD_DIM
    row1 = lax.broadcasted_iota(jnp.int32, (c, 1), 0)
    incl01, strict_wide, incl_wide, rsel = [], [], [], []
    for d in range(2):
        sgn = 1 - 2 * d
        incl01.append(_order_masks(d, c)[1].astype(BF16))
        strict_wide.append((rowc - colc) * sgn > 0)
        incl_wide.append((rowc - colc) * sgn >= 0)
        rsel.append(row1 == (c - 1 if d == 0 else 0))
    refs = ((rf_ref, vf_ref, kkf_ref, ldf_ref, kdf_ref, bdf_ref, yf_ref),
            (rb_ref, vb_ref, kkb_ref, ldb_ref, kdb_ref, bdb_ref, yb_ref))
    chains = [(d, p) for d in range(2) for p in range(N_PAIRS)]
    lanes = [slice(p * PAIR, (p + 1) * PAIR) for p in range(N_PAIRS)]

    def stack(x):
        return jnp.concatenate([jnp.where(m0, x, 0.0), jnp.where(m0, 0.0, x)], axis=0)

    sub = range(RWKV_CHUNKS_PER_STEP)
    items = [(u, d, p) for u in sub for d, p in chains]

    def stages(j):
        def rows_of(u, d):
            cj = j * len(sub) + u
            return pl.ds(pl.multiple_of((cj if d == 0 else n_chunks - 1 - cj) * c, c), c)

        rows = {(u, d): rows_of(u, d) for u in sub for d in range(2)}
        cs_all = {ud: _dot01_left(incl01[ud[1]], refs[ud[1]][3][rows[ud], :]) for ud in rows}
        yield
        a_s, r_t, b_s, k_s, v_s, b_end, k_end, g_tot = [], [], [], [], [], [], [], []
        for u, d, p in items:
            r_ref, v_ref, kk_ref, ld_ref, kd_ref, bd_ref, _ = refs[d]
            rw, ln = rows[u, d], lanes[p]
            cs = cs_all[u, d][:, ln]
            cs_last = jnp.sum(jnp.where(rsel[d], cs, 0.0), axis=0, keepdims=True)
            g_neg = jnp.exp(-cs)
            g_end = jnp.exp(cs_last - cs)
            g_tot.append(jnp.exp(cs_last))
            kdv = kd_ref[rw, ln].astype(F32)
            bdv = bd_ref[rw, ln].astype(F32)
            a_w = (-kk_ref[rw, ln].astype(F32) * jnp.exp(cs - ld_ref[rw, ln]))
            a_s.append((a_w.astype(BF16), stack(a_w).astype(BF16)))
            r_t.append((r_ref[rw, ln].astype(F32) * jnp.exp(cs)).astype(BF16))
            b_s.append(stack(bdv * g_neg).astype(BF16))
            k_s.append(stack(kdv * g_neg).astype(BF16))
            v_s.append(stack(v_ref[rw, ln].astype(F32)).astype(BF16))
            b_end.append(stack(bdv * g_end).astype(BF16))
            k_end.append(stack(kdv * g_end).astype(BF16))
            if p == N_PAIRS - 1:
                yield
        n = range(len(items))
        dirs = [d for _, d, _ in items]
        gram = [_dot_nt(jnp.concatenate([a_s[q][0], r_t[q]], axis=0),
                        jnp.concatenate([b_s[q], k_s[q]], axis=0)) for q in n]
        yield
        a_ab =[jnp.where(strict_wide[dirs[q]], gram[q][0:c, 0:2 * c], 0.0) for q in n]
        a_ak = [jnp.where(strict_wide[dirs[q]], gram[q][0:c, 2 * c:4 * c], 0.0).astype(BF16) for q in n]
        p_rb = [jnp.where(incl_wide[dirs[q]], gram[q][c:2 * c, 0:2 * c], 0.0).astype(BF16) for q in n]
        p_rk = [jnp.where(incl_wide[dirs[q]], gram[q][c:2 * c, 2 * c:4 * c], 0.0).astype(BF16) for q in n]
        minv = [eye_wide + a_ab[q] for q in n]
        pw = [a_ab[q] for q in n]
        pw = [_dot(pw[q].astype(BF16), stack(pw[q]).astype(BF16)) for q in n]
        yield
        levels = int(math.log2(c))
        for k in range(2, levels):
            both = [_dot(pw[q].astype(BF16),
                         jnp.concatenate([stack(pw[q]), stack(minv[q])], axis=1).astype(BF16)) for q in n]
            minv = [minv[q] + both[q][:, 2 * c:4 * c] for q in n]
            pw = [both[q][:, 0:2 * c] for q in n]
            yield
        minv = [minv[q] + _dot(pw[q].astype(BF16), stack(minv[q]).astype(BF16)) for q in n]
        akv = [_dot(a_ak[q], v_s[q]) for q in n]
        yield
        eff = [_dot(minv[q].astype(BF16),
                    jnp.concatenate([a_s[q][1], stack(akv[q]).astype(BF16)], axis=1)) for q in n]
        bk_end = [jnp.concatenate([b_end[q], k_end[q]], axis=0) for q in n]
        p_both = [jnp.concatenate([p_rb[q], p_rk[q]], axis=1) for q in n]
        yield
        st = [st_scr[d, p] for d, p in chains]
        nc = range(len(chains))
        for u in sub:
            q0 = u * len(chains)
            fs = [_dot_nt(jnp.concatenate([eff[q0 + m][:, 0:PAIR].astype(BF16), r_t[q0 + m]], axis=0),
                          st[m].astype(BF16)) for m in nc]
            uv = [jnp.concatenate([stack(fs[m][0:c] + eff[q0 + m][:, PAIR:2 * PAIR]).astype(BF16),
                                   v_s[q0 + m]], axis=0) for m in nc]
            st = [st[m] * g_tot[q0 + m] + _dot_tn(uv[m], bk_end[q0 + m]) for m in nc]
            for m, (d, p) in enumerate(chains):
                refs[d][6][rows[u, d], lanes[p]] = fs[m][c:2 * c] + _dot(p_both[q0 + m], uv[m])
            yield
        for m, (d, p) in enumerate(chains):
            st_scr[d, p] = st[m]

    return stages


def _ssd_scan_body(xf_ref, dtf_ref, xb_ref, dtb_ref, a_ref, yf_ref, yb_ref, st_scr, n_chunks):
    c = SCAN_CHUNK
    hd = SSD_HEAD_DIM
    row1 = lax.broadcasted_iota(jnp.int32, (c, 1), 0)
    m0 = lax.broadcasted_iota(jnp.int32, (c, PAIR), 1) < hd
    before_eq = [_order_masks(d, c)[1] for d in range(2)]
    incl01 = [before_eq[d].astype(BF16) for d in range(2)]
    rsel = [row1 == (c - 1 if d == 0 else 0) for d in range(2)]
    refs = ((xf_ref, dtf_ref, yf_ref), (xb_ref, dtb_ref, yb_ref))
    chains = [(d, p) for d in range(2) for p in range(N_PAIRS)]
    n = range(len(chains))
    group = [p // (N_PAIRS // SSD_GROUPS) for _, p in chains]
    lanes = [slice(p * PAIR, (p + 1) * PAIR) for _, p in chains]

    def stages(j):
        rows = (pl.ds(pl.multiple_of(j * c, c), c),
                pl.ds(pl.multiple_of((n_chunks - 1 - j) * c, c), c))
        dt = [refs[d][1][rows[d], :] for d in range(2)]
        cs_all = [_dot01_left(incl01[d], dt[d] * a_ref[d]) for d in range(2)]
        bm = [[refs[d][0][rows[d], SSD_WIDTH + g * SSD_STATE:SSD_WIDTH + (g + 1) * SSD_STATE]
               for g in range(SSD_GROUPS)] for d in range(2)]
        cm = [[refs[d][0][rows[d], SSD_WIDTH + (SSD_GROUPS + g) * SSD_STATE:
                          SSD_WIDTH + (SSD_GROUPS + g + 1) * SSD_STATE]
               for g in range(SSD_GROUPS)] for d in range(2)]
        cb = [[_dot_nt(cm[d][g], bm[d][g]) for g in range(SSD_GROUPS)] for d in range(2)]
        yield
        st = [st_scr[d, p] for d, p in chains]
        y_st = [_dot(cm[chains[q][0]][group[q]], st[q].astype(BF16)) for q in n]
        cs = [cs_all[chains[q][0]][:, lanes[q]] for q in n]
        xdt = [refs[chains[q][0]][0][rows[chains[q][0]], lanes[q]].astype(F32) * dt[chains[q][0]][:, lanes[q]]
               for q in n]
        yield
        probs = []
        for q in n:
            d = chains[q][0]
            cs_t = cs[q].T
            both = []
            for hh in range(2):
                col = cs[q][:, hh * hd:hh * hd + 1]
                rowv = cs_t[hh * hd:hh * hd + 1, :]
                dec = jnp.exp(jnp.where(before_eq[d], col - rowv, -jnp.inf))
                both.append((cb[d][group[q]] * dec).astype(BF16))
            probs.append(jnp.concatenate(both, axis=1))
            yield
        xs2 = [jnp.concatenate([jnp.where(m0, xdt[q], 0.0), jnp.where(m0, 0.0, xdt[q])],
                               axis=0).astype(BF16) for q in n]
        y_in = [_dot(probs[q], xs2[q]) for q in n]
        yield
        for q in n:
            d = chains[q][0]
            refs[d][2][rows[d], lanes[q]] = y_in[q] + jnp.exp(cs[q]) * y_st[q]
        yield
        for q, (d, p) in enumerate(chains):
            cs_last = jnp.sum(jnp.where(rsel[d], cs[q], 0.0), axis=0, keepdims=True)
            xe = (xdt[q] * jnp.exp(cs_last - cs[q])).astype(BF16)
            st_scr[d, p] = st[q] * jnp.exp(cs_last) + _dot_tn(bm[d][group[q]], xe)

    return stages


def _even_scan_kernel(rf_ref, vf_ref, kkf_ref, ldf_ref, kdf_ref, bdf_ref,
                      rb_ref, vb_ref, kkb_ref, ldb_ref, kdb_ref, bdb_ref,
                      xf_ref, dtf_ref, xb_ref, dtb_ref, a_ref, s0r_ref, s0s_ref,
                      yrf_ref, yrb_ref, ysf_ref, ysb_ref, sfr_ref, sfs_ref, str_scr, sts_scr, *, n_steps):
    i = pl.program_id(1)

    @pl.when(i == 0)
    def _():
        str_scr[...] = s0r_ref[...]
        sts_scr[...] = s0s_ref[...]

    rwkv_stages = _rwkv_scan_body(rf_ref, vf_ref, kkf_ref, ldf_ref, kdf_ref, bdf_ref,
                                  rb_ref, vb_ref, kkb_ref, ldb_ref, kdb_ref, bdb_ref,
                                  yrf_ref, yrb_ref, str_scr, n_steps * RWKV_CHUNKS_PER_STEP)
    ssd_stages = _ssd_scan_body(xf_ref, dtf_ref, xb_ref, dtb_ref, a_ref, ysf_ref, ysb_ref, sts_scr, n_steps)

    def step(j, carry):
        live = [rwkv_stages(j), ssd_stages(j)]
        while live:
            for gen in list(live):
                if next(gen, StopIteration) is StopIteration:
                    live.remove(gen)
        return carry

    lax.fori_loop(0, n_steps, step, 0)

    @pl.when(i == pl.num_programs(1) - 1)
    def _():
        sfr_ref[...] = str_scr[...]
        sfs_ref[...] = sts_scr[...]


def _even_scan_call(feat, xbc, dtbc, a_rep, s0_rk, s0_sd, tb):
    r, v, kk, _, _, ld, kd, bd = feat
    b, t, w = r.shape
    tb = min(tb, t)
    nb = t // tb
    assert SCAN_CHUNK == RWKV_CHUNKS_PER_STEP * RWKV_CHUNK
    tok_f = pl.BlockSpec((None, tb, w), lambda bi, i: (bi, i, 0))
    tok_b = pl.BlockSpec((None, tb, w), lambda bi, i: (bi, nb - 1 - i, 0))
    dir_f = pl.BlockSpec((None, None, tb, w), lambda bi, i: (0, bi, i, 0))
    dir_b = pl.BlockSpec((None, None, tb, w), lambda bi, i: (1, bi, nb - 1 - i, 0))
    xbc_f = pl.BlockSpec((None, tb, SSD_XBC), lambda bi, i: (bi, i, 0))
    xbc_b = pl.BlockSpec((None, tb, SSD_XBC), lambda bi, i: (bi, nb - 1 - i, 0))
    st_r = pl.BlockSpec((2, None, N_PAIRS, PAIR, PAIR), lambda bi, i: (0, bi, 0, 0, 0))
    st_s = pl.BlockSpec((2, None, N_PAIRS, SSD_STATE, PAIR), lambda bi, i: (0, bi, 0, 0, 0))
    y_sd = jax.ShapeDtypeStruct((b, t, w), F32)
    return pl.pallas_call(
        functools.partial(_even_scan_kernel, n_steps=tb // SCAN_CHUNK),
        name="even_scan",
        grid=(b, nb),
        in_specs=[tok_f, tok_f, tok_f, dir_f, dir_f, dir_f, tok_b, tok_b, tok_b, dir_b, dir_b, dir_b,
                  xbc_f, dir_f, xbc_b, dir_b,
                  pl.BlockSpec((2, 1, SSD_WIDTH), lambda bi, i: (0, 0, 0)), st_r, st_s],
        out_specs=[tok_f, tok_b, tok_f, tok_b, st_r, st_s],
        out_shape=[y_sd, y_sd, y_sd, y_sd,
                   jax.ShapeDtypeStruct((2, b, N_PAIRS, PAIR, PAIR), F32),
                   jax.ShapeDtypeStruct((2, b, N_PAIRS, SSD_STATE, PAIR), F32)],
        scratch_shapes=[pltpu.VMEM((2, N_PAIRS, PAIR, PAIR), F32),
                        pltpu.VMEM((2, N_PAIRS, SSD_STATE, PAIR), F32)],
        compiler_params=_cparams(("parallel", "arbitrary")),
    )(r, v, kk, ld, kd, bd, r, v, kk, ld, kd, bd, xbc, dtbc, xbc, dtbc, a_rep, s0_rk, s0_sd)


def _even_finish_kernel(yrf_ref, yrb_ref, bonus_ref, gate_ref, ysf_ref, ysb_ref, xs_ref, z_ref, x_ref,
                        m_ref, lnw_ref, lnb_ref, dsk_ref, nw_ref, j_ref, wo_ref, o_ref):
    jm = j_ref[...]
    y = yrf_ref[...] + yrb_ref[...]
    inv_n = 1.0 / RWKV_HEAD_DIM
    mean = _head_sum(y, jm) * inv_n
    yc = y - mean
    var = _head_sum(yc * yc, jm) * inv_n
    y = yc * lax.rsqrt(var + RWKV_GN_EPS) * lnw_ref[...] + lnb_ref[...]
    y_rk = (y + bonus_ref[...].astype(F32)) * gate_ref[...].astype(F32)
    s = ysf_ref[...] + ysb_ref[...] + dsk_ref[...] * xs_ref[...].astype(F32)
    s = s * _silu(z_ref[...].astype(F32))
    s = s * lax.rsqrt(jnp.mean(s * s, axis=-1, keepdims=True) + NORM_EPS) * nw_ref[...]
    out = _dot(y_rk.astype(BF16), wo_ref[0:RWKV_WIDTH, :]) + \
        _dot(s.astype(BF16), wo_ref[RWKV_WIDTH:RWKV_WIDTH + SSD_WIDTH, :])
    o_ref[...] = x_ref[...] + m_ref[2:3, :] * out


def _even_finish_call(yrk, bonus, gate, ysd, xbc, proj, x, mods, p, tm):
    b, t, d = x.shape
    tm = min(tm, t)
    w = RWKV_WIDTH
    per_batch = mods.shape[0] > 1
    full = lambda shape: pl.BlockSpec(shape, lambda bi, i: (0,) * len(shape))
    tok = pl.BlockSpec((None, tm, w), lambda bi, i: (bi, i, 0))
    return pl.pallas_call(
        _even_finish_kernel,
        name="even_finish",
        grid=(b, t // tm),
        in_specs=[tok, tok, tok, tok, tok, tok,
                  pl.BlockSpec((None, tm, SSD_WIDTH), lambda bi, i: (bi, i, 0)),
                  pl.BlockSpec((None, tm, SSD_WIDTH), lambda bi, i: (bi, i, 0)),
                  pl.BlockSpec((None, tm, d), lambda bi, i: (bi, i, 0)),
                  pl.BlockSpec((None, N_MOD, d), (lambda bi, i: (bi, 0, 0)) if per_batch
                               else (lambda bi, i: (0, 0, 0))),
                  full((1, w)), full((1, w)), full((1, w)), full((1, w)),
                  full((V7X_MXU_DIM, V7X_MXU_DIM)),
                  full((2 * w, d))],
        out_specs=pl.BlockSpec((None, tm, d), lambda bi, i: (bi, i, 0)),
        out_shape=jax.ShapeDtypeStruct((b, t, d), F32),
        compiler_params=_cparams(("parallel", "parallel")),
    )(yrk[0], yrk[1], bonus, gate, ysd[0], ysd[1], xbc, proj, x, mods, p['ln_w'], p['ln_b'], p['d_skip'],
      p['norm_w'],
      p['head_sum'], p['w_out'])


def _ret_scan_kernel(*refs, n_chunks, mode):
    if mode == 'state':
        k_ref, v_ref, lg_ref, s0_ref, sf_ref, st_scr, sc_scr = refs
        d = pl.program_id(0)
    elif mode == 'bwd':
        q_ref, k_ref, v_ref, lg_ref, s0_ref, y_ref, st_scr, dec_scr, sc_scr = refs
        d = 1
    else:
        (q_ref, k_ref, v_ref, lg_ref, s0_ref, yb_ref, g_ref, x_ref, m_ref, wo_ref, o_ref,
         st_scr, dec_scr, sc_scr, ysum_scr) = refs
        d = 0
    i = pl.program_id(2)
    c = SCAN_CHUNK
    dk, dv = RET_QK_DIM, RET_V_DIM
    heads = range(RET_HEADS)
    lg_all = lg_ref[...]

    @pl.when(i == 0)
    def _():
        st_scr[...] = s0_ref[...]
        _, before_eq = _order_masks(d, c)
        row = lax.broadcasted_iota(jnp.int32, (c, c), 0)
        col = lax.broadcasted_iota(jnp.int32, (c, c), 1)
        rel = jnp.abs(row - col).astype(F32)
        pos = (row + d * (c - 1 - 2 * row)).astype(F32)
        for h in heads:
            lg = lg_all[:, h * dk:h * dk + 1]
            if mode != 'state':
                dec_scr[h] = jnp.where(before_eq, jnp.exp(rel * lg), 0.0)
                sc_scr[h, 0] = jnp.exp((pos + 1.0) * lg).astype(BF16)
            sc_scr[h, 1] = jnp.exp((c - 1.0 - pos) * lg).astype(BF16)

    def chunk_body(j, carry):
        cj = j + d * (n_chunks - 1 - 2 * j)
        rows = pl.ds(pl.multiple_of(cj * c, c), c)
        ks = [k_ref[rows, h * dk:(h + 1) * dk] for h in heads]
        vs = [v_ref[rows, h * dv:(h + 1) * dv] for h in heads]
        st = [st_scr[h] for h in heads]
        if mode != 'state':
            qs = [q_ref[rows, h * dk:(h + 1) * dk] for h in heads]
            qk = [_dot_nt(qs[h], ks[h]) for h in heads]
            scores = [(qk[h] * dec_scr[h]).astype(BF16) for h in heads]
            y_st = [_dot(qs[h] * sc_scr[h, 0], st[h].astype(BF16)) for h in heads]
            for h in heads:
                cols = slice(h * dv, (h + 1) * dv)
                y = _dot(scores[h], vs[h]) + y_st[h]
                if mode == 'bwd':
                    y_ref[rows, cols] = y.astype(y_ref.dtype)
                else:
                    ysum_scr[rows, cols] = y + yb_ref[rows, cols].astype(F32)
        for h in heads:
            lg = lg_all[:, h * dk:h * dk + 1]
            st_scr[h] = st[h] * jnp.exp(c * lg) + _dot_tn(ks[h] * sc_scr[h, 1], vs[h])
        return carry

    lax.fori_loop(0, n_chunks, chunk_body, 0, unroll=2)

    if mode == 'state':
        @pl.when(i == pl.num_programs(2) - 1)
        def _():
            sf_ref[...] = st_scr[...]
    elif mode == 'fwd':
        parts = []
        for h in heads:
            yh = ysum_scr[:, h * dv:(h + 1) * dv]
            parts.append(yh * lax.rsqrt(jnp.mean(yh * yh, axis=-1, keepdims=True) + NORM_EPS))
        act = (_silu(g_ref[...].astype(F32)) * jnp.concatenate(parts, axis=1)).astype(BF16)
        o_ref[...] = x_ref[...] + m_ref[2:3, :] * _dot(act, wo_ref[...])


def _ret_scan_call(proj, lg_rep, s0, tb, mode, finish=None):
    b, t, _ = proj.shape
    tb = min(tb, t)
    nb = t // tb
    n_dirs = 2 if mode == 'state' else 1
    d0 = 1 if mode == 'bwd' else 0
    blk = lambda dd, i: i + (dd + d0) * (nb - 1 - 2 * i)
    tok = lambda width, col: pl.BlockSpec((None, tb, width), lambda dd, bi, i: (bi, blk(dd, i), col))
    st = pl.BlockSpec((None, None, RET_HEADS, RET_QK_DIM, RET_V_DIM),
                      lambda dd, bi, i: (dd + d0, bi, 0, 0, 0))
    q_spec, k_spec, v_spec = tok(RET_QK, 0), tok(RET_QK, 1), tok(RET_V, 2 * RET_QK // RET_V)
    lg_spec = pl.BlockSpec((None, 1, RET_QK), lambda dd, bi, i: (dd + d0, 0, 0))
    scratch = [pltpu.VMEM((RET_HEADS, RET_QK_DIM, RET_V_DIM), F32)]
    if mode != 'state':
        scratch.append(pltpu.VMEM((RET_HEADS, SCAN_CHUNK, SCAN_CHUNK), F32))
    scratch.append(pltpu.VMEM((RET_HEADS, 2, SCAN_CHUNK, RET_QK_DIM), BF16))
    if mode == 'state':
        in_specs, args = [k_spec, v_spec, lg_spec, st], [proj, proj, lg_rep, s0]
        out_specs = st
        out_shape = jax.ShapeDtypeStruct((2, b, RET_HEADS, RET_QK_DIM, RET_V_DIM), F32)
    elif mode == 'bwd':
        in_specs, args = [q_spec, k_spec, v_spec, lg_spec, st], [proj, proj, proj, lg_rep, s0]
        out_specs = tok(RET_V, 0)
        out_shape = jax.ShapeDtypeStruct((b, t, RET_V), BF16)
    else:
        y_bwd, x, mods, w_out = finish
        d = x.shape[-1]
        in_specs = [q_spec, k_spec, v_spec, lg_spec, st, tok(RET_V, 0),
                    tok(RET_V, (2 * RET_QK + RET_V) // RET_V), tok(d, 0),
                    pl.BlockSpec((None, N_MOD, d), lambda dd, bi, i: (bi, 0, 0)),
                    pl.BlockSpec((RET_V, d), lambda dd, bi, i: (0, 0))]
        args = [proj, proj, proj, lg_rep, s0, y_bwd, proj, x, mods, w_out]
        out_specs = tok(d, 0)
        out_shape = jax.ShapeDtypeStruct((b, t, d), F32)
        scratch.append(pltpu.VMEM((tb, RET_V), F32))
    return pl.pallas_call(
        functools.partial(_ret_scan_kernel, n_chunks=tb // SCAN_CHUNK, mode=mode),
        name="ret_scan_" + mode,
        grid=(n_dirs, b, nb),
        in_specs=in_specs,
        out_specs=out_specs,
        out_shape=out_shape,
        scratch_shapes=scratch,
        compiler_params=_cparams(("parallel", "parallel", "arbitrary")),
    )(*args)


def _ffn_kernel(*refs, on_grid, final_norm):
    if on_grid:
        x_ref, xn_ref, g_ref, m_ref, wu_ref, cw_ref, cb_ref, wd_ref = refs[:8]
        rest = refs[8:]
    else:
        x_ref, g_ref, m_ref, wu_ref, cw_ref, cb_ref, wd_ref = refs[:7]
        rest = refs[7:]
    if final_norm:
        fg_ref, o_ref, *scrs = rest
    else:
        o_ref, *scrs = rest
    gate_scr, val_scr, *scrs = scrs
    if on_grid:
        gtop_scr, *act_scrs = scrs
    else:
        act_scrs = scrs
    i = pl.program_id(1)
    n_tiles = pl.num_programs(1)
    tm = x_ref.shape[0]

    def norm_mod(xv):
        hv = xv * lax.rsqrt(jnp.mean(xv * xv, axis=-1, keepdims=True) + NORM_EPS) * g_ref[...]
        return (hv * (1.0 + m_ref[4:5, :]) + m_ref[3:4, :]).astype(BF16)

    x = x_ref[...]
    h = norm_mod(x)
    row = lax.broadcasted_iota(jnp.int32, (tm, 1), 0)
    if on_grid:
        col = row % GRID_W
        ok_left = col > 0
        ok_right = col < GRID_W - 1
        zero = jnp.zeros((GRID_W, x.shape[1]), BF16)
        h_ext = jnp.concatenate([h, jnp.where(i < n_tiles - 1, norm_mod(xn_ref[...]), zero)], axis=0)

        @pl.when(i == 0)
        def _():
            gtop_scr[...] = jnp.zeros_like(gtop_scr)
    else:
        ok_left = row > 0
        ok_right = row < tm - 1
    n_chunks = D_FF // FFN_COL_CHUNK

    def up_proj(j):
        cols = slice(j * FFN_COL_CHUNK, (j + 1) * FFN_COL_CHUNK)
        vcols = slice(D_FF + j * FFN_COL_CHUNK, D_FF + (j + 1) * FFN_COL_CHUNK)
        if on_grid:
            gate_scr[j % n_buf, GRID_W:, :] = _dot(h_ext, wu_ref[:, cols])
        else:
            gate_scr[j % n_buf] = _dot(h, wu_ref[:, cols])
        val_scr[j % n_buf] = _dot(h, wu_ref[:, vcols])

    out = None
    n_buf = gate_scr.shape[0]
    for j in range(n_buf - 1):
        up_proj(j)
    for j in range(n_chunks):
        cols = slice(j * FFN_COL_CHUNK, (j + 1) * FFN_COL_CHUNK)
        buf = j % n_buf
        if j + n_buf - 1 < n_chunks:
            up_proj(j + n_buf - 1)
        grp, slot = divmod(j, FFN_DOWN_GROUP)
        act_scr = act_scrs[grp]
        acols = slice(slot * FFN_COL_CHUNK, (slot + 1) * FFN_COL_CHUNK)
        if on_grid:
            gate_scr[buf, 0:GRID_W, :] = gtop_scr[:, cols]
            gtop_scr[:, cols] = gate_scr[buf, tm:tm + GRID_W, :]
            rows3 = [gate_scr[buf, dr * GRID_W:dr * GRID_W + tm, :] for dr in range(3)]
            taps = [rows3[0] * cw_ref[dc:dc + 1, cols] + rows3[1] * cw_ref[3 + dc:4 + dc, cols]
                    + rows3[2] * cw_ref[6 + dc:7 + dc, cols] for dc in range(3)]
        else:
            gate = gate_scr[buf]
            taps = [gate * cw_ref[3 + dc:4 + dc, cols] for dc in range(3)]
        acc = cb_ref[:, cols] + taps[1] + jnp.where(ok_left, pltpu.roll(taps[0], 1, axis=0), 0.0) \
            + jnp.where(ok_right, pltpu.roll(taps[2], tm - 1, axis=0), 0.0)
        act_scr[:, acols] = (_gelu_tanh(acc) * val_scr[buf]).astype(BF16)
        if slot + 1 == FFN_DOWN_GROUP or j + 1 == n_chunks:
            width = (slot + 1) * FFN_COL_CHUNK
            k0 = grp * FFN_DOWN_GROUP * FFN_COL_CHUNK
            part = _dot(act_scr[:, 0:width], wd_ref[k0:k0 + width, :])
            out = part if out is None else out + part
    out = x + m_ref[5:6, :] * out
    if final_norm:
        out = out * lax.rsqrt(jnp.mean(out * out, axis=-1, keepdims=True) + NORM_EPS) * fg_ref[...]
    o_ref[...] = out


def _ffn_call(x, norm_g, mods, w_up, conv_w9, conv_b, w_down, *, tm, on_grid, final_g=None):
    b, t, d = x.shape
    tm = min(tm, t)
    n_tiles = t // tm
    per_batch = mods.shape[0] > 1
    full = lambda shape: pl.BlockSpec(shape, lambda bi, i: (0,) * len(shape))
    in_specs = [pl.BlockSpec((None, tm, d), lambda bi, i: (bi, i, 0))]
    args = [x]
    scratch = [pltpu.VMEM((tm, FFN_DOWN_GROUP * FFN_COL_CHUNK), BF16)
               for _ in range(-(-D_FF // (FFN_DOWN_GROUP * FFN_COL_CHUNK)))]
    if on_grid:
        r = tm // GRID_W
        last = t // GRID_W - 1
        in_specs.append(pl.BlockSpec((None, GRID_W, d), lambda bi, i: (bi, jnp.minimum((i + 1) * r, last), 0)))
        args.append(x)
        scratch.insert(0, pltpu.VMEM((GRID_W, D_FF), F32))
    else:
        assert n_tiles == 1
    gate_rows = tm + 2 * GRID_W if on_grid else tm
    scratch = [pltpu.VMEM((FFN_UP_BUFFERS, gate_rows, FFN_COL_CHUNK), F32),
               pltpu.VMEM((FFN_UP_BUFFERS, tm, FFN_COL_CHUNK), F32)] + scratch
    in_specs += [full((1, d)),
                 pl.BlockSpec((None, N_MOD, d), (lambda bi, i: (bi, 0, 0)) if per_batch
                              else (lambda bi, i: (0, 0, 0))),
                 full((d, 2 * D_FF)), full((9, D_FF)), full((1, D_FF)), full((D_FF, d))]
    args += [norm_g, mods, w_up, conv_w9, conv_b, w_down]
    if final_g is not None:
        in_specs.append(full((1, d)))
        args.append(final_g)
    return pl.pallas_call(
        functools.partial(_ffn_kernel, on_grid=on_grid, final_norm=final_g is not None),
        name="conv_ffn",
        grid=(b, n_tiles),
        in_specs=in_specs,
        out_specs=pl.BlockSpec((None, tm, d), lambda bi, i: (bi, i, 0)),
        out_shape=jax.ShapeDtypeStruct((b, t, d), F32),
        scratch_shapes=scratch,
        compiler_params=_cparams(("parallel", "arbitrary")),
    )(*args)


def _block_diag2(a, b):
    za = jnp.zeros((a.shape[0], b.shape[1]), a.dtype)
    zb = jnp.zeros((b.shape[0], a.shape[1]), a.dtype)
    return jnp.concatenate([jnp.concatenate([a, za], axis=1), jnp.concatenate([zb, b], axis=1)], axis=0)


def _pad_cols(a, n, fill=0.0):
    return jnp.pad(a, ((0, 0), (0, n - a.shape[1])), constant_values=fill)


def _even_params(j, ev_w_in, ev_mu_prev, ev_mu_next, rk_w0_f, rk_w0_b, rk_w2_f, rk_w2_b, rk_a0_f,
                 rk_a0_b, rk_a2_f, rk_a2_b, rk_g2, rk_k_k, rk_k_a, rk_r_k, rk_ln_w, rk_ln_b,
                 ssd_conv_w, ssd_conv_b, ssd_dt_bias_f, ssd_dt_bias_b, ssd_a_log_f, ssd_a_log_b,
                 ssd_d, ssd_norm_w, ev_w_out):
    w_in = ev_w_in[j]
    rw = w_in[:, :RWKV_COLS]
    z = w_in[:, RWKV_COLS:RWKV_COLS + SSD_WIDTH]
    xbc = w_in[:, RWKV_COLS + SSD_WIDTH:RWKV_COLS + SSD_WIDTH + SSD_XBC]
    dts = w_in[:, RWKV_COLS + SSD_WIDTH + SSD_XBC:]
    w_packed = jnp.concatenate([_pad_cols(jnp.concatenate([rw, dts], axis=1), EV_RW_BLOCK), xbc, z], axis=1)
    head = jnp.arange(V7X_MXU_DIM) // RWKV_HEAD_DIM
    head_sum = (head[:, None] == head[None, :]).astype(BF16)
    lane = jnp.arange(V7X_LANES)[:, None]
    tgt = jnp.arange(2 * SSD_WIDTH)[None, :]
    dt_expand = (lane == (tgt // SSD_WIDTH) * SSD_HEADS + (tgt % SSD_WIDTH) // SSD_HEAD_DIM).astype(BF16)
    rep = lambda a: jnp.repeat(a, SSD_HEAD_DIM)[None, :]
    row = lambda a: a[None, :]
    return {
        'w_in': w_packed.astype(BF16),
        'mu_self': _pad_cols(row(1.0 - ev_mu_prev[j] - ev_mu_next[j]), EV_RW_BLOCK, 1.0),
        'mu_prev': _pad_cols(row(ev_mu_prev[j]), EV_RW_BLOCK),
        'mu_next': _pad_cols(row(ev_mu_next[j]), EV_RW_BLOCK),
        'w0': row(jnp.concatenate([rk_w0_f[j], rk_w0_b[j]])),
        'w2': _block_diag2(rk_w2_f[j], rk_w2_b[j]).astype(BF16),
        'a0': row(jnp.concatenate([rk_a0_f[j], rk_a0_b[j]])),
        'a2': _block_diag2(rk_a2_f[j], rk_a2_b[j]).astype(BF16),
        'g2': rk_g2[j].astype(BF16),
        'k_k': row(rk_k_k[j]), 'k_a': row(rk_k_a[j]), 'r_k': row(rk_r_k[j].reshape(-1)),
        'ln_w': row(rk_ln_w[j]), 'ln_b': row(rk_ln_b[j]),
        'head_sum': head_sum,
        'conv_w': ssd_conv_w[j], 'conv_b': row(ssd_conv_b[j]),
        'dt_bias': _pad_cols(row(jnp.concatenate([ssd_dt_bias_f[j], ssd_dt_bias_b[j]])), V7X_LANES),
        'dt_expand': dt_expand,
        'a_rep': jnp.stack([rep(-jnp.exp(ssd_a_log_f[j])), rep(-jnp.exp(ssd_a_log_b[j]))]),
        'd_skip': rep(ssd_d[j]),
        'norm_w': row(ssd_norm_w[j]),
        'w_out': ev_w_out[j].astype(BF16),
    }


def _rope_tables(t):
    n = RET_QK_DIM // 4
    pos = jnp.arange(t)
    row = (pos // GRID_W).astype(F32)
    col = (pos % GRID_W).astype(F32)
    inv = ROPE_BASE ** (-jnp.arange(n, dtype=F32) / n)
    ang = jnp.concatenate([row[:, None] * inv, col[:, None] * inv], axis=-1)
    cos, sin = jnp.cos(ang), jnp.sin(ang)
    return jnp.concatenate([cos, cos], axis=-1), jnp.concatenate([-sin, sin], axis=-1)


def _conv_ffn(x, mods, norm_g, w_up, conv_w9, conv_b, w_down, *, on_grid, final_g=None):
    return _ffn_call(x, norm_g, mods, w_up, conv_w9, conv_b, w_down, tm=512, on_grid=on_grid,
                     final_g=final_g)


def _even_layer(x, ctx, mods_x, mods_c, norm_g, p):
    b = x.shape[0]

    def features(h, mods):
        *feat, xbc, dtbc, z = _even_feat_call(h, norm_g, mods, p, 512)
        return z, feat, xbc, dtbc

    proj_c, feat_c, xbc_c, dt_c = features(ctx, mods_c)
    proj_x, feat_x, xbc_x, dt_x = features(x, mods_x)
    s0 = jnp.zeros((2, b, N_PAIRS, PAIR, PAIR), F32)
    h0 = jnp.zeros((2, b, N_PAIRS, SSD_STATE, PAIR), F32)
    *y_c, s_ctx, h_ctx = _even_scan_call(feat_c, xbc_c, dt_c, p['a_rep'], s0, h0, 256)
    *y_x, _, _ = _even_scan_call(feat_x, xbc_x, dt_x, p['a_rep'], s_ctx, h_ctx, 512)
    yrk_c, ysd_c = y_c[0:2], y_c[2:4]
    yrk_x, ysd_x = y_x[0:2], y_x[2:4]
    x = _even_finish_call(yrk_x, feat_x[3], feat_x[4], ysd_x, xbc_x, proj_x, x, mods_x, p, 512)
    ctx = _even_finish_call(yrk_c, feat_c[3], feat_c[4], ysd_c, xbc_c, proj_c, ctx, mods_c, p, 256)
    return x, ctx


def _odd_layer(x, ctx, mods_x, mods_c, norm_g, w_in, lg_rep, w_out):
    b, t, _ = x.shape
    proj_c = _nm_call(ctx, norm_g, mods_c, w_in, shift_row=0, tm=256, tn=512, out_dtype=BF16,
                      qk_mode='scale')
    proj_x = _nm_call(x, norm_g, mods_x, w_in, shift_row=0, tm=512, tn=512, out_dtype=BF16,
                      qk_mode='rope', rope=_rope_tables(t))
    s0 = jnp.zeros((2, b, RET_HEADS, RET_QK_DIM, RET_V_DIM), F32)
    s_ctx = _ret_scan_call(proj_c, lg_rep, s0, 256, 'state')
    y_bwd = _ret_scan_call(proj_x, lg_rep, s_ctx, 512, 'bwd')
    return _ret_scan_call(proj_x, lg_rep, s_ctx, 512, 'fwd', finish=(y_bwd, x, mods_x, w_out))


def kernel(x, c, ctx, c_ctx, mod_w, mod_b, norm1_g, norm2_g, ffn_w_up, ffn_conv_w, ffn_conv_b, ffn_w_down, ev_w_in, ev_mu_prev, ev_mu_next, rk_w0_f, rk_w0_b, rk_w2_f, rk_w2_b, rk_a0_f, rk_a0_b, rk_a2_f, rk_a2_b, rk_g2, rk_k_k, rk_k_a, rk_r_k, rk_ln_w, rk_ln_b, ssd_conv_w, ssd_conv_b, ssd_dt_bias_f, ssd_dt_bias_b, ssd_a_log_f, ssd_a_log_b, ssd_d, ssd_norm_w, ev_w_out, ret_w_in, ret_log2_f, ret_log2_b, ret_w_out, final_norm_g):
    b, t, d = x.shape
    depth = mod_w.shape[0]
    rows = -(-(b + 1) // V7X_SUBLANES) * V7X_SUBLANES
    cond = jnp.concatenate([c, c_ctx[None, :], jnp.zeros((rows - b - 1, d), F32)], axis=0)
    mods = _mod_call(cond, mod_w, mod_b).reshape(depth, rows, N_MOD, d)
    for i in range(depth):
        need_ctx = i < depth - 1
        mods_x = mods[i, :b]
        mods_c = mods[i, b:b + 1]
        j = i // 2
        g1 = norm1_g[i][None, :]
        if i % 2 == 0:
            p = _even_params(j, ev_w_in, ev_mu_prev, ev_mu_next, rk_w0_f, rk_w0_b, rk_w2_f, rk_w2_b,
                             rk_a0_f, rk_a0_b, rk_a2_f, rk_a2_b, rk_g2, rk_k_k, rk_k_a, rk_r_k,
                             rk_ln_w, rk_ln_b, ssd_conv_w, ssd_conv_b, ssd_dt_bias_f, ssd_dt_bias_b,
                             ssd_a_log_f, ssd_a_log_b, ssd_d, ssd_norm_w, ev_w_out)
            x, ctx_mixed = _even_layer(x, ctx, mods_x, mods_c, g1, p)
        else:
            lg = jnp.stack([jnp.log1p(-jnp.exp2(-ret_log2_f[j])), jnp.log1p(-jnp.exp2(-ret_log2_b[j]))])
            lg_rep = jnp.repeat(lg, RET_QK_DIM, axis=-1)[:, None, :]
            x = _odd_layer(x, ctx, mods_x, mods_c, g1, ret_w_in[j].astype(BF16), lg_rep,
                           ret_w_out[j].astype(BF16))
            ctx_mixed = None
        g2 = norm2_g[i][None, :]
        w_up = ffn_w_up[i].astype(BF16)
        w_down = ffn_w_down[i].astype(BF16)
        conv_w9 = ffn_conv_w[i].reshape(9, D_FF)
        conv_b = ffn_conv_b[i][None, :]
        last = i == depth - 1
        x = _conv_ffn(x, mods_x, g2, w_up, conv_w9, conv_b, w_down, on_grid=True,
                      final_g=final_norm_g[None, :] if last else None)
        if need_ctx:
            ctx = _conv_ffn(ctx_mixed, mods_c, g2, w_up, conv_w9, conv_b, w_down, on_grid=False)
    return x
```

```python
import functools
import math

import jax
import jax.numpy as jnp
from jax import lax
from jax.experimental import pallas as pl
from jax.experimental.pallas import tpu as pltpu

F32 = jnp.float32
BF16 = jnp.bfloat16

D_MODEL = 1024
GRID_W = 64
N_MOD = 6
NORM_EPS = 1e-6
RWKV_HEADS = 8
RWKV_HEAD_DIM = 64
RWKV_WIDTH = RWKV_HEADS * RWKV_HEAD_DIM
DECAY_LORA = 64
ICLR_LORA = 64
GATE_LORA = 128
RWKV_GN_EPS = 64e-5
RWKV_COLS = 3 * RWKV_WIDTH + 2 * DECAY_LORA + 2 * ICLR_LORA + GATE_LORA
SSD_HEADS = 8
SSD_HEAD_DIM = 64
SSD_WIDTH = SSD_HEADS * SSD_HEAD_DIM
SSD_GROUPS = 2
SSD_STATE = 128
SSD_XBC = SSD_WIDTH + 2 * SSD_GROUPS * SSD_STATE
RET_HEADS = 8
RET_QK_DIM = 128
RET_V_DIM = 256
RET_QK = RET_HEADS * RET_QK_DIM
RET_V = RET_HEADS * RET_V_DIM
ROPE_BASE = 10000.0
D_FF = 2816

V7X_LANES = 128
V7X_SUBLANES = 8
V7X_MXU_DIM = 256
V7X_VMEM_LIMIT_BYTES = 56 * 1024 * 1024

RWKV_CHUNK = 64
RWKV_CHUNKS_PER_STEP = 2
SCAN_CHUNK = 128
FFN_COL_CHUNK = 256
FFN_DOWN_GROUP = 6
FFN_UP_BUFFERS = 2
PAIR = 2 * RWKV_HEAD_DIM
N_PAIRS = RWKV_HEADS // 2
EV_RW_BLOCK = 2048
EV_DT_OFF = RWKV_COLS
EV_XBC_OFF = EV_RW_BLOCK
EV_Z_OFF = EV_RW_BLOCK + SSD_XBC
EV_COLS = EV_Z_OFF + SSD_WIDTH
EV_PROJ_CHUNK = 1024


def _cparams(sem):
    return pltpu.CompilerParams(dimension_semantics=sem, vmem_limit_bytes=V7X_VMEM_LIMIT_BYTES)


def _split3(x):
    hi = x.astype(BF16)
    r1 = x - hi.astype(F32)
    mid = r1.astype(BF16)
    lo = (r1 - mid.astype(F32)).astype(BF16)
    return hi, mid, lo


def _dot(a, b):
    return jnp.dot(a, b, preferred_element_type=F32)


def _dot_nt(a, b):
    return lax.dot_general(a, b, (((1,), (1,)), ((), ())), preferred_element_type=F32)


def _dot_tn(a, b):
    return lax.dot_general(a, b, (((0,), (0,)), ((), ())), preferred_element_type=F32)


def _dot01(x, m01):
    hi, mid, lo = _split3(x)
    return _dot(hi, m01) + _dot(mid, m01) + _dot(lo, m01)


def _head_sum(x, j01):
    hi = x.astype(BF16)
    lo = (x - hi.astype(F32)).astype(BF16)
    n = j01.shape[0]
    return jnp.concatenate([_dot(hi[:, g * n:(g + 1) * n], j01) + _dot(lo[:, g * n:(g + 1) * n], j01)
                            for g in range(x.shape[1] // n)], axis=1)


def _dot01_left(m01, x):
    hi, mid, lo = _split3(x)
    return _dot(m01, hi) + _dot(m01, mid) + _dot(m01, lo)


def _sigmoid(x):
    return 0.5 * jnp.tanh(0.5 * x) + 0.5


def _silu(x):
    half = 0.5 * x
    return half + half * jnp.tanh(half)


def _softplus(x):
    return jnp.maximum(x, 0.0) + jnp.log1p(jnp.exp(-jnp.abs(x)))


def _gelu_tanh(x):
    c = math.sqrt(2.0 / math.pi)
    half = 0.5 * x
    return half + half * jnp.tanh(x * (c + (0.044715 * c) * (x * x)))


def _order_masks(d, n):
    row = lax.broadcasted_iota(jnp.int32, (n, n), 0)
    col = lax.broadcasted_iota(jnp.int32, (n, n), 1)
    diff = (row - col) * (1 - 2 * d)
    return diff > 0, diff >= 0


def _mod_kernel(c_ref, w_ref, b_ref, o_ref):
    h = _silu(c_ref[...])
    hi, mid, lo = _split3(h)
    w = w_ref[...]
    wh = w.astype(BF16)
    wl = (w - wh.astype(F32)).astype(BF16)
    acc = _dot(hi, wh) + _dot(mid, wh) + _dot(hi, wl)
    o_ref[...] = acc + b_ref[...]


def _mod_call(cond, mod_w, mod_b):
    depth, d, n = mod_w.shape
    rows = cond.shape[0]
    tn = 1024
    return pl.pallas_call(
        _mod_kernel,
        name="adaln_mod",
        grid=(depth, n // tn),
        in_specs=[pl.BlockSpec((rows, d), lambda l, j: (0, 0)),
                  pl.BlockSpec((None, d, tn), lambda l, j: (l, 0, j)),
                  pl.BlockSpec((None, 1, tn), lambda l, j: (l, 0, j))],
        out_specs=pl.BlockSpec((None, rows, tn), lambda l, j: (l, 0, j)),
        out_shape=jax.ShapeDtypeStruct((depth, rows, n), F32),
        compiler_params=_cparams(("parallel", "parallel")),
    )(cond, mod_w, mod_b.reshape(depth, 1, n))


def _nm_kernel(*refs, shift_row, tn, qk_mode):
    if qk_mode == 'rope':
        x_ref, g_ref, m_ref, w_ref, cos_ref, sin_ref, o_ref = refs
    else:
        x_ref, g_ref, m_ref, w_ref, o_ref = refs
    x = x_ref[...]
    h = x * lax.rsqrt(jnp.mean(x * x, axis=-1, keepdims=True) + NORM_EPS) * g_ref[...]
    h = h * (1.0 + m_ref[shift_row + 1:shift_row + 2, :]) + m_ref[shift_row:shift_row + 1, :]
    h = h.astype(BF16)
    dk = RET_QK_DIM
    for j in range(w_ref.shape[1] // tn):
        cols = slice(j * tn, (j + 1) * tn)
        y = _dot(h, w_ref[:, cols])
        if qk_mode is not None and j * tn < 2 * RET_QK:
            scale = dk ** -0.5 if j * tn >= RET_QK else 1.0
            heads = []
            for hh in range(tn // dk):
                yh = y[:, hh * dk:(hh + 1) * dk]
                if qk_mode == 'rope':
                    yh = yh * cos_ref[...] + pltpu.roll(yh, dk // 2, axis=1) * sin_ref[...]
                heads.append(yh * scale if scale != 1.0 else yh)
            y = jnp.concatenate(heads, axis=1)
        o_ref[:, cols] = y.astype(o_ref.dtype)


def _nm_call(x, g, mods, w, *, shift_row, tm, tn, out_dtype=F32, qk_mode=None, rope=None):
    b, t, d = x.shape
    n = w.shape[1]
    tm = min(tm, t)
    per_batch = mods.shape[0] > 1
    in_specs = [pl.BlockSpec((None, tm, d), lambda bi, i: (bi, i, 0)),
                pl.BlockSpec((1, d), lambda bi, i: (0, 0)),
                pl.BlockSpec((None, N_MOD, d), (lambda bi, i: (bi, 0, 0)) if per_batch
                             else (lambda bi, i: (0, 0, 0))),
                pl.BlockSpec((d, n), lambda bi, i: (0, 0))]
    args = [x, g, mods, w]
    if qk_mode == 'rope':
        tab = pl.BlockSpec((tm, RET_QK_DIM), lambda bi, i: (i, 0))
        in_specs += [tab, tab]
        args += list(rope)
    return pl.pallas_call(
        functools.partial(_nm_kernel, shift_row=shift_row, tn=tn, qk_mode=qk_mode),
        name="norm_mod_matmul",
        grid=(b, t // tm),
        in_specs=in_specs,
        out_specs=pl.BlockSpec((None, tm, n), lambda bi, i: (bi, i, 0)),
        out_shape=jax.ShapeDtypeStruct((b, t, n), out_dtype),
        compiler_params=_cparams(("parallel", "parallel")),
    )(*args)


def _even_feat_kernel(x_ref, xp_ref, xn_ref, g_ref, m_ref, w_ref,
                      mus_ref, mup_ref, mun_ref, w0_ref, w2_ref, a0_ref, a2_ref,
                      g2_ref, kk_ref, ka_ref, rk_ref, j_ref, cw_ref, cb_ref, dtb_ref, e_ref,
                      r_ref, v_ref, kkn_ref, bonus_ref, gate_ref, ld_ref, kd_ref, bd_ref,
                      xbc_ref, dtbc_ref, z_ref):
    i = pl.program_id(1)
    n_tiles = pl.num_programs(1)
    tm = x_ref.shape[0]
    halo = V7X_SUBLANES
    ext = tm + 2 * halo

    def norm_mod(xv):
        hv = xv * lax.rsqrt(jnp.mean(xv * xv, axis=-1, keepdims=True) + NORM_EPS) * g_ref[...]
        return (hv * (1.0 + m_ref[1:2, :]) + m_ref[0:1, :]).astype(BF16)

    h = norm_mod(x_ref[...])
    zero = jnp.zeros((halo, x_ref.shape[1]), BF16)
    h_ext = jnp.concatenate([jnp.where(i > 0, norm_mod(xp_ref[...]), zero), h,
                             jnp.where(i < n_tiles - 1, norm_mod(xn_ref[...]), zero)], axis=0)

    def proj3(cols):
        ye = _dot(h_ext, w_ref[:, cols])
        return (ye[halo:halo + tm], pltpu.roll(ye, 1, axis=0)[halo:halo + tm],
                pltpu.roll(ye, ext - 1, axis=0)[halo:halo + tm])

    w = RWKV_WIDTH
    rw = []
    pc = EV_PROJ_CHUNK
    for j in range(EV_RW_BLOCK // pc):
        cols = slice(j * pc, (j + 1) * pc)
        cur, prev, nxt = proj3(cols)
        rw.append(cur * mus_ref[:, cols] + prev * mup_ref[:, cols] + nxt * mun_ref[:, cols])
    rw = jnp.concatenate(rw, axis=1)
    r, k, v, lora = (rw[:, m * w:(m + 1) * w] for m in range(EV_RW_BLOCK // w))
    wd = lora[:, 0:2 * DECAY_LORA]
    ad = lora[:, 2 * DECAY_LORA:2 * DECAY_LORA + 2 * ICLR_LORA]
    gd = lora[:, 2 * DECAY_LORA + 2 * ICLR_LORA:2 * DECAY_LORA + 2 * ICLR_LORA + GATE_LORA]
    dt_raw = lora[:, EV_DT_OFF - 3 * w:EV_RW_BLOCK - 3 * w]
    jm = j_ref[...]

    for j in range(SSD_XBC // pc):
        cols = slice(j * pc, (j + 1) * pc)
        cur, prev, nxt = proj3(slice(EV_XBC_OFF + j * pc, EV_XBC_OFF + (j + 1) * pc))
        y = prev * cw_ref[0:1, cols] + cur * cw_ref[1:2, cols] + nxt * cw_ref[2:3, cols] + cb_ref[:, cols]
        xbc_ref[:, cols] = _silu(y).astype(xbc_ref.dtype)
    z_ref[...] = _dot(h, w_ref[:, EV_Z_OFF:EV_Z_OFF + SSD_WIDTH]).astype(z_ref.dtype)
    dt = _dot01(_softplus(dt_raw + dtb_ref[...]), e_ref[...])
    dtbc_ref[0] = dt[:, 0:SSD_WIDTH]
    dtbc_ref[1] = dt[:, SSD_WIDTH:2 * SSD_WIDTH]

    kk = k * kk_ref[...]
    ss = _head_sum(kk * kk, jm)
    kk = kk / jnp.maximum(jnp.sqrt(ss), 1e-12)
    r_ref[...] = r.astype(r_ref.dtype)
    v_ref[...] = v.astype(v_ref.dtype)
    kkn_ref[...] = kk.astype(kkn_ref.dtype)
    bonus_ref[...] = (_head_sum(r * k * rk_ref[...], jm) * v).astype(bonus_ref.dtype)
    gate_ref[...] = _dot(_sigmoid(gd).astype(BF16), g2_ref[...]).astype(gate_ref.dtype)

    zw = _dot(jnp.tanh(wd).astype(BF16), w2_ref[...]) + w0_ref[...]
    za = _dot(ad.astype(BF16), a2_ref[...]) + a0_ref[...]
    for di in range(2):
        ld_ref[di] = -math.exp(-0.5) * _sigmoid(zw[:, di * w:(di + 1) * w])
        iclr = _sigmoid(za[:, di * w:(di + 1) * w])
        kd_ref[di] = (k * (1.0 + (iclr - 1.0) * ka_ref[...])).astype(kd_ref.dtype)
        bd_ref[di] = (kk * iclr).astype(bd_ref.dtype)


def _even_feat_call(x, norm_g, mods, p, tm):
    b, t, d = x.shape
    tm = min(tm, t)
    n_tiles = t // tm
    w = RWKV_WIDTH
    per_batch = mods.shape[0] > 1
    r8 = tm // V7X_SUBLANES
    last = t // V7X_SUBLANES - 1
    full = lambda shape: pl.BlockSpec(shape, lambda bi, i: (0,) * len(shape))
    tok = pl.BlockSpec((None, tm, w), lambda bi, i: (bi, i, 0))
    tok2 = pl.BlockSpec((2, None, tm, w), lambda bi, i: (0, bi, i, 0))
    sd = jax.ShapeDtypeStruct((b, t, w), BF16)
    sd2 = jax.ShapeDtypeStruct((2, b, t, w), BF16)
    f2 = jax.ShapeDtypeStruct((2, b, t, w), F32)
    return pl.pallas_call(
        _even_feat_kernel,
        name="even_feat",
        grid=(b, n_tiles),
        in_specs=[pl.BlockSpec((None, tm, d), lambda bi, i: (bi, i, 0)),
                  pl.BlockSpec((None, V7X_SUBLANES, d), lambda bi, i: (bi, jnp.maximum(i * r8 - 1, 0), 0)),
                  pl.BlockSpec((None, V7X_SUBLANES, d), lambda bi, i: (bi, jnp.minimum((i + 1) * r8, last), 0)),
                  full((1, d)),
                  pl.BlockSpec((None, N_MOD, d), (lambda bi, i: (bi, 0, 0)) if per_batch
                               else (lambda bi, i: (0, 0, 0))),
                  full((d, EV_COLS)),
                  full((1, EV_RW_BLOCK)), full((1, EV_RW_BLOCK)), full((1, EV_RW_BLOCK)),
                  full((1, 2 * w)), full((2 * DECAY_LORA, 2 * w)),
                  full((1, 2 * w)), full((2 * ICLR_LORA, 2 * w)),
                  full((GATE_LORA, w)), full((1, w)), full((1, w)), full((1, w)),
                  full((V7X_MXU_DIM, V7X_MXU_DIM)),
                  full((3, SSD_XBC)), full((1, SSD_XBC)), full((1, V7X_LANES)),
                  full((V7X_LANES, 2 * SSD_WIDTH))],
        out_specs=[tok, tok, tok, tok, tok, tok2, tok2, tok2,
                   pl.BlockSpec((None, tm, SSD_XBC), lambda bi, i: (bi, i, 0)), tok2, tok],
        out_shape=[sd, sd, sd, sd, sd, f2, sd2, sd2,
                   jax.ShapeDtypeStruct((b, t, SSD_XBC), BF16), f2, sd],
        compiler_params=_cparams(("parallel", "parallel")),
    )(x, x, x, norm_g, mods, p['w_in'], p['mu_self'], p['mu_prev'], p['mu_next'], p['w0'], p['w2'],
      p['a0'], p['a2'], p['g2'], p['k_k'], p['k_a'], p['r_k'], p['head_sum'],
      p['conv_w'], p['conv_b'], p['dt_bias'], p['dt_expand'])


def _rwkv_scan_body(rf_ref, vf_ref, kkf_ref, ldf_ref, kdf_ref, bdf_ref,
                    rb_ref, vb_ref, kkb_ref, ldb_ref, kdb_ref, bdb_ref, yf_ref, yb_ref, st_scr, n_chunks):
    c = RWKV_CHUNK
    rowc = lax.broadcasted_iota(jnp.int32, (c, 2 * c), 0)
    colc = lax.broadcasted_iota(jnp.int32, (c, 2 * c), 1) % c
    eye_wide = (rowc == colc).astype(F32)
    m0 = lax.broadcasted_iota(jnp.int32, (c, PAIR), 1) < RWKV_HEAD_DIM
    row1 = lax.broadcasted_iota(jnp.int32, (c, 1), 0)
    incl01, strict_wide, incl_wide, rsel = [], [], [], []
    for d in range(2):
        sgn = 1 - 2 * d
        incl01.append(_order_masks(d, c)[1].astype(BF16))
        strict_wide.append((rowc - colc) * sgn > 0)
        incl_wide.append((rowc - colc) * sgn >= 0)
        rsel.append(row1 == (c - 1 if d == 0 else 0))
    refs = ((rf_ref, vf_ref, kkf_ref, ldf_ref, kdf_ref, bdf_ref, yf_ref),
            (rb_ref, vb_ref, kkb_ref, ldb_ref, kdb_ref, bdb_ref, yb_ref))
    chains = [(d, p) for d in range(2) for p in range(N_PAIRS)]
    lanes = [slice(p * PAIR, (p + 1) * PAIR) for p in range(N_PAIRS)]

    def stack(x):
        return jnp.concatenate([jnp.where(m0, x, 0.0), jnp.where(m0, 0.0, x)], axis=0)

    sub = range(RWKV_CHUNKS_PER_STEP)
    items = [(u, d, p) for u in sub for d, p in chains]

    def stages(j):
        def rows_of(u, d):
            cj = j * len(sub) + u
            return pl.ds(pl.multiple_of((cj if d == 0 else n_chunks - 1 - cj) * c, c), c)

        rows = {(u, d): rows_of(u, d) for u in sub for d in range(2)}
        cs_all = {ud: _dot01_left(incl01[ud[1]], refs[ud[1]][3][rows[ud], :]) for ud in rows}
        yield
        a_s, r_t, b_s, k_s, v_s, b_end, k_end, g_tot = [], [], [], [], [], [], [], []
        for u, d, p in items:
            r_ref, v_ref, kk_ref, ld_ref, kd_ref, bd_ref, _ = refs[d]
            rw, ln = rows[u, d], lanes[p]
            cs = cs_all[u, d][:, ln]
            cs_last = jnp.sum(jnp.where(rsel[d], cs, 0.0), axis=0, keepdims=True)
            g_neg = jnp.exp(-cs)
            g_end = jnp.exp(cs_last - cs)
            g_tot.append(jnp.exp(cs_last))
            kdv = kd_ref[rw, ln].astype(F32)
            bdv = bd_ref[rw, ln].astype(F32)
            a_w = (-kk_ref[rw, ln].astype(F32) * jnp.exp(cs - ld_ref[rw, ln]))
            a_s.append((a_w.astype(BF16), stack(a_w).astype(BF16)))
            r_t.append((r_ref[rw, ln].astype(F32) * jnp.exp(cs)).astype(BF16))
            b_s.append(stack(bdv * g_neg).astype(BF16))
            k_s.append(stack(kdv * g_neg).astype(BF16))
            v_s.append(stack(v_ref[rw, ln].astype(F32)).astype(BF16))
            b_end.append(stack(bdv * g_end).astype(BF16))
            k_end.append(stack(kdv * g_end).astype(BF16))
            if p == N_PAIRS - 1:
                yield
        n = range(len(items))
        dirs = [d for _, d, _ in items]
        gram = [_dot_nt(jnp.concatenate([a_s[q][0], r_t[q]], axis=0),
                        jnp.concatenate([b_s[q], k_s[q]], axis=0)) for q in n]
        yield
        a_ab =[jnp.where(strict_wide[dirs[q]], gram[q][0:c, 0:2 * c], 0.0) for q in n]
        a_ak = [jnp.where(strict_wide[dirs[q]], gram[q][0:c, 2 * c:4 * c], 0.0).astype(BF16) for q in n]
        p_rb = [jnp.where(incl_wide[dirs[q]], gram[q][c:2 * c, 0:2 * c], 0.0).astype(BF16) for q in n]
        p_rk = [jnp.where(incl_wide[dirs[q]], gram[q][c:2 * c, 2 * c:4 * c], 0.0).astype(BF16) for q in n]
        minv = [eye_wide + a_ab[q] for q in n]
        pw = [a_ab[q] for q in n]
        pw = [_dot(pw[q].astype(BF16), stack(pw[q]).astype(BF16)) for q in n]
        yield
        levels = int(math.log2(c))
        for k in range(2, levels):
            both = [_dot(pw[q].astype(BF16),
                         jnp.concatenate([stack(pw[q]), stack(minv[q])], axis=1).astype(BF16)) for q in n]
            minv = [minv[q] + both[q][:, 2 * c:4 * c] for q in n]
            pw = [both[q][:, 0:2 * c] for q in n]
            yield
        minv = [minv[q] + _dot(pw[q].astype(BF16), stack(minv[q]).astype(BF16)) for q in n]
        akv = [_dot(a_ak[q], v_s[q]) for q in n]
        yield
        eff = [_dot(minv[q].astype(BF16),
                    jnp.concatenate([a_s[q][1], stack(akv[q]).astype(BF16)], axis=1)) for q in n]
        bk_end = [jnp.concatenate([b_end[q], k_end[q]], axis=0) for q in n]
        p_both = [jnp.concatenate([p_rb[q], p_rk[q]], axis=1) for q in n]
        yield
        st = [st_scr[d, p] for d, p in chains]
        nc = range(len(chains))
        for u in sub:
            q0 = u * len(chains)
            fs = [_dot_nt(jnp.concatenate([eff[q0 + m][:, 0:PAIR].astype(BF16), r_t[q0 + m]], axis=0),
                          st[m].astype(BF16)) for m in nc]
            uv = [jnp.concatenate([stack(fs[m][0:c] + eff[q0 + m][:, PAIR:2 * PAIR]).astype(BF16),
                                   v_s[q0 + m]], axis=0) for m in nc]
            st = [st[m] * g_tot[q0 + m] + _dot_tn(uv[m], bk_end[q0 + m]) for m in nc]
            for m, (d, p) in enumerate(chains):
                refs[d][6][rows[u, d], lanes[p]] = fs[m][c:2 * c] + _dot(p_both[q0 + m], uv[m])
            yield
        for m, (d, p) in enumerate(chains):
            st_scr[d, p] = st[m]

    return stages


def _ssd_scan_body(xf_ref, dtf_ref, xb_ref, dtb_ref, a_ref, yf_ref, yb_ref, st_scr, n_chunks):
    c = SCAN_CHUNK
    hd = SSD_HEAD_DIM
    row1 = lax.broadcasted_iota(jnp.int32, (c, 1), 0)
    m0 = lax.broadcasted_iota(jnp.int32, (c, PAIR), 1) < hd
    before_eq = [_order_masks(d, c)[1] for d in range(2)]
    incl01 = [before_eq[d].astype(BF16) for d in range(2)]
    rsel = [row1 == (c - 1 if d == 0 else 0) for d in range(2)]
    refs = ((xf_ref, dtf_ref, yf_ref), (xb_ref, dtb_ref, yb_ref))
    chains = [(d, p) for d in range(2) for p in range(N_PAIRS)]
    n = range(len(chains))
    group = [p // (N_PAIRS // SSD_GROUPS) for _, p in chains]
    lanes = [slice(p * PAIR, (p + 1) * PAIR) for _, p in chains]

    def stages(j):
        rows = (pl.ds(pl.multiple_of(j * c, c), c),
                pl.ds(pl.multiple_of((n_chunks - 1 - j) * c, c), c))
        dt = [refs[d][1][rows[d], :] for d in range(2)]
        cs_all = [_dot01_left(incl01[d], dt[d] * a_ref[d]) for d in range(2)]
        bm = [[refs[d][0][rows[d], SSD_WIDTH + g * SSD_STATE:SSD_WIDTH + (g + 1) * SSD_STATE]
               for g in range(SSD_GROUPS)] for d in range(2)]
        cm = [[refs[d][0][rows[d], SSD_WIDTH + (SSD_GROUPS + g) * SSD_STATE:
                          SSD_WIDTH + (SSD_GROUPS + g + 1) * SSD_STATE]
               for g in range(SSD_GROUPS)] for d in range(2)]
        cb = [[_dot_nt(cm[d][g], bm[d][g]) for g in range(SSD_GROUPS)] for d in range(2)]
        yield
        st = [st_scr[d, p] for d, p in chains]
        y_st = [_dot(cm[chains[q][0]][group[q]], st[q].astype(BF16)) for q in n]
        cs = [cs_all[chains[q][0]][:, lanes[q]] for q in n]
        xdt = [refs[chains[q][0]][0][rows[chains[q][0]], lanes[q]].astype(F32) * dt[chains[q][0]][:, lanes[q]]
               for q in n]
        yield
        probs = []
        for q in n:
            d = chains[q][0]
            cs_t = cs[q].T
            both = []
            for hh in range(2):
                col = cs[q][:, hh * hd:hh * hd + 1]
                rowv = cs_t[hh * hd:hh * hd + 1, :]
                dec = jnp.exp(jnp.where(before_eq[d], col - rowv, -jnp.inf))
                both.append((cb[d][group[q]] * dec).astype(BF16))
            probs.append(jnp.concatenate(both, axis=1))
            yield
        xs2 = [jnp.concatenate([jnp.where(m0, xdt[q], 0.0), jnp.where(m0, 0.0, xdt[q])],
                               axis=0).astype(BF16) for q in n]
        y_in = [_dot(probs[q], xs2[q]) for q in n]
        yield
        for q in n:
            d = chains[q][0]
            refs[d][2][rows[d], lanes[q]] = y_in[q] + jnp.exp(cs[q]) * y_st[q]
        yield
        for q, (d, p) in enumerate(chains):
            cs_last = jnp.sum(jnp.where(rsel[d], cs[q], 0.0), axis=0, keepdims=True)
            xe = (xdt[q] * jnp.exp(cs_last - cs[q])).astype(BF16)
            st_scr[d, p] = st[q] * jnp.exp(cs_last) + _dot_tn(bm[d][group[q]], xe)

    return stages


def _even_scan_kernel(rf_ref, vf_ref, kkf_ref, ldf_ref, kdf_ref, bdf_ref,
                      rb_ref, vb_ref, kkb_ref, ldb_ref, kdb_ref, bdb_ref,
                      xf_ref, dtf_ref, xb_ref, dtb_ref, a_ref, s0r_ref, s0s_ref,
                      yrf_ref, yrb_ref, ysf_ref, ysb_ref, sfr_ref, sfs_ref, str_scr, sts_scr, *, n_steps):
    i = pl.program_id(1)

    @pl.when(i == 0)
    def _():
        str_scr[...] = s0r_ref[...]
        sts_scr[...] = s0s_ref[...]

    rwkv_stages = _rwkv_scan_body(rf_ref, vf_ref, kkf_ref, ldf_ref, kdf_ref, bdf_ref,
                                  rb_ref, vb_ref, kkb_ref, ldb_ref, kdb_ref, bdb_ref,
                                  yrf_ref, yrb_ref, str_scr, n_steps * RWKV_CHUNKS_PER_STEP)
    ssd_stages = _ssd_scan_body(xf_ref, dtf_ref, xb_ref, dtb_ref, a_ref, ysf_ref, ysb_ref, sts_scr, n_steps)

    def step(j, carry):
        live = [rwkv_stages(j), ssd_stages(j)]
        while live:
            for gen in list(live):
                if next(gen, StopIteration) is StopIteration:
                    live.remove(gen)
        return carry

    lax.fori_loop(0, n_steps, step, 0)

    @pl.when(i == pl.num_programs(1) - 1)
    def _():
        sfr_ref[...] = str_scr[...]
        sfs_ref[...] = sts_scr[...]


def _even_scan_call(feat, xbc, dtbc, a_rep, s0_rk, s0_sd, tb):
    r, v, kk, _, _, ld, kd, bd = feat
    b, t, w = r.shape
    tb = min(tb, t)
    nb = t // tb
    assert SCAN_CHUNK == RWKV_CHUNKS_PER_STEP * RWKV_CHUNK
    tok_f = pl.BlockSpec((None, tb, w), lambda bi, i: (bi, i, 0))
    tok_b = pl.BlockSpec((None, tb, w), lambda bi, i: (bi, nb - 1 - i, 0))
    dir_f = pl.BlockSpec((None, None, tb, w), lambda bi, i: (0, bi, i, 0))
    dir_b = pl.BlockSpec((None, None, tb, w), lambda bi, i: (1, bi, nb - 1 - i, 0))
    xbc_f = pl.BlockSpec((None, tb, SSD_XBC), lambda bi, i: (bi, i, 0))
    xbc_b = pl.BlockSpec((None, tb, SSD_XBC), lambda bi, i: (bi, nb - 1 - i, 0))
    st_r = pl.BlockSpec((2, None, N_PAIRS, PAIR, PAIR), lambda bi, i: (0, bi, 0, 0, 0))
    st_s = pl.BlockSpec((2, None, N_PAIRS, SSD_STATE, PAIR), lambda bi, i: (0, bi, 0, 0, 0))
    y_sd = jax.ShapeDtypeStruct((b, t, w), F32)
    return pl.pallas_call(
        functools.partial(_even_scan_kernel, n_steps=tb // SCAN_CHUNK),
        name="even_scan",
        grid=(b, nb),
        in_specs=[tok_f, tok_f, tok_f, dir_f, dir_f, dir_f, tok_b, tok_b, tok_b, dir_b, dir_b, dir_b,
                  xbc_f, dir_f, xbc_b, dir_b,
                  pl.BlockSpec((2, 1, SSD_WIDTH), lambda bi, i: (0, 0, 0)), st_r, st_s],
        out_specs=[tok_f, tok_b, tok_f, tok_b, st_r, st_s],
        out_shape=[y_sd, y_sd, y_sd, y_sd,
                   jax.ShapeDtypeStruct((2, b, N_PAIRS, PAIR, PAIR), F32),
                   jax.ShapeDtypeStruct((2, b, N_PAIRS, SSD_STATE, PAIR), F32)],
        scratch_shapes=[pltpu.VMEM((2, N_PAIRS, PAIR, PAIR), F32),
                        pltpu.VMEM((2, N_PAIRS, SSD_STATE, PAIR), F32)],
        compiler_params=_cparams(("parallel", "arbitrary")),
    )(r, v, kk, ld, kd, bd, r, v, kk, ld, kd, bd, xbc, dtbc, xbc, dtbc, a_rep, s0_rk, s0_sd)


def _even_finish_kernel(yrf_ref, yrb_ref, bonus_ref, gate_ref, ysf_ref, ysb_ref, xs_ref, z_ref, x_ref,
                        m_ref, lnw_ref, lnb_ref, dsk_ref, nw_ref, j_ref, wo_ref, o_ref):
    jm = j_ref[...]
    y = yrf_ref[...] + yrb_ref[...]
    inv_n = 1.0 / RWKV_HEAD_DIM
    mean = _head_sum(y, jm) * inv_n
    yc = y - mean
    var = _head_sum(yc * yc, jm) * inv_n
    y = yc * lax.rsqrt(var + RWKV_GN_EPS) * lnw_ref[...] + lnb_ref[...]
    y_rk = (y + bonus_ref[...].astype(F32)) * gate_ref[...].astype(F32)
    s = ysf_ref[...] + ysb_ref[...] + dsk_ref[...] * xs_ref[...].astype(F32)
    s = s * _silu(z_ref[...].astype(F32))
    s = s * lax.rsqrt(jnp.mean(s * s, axis=-1, keepdims=True) + NORM_EPS) * nw_ref[...]
    out = _dot(y_rk.astype(BF16), wo_ref[0:RWKV_WIDTH, :]) + \
        _dot(s.astype(BF16), wo_ref[RWKV_WIDTH:RWKV_WIDTH + SSD_WIDTH, :])
    o_ref[...] = x_ref[...] + m_ref[2:3, :] * out


def _even_finish_call(yrk, bonus, gate, ysd, xbc, proj, x, mods, p, tm):
    b, t, d = x.shape
    tm = min(tm, t)
    w = RWKV_WIDTH
    per_batch = mods.shape[0] > 1
    full = lambda shape: pl.BlockSpec(shape, lambda bi, i: (0,) * len(shape))
    tok = pl.BlockSpec((None, tm, w), lambda bi, i: (bi, i, 0))
    return pl.pallas_call(
        _even_finish_kernel,
        name="even_finish",
        grid=(b, t // tm),
        in_specs=[tok, tok, tok, tok, tok, tok,
                  pl.BlockSpec((None, tm, SSD_WIDTH), lambda bi, i: (bi, i, 0)),
                  pl.BlockSpec((None, tm, SSD_WIDTH), lambda bi, i: (bi, i, 0)),
                  pl.BlockSpec((None, tm, d), lambda bi, i: (bi, i, 0)),
                  pl.BlockSpec((None, N_MOD, d), (lambda bi, i: (bi, 0, 0)) if per_batch
                               else (lambda bi, i: (0, 0, 0))),
                  full((1, w)), full((1, w)), full((1, w)), full((1, w)),
                  full((V7X_MXU_DIM, V7X_MXU_DIM)),
                  full((2 * w, d))],
        out_specs=pl.BlockSpec((None, tm, d), lambda bi, i: (bi, i, 0)),
        out_shape=jax.ShapeDtypeStruct((b, t, d), F32),
        compiler_params=_cparams(("parallel", "parallel")),
    )(yrk[0], yrk[1], bonus, gate, ysd[0], ysd[1], xbc, proj, x, mods, p['ln_w'], p['ln_b'], p['d_skip'],
      p['norm_w'],
      p['head_sum'], p['w_out'])


def _ret_scan_kernel(*refs, n_chunks, mode):
    if mode == 'state':
        k_ref, v_ref, lg_ref, s0_ref, sf_ref, st_scr, sc_scr = refs
        d = pl.program_id(0)
    elif mode == 'bwd':
        q_ref, k_ref, v_ref, lg_ref, s0_ref, y_ref, st_scr, dec_scr, sc_scr = refs
        d = 1
    else:
        (q_ref, k_ref, v_ref, lg_ref, s0_ref, yb_ref, g_ref, x_ref, m_ref, wo_ref, o_ref,
         st_scr, dec_scr, sc_scr, ysum_scr) = refs
        d = 0
    i = pl.program_id(2)
    c = SCAN_CHUNK
    dk, dv = RET_QK_DIM, RET_V_DIM
    heads = range(RET_HEADS)
    lg_all = lg_ref[...]

    @pl.when(i == 0)
    def _():
        st_scr[...] = s0_ref[...]
        _, before_eq = _order_masks(d, c)
        row = lax.broadcasted_iota(jnp.int32, (c, c), 0)
        col = lax.broadcasted_iota(jnp.int32, (c, c), 1)
        rel = jnp.abs(row - col).astype(F32)
        pos = (row + d * (c - 1 - 2 * row)).astype(F32)
        for h in heads:
            lg = lg_all[:, h * dk:h * dk + 1]
            if mode != 'state':
                dec_scr[h] = jnp.where(before_eq, jnp.exp(rel * lg), 0.0)
                sc_scr[h, 0] = jnp.exp((pos + 1.0) * lg).astype(BF16)
            sc_scr[h, 1] = jnp.exp((c - 1.0 - pos) * lg).astype(BF16)

    def chunk_body(j, carry):
        cj = j + d * (n_chunks - 1 - 2 * j)
        rows = pl.ds(pl.multiple_of(cj * c, c), c)
        ks = [k_ref[rows, h * dk:(h + 1) * dk] for h in heads]
        vs = [v_ref[rows, h * dv:(h + 1) * dv] for h in heads]
        st = [st_scr[h] for h in heads]
        if mode != 'state':
            qs = [q_ref[rows, h * dk:(h + 1) * dk] for h in heads]
            qk = [_dot_nt(qs[h], ks[h]) for h in heads]
            scores = [(qk[h] * dec_scr[h]).astype(BF16) for h in heads]
            y_st = [_dot(qs[h] * sc_scr[h, 0], st[h].astype(BF16)) for h in heads]
            for h in heads:
                cols = slice(h * dv, (h + 1) * dv)
                y = _dot(scores[h], vs[h]) + y_st[h]
                if mode == 'bwd':
                    y_ref[rows, cols] = y.astype(y_ref.dtype)
                else:
                    ysum_scr[rows, cols] = y + yb_ref[rows, cols].astype(F32)
        for h in heads:
            lg = lg_all[:, h * dk:h * dk + 1]
            st_scr[h] = st[h] * jnp.exp(c * lg) + _dot_tn(ks[h] * sc_scr[h, 1], vs[h])
        return carry

    lax.fori_loop(0, n_chunks, chunk_body, 0, unroll=4)

    if mode == 'state':
        @pl.when(i == pl.num_programs(2) - 1)
        def _():
            sf_ref[...] = st_scr[...]
    elif mode == 'fwd':
        parts = []
        for h in heads:
            yh = ysum_scr[:, h * dv:(h + 1) * dv]
            parts.append(yh * lax.rsqrt(jnp.mean(yh * yh, axis=-1, keepdims=True) + NORM_EPS))
        act = (_silu(g_ref[...].astype(F32)) * jnp.concatenate(parts, axis=1)).astype(BF16)
        o_ref[...] = x_ref[...] + m_ref[2:3, :] * _dot(act, wo_ref[...])


def _ret_scan_call(proj, lg_rep, s0, tb, mode, finish=None):
    b, t, _ = proj.shape
    tb = min(tb, t)
    nb = t // tb
    n_dirs = 2 if mode == 'state' else 1
    d0 = 1 if mode == 'bwd' else 0
    blk = lambda dd, i: i + (dd + d0) * (nb - 1 - 2 * i)
    tok = lambda width, col: pl.BlockSpec((None, tb, width), lambda dd, bi, i: (bi, blk(dd, i), col))
    st = pl.BlockSpec((None, None, RET_HEADS, RET_QK_DIM, RET_V_DIM),
                      lambda dd, bi, i: (dd + d0, bi, 0, 0, 0))
    q_spec, k_spec, v_spec = tok(RET_QK, 0), tok(RET_QK, 1), tok(RET_V, 2 * RET_QK // RET_V)
    lg_spec = pl.BlockSpec((None, 1, RET_QK), lambda dd, bi, i: (dd + d0, 0, 0))
    scratch = [pltpu.VMEM((RET_HEADS, RET_QK_DIM, RET_V_DIM), F32)]
    if mode != 'state':
        scratch.append(pltpu.VMEM((RET_HEADS, SCAN_CHUNK, SCAN_CHUNK), F32))
    scratch.append(pltpu.VMEM((RET_HEADS, 2, SCAN_CHUNK, RET_QK_DIM), BF16))
    if mode == 'state':
        in_specs, args = [k_spec, v_spec, lg_spec, st], [proj, proj, lg_rep, s0]
        out_specs = st
        out_shape = jax.ShapeDtypeStruct((2, b, RET_HEADS, RET_QK_DIM, RET_V_DIM), F32)
    elif mode == 'bwd':
        in_specs, args = [q_spec, k_spec, v_spec, lg_spec, st], [proj, proj, proj, lg_rep, s0]
        out_specs = tok(RET_V, 0)
        out_shape = jax.ShapeDtypeStruct((b, t, RET_V), BF16)
    else:
        y_bwd, x, mods, w_out = finish
        d = x.shape[-1]
        in_specs = [q_spec, k_spec, v_spec, lg_spec, st, tok(RET_V, 0),
                    tok(RET_V, (2 * RET_QK + RET_V) // RET_V), tok(d, 0),
                    pl.BlockSpec((None, N_MOD, d), lambda dd, bi, i: (bi, 0, 0)),
                    pl.BlockSpec((RET_V, d), lambda dd, bi, i: (0, 0))]
        args = [proj, proj, proj, lg_rep, s0, y_bwd, proj, x, mods, w_out]
        out_specs = tok(d, 0)
        out_shape = jax.ShapeDtypeStruct((b, t, d), F32)
        scratch.append(pltpu.VMEM((tb, RET_V), F32))
    return pl.pallas_call(
        functools.partial(_ret_scan_kernel, n_chunks=tb // SCAN_CHUNK, mode=mode),
        name="ret_scan_" + mode,
        grid=(n_dirs, b, nb),
        in_specs=in_specs,
        out_specs=out_specs,
        out_shape=out_shape,
        scratch_shapes=scratch,
        compiler_params=_cparams(("parallel", "parallel", "arbitrary")),
    )(*args)


def _ffn_kernel(*refs, on_grid, final_norm):
    if on_grid:
        x_ref, xn_ref, g_ref, m_ref, wu_ref, cw_ref, cb_ref, wd_ref = refs[:8]
        rest = refs[8:]
    else:
        x_ref, g_ref, m_ref, wu_ref, cw_ref, cb_ref, wd_ref = refs[:7]
        rest = refs[7:]
    if final_norm:
        fg_ref, o_ref, *scrs = rest
    else:
        o_ref, *scrs = rest
    gate_scr, val_scr, *scrs = scrs
    if on_grid:
        gtop_scr, *act_scrs = scrs
    else:
        act_scrs = scrs
    i = pl.program_id(1)
    n_tiles = pl.num_programs(1)
    tm = x_ref.shape[0]

    def norm_mod(xv):
        hv = xv * lax.rsqrt(jnp.mean(xv * xv, axis=-1, keepdims=True) + NORM_EPS) * g_ref[...]
        return (hv * (1.0 + m_ref[4:5, :]) + m_ref[3:4, :]).astype(BF16)

    x = x_ref[...]
    h = norm_mod(x)
    row = lax.broadcasted_iota(jnp.int32, (tm, 1), 0)
    if on_grid:
        col = row % GRID_W
        ok_left = col > 0
        ok_right = col < GRID_W - 1
        zero = jnp.zeros((GRID_W, x.shape[1]), BF16)
        h_ext = jnp.concatenate([h, jnp.where(i < n_tiles - 1, norm_mod(xn_ref[...]), zero)], axis=0)

        @pl.when(i == 0)
        def _():
            gtop_scr[...] = jnp.zeros_like(gtop_scr)
    else:
        ok_left = row > 0
        ok_right = row < tm - 1
    n_chunks = D_FF // FFN_COL_CHUNK

    def up_proj(j):
        cols = slice(j * FFN_COL_CHUNK, (j + 1) * FFN_COL_CHUNK)
        vcols = slice(D_FF + j * FFN_COL_CHUNK, D_FF + (j + 1) * FFN_COL_CHUNK)
        if on_grid:
            gate_scr[j % n_buf, GRID_W:, :] = _dot(h_ext, wu_ref[:, cols])
        else:
            gate_scr[j % n_buf] = _dot(h, wu_ref[:, cols])
        val_scr[j % n_buf] = _dot(h, wu_ref[:, vcols])

    out = None
    n_buf = gate_scr.shape[0]
    for j in range(n_buf - 1):
        up_proj(j)
    for j in range(n_chunks):
        cols = slice(j * FFN_COL_CHUNK, (j + 1) * FFN_COL_CHUNK)
        buf = j % n_buf
        if j + n_buf - 1 < n_chunks:
            up_proj(j + n_buf - 1)
        grp, slot = divmod(j, FFN_DOWN_GROUP)
        act_scr = act_scrs[grp]
        acols = slice(slot * FFN_COL_CHUNK, (slot + 1) * FFN_COL_CHUNK)
        if on_grid:
            gate_scr[buf, 0:GRID_W, :] = gtop_scr[:, cols]
            gtop_scr[:, cols] = gate_scr[buf, tm:tm + GRID_W, :]
            rows3 = [gate_scr[buf, dr * GRID_W:dr * GRID_W + tm, :] for dr in range(3)]
            taps = [rows3[0] * cw_ref[dc:dc + 1, cols] + rows3[1] * cw_ref[3 + dc:4 + dc, cols]
                    + rows3[2] * cw_ref[6 + dc:7 + dc, cols] for dc in range(3)]
        else:
            gate = gate_scr[buf]
            taps = [gate * cw_ref[3 + dc:4 + dc, cols] for dc in range(3)]
        acc = cb_ref[:, cols] + taps[1] + jnp.where(ok_left, pltpu.roll(taps[0], 1, axis=0), 0.0) \
            + jnp.where(ok_right, pltpu.roll(taps[2], tm - 1, axis=0), 0.0)
        act_scr[:, acols] = (_gelu_tanh(acc) * val_scr[buf]).astype(BF16)
        if slot + 1 == FFN_DOWN_GROUP or j + 1 == n_chunks:
            width = (slot + 1) * FFN_COL_CHUNK
            k0 = grp * FFN_DOWN_GROUP * FFN_COL_CHUNK
            part = _dot(act_scr[:, 0:width], wd_ref[k0:k0 + width, :])
            out = part if out is None else out + part
    out = x + m_ref[5:6, :] * out
    if final_norm:
        out = out * lax.rsqrt(jnp.mean(out * out, axis=-1, keepdims=True) + NORM_EPS) * fg_ref[...]
    o_ref[...] = out


def _ffn_call(x, norm_g, mods, w_up, conv_w9, conv_b, w_down, *, tm, on_grid, final_g=None):
    b, t, d = x.shape
    tm = min(tm, t)
    n_tiles = t // tm
    per_batch = mods.shape[0] > 1
    full = lambda shape: pl.BlockSpec(shape, lambda bi, i: (0,) * len(shape))
    in_specs = [pl.BlockSpec((None, tm, d), lambda bi, i: (bi, i, 0))]
    args = [x]
    scratch = [pltpu.VMEM((tm, FFN_DOWN_GROUP * FFN_COL_CHUNK), BF16)
               for _ in range(-(-D_FF // (FFN_DOWN_GROUP * FFN_COL_CHUNK)))]
    if on_grid:
        r = tm // GRID_W
        last = t // GRID_W - 1
        in_specs.append(pl.BlockSpec((None, GRID_W, d), lambda bi, i: (bi, jnp.minimum((i + 1) * r, last), 0)))
        args.append(x)
        scratch.insert(0, pltpu.VMEM((GRID_W, D_FF), F32))
    else:
        assert n_tiles == 1
    gate_rows = tm + 2 * GRID_W if on_grid else tm
    scratch = [pltpu.VMEM((FFN_UP_BUFFERS, gate_rows, FFN_COL_CHUNK), F32),
               pltpu.VMEM((FFN_UP_BUFFERS, tm, FFN_COL_CHUNK), F32)] + scratch
    in_specs += [full((1, d)),
                 pl.BlockSpec((None, N_MOD, d), (lambda bi, i: (bi, 0, 0)) if per_batch
                              else (lambda bi, i: (0, 0, 0))),
                 full((d, 2 * D_FF)), full((9, D_FF)), full((1, D_FF)), full((D_FF, d))]
    args += [norm_g, mods, w_up, conv_w9, conv_b, w_down]
    if final_g is not None:
        in_specs.append(full((1, d)))
        args.append(final_g)
    return pl.pallas_call(
        functools.partial(_ffn_kernel, on_grid=on_grid, final_norm=final_g is not None),
        name="conv_ffn",
        grid=(b, n_tiles),
        in_specs=in_specs,
        out_specs=pl.BlockSpec((None, tm, d), lambda bi, i: (bi, i, 0)),
        out_shape=jax.ShapeDtypeStruct((b, t, d), F32),
        scratch_shapes=scratch,
        compiler_params=_cparams(("parallel", "arbitrary")),
    )(*args)


def _block_diag2(a, b):
    za = jnp.zeros((a.shape[0], b.shape[1]), a.dtype)
    zb = jnp.zeros((b.shape[0], a.shape[1]), a.dtype)
    return jnp.concatenate([jnp.concatenate([a, za], axis=1), jnp.concatenate([zb, b], axis=1)], axis=0)


def _pad_cols(a, n, fill=0.0):
    return jnp.pad(a, ((0, 0), (0, n - a.shape[1])), constant_values=fill)


def _even_params(j, ev_w_in, ev_mu_prev, ev_mu_next, rk_w0_f, rk_w0_b, rk_w2_f, rk_w2_b, rk_a0_f,
                 rk_a0_b, rk_a2_f, rk_a2_b, rk_g2, rk_k_k, rk_k_a, rk_r_k, rk_ln_w, rk_ln_b,
                 ssd_conv_w, ssd_conv_b, ssd_dt_bias_f, ssd_dt_bias_b, ssd_a_log_f, ssd_a_log_b,
                 ssd_d, ssd_norm_w, ev_w_out):
    w_in = ev_w_in[j]
    rw = w_in[:, :RWKV_COLS]
    z = w_in[:, RWKV_COLS:RWKV_COLS + SSD_WIDTH]
    xbc = w_in[:, RWKV_COLS + SSD_WIDTH:RWKV_COLS + SSD_WIDTH + SSD_XBC]
    dts = w_in[:, RWKV_COLS + SSD_WIDTH + SSD_XBC:]
    w_packed = jnp.concatenate([_pad_cols(jnp.concatenate([rw, dts], axis=1), EV_RW_BLOCK), xbc, z], axis=1)
    head = jnp.arange(V7X_MXU_DIM) // RWKV_HEAD_DIM
    head_sum = (head[:, None] == head[None, :]).astype(BF16)
    lane = jnp.arange(V7X_LANES)[:, None]
    tgt = jnp.arange(2 * SSD_WIDTH)[None, :]
    dt_expand = (lane == (tgt // SSD_WIDTH) * SSD_HEADS + (tgt % SSD_WIDTH) // SSD_HEAD_DIM).astype(BF16)
    rep = lambda a: jnp.repeat(a, SSD_HEAD_DIM)[None, :]
    row = lambda a: a[None, :]
    return {
        'w_in': w_packed.astype(BF16),
        'mu_self': _pad_cols(row(1.0 - ev_mu_prev[j] - ev_mu_next[j]), EV_RW_BLOCK, 1.0),
        'mu_prev': _pad_cols(row(ev_mu_prev[j]), EV_RW_BLOCK),
        'mu_next': _pad_cols(row(ev_mu_next[j]), EV_RW_BLOCK),
        'w0': row(jnp.concatenate([rk_w0_f[j], rk_w0_b[j]])),
        'w2': _block_diag2(rk_w2_f[j], rk_w2_b[j]).astype(BF16),
        'a0': row(jnp.concatenate([rk_a0_f[j], rk_a0_b[j]])),
        'a2': _block_diag2(rk_a2_f[j], rk_a2_b[j]).astype(BF16),
        'g2': rk_g2[j].astype(BF16),
        'k_k': row(rk_k_k[j]), 'k_a': row(rk_k_a[j]), 'r_k': row(rk_r_k[j].reshape(-1)),
        'ln_w': row(rk_ln_w[j]), 'ln_b': row(rk_ln_b[j]),
        'head_sum': head_sum,
        'conv_w': ssd_conv_w[j], 'conv_b': row(ssd_conv_b[j]),
        'dt_bias': _pad_cols(row(jnp.concatenate([ssd_dt_bias_f[j], ssd_dt_bias_b[j]])), V7X_LANES),
        'dt_expand': dt_expand,
        'a_rep': jnp.stack([rep(-jnp.exp(ssd_a_log_f[j])), rep(-jnp.exp(ssd_a_log_b[j]))]),
        'd_skip': rep(ssd_d[j]),
        'norm_w': row(ssd_norm_w[j]),
        'w_out': ev_w_out[j].astype(BF16),
    }


def _rope_tables(t):
    n = RET_QK_DIM // 4
    pos = jnp.arange(t)
    row = (pos // GRID_W).astype(F32)
    col = (pos % GRID_W).astype(F32)
    inv = ROPE_BASE ** (-jnp.arange(n, dtype=F32) / n)
    ang = jnp.concatenate([row[:, None] * inv, col[:, None] * inv], axis=-1)
    cos, sin = jnp.cos(ang), jnp.sin(ang)
    return jnp.concatenate([cos, cos], axis=-1), jnp.concatenate([-sin, sin], axis=-1)


def _conv_ffn(x, mods, norm_g, w_up, conv_w9, conv_b, w_down, *, on_grid, final_g=None):
    return _ffn_call(x, norm_g, mods, w_up, conv_w9, conv_b, w_down, tm=512, on_grid=on_grid,
                     final_g=final_g)


def _even_layer(x, ctx, mods_x, mods_c, norm_g, p):
    b = x.shape[0]

    def features(h, mods):
        *feat, xbc, dtbc, z = _even_feat_call(h, norm_g, mods, p, 512)
        return z, feat, xbc, dtbc

    proj_c, feat_c, xbc_c, dt_c = features(ctx, mods_c)
    proj_x, feat_x, xbc_x, dt_x = features(x, mods_x)
    s0 = jnp.zeros((2, b, N_PAIRS, PAIR, PAIR), F32)
    h0 = jnp.zeros((2, b, N_PAIRS, SSD_STATE, PAIR), F32)
    *y_c, s_ctx, h_ctx = _even_scan_call(feat_c, xbc_c, dt_c, p['a_rep'], s0, h0, 256)
    *y_x, _, _ = _even_scan_call(feat_x, xbc_x, dt_x, p['a_rep'], s_ctx, h_ctx, 512)
    yrk_c, ysd_c = y_c[0:2], y_c[2:4]
    yrk_x, ysd_x = y_x[0:2], y_x[2:4]
    x = _even_finish_call(yrk_x, feat_x[3], feat_x[4], ysd_x, xbc_x, proj_x, x, mods_x, p, 1024)
    ctx = _even_finish_call(yrk_c, feat_c[3], feat_c[4], ysd_c, xbc_c, proj_c, ctx, mods_c, p, 256)
    return x, ctx


def _odd_layer(x, ctx, mods_x, mods_c, norm_g, w_in, lg_rep, w_out):
    b, t, _ = x.shape
    proj_c = _nm_call(ctx, norm_g, mods_c, w_in, shift_row=0, tm=256, tn=512, out_dtype=BF16,
                      qk_mode='scale')
    proj_x = _nm_call(x, norm_g, mods_x, w_in, shift_row=0, tm=512, tn=512, out_dtype=BF16,
                      qk_mode='rope', rope=_rope_tables(t))
    s0 = jnp.zeros((2, b, RET_HEADS, RET_QK_DIM, RET_V_DIM), F32)
    s_ctx = _ret_scan_call(proj_c, lg_rep, s0, 256, 'state')
    y_bwd = _ret_scan_call(proj_x, lg_rep, s_ctx, 512, 'bwd')
    return _ret_scan_call(proj_x, lg_rep, s_ctx, 512, 'fwd', finish=(y_bwd, x, mods_x, w_out))


def kernel(x, c, ctx, c_ctx, mod_w, mod_b, norm1_g, norm2_g, ffn_w_up, ffn_conv_w, ffn_conv_b, ffn_w_down, ev_w_in, ev_mu_prev, ev_mu_next, rk_w0_f, rk_w0_b, rk_w2_f, rk_w2_b, rk_a0_f, rk_a0_b, rk_a2_f, rk_a2_b, rk_g2, rk_k_k, rk_k_a, rk_r_k, rk_ln_w, rk_ln_b, ssd_conv_w, ssd_conv_b, ssd_dt_bias_f, ssd_dt_bias_b, ssd_a_log_f, ssd_a_log_b, ssd_d, ssd_norm_w, ev_w_out, ret_w_in, ret_log2_f, ret_log2_b, ret_w_out, final_norm_g):
    b, t, d = x.shape
    depth = mod_w.shape[0]
    rows = -(-(b + 1) // V7X_SUBLANES) * V7X_SUBLANES
    cond = jnp.concatenate([c, c_ctx[None, :], jnp.zeros((rows - b - 1, d), F32)], axis=0)
    mods = _mod_call(cond, mod_w, mod_b).reshape(depth, rows, N_MOD, d)
    for i in range(depth):
        need_ctx = i < depth - 1
        mods_x = mods[i, :b]
        mods_c = mods[i, b:b + 1]
        j = i // 2
        g1 = norm1_g[i][None, :]
        if i % 2 == 0:
            p = _even_params(j, ev_w_in, ev_mu_prev, ev_mu_next, rk_w0_f, rk_w0_b, rk_w2_f, rk_w2_b,
                             rk_a0_f, rk_a0_b, rk_a2_f, rk_a2_b, rk_g2, rk_k_k, rk_k_a, rk_r_k,
                             rk_ln_w, rk_ln_b, ssd_conv_w, ssd_conv_b, ssd_dt_bias_f, ssd_dt_bias_b,
                             ssd_a_log_f, ssd_a_log_b, ssd_d, ssd_norm_w, ev_w_out)
            x, ctx_mixed = _even_layer(x, ctx, mods_x, mods_c, g1, p)
        else:
            lg = jnp.stack([jnp.log1p(-jnp.exp2(-ret_log2_f[j])), jnp.log1p(-jnp.exp2(-ret_log2_b[j]))])
            lg_rep = jnp.repeat(lg, RET_QK_DIM, axis=-1)[:, None, :]
            x = _odd_layer(x, ctx, mods_x, mods_c, g1, ret_w_in[j].astype(BF16), lg_rep,
                           ret_w_out[j].astype(BF16))
            ctx_mixed = None
        g2 = norm2_g[i][None, :]
        w_up = ffn_w_up[i].astype(BF16)
        w_down = ffn_w_down[i].astype(BF16)
        conv_w9 = ffn_conv_w[i].reshape(9, D_FF)
        conv_b = ffn_conv_b[i][None, :]
        last = i == depth - 1
        x = _conv_ffn(x, mods_x, g2, w_up, conv_w9, conv_b, w_down, on_grid=True,
                      final_g=final_norm_g[None, :] if last else None)
        if need_ctx:
            ctx = _conv_ffn(ctx_mixed, mods_c, g2, w_up, conv_w9, conv_b, w_down, on_grid=False)
    return x
```

```python
import functools
import math

import jax
import jax.numpy as jnp
from jax import lax
from jax.experimental import pallas as pl
from jax.experimental.pallas import tpu as pltpu

F32 = jnp.float32
BF16 = jnp.bfloat16

GRID_W = 64
N_MOD = 6
NORM_EPS = 1e-6
RWKV_HEADS = 8
RWKV_HEAD_DIM = 64
RWKV_WIDTH = RWKV_HEADS * RWKV_HEAD_DIM
DECAY_LORA = 64
ICLR_LORA = 64
GATE_LORA = 128
RWKV_GN_EPS = 64e-5
RWKV_COLS = 3 * RWKV_WIDTH + 2 * DECAY_LORA + 2 * ICLR_LORA + GATE_LORA
SSD_HEADS = 8
SSD_HEAD_DIM = 64
SSD_WIDTH = SSD_HEADS * SSD_HEAD_DIM
SSD_GROUPS = 2
SSD_STATE = 128
SSD_XBC = SSD_WIDTH + 2 * SSD_GROUPS * SSD_STATE
RET_HEADS = 8
RET_QK_DIM = 128
RET_V_DIM = 256
RET_QK = RET_HEADS * RET_QK_DIM
RET_V = RET_HEADS * RET_V_DIM
ROPE_BASE = 10000.0
D_FF = 2816

V7X_LANES = 128
V7X_SUBLANES = 8
V7X_MXU_DIM = 256
V7X_VMEM_LIMIT_BYTES = 56 * 1024 * 1024

RWKV_CHUNK = 64
RWKV_CHUNKS_PER_STEP = 2
SCAN_CHUNK = 128
FFN_COL_CHUNK = 256
FFN_DOWN_GROUP = 6
FFN_UP_BUFFERS = 2
PAIR = 2 * RWKV_HEAD_DIM
N_PAIRS = RWKV_HEADS // 2
EV_RW_BLOCK = 2048
EV_DT_OFF = RWKV_COLS
EV_XBC_OFF = EV_RW_BLOCK
EV_Z_OFF = EV_RW_BLOCK + SSD_XBC
EV_COLS = EV_Z_OFF + SSD_WIDTH
EV_PROJ_CHUNK = 1024


def _cparams(sem):
    return pltpu.CompilerParams(dimension_semantics=sem, vmem_limit_bytes=V7X_VMEM_LIMIT_BYTES)


def _split3(x):
    hi = x.astype(BF16)
    r1 = x - hi.astype(F32)
    mid = r1.astype(BF16)
    lo = (r1 - mid.astype(F32)).astype(BF16)
    return hi, mid, lo


def _dot(a, b):
    return jnp.dot(a, b, preferred_element_type=F32)


def _dot_nt(a, b):
    return lax.dot_general(a, b, (((1,), (1,)), ((), ())), preferred_element_type=F32)


def _dot_tn(a, b):
    return lax.dot_general(a, b, (((0,), (0,)), ((), ())), preferred_element_type=F32)


def _dot01(x, m01):
    hi, mid, lo = _split3(x)
    return _dot(hi, m01) + _dot(mid, m01) + _dot(lo, m01)


def _head_sum(x, j01):
    hi = x.astype(BF16)
    lo = (x - hi.astype(F32)).astype(BF16)
    n = j01.shape[0]
    return jnp.concatenate([_dot(hi[:, g * n:(g + 1) * n], j01) + _dot(lo[:, g * n:(g + 1) * n], j01)
                            for g in range(x.shape[1] // n)], axis=1)


def _dot01_left(m01, x):
    hi, mid, lo = _split3(x)
    return _dot(m01, hi) + _dot(m01, mid) + _dot(m01, lo)


def _sigmoid(x):
    return 0.5 * jnp.tanh(0.5 * x) + 0.5


def _silu(x):
    half = 0.5 * x
    return half + half * jnp.tanh(half)


def _softplus(x):
    return jnp.maximum(x, 0.0) + jnp.log1p(jnp.exp(-jnp.abs(x)))


def _gelu_tanh(x):
    c = math.sqrt(2.0 / math.pi)
    half = 0.5 * x
    return half + half * jnp.tanh(x * (c + (0.044715 * c) * (x * x)))


def _order_masks(d, n):
    row = lax.broadcasted_iota(jnp.int32, (n, n), 0)
    col = lax.broadcasted_iota(jnp.int32, (n, n), 1)
    diff = (row - col) * (1 - 2 * d)
    return diff > 0, diff >= 0


def _mod_kernel(c_ref, w_ref, b_ref, o_ref):
    h = _silu(c_ref[...])
    hi, mid, lo = _split3(h)
    w = w_ref[...]
    wh = w.astype(BF16)
    wl = (w - wh.astype(F32)).astype(BF16)
    acc = _dot(hi, wh) + _dot(mid, wh) + _dot(hi, wl)
    o_ref[...] = acc + b_ref[...]


def _mod_call(cond, mod_w, mod_b):
    depth, d, n = mod_w.shape
    rows = cond.shape[0]
    tn = 1024
    return pl.pallas_call(
        _mod_kernel,
        name="adaln_mod",
        grid=(depth, n // tn),
        in_specs=[pl.BlockSpec((rows, d), lambda l, j: (0, 0)),
                  pl.BlockSpec((None, d, tn), lambda l, j: (l, 0, j)),
                  pl.BlockSpec((None, 1, tn), lambda l, j: (l, 0, j))],
        out_specs=pl.BlockSpec((None, rows, tn), lambda l, j: (l, 0, j)),
        out_shape=jax.ShapeDtypeStruct((depth, rows, n), F32),
        compiler_params=_cparams(("parallel", "parallel")),
    )(cond, mod_w, mod_b.reshape(depth, 1, n))


def _nm_kernel(*refs, shift_row, tn, qk_mode):
    if qk_mode == 'rope':
        x_ref, g_ref, m_ref, w_ref, cos_ref, sin_ref, o_ref = refs
    else:
        x_ref, g_ref, m_ref, w_ref, o_ref = refs
    x = x_ref[...]
    h = x * lax.rsqrt(jnp.mean(x * x, axis=-1, keepdims=True) + NORM_EPS) * g_ref[...]
    h = h * (1.0 + m_ref[shift_row + 1:shift_row + 2, :]) + m_ref[shift_row:shift_row + 1, :]
    h = h.astype(BF16)
    dk = RET_QK_DIM
    for j in range(w_ref.shape[1] // tn):
        cols = slice(j * tn, (j + 1) * tn)
        y = _dot(h, w_ref[:, cols])
        if qk_mode is not None and j * tn < 2 * RET_QK:
            scale = dk ** -0.5 if j * tn >= RET_QK else 1.0
            heads = []
            for hh in range(tn // dk):
                yh = y[:, hh * dk:(hh + 1) * dk]
                if qk_mode == 'rope':
                    yh = yh * cos_ref[...] + pltpu.roll(yh, dk // 2, axis=1) * sin_ref[...]
                heads.append(yh * scale if scale != 1.0 else yh)
            y = jnp.concatenate(heads, axis=1)
        o_ref[:, cols] = y.astype(o_ref.dtype)


def _nm_call(x, g, mods, w, *, shift_row, tm, tn, out_dtype=F32, qk_mode=None, rope=None):
    b, t, d = x.shape
    n = w.shape[1]
    tm = min(tm, t)
    per_batch = mods.shape[0] > 1
    in_specs = [pl.BlockSpec((None, tm, d), lambda bi, i: (bi, i, 0)),
                pl.BlockSpec((1, d), lambda bi, i: (0, 0)),
                pl.BlockSpec((None, N_MOD, d), (lambda bi, i: (bi, 0, 0)) if per_batch
                             else (lambda bi, i: (0, 0, 0))),
                pl.BlockSpec((d, n), lambda bi, i: (0, 0))]
    args = [x, g, mods, w]
    if qk_mode == 'rope':
        tab = pl.BlockSpec((tm, RET_QK_DIM), lambda bi, i: (i, 0))
        in_specs += [tab, tab]
        args += list(rope)
    return pl.pallas_call(
        functools.partial(_nm_kernel, shift_row=shift_row, tn=tn, qk_mode=qk_mode),
        name="norm_mod_matmul",
        grid=(b, t // tm),
        in_specs=in_specs,
        out_specs=pl.BlockSpec((None, tm, n), lambda bi, i: (bi, i, 0)),
        out_shape=jax.ShapeDtypeStruct((b, t, n), out_dtype),
        compiler_params=_cparams(("parallel", "parallel")),
    )(*args)


def _even_feat_kernel(x_ref, xp_ref, xn_ref, g_ref, m_ref, w_ref,
                      mus_ref, mup_ref, mun_ref, w0_ref, w2_ref, a0_ref, a2_ref,
                      g2_ref, kk_ref, ka_ref, rk_ref, j_ref, cw_ref, cb_ref, dtb_ref, e_ref,
                      r_ref, v_ref, kkn_ref, bonus_ref, gate_ref, ld_ref, kd_ref, bd_ref,
                      xbc_ref, dtbc_ref, z_ref):
    i = pl.program_id(1)
    n_tiles = pl.num_programs(1)
    tm = x_ref.shape[0]
    halo = V7X_SUBLANES
    ext = tm + 2 * halo

    def norm_mod(xv):
        hv = xv * lax.rsqrt(jnp.mean(xv * xv, axis=-1, keepdims=True) + NORM_EPS) * g_ref[...]
        return (hv * (1.0 + m_ref[1:2, :]) + m_ref[0:1, :]).astype(BF16)

    h = norm_mod(x_ref[...])
    zero = jnp.zeros((halo, x_ref.shape[1]), BF16)
    h_ext = jnp.concatenate([jnp.where(i > 0, norm_mod(xp_ref[...]), zero), h,
                             jnp.where(i < n_tiles - 1, norm_mod(xn_ref[...]), zero)], axis=0)

    def proj3(cols):
        ye = _dot(h_ext, w_ref[:, cols])
        return (ye[halo:halo + tm], pltpu.roll(ye, 1, axis=0)[halo:halo + tm],
                pltpu.roll(ye, ext - 1, axis=0)[halo:halo + tm])

    w = RWKV_WIDTH
    rw = []
    pc = EV_PROJ_CHUNK
    for j in range(EV_RW_BLOCK // pc):
        cols = slice(j * pc, (j + 1) * pc)
        cur, prev, nxt = proj3(cols)
        rw.append(cur * mus_ref[:, cols] + prev * mup_ref[:, cols] + nxt * mun_ref[:, cols])
    rw = jnp.concatenate(rw, axis=1)
    r, k, v, lora = (rw[:, m * w:(m + 1) * w] for m in range(EV_RW_BLOCK // w))
    wd = lora[:, 0:2 * DECAY_LORA]
    ad = lora[:, 2 * DECAY_LORA:2 * DECAY_LORA + 2 * ICLR_LORA]
    gd = lora[:, 2 * DECAY_LORA + 2 * ICLR_LORA:2 * DECAY_LORA + 2 * ICLR_LORA + GATE_LORA]
    dt_raw = lora[:, EV_DT_OFF - 3 * w:EV_RW_BLOCK - 3 * w]
    jm = j_ref[...]

    for j in range(SSD_XBC // pc):
        cols = slice(j * pc, (j + 1) * pc)
        cur, prev, nxt = proj3(slice(EV_XBC_OFF + j * pc, EV_XBC_OFF + (j + 1) * pc))
        y = prev * cw_ref[0:1, cols] + cur * cw_ref[1:2, cols] + nxt * cw_ref[2:3, cols] + cb_ref[:, cols]
        xbc_ref[:, cols] = _silu(y).astype(xbc_ref.dtype)
    z_ref[...] = _dot(h, w_ref[:, EV_Z_OFF:EV_Z_OFF + SSD_WIDTH]).astype(z_ref.dtype)
    dt = _dot01(_softplus(dt_raw + dtb_ref[...]), e_ref[...])
    dtbc_ref[0] = dt[:, 0:SSD_WIDTH]
    dtbc_ref[1] = dt[:, SSD_WIDTH:2 * SSD_WIDTH]

    kk = k * kk_ref[...]
    ss = _head_sum(kk * kk, jm)
    kk = kk / jnp.maximum(jnp.sqrt(ss), 1e-12)
    r_ref[...] = r.astype(r_ref.dtype)
    v_ref[...] = v.astype(v_ref.dtype)
    kkn_ref[...] = kk.astype(kkn_ref.dtype)
    bonus_ref[...] = (_head_sum(r * k * rk_ref[...], jm) * v).astype(bonus_ref.dtype)
    gate_ref[...] = _dot(_sigmoid(gd).astype(BF16), g2_ref[...]).astype(gate_ref.dtype)

    zw = _dot(jnp.tanh(wd).astype(BF16), w2_ref[...]) + w0_ref[...]
    za = _dot(ad.astype(BF16), a2_ref[...]) + a0_ref[...]
    for di in range(2):
        ld_ref[di] = -math.exp(-0.5) * _sigmoid(zw[:, di * w:(di + 1) * w])
        iclr = _sigmoid(za[:, di * w:(di + 1) * w])
        kd_ref[di] = (k * (1.0 + (iclr - 1.0) * ka_ref[...])).astype(kd_ref.dtype)
        bd_ref[di] = (kk * iclr).astype(bd_ref.dtype)


def _even_feat_call(x, norm_g, mods, p, tm):
    b, t, d = x.shape
    tm = min(tm, t)
    n_tiles = t // tm
    w = RWKV_WIDTH
    per_batch = mods.shape[0] > 1
    r8 = tm // V7X_SUBLANES
    last = t // V7X_SUBLANES - 1
    full = lambda shape: pl.BlockSpec(shape, lambda bi, i: (0,) * len(shape))
    tok = pl.BlockSpec((None, tm, w), lambda bi, i: (bi, i, 0))
    tok2 = pl.BlockSpec((2, None, tm, w), lambda bi, i: (0, bi, i, 0))
    sd = jax.ShapeDtypeStruct((b, t, w), BF16)
    sd2 = jax.ShapeDtypeStruct((2, b, t, w), BF16)
    f2 = jax.ShapeDtypeStruct((2, b, t, w), F32)
    return pl.pallas_call(
        _even_feat_kernel,
        name="even_feat",
        grid=(b, n_tiles),
        in_specs=[pl.BlockSpec((None, tm, d), lambda bi, i: (bi, i, 0)),
                  pl.BlockSpec((None, V7X_SUBLANES, d), lambda bi, i: (bi, jnp.maximum(i * r8 - 1, 0), 0)),
                  pl.BlockSpec((None, V7X_SUBLANES, d), lambda bi, i: (bi, jnp.minimum((i + 1) * r8, last), 0)),
                  full((1, d)),
                  pl.BlockSpec((None, N_MOD, d), (lambda bi, i: (bi, 0, 0)) if per_batch
                               else (lambda bi, i: (0, 0, 0))),
                  full((d, EV_COLS)),
                  full((1, EV_RW_BLOCK)), full((1, EV_RW_BLOCK)), full((1, EV_RW_BLOCK)),
                  full((1, 2 * w)), full((2 * DECAY_LORA, 2 * w)),
                  full((1, 2 * w)), full((2 * ICLR_LORA, 2 * w)),
                  full((GATE_LORA, w)), full((1, w)), full((1, w)), full((1, w)),
                  full((V7X_MXU_DIM, V7X_MXU_DIM)),
                  full((3, SSD_XBC)), full((1, SSD_XBC)), full((1, V7X_LANES)),
                  full((V7X_LANES, 2 * SSD_WIDTH))],
        out_specs=[tok, tok, tok, tok, tok, tok2, tok2, tok2,
                   pl.BlockSpec((None, tm, SSD_XBC), lambda bi, i: (bi, i, 0)), tok2, tok],
        out_shape=[sd, sd, sd, sd, sd, f2, sd2, sd2,
                   jax.ShapeDtypeStruct((b, t, SSD_XBC), BF16), f2, sd],
        compiler_params=_cparams(("parallel", "parallel")),
    )(x, x, x, norm_g, mods, p['w_in'], p['mu_self'], p['mu_prev'], p['mu_next'], p['w0'], p['w2'],
      p['a0'], p['a2'], p['g2'], p['k_k'], p['k_a'], p['r_k'], p['head_sum'],
      p['conv_w'], p['conv_b'], p['dt_bias'], p['dt_expand'])


def _rwkv_scan_body(rf_ref, vf_ref, kkf_ref, ldf_ref, kdf_ref, bdf_ref,
                    rb_ref, vb_ref, kkb_ref, ldb_ref, kdb_ref, bdb_ref, yf_ref, yb_ref, st_scr, n_chunks):
    c = RWKV_CHUNK
    rowc = lax.broadcasted_iota(jnp.int32, (c, 2 * c), 0)
    colc = lax.broadcasted_iota(jnp.int32, (c, 2 * c), 1) % c
    eye_wide = (rowc == colc).astype(F32)
    m0 = lax.broadcasted_iota(jnp.int32, (c, PAIR), 1) < RWKV_HEAD_DIM
    row1 = lax.broadcasted_iota(jnp.int32, (c, 1), 0)
    incl01, strict_wide, incl_wide, rsel = [], [], [], []
    for d in range(2):
        sgn = 1 - 2 * d
        incl01.append(_order_masks(d, c)[1].astype(BF16))
        strict_wide.append((rowc - colc) * sgn > 0)
        incl_wide.append((rowc - colc) * sgn >= 0)
        rsel.append(row1 == (c - 1 if d == 0 else 0))
    refs = ((rf_ref, vf_ref, kkf_ref, ldf_ref, kdf_ref, bdf_ref, yf_ref),
            (rb_ref, vb_ref, kkb_ref, ldb_ref, kdb_ref, bdb_ref, yb_ref))
    chains = [(d, p) for d in range(2) for p in range(N_PAIRS)]
    lanes = [slice(p * PAIR, (p + 1) * PAIR) for p in range(N_PAIRS)]

    def stack(x):
        xb = x.astype(BF16)
        zero = jnp.zeros_like(xb)
        return jnp.concatenate([jnp.where(m0, xb, zero), jnp.where(m0, zero, xb)], axis=0)

    sub = range(RWKV_CHUNKS_PER_STEP)
    items = [(u, d, p) for u in sub for d, p in chains]

    def stages(j):
        def rows_of(u, d):
            cj = j * len(sub) + u
            return pl.ds(pl.multiple_of((cj if d == 0 else n_chunks - 1 - cj) * c, c), c)

        rows = {(u, d): rows_of(u, d) for u in sub for d in range(2)}
        cs_all = {ud: _dot01_left(incl01[ud[1]], refs[ud[1]][3][rows[ud], :]) for ud in rows}
        yield
        a_s, r_t, b_s, k_s, v_s, b_end, k_end, g_tot = [], [], [], [], [], [], [], []
        for u, d, p in items:
            r_ref, v_ref, kk_ref, ld_ref, kd_ref, bd_ref, _ = refs[d]
            rw, ln = rows[u, d], lanes[p]
            cs = cs_all[u, d][:, ln]
            cs_last = jnp.sum(jnp.where(rsel[d], cs, 0.0), axis=0, keepdims=True)
            g_neg = jnp.exp(-cs)
            g_end = jnp.exp(cs_last - cs)
            g_tot.append(jnp.exp(cs_last))
            kdv = kd_ref[rw, ln].astype(F32)
            bdv = bd_ref[rw, ln].astype(F32)
            a_w = (-kk_ref[rw, ln].astype(F32) * jnp.exp(cs - ld_ref[rw, ln]))
            a_s.append((a_w.astype(BF16), stack(a_w)))
            r_t.append((r_ref[rw, ln].astype(F32) * jnp.exp(cs)).astype(BF16))
            b_s.append(stack(bdv * g_neg))
            k_s.append(stack(kdv * g_neg))
            v_s.append(stack(v_ref[rw, ln]))
            b_end.append(stack(bdv * g_end))
            k_end.append(stack(kdv * g_end))
            if p == N_PAIRS - 1:
                yield
        n = range(len(items))
        dirs = [d for _, d, _ in items]
        gram = [_dot_nt(jnp.concatenate([a_s[q][0], r_t[q]], axis=0),
                        jnp.concatenate([b_s[q], k_s[q]], axis=0)) for q in n]
        yield
        a_ab =[jnp.where(strict_wide[dirs[q]], gram[q][0:c, 0:2 * c], 0.0) for q in n]
        a_ak = [jnp.where(strict_wide[dirs[q]], gram[q][0:c, 2 * c:4 * c], 0.0).astype(BF16) for q in n]
        p_rb = [jnp.where(incl_wide[dirs[q]], gram[q][c:2 * c, 0:2 * c], 0.0).astype(BF16) for q in n]
        p_rk = [jnp.where(incl_wide[dirs[q]], gram[q][c:2 * c, 2 * c:4 * c], 0.0).astype(BF16) for q in n]
        minv = [eye_wide + a_ab[q] for q in n]
        pw = [a_ab[q] for q in n]
        pw = [_dot(pw[q].astype(BF16), stack(pw[q])) for q in n]
        yield
        levels = int(math.log2(c))
        for k in range(2, levels):
            both = [_dot(pw[q].astype(BF16),
                         jnp.concatenate([stack(pw[q]), stack(minv[q])], axis=1)) for q in n]
            minv = [minv[q] + both[q][:, 2 * c:4 * c] for q in n]
            pw = [both[q][:, 0:2 * c] for q in n]
            yield
        minv = [minv[q] + _dot(pw[q].astype(BF16), stack(minv[q])) for q in n]
        akv = [_dot(a_ak[q], v_s[q]) for q in n]
        yield
        eff = [_dot(minv[q].astype(BF16),
                    jnp.concatenate([a_s[q][1], stack(akv[q])], axis=1)) for q in n]
        bk_end = [jnp.concatenate([b_end[q], k_end[q]], axis=0) for q in n]
        p_both = [jnp.concatenate([p_rb[q], p_rk[q]], axis=1) for q in n]
        yield
        st = [st_scr[d, p] for d, p in chains]
        nc = range(len(chains))
        for u in sub:
            q0 = u * len(chains)
            fs = [_dot_nt(jnp.concatenate([eff[q0 + m][:, 0:PAIR].astype(BF16), r_t[q0 + m]], axis=0),
                          st[m].astype(BF16)) for m in nc]
            uv = [jnp.concatenate([stack(fs[m][0:c] + eff[q0 + m][:, PAIR:2 * PAIR]), v_s[q0 + m]], axis=0)
                  for m in nc]
            st = [st[m] * g_tot[q0 + m] + _dot_tn(uv[m], bk_end[q0 + m]) for m in nc]
            for m, (d, p) in enumerate(chains):
                refs[d][6][rows[u, d], lanes[p]] = fs[m][c:2 * c] + _dot(p_both[q0 + m], uv[m])
            yield
        for m, (d, p) in enumerate(chains):
            st_scr[d, p] = st[m]

    return stages


def _ssd_scan_body(xf_ref, dtf_ref, xb_ref, dtb_ref, a_ref, yf_ref, yb_ref, st_scr, n_chunks):
    c = SCAN_CHUNK
    hd = SSD_HEAD_DIM
    row1 = lax.broadcasted_iota(jnp.int32, (c, 1), 0)
    m0 = lax.broadcasted_iota(jnp.int32, (c, PAIR), 1) < hd
    before_eq = [_order_masks(d, c)[1] for d in range(2)]
    incl01 = [before_eq[d].astype(BF16) for d in range(2)]
    rsel = [row1 == (c - 1 if d == 0 else 0) for d in range(2)]
    refs = ((xf_ref, dtf_ref, yf_ref), (xb_ref, dtb_ref, yb_ref))
    chains = [(d, p) for d in range(2) for p in range(N_PAIRS)]
    n = range(len(chains))
    group = [p // (N_PAIRS // SSD_GROUPS) for _, p in chains]
    lanes = [slice(p * PAIR, (p + 1) * PAIR) for _, p in chains]

    def stages(j):
        rows = (pl.ds(pl.multiple_of(j * c, c), c),
                pl.ds(pl.multiple_of((n_chunks - 1 - j) * c, c), c))
        dt = [refs[d][1][rows[d], :] for d in range(2)]
        cs_all = [_dot01_left(incl01[d], dt[d] * a_ref[d]) for d in range(2)]
        bm = [[refs[d][0][rows[d], SSD_WIDTH + g * SSD_STATE:SSD_WIDTH + (g + 1) * SSD_STATE]
               for g in range(SSD_GROUPS)] for d in range(2)]
        cm = [[refs[d][0][rows[d], SSD_WIDTH + (SSD_GROUPS + g) * SSD_STATE:
                          SSD_WIDTH + (SSD_GROUPS + g + 1) * SSD_STATE]
               for g in range(SSD_GROUPS)] for d in range(2)]
        cb = [[_dot_nt(cm[d][g], bm[d][g]) for g in range(SSD_GROUPS)] for d in range(2)]
        yield
        st = [st_scr[d, p] for d, p in chains]
        y_st = [_dot(cm[chains[q][0]][group[q]], st[q].astype(BF16)) for q in n]
        cs = [cs_all[chains[q][0]][:, lanes[q]] for q in n]
        xdt = [refs[chains[q][0]][0][rows[chains[q][0]], lanes[q]].astype(F32) * dt[chains[q][0]][:, lanes[q]]
               for q in n]
        yield
        probs = []
        for q in n:
            d = chains[q][0]
            cs_t = cs[q].T
            both = []
            for hh in range(2):
                col = cs[q][:, hh * hd:hh * hd + 1]
                rowv = cs_t[hh * hd:hh * hd + 1, :]
                dec = jnp.exp(jnp.where(before_eq[d], col - rowv, -jnp.inf))
                both.append((cb[d][group[q]] * dec).astype(BF16))
            probs.append(jnp.concatenate(both, axis=1))
            yield
        xs2 = [jnp.concatenate([jnp.where(m0, xdt[q], 0.0), jnp.where(m0, 0.0, xdt[q])],
                               axis=0).astype(BF16) for q in n]
        y_in = [_dot(probs[q], xs2[q]) for q in n]
        yield
        for q in n:
            d = chains[q][0]
            refs[d][2][rows[d], lanes[q]] = y_in[q] + jnp.exp(cs[q]) * y_st[q]
        yield
        for q, (d, p) in enumerate(chains):
            cs_last = jnp.sum(jnp.where(rsel[d], cs[q], 0.0), axis=0, keepdims=True)
            xe = (xdt[q] * jnp.exp(cs_last - cs[q])).astype(BF16)
            st_scr[d, p] = st[q] * jnp.exp(cs_last) + _dot_tn(bm[d][group[q]], xe)

    return stages


def _even_scan_kernel(rf_ref, vf_ref, kkf_ref, ldf_ref, kdf_ref, bdf_ref,
                      rb_ref, vb_ref, kkb_ref, ldb_ref, kdb_ref, bdb_ref,
                      xf_ref, dtf_ref, xb_ref, dtb_ref, a_ref, s0r_ref, s0s_ref,
                      yrf_ref, yrb_ref, ysf_ref, ysb_ref, sfr_ref, sfs_ref, str_scr, sts_scr, *, n_steps):
    i = pl.program_id(1)

    @pl.when(i == 0)
    def _():
        str_scr[...] = s0r_ref[...]
        sts_scr[...] = s0s_ref[...]

    rwkv_stages = _rwkv_scan_body(rf_ref, vf_ref, kkf_ref, ldf_ref, kdf_ref, bdf_ref,
                                  rb_ref, vb_ref, kkb_ref, ldb_ref, kdb_ref, bdb_ref,
                                  yrf_ref, yrb_ref, str_scr, n_steps * RWKV_CHUNKS_PER_STEP)
    ssd_stages = _ssd_scan_body(xf_ref, dtf_ref, xb_ref, dtb_ref, a_ref, ysf_ref, ysb_ref, sts_scr, n_steps)

    def step(j, carry):
        live = [rwkv_stages(j), ssd_stages(j)]
        while live:
            for gen in list(live):
                if next(gen, StopIteration) is StopIteration:
                    live.remove(gen)
        return carry

    lax.fori_loop(0, n_steps, step, 0)

    @pl.when(i == pl.num_programs(1) - 1)
    def _():
        sfr_ref[...] = str_scr[...]
        sfs_ref[...] = sts_scr[...]


def _even_scan_call(feat, xbc, dtbc, a_rep, s0_rk, s0_sd, tb):
    r, v, kk, _, _, ld, kd, bd = feat
    b, t, w = r.shape
    tb = min(tb, t)
    nb = t // tb
    assert SCAN_CHUNK == RWKV_CHUNKS_PER_STEP * RWKV_CHUNK
    tok_f = pl.BlockSpec((None, tb, w), lambda bi, i: (bi, i, 0))
    tok_b = pl.BlockSpec((None, tb, w), lambda bi, i: (bi, nb - 1 - i, 0))
    dir_f = pl.BlockSpec((None, None, tb, w), lambda bi, i: (0, bi, i, 0))
    dir_b = pl.BlockSpec((None, None, tb, w), lambda bi, i: (1, bi, nb - 1 - i, 0))
    xbc_f = pl.BlockSpec((None, tb, SSD_XBC), lambda bi, i: (bi, i, 0))
    xbc_b = pl.BlockSpec((None, tb, SSD_XBC), lambda bi, i: (bi, nb - 1 - i, 0))
    st_r = pl.BlockSpec((2, None, N_PAIRS, PAIR, PAIR), lambda bi, i: (0, bi, 0, 0, 0))
    st_s = pl.BlockSpec((2, None, N_PAIRS, SSD_STATE, PAIR), lambda bi, i: (0, bi, 0, 0, 0))
    y_sd = jax.ShapeDtypeStruct((b, t, w), F32)
    return pl.pallas_call(
        functools.partial(_even_scan_kernel, n_steps=tb // SCAN_CHUNK),
        name="even_scan",
        grid=(b, nb),
        in_specs=[tok_f, tok_f, tok_f, dir_f, dir_f, dir_f, tok_b, tok_b, tok_b, dir_b, dir_b, dir_b,
                  xbc_f, dir_f, xbc_b, dir_b,
                  pl.BlockSpec((2, 1, SSD_WIDTH), lambda bi, i: (0, 0, 0)), st_r, st_s],
        out_specs=[tok_f, tok_b, tok_f, tok_b, st_r, st_s],
        out_shape=[y_sd, y_sd, y_sd, y_sd,
                   jax.ShapeDtypeStruct((2, b, N_PAIRS, PAIR, PAIR), F32),
                   jax.ShapeDtypeStruct((2, b, N_PAIRS, SSD_STATE, PAIR), F32)],
        scratch_shapes=[pltpu.VMEM((2, N_PAIRS, PAIR, PAIR), F32),
                        pltpu.VMEM((2, N_PAIRS, SSD_STATE, PAIR), F32)],
        compiler_params=_cparams(("parallel", "arbitrary")),
    )(r, v, kk, ld, kd, bd, r, v, kk, ld, kd, bd, xbc, dtbc, xbc, dtbc, a_rep, s0_rk, s0_sd)


def _even_finish_kernel(yrf_ref, yrb_ref, bonus_ref, gate_ref, ysf_ref, ysb_ref, xs_ref, z_ref, x_ref,
                        m_ref, lnw_ref, lnb_ref, dsk_ref, nw_ref, j_ref, wo_ref, o_ref):
    jm = j_ref[...]
    y = yrf_ref[...] + yrb_ref[...]
    inv_n = 1.0 / RWKV_HEAD_DIM
    mean = _head_sum(y, jm) * inv_n
    yc = y - mean
    var = _head_sum(yc * yc, jm) * inv_n
    y = yc * lax.rsqrt(var + RWKV_GN_EPS) * lnw_ref[...] + lnb_ref[...]
    y_rk = (y + bonus_ref[...].astype(F32)) * gate_ref[...].astype(F32)
    s = ysf_ref[...] + ysb_ref[...] + dsk_ref[...] * xs_ref[...].astype(F32)
    s = s * _silu(z_ref[...].astype(F32))
    s = s * lax.rsqrt(jnp.mean(s * s, axis=-1, keepdims=True) + NORM_EPS) * nw_ref[...]
    out = _dot(y_rk.astype(BF16), wo_ref[0:RWKV_WIDTH, :]) + \
        _dot(s.astype(BF16), wo_ref[RWKV_WIDTH:RWKV_WIDTH + SSD_WIDTH, :])
    o_ref[...] = x_ref[...] + m_ref[2:3, :] * out


def _even_finish_call(yrk, bonus, gate, ysd, xbc, proj, x, mods, p, tm):
    b, t, d = x.shape
    tm = min(tm, t)
    w = RWKV_WIDTH
    per_batch = mods.shape[0] > 1
    full = lambda shape: pl.BlockSpec(shape, lambda bi, i: (0,) * len(shape))
    tok = pl.BlockSpec((None, tm, w), lambda bi, i: (bi, i, 0))
    return pl.pallas_call(
        _even_finish_kernel,
        name="even_finish",
        grid=(b, t // tm),
        in_specs=[tok, tok, tok, tok, tok, tok,
                  pl.BlockSpec((None, tm, SSD_WIDTH), lambda bi, i: (bi, i, 0)),
                  pl.BlockSpec((None, tm, SSD_WIDTH), lambda bi, i: (bi, i, 0)),
                  pl.BlockSpec((None, tm, d), lambda bi, i: (bi, i, 0)),
                  pl.BlockSpec((None, N_MOD, d), (lambda bi, i: (bi, 0, 0)) if per_batch
                               else (lambda bi, i: (0, 0, 0))),
                  full((1, w)), full((1, w)), full((1, w)), full((1, w)),
                  full((V7X_MXU_DIM, V7X_MXU_DIM)),
                  full((2 * w, d))],
        out_specs=pl.BlockSpec((None, tm, d), lambda bi, i: (bi, i, 0)),
        out_shape=jax.ShapeDtypeStruct((b, t, d), F32),
        compiler_params=_cparams(("parallel", "parallel")),
    )(yrk[0], yrk[1], bonus, gate, ysd[0], ysd[1], xbc, proj, x, mods, p['ln_w'], p['ln_b'], p['d_skip'],
      p['norm_w'],
      p['head_sum'], p['w_out'])


def _ret_scan_kernel(*refs, n_chunks, mode):
    if mode == 'state':
        k_ref, v_ref, lg_ref, s0_ref, sf_ref, st_scr, sc_scr = refs
        d = pl.program_id(0)
    elif mode == 'bwd':
        q_ref, k_ref, v_ref, lg_ref, s0_ref, y_ref, st_scr, dec_scr, sc_scr = refs
        d = 1
    else:
        (q_ref, k_ref, v_ref, lg_ref, s0_ref, yb_ref, g_ref, x_ref, m_ref, wo_ref, o_ref,
         st_scr, dec_scr, sc_scr, ysum_scr) = refs
        d = 0
    i = pl.program_id(2)
    c = SCAN_CHUNK
    dk, dv = RET_QK_DIM, RET_V_DIM
    heads = range(RET_HEADS)
    lg_all = lg_ref[...]

    @pl.when(i == 0)
    def _():
        st_scr[...] = s0_ref[...]
        _, before_eq = _order_masks(d, c)
        row = lax.broadcasted_iota(jnp.int32, (c, c), 0)
        col = lax.broadcasted_iota(jnp.int32, (c, c), 1)
        rel = jnp.abs(row - col).astype(F32)
        pos = (row + d * (c - 1 - 2 * row)).astype(F32)
        for h in heads:
            lg = lg_all[:, h * dk:h * dk + 1]
            if mode != 'state':
                dec_scr[h] = jnp.where(before_eq, jnp.exp(rel * lg), 0.0)
                sc_scr[h, 0] = jnp.exp((pos + 1.0) * lg).astype(BF16)
            sc_scr[h, 1] = jnp.exp((c - 1.0 - pos) * lg).astype(BF16)

    def chunk_body(j, carry):
        cj = j + d * (n_chunks - 1 - 2 * j)
        rows = pl.ds(pl.multiple_of(cj * c, c), c)
        ks = [k_ref[rows, h * dk:(h + 1) * dk] for h in heads]
        vs = [v_ref[rows, h * dv:(h + 1) * dv] for h in heads]
        st = [st_scr[h] for h in heads]
        if mode != 'state':
            qs = [q_ref[rows, h * dk:(h + 1) * dk] for h in heads]
            qk = [_dot_nt(qs[h], ks[h]) for h in heads]
            scores = [(qk[h] * dec_scr[h]).astype(BF16) for h in heads]
            y_st = [_dot(qs[h] * sc_scr[h, 0], st[h].astype(BF16)) for h in heads]
            for h in heads:
                cols = slice(h * dv, (h + 1) * dv)
                y = _dot(scores[h], vs[h]) + y_st[h]
                if mode == 'bwd':
                    y_ref[rows, cols] = y.astype(y_ref.dtype)
                else:
                    ysum_scr[rows, cols] = y + yb_ref[rows, cols].astype(F32)
        for h in heads:
            lg = lg_all[:, h * dk:h * dk + 1]
            st_scr[h] = st[h] * jnp.exp(c * lg) + _dot_tn(ks[h] * sc_scr[h, 1], vs[h])
        return carry

    lax.fori_loop(0, n_chunks, chunk_body, 0, unroll=4)

    if mode == 'state':
        @pl.when(i == pl.num_programs(2) - 1)
        def _():
            sf_ref[...] = st_scr[...]
    elif mode == 'fwd':
        parts = []
        for h in heads:
            yh = ysum_scr[:, h * dv:(h + 1) * dv]
            parts.append(yh * lax.rsqrt(jnp.mean(yh * yh, axis=-1, keepdims=True) + NORM_EPS))
        act = (_silu(g_ref[...].astype(F32)) * jnp.concatenate(parts, axis=1)).astype(BF16)
        o_ref[...] = x_ref[...] + m_ref[2:3, :] * _dot(act, wo_ref[...])


def _ret_scan_call(proj, lg_rep, s0, tb, mode, finish=None):
    b, t, _ = proj.shape
    tb = min(tb, t)
    nb = t // tb
    n_dirs = 2 if mode == 'state' else 1
    d0 = 1 if mode == 'bwd' else 0
    blk = lambda dd, i: i + (dd + d0) * (nb - 1 - 2 * i)
    tok = lambda width, col: pl.BlockSpec((None, tb, width), lambda dd, bi, i: (bi, blk(dd, i), col))
    st = pl.BlockSpec((None, None, RET_HEADS, RET_QK_DIM, RET_V_DIM),
                      lambda dd, bi, i: (dd + d0, bi, 0, 0, 0))
    q_spec, k_spec, v_spec = tok(RET_QK, 0), tok(RET_QK, 1), tok(RET_V, 2 * RET_QK // RET_V)
    lg_spec = pl.BlockSpec((None, 1, RET_QK), lambda dd, bi, i: (dd + d0, 0, 0))
    scratch = [pltpu.VMEM((RET_HEADS, RET_QK_DIM, RET_V_DIM), F32)]
    if mode != 'state':
        scratch.append(pltpu.VMEM((RET_HEADS, SCAN_CHUNK, SCAN_CHUNK), F32))
    scratch.append(pltpu.VMEM((RET_HEADS, 2, SCAN_CHUNK, RET_QK_DIM), BF16))
    if mode == 'state':
        in_specs, args = [k_spec, v_spec, lg_spec, st], [proj, proj, lg_rep, s0]
        out_specs = st
        out_shape = jax.ShapeDtypeStruct((2, b, RET_HEADS, RET_QK_DIM, RET_V_DIM), F32)
    elif mode == 'bwd':
        in_specs, args = [q_spec, k_spec, v_spec, lg_spec, st], [proj, proj, proj, lg_rep, s0]
        out_specs = tok(RET_V, 0)
        out_shape = jax.ShapeDtypeStruct((b, t, RET_V), BF16)
    else:
        y_bwd, x, mods, w_out = finish
        d = x.shape[-1]
        in_specs = [q_spec, k_spec, v_spec, lg_spec, st, tok(RET_V, 0),
                    tok(RET_V, (2 * RET_QK + RET_V) // RET_V), tok(d, 0),
                    pl.BlockSpec((None, N_MOD, d), lambda dd, bi, i: (bi, 0, 0)),
                    pl.BlockSpec((RET_V, d), lambda dd, bi, i: (0, 0))]
        args = [proj, proj, proj, lg_rep, s0, y_bwd, proj, x, mods, w_out]
        out_specs = tok(d, 0)
        out_shape = jax.ShapeDtypeStruct((b, t, d), F32)
        scratch.append(pltpu.VMEM((tb, RET_V), F32))
    return pl.pallas_call(
        functools.partial(_ret_scan_kernel, n_chunks=tb // SCAN_CHUNK, mode=mode),
        name="ret_scan_" + mode,
        grid=(n_dirs, b, nb),
        in_specs=in_specs,
        out_specs=out_specs,
        out_shape=out_shape,
        scratch_shapes=scratch,
        compiler_params=_cparams(("parallel", "parallel", "arbitrary")),
    )(*args)


def _ffn_kernel(*refs, on_grid, final_norm):
    if on_grid:
        x_ref, xn_ref, g_ref, m_ref, wu_ref, cw_ref, cb_ref, wd_ref = refs[:8]
        rest = refs[8:]
    else:
        x_ref, g_ref, m_ref, wu_ref, cw_ref, cb_ref, wd_ref = refs[:7]
        rest = refs[7:]
    if final_norm:
        fg_ref, o_ref, *scrs = rest
    else:
        o_ref, *scrs = rest
    gate_scr, val_scr, *scrs = scrs
    if on_grid:
        gtop_scr, *act_scrs = scrs
    else:
        act_scrs = scrs
    i = pl.program_id(1)
    n_tiles = pl.num_programs(1)
    tm = x_ref.shape[0]

    def norm_mod(xv):
        hv = xv * lax.rsqrt(jnp.mean(xv * xv, axis=-1, keepdims=True) + NORM_EPS) * g_ref[...]
        return (hv * (1.0 + m_ref[4:5, :]) + m_ref[3:4, :]).astype(BF16)

    x = x_ref[...]
    h = norm_mod(x)
    row = lax.broadcasted_iota(jnp.int32, (tm, 1), 0)
    if on_grid:
        col = row % GRID_W
        ok_left = col > 0
        ok_right = col < GRID_W - 1
        zero = jnp.zeros((GRID_W, x.shape[1]), BF16)
        h_ext = jnp.concatenate([h, jnp.where(i < n_tiles - 1, norm_mod(xn_ref[...]), zero)], axis=0)

        @pl.when(i == 0)
        def _():
            gtop_scr[...] = jnp.zeros_like(gtop_scr)
    else:
        ok_left = row > 0
        ok_right = row < tm - 1
    n_chunks = D_FF // FFN_COL_CHUNK

    def up_proj(j):
        cols = slice(j * FFN_COL_CHUNK, (j + 1) * FFN_COL_CHUNK)
        vcols = slice(D_FF + j * FFN_COL_CHUNK, D_FF + (j + 1) * FFN_COL_CHUNK)
        if on_grid:
            gate_scr[j % n_buf, GRID_W:, :] = _dot(h_ext, wu_ref[:, cols])
        else:
            gate_scr[j % n_buf] = _dot(h, wu_ref[:, cols])
        val_scr[j % n_buf] = _dot(h, wu_ref[:, vcols])

    out = None
    n_buf = gate_scr.shape[0]
    for j in range(n_buf - 1):
        up_proj(j)
    for j in range(n_chunks):
        cols = slice(j * FFN_COL_CHUNK, (j + 1) * FFN_COL_CHUNK)
        buf = j % n_buf
        if j + n_buf - 1 < n_chunks:
            up_proj(j + n_buf - 1)
        grp, slot = divmod(j, FFN_DOWN_GROUP)
        act_scr = act_scrs[grp]
        acols = slice(slot * FFN_COL_CHUNK, (slot + 1) * FFN_COL_CHUNK)
        if on_grid:
            gate_scr[buf, 0:GRID_W, :] = gtop_scr[:, cols]
            gtop_scr[:, cols] = gate_scr[buf, tm:tm + GRID_W, :]
            rows3 = [gate_scr[buf, dr * GRID_W:dr * GRID_W + tm, :] for dr in range(3)]
            taps = [rows3[0] * cw_ref[dc:dc + 1, cols] + rows3[1] * cw_ref[3 + dc:4 + dc, cols]
                    + rows3[2] * cw_ref[6 + dc:7 + dc, cols] for dc in range(3)]
        else:
            gate = gate_scr[buf]
            taps = [gate * cw_ref[3 + dc:4 + dc, cols] for dc in range(3)]
        acc = cb_ref[:, cols] + taps[1] + jnp.where(ok_left, pltpu.roll(taps[0], 1, axis=0), 0.0) \
            + jnp.where(ok_right, pltpu.roll(taps[2], tm - 1, axis=0), 0.0)
        act_scr[:, acols] = (_gelu_tanh(acc) * val_scr[buf]).astype(BF16)
        if slot + 1 == FFN_DOWN_GROUP or j + 1 == n_chunks:
            width = (slot + 1) * FFN_COL_CHUNK
            k0 = grp * FFN_DOWN_GROUP * FFN_COL_CHUNK
            part = _dot(act_scr[:, 0:width], wd_ref[k0:k0 + width, :])
            out = part if out is None else out + part
    out = x + m_ref[5:6, :] * out
    if final_norm:
        out = out * lax.rsqrt(jnp.mean(out * out, axis=-1, keepdims=True) + NORM_EPS) * fg_ref[...]
    o_ref[...] = out


def _ffn_call(x, norm_g, mods, w_up, conv_w9, conv_b, w_down, *, tm, on_grid, final_g=None):
    b, t, d = x.shape
    tm = min(tm, t)
    n_tiles = t // tm
    per_batch = mods.shape[0] > 1
    full = lambda shape: pl.BlockSpec(shape, lambda bi, i: (0,) * len(shape))
    in_specs = [pl.BlockSpec((None, tm, d), lambda bi, i: (bi, i, 0))]
    args = [x]
    scratch = [pltpu.VMEM((tm, FFN_DOWN_GROUP * FFN_COL_CHUNK), BF16)
               for _ in range(-(-D_FF // (FFN_DOWN_GROUP * FFN_COL_CHUNK)))]
    if on_grid:
        r = tm // GRID_W
        last = t // GRID_W - 1
        in_specs.append(pl.BlockSpec((None, GRID_W, d), lambda bi, i: (bi, jnp.minimum((i + 1) * r, last), 0)))
        args.append(x)
        scratch.insert(0, pltpu.VMEM((GRID_W, D_FF), F32))
    else:
        assert n_tiles == 1
    gate_rows = tm + 2 * GRID_W if on_grid else tm
    scratch = [pltpu.VMEM((FFN_UP_BUFFERS, gate_rows, FFN_COL_CHUNK), F32),
               pltpu.VMEM((FFN_UP_BUFFERS, tm, FFN_COL_CHUNK), F32)] + scratch
    in_specs += [full((1, d)),
                 pl.BlockSpec((None, N_MOD, d), (lambda bi, i: (bi, 0, 0)) if per_batch
                              else (lambda bi, i: (0, 0, 0))),
                 full((d, 2 * D_FF)), full((9, D_FF)), full((1, D_FF)), full((D_FF, d))]
    args += [norm_g, mods, w_up, conv_w9, conv_b, w_down]
    if final_g is not None:
        in_specs.append(full((1, d)))
        args.append(final_g)
    return pl.pallas_call(
        functools.partial(_ffn_kernel, on_grid=on_grid, final_norm=final_g is not None),
        name="conv_ffn",
        grid=(b, n_tiles),
        in_specs=in_specs,
        out_specs=pl.BlockSpec((None, tm, d), lambda bi, i: (bi, i, 0)),
        out_shape=jax.ShapeDtypeStruct((b, t, d), F32),
        scratch_shapes=scratch,
        compiler_params=_cparams(("parallel", "arbitrary")),
    )(*args)


def _block_diag2(a, b):
    za = jnp.zeros((a.shape[0], b.shape[1]), a.dtype)
    zb = jnp.zeros((b.shape[0], a.shape[1]), a.dtype)
    return jnp.concatenate([jnp.concatenate([a, za], axis=1), jnp.concatenate([zb, b], axis=1)], axis=0)


def _pad_cols(a, n, fill=0.0):
    return jnp.pad(a, ((0, 0), (0, n - a.shape[1])), constant_values=fill)


def _even_params(j, ev_w_in, ev_mu_prev, ev_mu_next, rk_w0_f, rk_w0_b, rk_w2_f, rk_w2_b, rk_a0_f,
                 rk_a0_b, rk_a2_f, rk_a2_b, rk_g2, rk_k_k, rk_k_a, rk_r_k, rk_ln_w, rk_ln_b,
                 ssd_conv_w, ssd_conv_b, ssd_dt_bias_f, ssd_dt_bias_b, ssd_a_log_f, ssd_a_log_b,
                 ssd_d, ssd_norm_w, ev_w_out):
    w_in = ev_w_in[j]
    rw = w_in[:, :RWKV_COLS]
    z = w_in[:, RWKV_COLS:RWKV_COLS + SSD_WIDTH]
    xbc = w_in[:, RWKV_COLS + SSD_WIDTH:RWKV_COLS + SSD_WIDTH + SSD_XBC]
    dts = w_in[:, RWKV_COLS + SSD_WIDTH + SSD_XBC:]
    w_packed = jnp.concatenate([_pad_cols(jnp.concatenate([rw, dts], axis=1), EV_RW_BLOCK), xbc, z], axis=1)
    head = jnp.arange(V7X_MXU_DIM) // RWKV_HEAD_DIM
    head_sum = (head[:, None] == head[None, :]).astype(BF16)
    lane = jnp.arange(V7X_LANES)[:, None]
    tgt = jnp.arange(2 * SSD_WIDTH)[None, :]
    dt_expand = (lane == (tgt // SSD_WIDTH) * SSD_HEADS + (tgt % SSD_WIDTH) // SSD_HEAD_DIM).astype(BF16)
    rep = lambda a: jnp.repeat(a, SSD_HEAD_DIM)[None, :]
    row = lambda a: a[None, :]
    return {
        'w_in': w_packed.astype(BF16),
        'mu_self': _pad_cols(row(1.0 - ev_mu_prev[j] - ev_mu_next[j]), EV_RW_BLOCK, 1.0),
        'mu_prev': _pad_cols(row(ev_mu_prev[j]), EV_RW_BLOCK),
        'mu_next': _pad_cols(row(ev_mu_next[j]), EV_RW_BLOCK),
        'w0': row(jnp.concatenate([rk_w0_f[j], rk_w0_b[j]])),
        'w2': _block_diag2(rk_w2_f[j], rk_w2_b[j]).astype(BF16),
        'a0': row(jnp.concatenate([rk_a0_f[j], rk_a0_b[j]])),
        'a2': _block_diag2(rk_a2_f[j], rk_a2_b[j]).astype(BF16),
        'g2': rk_g2[j].astype(BF16),
        'k_k': row(rk_k_k[j]), 'k_a': row(rk_k_a[j]), 'r_k': row(rk_r_k[j].reshape(-1)),
        'ln_w': row(rk_ln_w[j]), 'ln_b': row(rk_ln_b[j]),
        'head_sum': head_sum,
        'conv_w': ssd_conv_w[j], 'conv_b': row(ssd_conv_b[j]),
        'dt_bias': _pad_cols(row(jnp.concatenate([ssd_dt_bias_f[j], ssd_dt_bias_b[j]])), V7X_LANES),
        'dt_expand': dt_expand,
        'a_rep': jnp.stack([rep(-jnp.exp(ssd_a_log_f[j])), rep(-jnp.exp(ssd_a_log_b[j]))]),
        'd_skip': rep(ssd_d[j]),
        'norm_w': row(ssd_norm_w[j]),
        'w_out': ev_w_out[j].astype(BF16),
    }


def _rope_tables(t):
    n = RET_QK_DIM // 4
    pos = jnp.arange(t)
    row = (pos // GRID_W).astype(F32)
    col = (pos % GRID_W).astype(F32)
    inv = ROPE_BASE ** (-jnp.arange(n, dtype=F32) / n)
    ang = jnp.concatenate([row[:, None] * inv, col[:, None] * inv], axis=-1)
    cos, sin = jnp.cos(ang), jnp.sin(ang)
    return jnp.concatenate([cos, cos], axis=-1), jnp.concatenate([-sin, sin], axis=-1)


def _conv_ffn(x, mods, norm_g, w_up, conv_w9, conv_b, w_down, *, on_grid, final_g=None):
    return _ffn_call(x, norm_g, mods, w_up, conv_w9, conv_b, w_down, tm=512, on_grid=on_grid,
                     final_g=final_g)


def _even_layer(x, ctx, mods_x, mods_c, norm_g, p):
    b = x.shape[0]

    def features(h, mods):
        *feat, xbc, dtbc, z = _even_feat_call(h, norm_g, mods, p, 512)
        return z, feat, xbc, dtbc

    proj_c, feat_c, xbc_c, dt_c = features(ctx, mods_c)
    proj_x, feat_x, xbc_x, dt_x = features(x, mods_x)
    s0 = jnp.zeros((2, b, N_PAIRS, PAIR, PAIR), F32)
    h0 = jnp.zeros((2, b, N_PAIRS, SSD_STATE, PAIR), F32)
    *y_c, s_ctx, h_ctx = _even_scan_call(feat_c, xbc_c, dt_c, p['a_rep'], s0, h0, 256)
    *y_x, _, _ = _even_scan_call(feat_x, xbc_x, dt_x, p['a_rep'], s_ctx, h_ctx, 512)
    yrk_c, ysd_c = y_c[0:2], y_c[2:4]
    yrk_x, ysd_x = y_x[0:2], y_x[2:4]
    x = _even_finish_call(yrk_x, feat_x[3], feat_x[4], ysd_x, xbc_x, proj_x, x, mods_x, p, 1024)
    ctx = _even_finish_call(yrk_c, feat_c[3], feat_c[4], ysd_c, xbc_c, proj_c, ctx, mods_c, p, 256)
    return x, ctx


def _odd_layer(x, ctx, mods_x, mods_c, norm_g, w_in, lg_rep, w_out):
    b, t, _ = x.shape
    proj_c = _nm_call(ctx, norm_g, mods_c, w_in, shift_row=0, tm=256, tn=512, out_dtype=BF16,
                      qk_mode='scale')
    proj_x = _nm_call(x, norm_g, mods_x, w_in, shift_row=0, tm=512, tn=512, out_dtype=BF16,
                      qk_mode='rope', rope=_rope_tables(t))
    s0 = jnp.zeros((2, b, RET_HEADS, RET_QK_DIM, RET_V_DIM), F32)
    s_ctx = _ret_scan_call(proj_c, lg_rep, s0, 256, 'state')
    y_bwd = _ret_scan_call(proj_x, lg_rep, s_ctx, 512, 'bwd')
    return _ret_scan_call(proj_x, lg_rep, s_ctx, 512, 'fwd', finish=(y_bwd, x, mods_x, w_out))


def kernel(x, c, ctx, c_ctx, mod_w, mod_b, norm1_g, norm2_g, ffn_w_up, ffn_conv_w, ffn_conv_b, ffn_w_down, ev_w_in, ev_mu_prev, ev_mu_next, rk_w0_f, rk_w0_b, rk_w2_f, rk_w2_b, rk_a0_f, rk_a0_b, rk_a2_f, rk_a2_b, rk_g2, rk_k_k, rk_k_a, rk_r_k, rk_ln_w, rk_ln_b, ssd_conv_w, ssd_conv_b, ssd_dt_bias_f, ssd_dt_bias_b, ssd_a_log_f, ssd_a_log_b, ssd_d, ssd_norm_w, ev_w_out, ret_w_in, ret_log2_f, ret_log2_b, ret_w_out, final_norm_g):
    b, t, d = x.shape
    depth = mod_w.shape[0]
    rows = -(-(b + 1) // V7X_SUBLANES) * V7X_SUBLANES
    cond = jnp.concatenate([c, c_ctx[None, :], jnp.zeros((rows - b - 1, d), F32)], axis=0)
    mods = _mod_call(cond, mod_w, mod_b).reshape(depth, rows, N_MOD, d)
    for i in range(depth):
        need_ctx = i < depth - 1
        mods_x = mods[i, :b]
        mods_c = mods[i, b:b + 1]
        j = i // 2
        g1 = norm1_g[i][None, :]
        if i % 2 == 0:
            p = _even_params(j, ev_w_in, ev_mu_prev, ev_mu_next, rk_w0_f, rk_w0_b, rk_w2_f, rk_w2_b,
                             rk_a0_f, rk_a0_b, rk_a2_f, rk_a2_b, rk_g2, rk_k_k, rk_k_a, rk_r_k,
                             rk_ln_w, rk_ln_b, ssd_conv_w, ssd_conv_b, ssd_dt_bias_f, ssd_dt_bias_b,
                             ssd_a_log_f, ssd_a_log_b, ssd_d, ssd_norm_w, ev_w_out)
            x, ctx_mixed = _even_layer(x, ctx, mods_x, mods_c, g1, p)
        else:
            lg = jnp.stack([jnp.log1p(-jnp.exp2(-ret_log2_f[j])), jnp.log1p(-jnp.exp2(-ret_log2_b[j]))])
            lg_rep = jnp.repeat(lg, RET_QK_DIM, axis=-1)[:, None, :]
            x = _odd_layer(x, ctx, mods_x, mods_c, g1, ret_w_in[j].astype(BF16), lg_rep,
                           ret_w_out[j].astype(BF16))
            ctx_mixed = None
        g2 = norm2_g[i][None, :]
        w_up = ffn_w_up[i].astype(BF16)
        w_down = ffn_w_down[i].astype(BF16)
        conv_w9 = ffn_conv_w[i].reshape(9, D_FF)
        conv_b = ffn_conv_b[i][None, :]
        last = i == depth - 1
        x = _conv_ffn(x, mods_x, g2, w_up, conv_w9, conv_b, w_down, on_grid=True,
                      final_g=final_norm_g[None, :] if last else None)
        if need_ctx:
            ctx = _conv_ffn(ctx_mixed, mods_c, g2, w_up, conv_w9, conv_b, w_down, on_grid=False)
    return x
```

```python
import functools
import math

import jax
import jax.numpy as jnp
from jax import lax
from jax.experimental import pallas as pl
from jax.experimental.pallas import tpu as pltpu

F32 = jnp.float32
BF16 = jnp.bfloat16

GRID_W = 64
N_MOD = 6
NORM_EPS = 1e-6
RWKV_HEADS = 8
RWKV_HEAD_DIM = 64
RWKV_WIDTH = RWKV_HEADS * RWKV_HEAD_DIM
DECAY_LORA = 64
ICLR_LORA = 64
GATE_LORA = 128
RWKV_GN_EPS = 64e-5
RWKV_COLS = 3 * RWKV_WIDTH + 2 * DECAY_LORA + 2 * ICLR_LORA + GATE_LORA
SSD_HEADS = 8
SSD_HEAD_DIM = 64
SSD_WIDTH = SSD_HEADS * SSD_HEAD_DIM
SSD_GROUPS = 2
SSD_STATE = 128
SSD_XBC = SSD_WIDTH + 2 * SSD_GROUPS * SSD_STATE
RET_HEADS = 8
RET_QK_DIM = 128
RET_V_DIM = 256
RET_QK = RET_HEADS * RET_QK_DIM
RET_V = RET_HEADS * RET_V_DIM
ROPE_BASE = 10000.0
D_FF = 2816

V7X_LANES = 128
V7X_SUBLANES = 8
V7X_MXU_DIM = 256
V7X_VMEM_LIMIT_BYTES = 56 * 1024 * 1024

RWKV_CHUNK = 64
RWKV_CHUNKS_PER_STEP = 2
SCAN_CHUNK = 128
FFN_COL_CHUNK = 256
FFN_DOWN_GROUP = 6
FFN_UP_BUFFERS = 2
PAIR = 2 * RWKV_HEAD_DIM
N_PAIRS = RWKV_HEADS // 2
EV_RW_BLOCK = 2048
EV_DT_OFF = RWKV_COLS
EV_XBC_OFF = EV_RW_BLOCK
EV_Z_OFF = EV_RW_BLOCK + SSD_XBC
EV_COLS = EV_Z_OFF + SSD_WIDTH
EV_PROJ_CHUNK = 1024


def _cparams(sem):
    return pltpu.CompilerParams(dimension_semantics=sem, vmem_limit_bytes=V7X_VMEM_LIMIT_BYTES)


def _split3(x):
    hi = x.astype(BF16)
    r1 = x - hi.astype(F32)
    mid = r1.astype(BF16)
    lo = (r1 - mid.astype(F32)).astype(BF16)
    return hi, mid, lo


def _dot(a, b):
    return jnp.dot(a, b, preferred_element_type=F32)


def _dot_nt(a, b):
    return lax.dot_general(a, b, (((1,), (1,)), ((), ())), preferred_element_type=F32)


def _dot_tn(a, b):
    return lax.dot_general(a, b, (((0,), (0,)), ((), ())), preferred_element_type=F32)


def _dot01(x, m01):
    hi, mid, lo = _split3(x)
    return _dot(hi, m01) + _dot(mid, m01) + _dot(lo, m01)


def _head_sum(x, j01):
    hi = x.astype(BF16)
    lo = (x - hi.astype(F32)).astype(BF16)
    n = j01.shape[0]
    return jnp.concatenate([_dot(hi[:, g * n:(g + 1) * n], j01) + _dot(lo[:, g * n:(g + 1) * n], j01)
                            for g in range(x.shape[1] // n)], axis=1)


def _dot01_left(m01, x):
    hi, mid, lo = _split3(x)
    return _dot(m01, hi) + _dot(m01, mid) + _dot(m01, lo)


def _sigmoid(x):
    return 0.5 * jnp.tanh(0.5 * x) + 0.5


def _silu(x):
    half = 0.5 * x
    return half + half * jnp.tanh(half)


def _softplus(x):
    return jnp.maximum(x, 0.0) + jnp.log1p(jnp.exp(-jnp.abs(x)))


def _gelu_tanh(x):
    c = math.sqrt(2.0 / math.pi)
    half = 0.5 * x
    return half + half * jnp.tanh(x * (c + (0.044715 * c) * (x * x)))


def _order_masks(d, n):
    row = lax.broadcasted_iota(jnp.int32, (n, n), 0)
    col = lax.broadcasted_iota(jnp.int32, (n, n), 1)
    diff = (row - col) * (1 - 2 * d)
    return diff > 0, diff >= 0


def _mod_kernel(c_ref, w_ref, b_ref, o_ref):
    h = _silu(c_ref[...])
    hi, mid, lo = _split3(h)
    w = w_ref[...]
    wh = w.astype(BF16)
    wl = (w - wh.astype(F32)).astype(BF16)
    acc = _dot(hi, wh) + _dot(mid, wh) + _dot(hi, wl)
    o_ref[...] = acc + b_ref[...]


def _mod_call(cond, mod_w, mod_b):
    depth, d, n = mod_w.shape
    rows = cond.shape[0]
    tn = 1024
    return pl.pallas_call(
        _mod_kernel,
        name="adaln_mod",
        grid=(depth, n // tn),
        in_specs=[pl.BlockSpec((rows, d), lambda l, j: (0, 0)),
                  pl.BlockSpec((None, d, tn), lambda l, j: (l, 0, j)),
                  pl.BlockSpec((None, 1, tn), lambda l, j: (l, 0, j))],
        out_specs=pl.BlockSpec((None, rows, tn), lambda l, j: (l, 0, j)),
        out_shape=jax.ShapeDtypeStruct((depth, rows, n), F32),
        compiler_params=_cparams(("parallel", "parallel")),
    )(cond, mod_w, mod_b.reshape(depth, 1, n))


def _nm_kernel(*refs, shift_row, tn, qk_mode):
    if qk_mode == 'rope':
        x_ref, g_ref, m_ref, w_ref, cos_ref, sin_ref, o_ref = refs
    else:
        x_ref, g_ref, m_ref, w_ref, o_ref = refs
    x = x_ref[...]
    h = x * lax.rsqrt(jnp.mean(x * x, axis=-1, keepdims=True) + NORM_EPS) * g_ref[...]
    h = h * (1.0 + m_ref[shift_row + 1:shift_row + 2, :]) + m_ref[shift_row:shift_row + 1, :]
    h = h.astype(BF16)
    dk = RET_QK_DIM
    for j in range(w_ref.shape[1] // tn):
        cols = slice(j * tn, (j + 1) * tn)
        y = _dot(h, w_ref[:, cols])
        if qk_mode is not None and j * tn < 2 * RET_QK:
            scale = dk ** -0.5 if j * tn >= RET_QK else 1.0
            heads = []
            for hh in range(tn // dk):
                yh = y[:, hh * dk:(hh + 1) * dk]
                if qk_mode == 'rope':
                    yh = yh * cos_ref[...] + pltpu.roll(yh, dk // 2, axis=1) * sin_ref[...]
                heads.append(yh * scale if scale != 1.0 else yh)
            y = jnp.concatenate(heads, axis=1)
        o_ref[:, cols] = y.astype(o_ref.dtype)


def _nm_call(x, g, mods, w, *, shift_row, tm, tn, out_dtype=F32, qk_mode=None, rope=None):
    b, t, d = x.shape
    n = w.shape[1]
    tm = min(tm, t)
    per_batch = mods.shape[0] > 1
    in_specs = [pl.BlockSpec((None, tm, d), lambda bi, i: (bi, i, 0)),
                pl.BlockSpec((1, d), lambda bi, i: (0, 0)),
                pl.BlockSpec((None, N_MOD, d), (lambda bi, i: (bi, 0, 0)) if per_batch
                             else (lambda bi, i: (0, 0, 0))),
                pl.BlockSpec((d, n), lambda bi, i: (0, 0))]
    args = [x, g, mods, w]
    if qk_mode == 'rope':
        tab = pl.BlockSpec((tm, RET_QK_DIM), lambda bi, i: (i, 0))
        in_specs += [tab, tab]
        args += list(rope)
    return pl.pallas_call(
        functools.partial(_nm_kernel, shift_row=shift_row, tn=tn, qk_mode=qk_mode),
        name="norm_mod_matmul",
        grid=(b, t // tm),
        in_specs=in_specs,
        out_specs=pl.BlockSpec((None, tm, n), lambda bi, i: (bi, i, 0)),
        out_shape=jax.ShapeDtypeStruct((b, t, n), out_dtype),
        compiler_params=_cparams(("parallel", "parallel")),
    )(*args)


def _even_feat_kernel(x_ref, xp_ref, xn_ref, g_ref, m_ref, w_ref,
                      mus_ref, mup_ref, mun_ref, w0_ref, w2_ref, a0_ref, a2_ref,
                      g2_ref, kk_ref, ka_ref, rk_ref, j_ref, cw_ref, cb_ref, dtb_ref, e_ref,
                      r_ref, v_ref, kkn_ref, bonus_ref, gate_ref, ld_ref, kd_ref, bd_ref,
                      xbc_ref, dtbc_ref, z_ref):
    i = pl.program_id(1)
    n_tiles = pl.num_programs(1)
    tm = x_ref.shape[0]
    halo = V7X_SUBLANES
    ext = tm + 2 * halo

    def norm_mod(xv):
        hv = xv * lax.rsqrt(jnp.mean(xv * xv, axis=-1, keepdims=True) + NORM_EPS) * g_ref[...]
        return (hv * (1.0 + m_ref[1:2, :]) + m_ref[0:1, :]).astype(BF16)

    h = norm_mod(x_ref[...])
    zero = jnp.zeros((halo, x_ref.shape[1]), BF16)
    h_ext = jnp.concatenate([jnp.where(i > 0, norm_mod(xp_ref[...]), zero), h,
                             jnp.where(i < n_tiles - 1, norm_mod(xn_ref[...]), zero)], axis=0)

    def proj3(cols):
        ye = _dot(h_ext, w_ref[:, cols])
        return (ye[halo:halo + tm], pltpu.roll(ye, 1, axis=0)[halo:halo + tm],
                pltpu.roll(ye, ext - 1, axis=0)[halo:halo + tm])

    w = RWKV_WIDTH
    rw = []
    pc = EV_PROJ_CHUNK
    for j in range(EV_RW_BLOCK // pc):
        cols = slice(j * pc, (j + 1) * pc)
        cur, prev, nxt = proj3(cols)
        rw.append(cur * mus_ref[:, cols] + prev * mup_ref[:, cols] + nxt * mun_ref[:, cols])
    rw = jnp.concatenate(rw, axis=1)
    r, k, v, lora = (rw[:, m * w:(m + 1) * w] for m in range(EV_RW_BLOCK // w))
    wd = lora[:, 0:2 * DECAY_LORA]
    ad = lora[:, 2 * DECAY_LORA:2 * DECAY_LORA + 2 * ICLR_LORA]
    gd = lora[:, 2 * DECAY_LORA + 2 * ICLR_LORA:2 * DECAY_LORA + 2 * ICLR_LORA + GATE_LORA]
    dt_raw = lora[:, EV_DT_OFF - 3 * w:EV_RW_BLOCK - 3 * w]
    jm = j_ref[...]

    for j in range(SSD_XBC // pc):
        cols = slice(j * pc, (j + 1) * pc)
        cur, prev, nxt = proj3(slice(EV_XBC_OFF + j * pc, EV_XBC_OFF + (j + 1) * pc))
        y = prev * cw_ref[0:1, cols] + cur * cw_ref[1:2, cols] + nxt * cw_ref[2:3, cols] + cb_ref[:, cols]
        xbc_ref[:, cols] = _silu(y).astype(xbc_ref.dtype)
    z_ref[...] = _dot(h, w_ref[:, EV_Z_OFF:EV_Z_OFF + SSD_WIDTH]).astype(z_ref.dtype)
    dt = _dot01(_softplus(dt_raw + dtb_ref[...]), e_ref[...])
    dtbc_ref[0] = dt[:, 0:SSD_WIDTH]
    dtbc_ref[1] = dt[:, SSD_WIDTH:2 * SSD_WIDTH]

    kk = k * kk_ref[...]
    ss = _head_sum(kk * kk, jm)
    kk = kk / jnp.maximum(jnp.sqrt(ss), 1e-12)
    r_ref[...] = r.astype(r_ref.dtype)
    v_ref[...] = v.astype(v_ref.dtype)
    kkn_ref[...] = kk.astype(kkn_ref.dtype)
    bonus_ref[...] = (_head_sum(r * k * rk_ref[...], jm) * v).astype(bonus_ref.dtype)
    gate_ref[...] = _dot(_sigmoid(gd).astype(BF16), g2_ref[...]).astype(gate_ref.dtype)

    zw = _dot(jnp.tanh(wd).astype(BF16), w2_ref[...]) + w0_ref[...]
    za = _dot(ad.astype(BF16), a2_ref[...]) + a0_ref[...]
    for di in range(2):
        ld_ref[di] = -math.exp(-0.5) * _sigmoid(zw[:, di * w:(di + 1) * w])
        iclr = _sigmoid(za[:, di * w:(di + 1) * w])
        kd_ref[di] = (k * (1.0 + (iclr - 1.0) * ka_ref[...])).astype(kd_ref.dtype)
        bd_ref[di] = (kk * iclr).astype(bd_ref.dtype)


def _even_feat_call(x, norm_g, mods, p, tm):
    b, t, d = x.shape
    tm = min(tm, t)
    n_tiles = t // tm
    w = RWKV_WIDTH
    per_batch = mods.shape[0] > 1
    r8 = tm // V7X_SUBLANES
    last = t // V7X_SUBLANES - 1
    full = lambda shape: pl.BlockSpec(shape, lambda bi, i: (0,) * len(shape))
    tok = pl.BlockSpec((None, tm, w), lambda bi, i: (bi, i, 0))
    tok2 = pl.BlockSpec((2, None, tm, w), lambda bi, i: (0, bi, i, 0))
    sd = jax.ShapeDtypeStruct((b, t, w), BF16)
    sd2 = jax.ShapeDtypeStruct((2, b, t, w), BF16)
    f2 = jax.ShapeDtypeStruct((2, b, t, w), F32)
    return pl.pallas_call(
        _even_feat_kernel,
        name="even_feat",
        grid=(b, n_tiles),
        in_specs=[pl.BlockSpec((None, tm, d), lambda bi, i: (bi, i, 0)),
                  pl.BlockSpec((None, V7X_SUBLANES, d), lambda bi, i: (bi, jnp.maximum(i * r8 - 1, 0), 0)),
                  pl.BlockSpec((None, V7X_SUBLANES, d), lambda bi, i: (bi, jnp.minimum((i + 1) * r8, last), 0)),
                  full((1, d)),
                  pl.BlockSpec((None, N_MOD, d), (lambda bi, i: (bi, 0, 0)) if per_batch
                               else (lambda bi, i: (0, 0, 0))),
                  full((d, EV_COLS)),
                  full((1, EV_RW_BLOCK)), full((1, EV_RW_BLOCK)), full((1, EV_RW_BLOCK)),
                  full((1, 2 * w)), full((2 * DECAY_LORA, 2 * w)),
                  full((1, 2 * w)), full((2 * ICLR_LORA, 2 * w)),
                  full((GATE_LORA, w)), full((1, w)), full((1, w)), full((1, w)),
                  full((V7X_MXU_DIM, V7X_MXU_DIM)),
                  full((3, SSD_XBC)), full((1, SSD_XBC)), full((1, V7X_LANES)),
                  full((V7X_LANES, 2 * SSD_WIDTH))],
        out_specs=[tok, tok, tok, tok, tok, tok2, tok2, tok2,
                   pl.BlockSpec((None, tm, SSD_XBC), lambda bi, i: (bi, i, 0)), tok2, tok],
        out_shape=[sd, sd, sd, sd, sd, f2, sd2, sd2,
                   jax.ShapeDtypeStruct((b, t, SSD_XBC), BF16), f2, sd],
        compiler_params=_cparams(("parallel", "parallel")),
    )(x, x, x, norm_g, mods, p['w_in'], p['mu_self'], p['mu_prev'], p['mu_next'], p['w0'], p['w2'],
      p['a0'], p['a2'], p['g2'], p['k_k'], p['k_a'], p['r_k'], p['head_sum'],
      p['conv_w'], p['conv_b'], p['dt_bias'], p['dt_expand'])


def _rwkv_scan_body(rf_ref, vf_ref, kkf_ref, ldf_ref, kdf_ref, bdf_ref,
                    rb_ref, vb_ref, kkb_ref, ldb_ref, kdb_ref, bdb_ref, yf_ref, yb_ref, st_scr, n_chunks):
    c = RWKV_CHUNK
    rowc = lax.broadcasted_iota(jnp.int32, (c, 2 * c), 0)
    colc = lax.broadcasted_iota(jnp.int32, (c, 2 * c), 1) % c
    eye_wide = (rowc == colc).astype(F32)
    m0 = lax.broadcasted_iota(jnp.int32, (c, PAIR), 1) < RWKV_HEAD_DIM
    row1 = lax.broadcasted_iota(jnp.int32, (c, 1), 0)
    incl01, strict_wide, incl_wide, rsel = [], [], [], []
    for d in range(2):
        sgn = 1 - 2 * d
        incl01.append(_order_masks(d, c)[1].astype(BF16))
        strict_wide.append((rowc - colc) * sgn > 0)
        incl_wide.append((rowc - colc) * sgn >= 0)
        rsel.append(row1 == (c - 1 if d == 0 else 0))
    refs = ((rf_ref, vf_ref, kkf_ref, ldf_ref, kdf_ref, bdf_ref, yf_ref),
            (rb_ref, vb_ref, kkb_ref, ldb_ref, kdb_ref, bdb_ref, yb_ref))
    chains = [(d, p) for d in range(2) for p in range(N_PAIRS)]
    lanes = [slice(p * PAIR, (p + 1) * PAIR) for p in range(N_PAIRS)]

    def stack(x):
        xf = x.astype(F32)
        return jnp.concatenate([jnp.where(m0, xf, 0.0), jnp.where(m0, 0.0, xf)], axis=0).astype(BF16)

    sub = range(RWKV_CHUNKS_PER_STEP)
    items = [(u, d, p) for u in sub for d, p in chains]

    def stages(j):
        def rows_of(u, d):
            cj = j * len(sub) + u
            return pl.ds(pl.multiple_of((cj if d == 0 else n_chunks - 1 - cj) * c, c), c)

        rows = {(u, d): rows_of(u, d) for u in sub for d in range(2)}
        cs_all = {ud: _dot01_left(incl01[ud[1]], refs[ud[1]][3][rows[ud], :]) for ud in rows}
        yield
        a_s, r_t, b_s, k_s, v_s, b_end, k_end, g_tot = [], [], [], [], [], [], [], []
        for u, d, p in items:
            r_ref, v_ref, kk_ref, ld_ref, kd_ref, bd_ref, _ = refs[d]
            rw, ln = rows[u, d], lanes[p]
            cs = cs_all[u, d][:, ln]
            cs_last = jnp.sum(jnp.where(rsel[d], cs, 0.0), axis=0, keepdims=True)
            g_neg = jnp.exp(-cs)
            g_end = jnp.exp(cs_last - cs)
            g_tot.append(jnp.exp(cs_last))
            kdv = kd_ref[rw, ln].astype(F32)
            bdv = bd_ref[rw, ln].astype(F32)
            a_w = (-kk_ref[rw, ln].astype(F32) * jnp.exp(cs - ld_ref[rw, ln]))
            a_s.append((a_w.astype(BF16), stack(a_w)))
            r_t.append((r_ref[rw, ln].astype(F32) * jnp.exp(cs)).astype(BF16))
            b_s.append(stack(bdv * g_neg))
            k_s.append(stack(kdv * g_neg))
            v_s.append(stack(v_ref[rw, ln]))
            b_end.append(stack(bdv * g_end))
            k_end.append(stack(kdv * g_end))
            if p == N_PAIRS - 1:
                yield
        n = range(len(items))
        dirs = [d for _, d, _ in items]
        gram = [_dot_nt(jnp.concatenate([a_s[q][0], r_t[q]], axis=0),
                        jnp.concatenate([b_s[q], k_s[q]], axis=0)) for q in n]
        yield
        a_ab =[jnp.where(strict_wide[dirs[q]], gram[q][0:c, 0:2 * c], 0.0) for q in n]
        a_ak = [jnp.where(strict_wide[dirs[q]], gram[q][0:c, 2 * c:4 * c], 0.0).astype(BF16) for q in n]
        p_rb = [jnp.where(incl_wide[dirs[q]], gram[q][c:2 * c, 0:2 * c], 0.0).astype(BF16) for q in n]
        p_rk = [jnp.where(incl_wide[dirs[q]], gram[q][c:2 * c, 2 * c:4 * c], 0.0).astype(BF16) for q in n]
        minv = [eye_wide + a_ab[q] for q in n]
        pw = [a_ab[q] for q in n]
        pw = [_dot(pw[q].astype(BF16), stack(pw[q])) for q in n]
        yield
        levels = int(math.log2(c))
        for k in range(2, levels):
            both = [_dot(pw[q].astype(BF16),
                         jnp.concatenate([stack(pw[q]), stack(minv[q])], axis=1)) for q in n]
            minv = [minv[q] + both[q][:, 2 * c:4 * c] for q in n]
            pw = [both[q][:, 0:2 * c] for q in n]
            yield
        minv = [minv[q] + _dot(pw[q].astype(BF16), stack(minv[q])) for q in n]
        akv = [_dot(a_ak[q], v_s[q]) for q in n]
        yield
        eff = [_dot(minv[q].astype(BF16),
                    jnp.concatenate([a_s[q][1], stack(akv[q])], axis=1)) for q in n]
        bk_end = [jnp.concatenate([b_end[q], k_end[q]], axis=0) for q in n]
        p_both = [jnp.concatenate([p_rb[q], p_rk[q]], axis=1) for q in n]
        yield
        st = [st_scr[d, p] for d, p in chains]
        nc = range(len(chains))
        for u in sub:
            q0 = u * len(chains)
            fs = [_dot_nt(jnp.concatenate([eff[q0 + m][:, 0:PAIR].astype(BF16), r_t[q0 + m]], axis=0),
                          st[m].astype(BF16)) for m in nc]
            uv = [jnp.concatenate([stack(fs[m][0:c] + eff[q0 + m][:, PAIR:2 * PAIR]), v_s[q0 + m]], axis=0)
                  for m in nc]
            st = [st[m] * g_tot[q0 + m] + _dot_tn(uv[m], bk_end[q0 + m]) for m in nc]
            for m, (d, p) in enumerate(chains):
                refs[d][6][rows[u, d], lanes[p]] = fs[m][c:2 * c] + _dot(p_both[q0 + m], uv[m])
            yield
        for m, (d, p) in enumerate(chains):
            st_scr[d, p] = st[m]

    return stages


def _ssd_scan_body(xf_ref, dtf_ref, xb_ref, dtb_ref, a_ref, yf_ref, yb_ref, st_scr, n_chunks):
    c = SCAN_CHUNK
    hd = SSD_HEAD_DIM
    row1 = lax.broadcasted_iota(jnp.int32, (c, 1), 0)
    m0 = lax.broadcasted_iota(jnp.int32, (c, PAIR), 1) < hd
    before_eq = [_order_masks(d, c)[1] for d in range(2)]
    incl01 = [before_eq[d].astype(BF16) for d in range(2)]
    rsel = [row1 == (c - 1 if d == 0 else 0) for d in range(2)]
    refs = ((xf_ref, dtf_ref, yf_ref), (xb_ref, dtb_ref, yb_ref))
    chains = [(d, p) for d in range(2) for p in range(N_PAIRS)]
    n = range(len(chains))
    group = [p // (N_PAIRS // SSD_GROUPS) for _, p in chains]
    lanes = [slice(p * PAIR, (p + 1) * PAIR) for _, p in chains]

    def stages(j):
        rows = (pl.ds(pl.multiple_of(j * c, c), c),
                pl.ds(pl.multiple_of((n_chunks - 1 - j) * c, c), c))
        dt = [refs[d][1][rows[d], :] for d in range(2)]
        cs_all = [_dot01_left(incl01[d], dt[d] * a_ref[d]) for d in range(2)]
        bm = [[refs[d][0][rows[d], SSD_WIDTH + g * SSD_STATE:SSD_WIDTH + (g + 1) * SSD_STATE]
               for g in range(SSD_GROUPS)] for d in range(2)]
        cm = [[refs[d][0][rows[d], SSD_WIDTH + (SSD_GROUPS + g) * SSD_STATE:
                          SSD_WIDTH + (SSD_GROUPS + g + 1) * SSD_STATE]
               for g in range(SSD_GROUPS)] for d in range(2)]
        cb = [[_dot_nt(cm[d][g], bm[d][g]) for g in range(SSD_GROUPS)] for d in range(2)]
        yield
        st = [st_scr[d, p] for d, p in chains]
        y_st = [_dot(cm[chains[q][0]][group[q]], st[q].astype(BF16)) for q in n]
        cs = [cs_all[chains[q][0]][:, lanes[q]] for q in n]
        xdt = [refs[chains[q][0]][0][rows[chains[q][0]], lanes[q]].astype(F32) * dt[chains[q][0]][:, lanes[q]]
               for q in n]
        yield
        probs = []
        for q in n:
            d = chains[q][0]
            cs_t = cs[q].T
            both = []
            for hh in range(2):
                col = cs[q][:, hh * hd:hh * hd + 1]
                rowv = cs_t[hh * hd:hh * hd + 1, :]
                dec = jnp.exp(jnp.where(before_eq[d], col - rowv, -jnp.inf))
                both.append((cb[d][group[q]] * dec).astype(BF16))
            probs.append(jnp.concatenate(both, axis=1))
            yield
        xs2 = [jnp.concatenate([jnp.where(m0, xdt[q], 0.0), jnp.where(m0, 0.0, xdt[q])],
                               axis=0).astype(BF16) for q in n]
        y_in = [_dot(probs[q], xs2[q]) for q in n]
        yield
        for q in n:
            d = chains[q][0]
            refs[d][2][rows[d], lanes[q]] = y_in[q] + jnp.exp(cs[q]) * y_st[q]
        yield
        for q, (d, p) in enumerate(chains):
            cs_last = jnp.sum(jnp.where(rsel[d], cs[q], 0.0), axis=0, keepdims=True)
            xe = (xdt[q] * jnp.exp(cs_last - cs[q])).astype(BF16)
            st_scr[d, p] = st[q] * jnp.exp(cs_last) + _dot_tn(bm[d][group[q]], xe)

    return stages


def _even_scan_kernel(rf_ref, vf_ref, kkf_ref, ldf_ref, kdf_ref, bdf_ref,
                      rb_ref, vb_ref, kkb_ref, ldb_ref, kdb_ref, bdb_ref,
                      xf_ref, dtf_ref, xb_ref, dtb_ref, a_ref, s0r_ref, s0s_ref,
                      yrf_ref, yrb_ref, ysf_ref, ysb_ref, sfr_ref, sfs_ref, str_scr, sts_scr, *, n_steps):
    i = pl.program_id(1)

    @pl.when(i == 0)
    def _():
        str_scr[...] = s0r_ref[...]
        sts_scr[...] = s0s_ref[...]

    rwkv_stages = _rwkv_scan_body(rf_ref, vf_ref, kkf_ref, ldf_ref, kdf_ref, bdf_ref,
                                  rb_ref, vb_ref, kkb_ref, ldb_ref, kdb_ref, bdb_ref,
                                  yrf_ref, yrb_ref, str_scr, n_steps * RWKV_CHUNKS_PER_STEP)
    ssd_stages = _ssd_scan_body(xf_ref, dtf_ref, xb_ref, dtb_ref, a_ref, ysf_ref, ysb_ref, sts_scr, n_steps)

    def step(j, carry):
        live = [rwkv_stages(j), ssd_stages(j)]
        while live:
            for gen in list(live):
                if next(gen, StopIteration) is StopIteration:
                    live.remove(gen)
        return carry

    lax.fori_loop(0, n_steps, step, 0)

    @pl.when(i == pl.num_programs(1) - 1)
    def _():
        sfr_ref[...] = str_scr[...]
        sfs_ref[...] = sts_scr[...]


def _even_scan_call(feat, xbc, dtbc, a_rep, s0_rk, s0_sd, tb):
    r, v, kk, _, _, ld, kd, bd = feat
    b, t, w = r.shape
    tb = min(tb, t)
    nb = t // tb
    assert SCAN_CHUNK == RWKV_CHUNKS_PER_STEP * RWKV_CHUNK
    tok_f = pl.BlockSpec((None, tb, w), lambda bi, i: (bi, i, 0))
    tok_b = pl.BlockSpec((None, tb, w), lambda bi, i: (bi, nb - 1 - i, 0))
    dir_f = pl.BlockSpec((None, None, tb, w), lambda bi, i: (0, bi, i, 0))
    dir_b = pl.BlockSpec((None, None, tb, w), lambda bi, i: (1, bi, nb - 1 - i, 0))
    xbc_f = pl.BlockSpec((None, tb, SSD_XBC), lambda bi, i: (bi, i, 0))
    xbc_b = pl.BlockSpec((None, tb, SSD_XBC), lambda bi, i: (bi, nb - 1 - i, 0))
    st_r = pl.BlockSpec((2, None, N_PAIRS, PAIR, PAIR), lambda bi, i: (0, bi, 0, 0, 0))
    st_s = pl.BlockSpec((2, None, N_PAIRS, SSD_STATE, PAIR), lambda bi, i: (0, bi, 0, 0, 0))
    y_sd = jax.ShapeDtypeStruct((b, t, w), F32)
    return pl.pallas_call(
        functools.partial(_even_scan_kernel, n_steps=tb // SCAN_CHUNK),
        name="even_scan",
        grid=(b, nb),
        in_specs=[tok_f, tok_f, tok_f, dir_f, dir_f, dir_f, tok_b, tok_b, tok_b, dir_b, dir_b, dir_b,
                  xbc_f, dir_f, xbc_b, dir_b,
                  pl.BlockSpec((2, 1, SSD_WIDTH), lambda bi, i: (0, 0, 0)), st_r, st_s],
        out_specs=[tok_f, tok_b, tok_f, tok_b, st_r, st_s],
        out_shape=[y_sd, y_sd, y_sd, y_sd,
                   jax.ShapeDtypeStruct((2, b, N_PAIRS, PAIR, PAIR), F32),
                   jax.ShapeDtypeStruct((2, b, N_PAIRS, SSD_STATE, PAIR), F32)],
        scratch_shapes=[pltpu.VMEM((2, N_PAIRS, PAIR, PAIR), F32),
                        pltpu.VMEM((2, N_PAIRS, SSD_STATE, PAIR), F32)],
        compiler_params=_cparams(("parallel", "arbitrary")),
    )(r, v, kk, ld, kd, bd, r, v, kk, ld, kd, bd, xbc, dtbc, xbc, dtbc, a_rep, s0_rk, s0_sd)


def _even_finish_kernel(yrf_ref, yrb_ref, bonus_ref, gate_ref, ysf_ref, ysb_ref, xs_ref, z_ref, x_ref,
                        m_ref, lnw_ref, lnb_ref, dsk_ref, nw_ref, j_ref, wo_ref, o_ref):
    jm = j_ref[...]
    y = yrf_ref[...] + yrb_ref[...]
    inv_n = 1.0 / RWKV_HEAD_DIM
    mean = _head_sum(y, jm) * inv_n
    yc = y - mean
    var = _head_sum(yc * yc, jm) * inv_n
    y = yc * lax.rsqrt(var + RWKV_GN_EPS) * lnw_ref[...] + lnb_ref[...]
    y_rk = (y + bonus_ref[...].astype(F32)) * gate_ref[...].astype(F32)
    s = ysf_ref[...] + ysb_ref[...] + dsk_ref[...] * xs_ref[...].astype(F32)
    s = s * _silu(z_ref[...].astype(F32))
    s = s * lax.rsqrt(jnp.mean(s * s, axis=-1, keepdims=True) + NORM_EPS) * nw_ref[...]
    out = _dot(y_rk.astype(BF16), wo_ref[0:RWKV_WIDTH, :]) + \
        _dot(s.astype(BF16), wo_ref[RWKV_WIDTH:RWKV_WIDTH + SSD_WIDTH, :])
    o_ref[...] = x_ref[...] + m_ref[2:3, :] * out


def _even_finish_call(yrk, bonus, gate, ysd, xbc, proj, x, mods, p, tm):
    b, t, d = x.shape
    tm = min(tm, t)
    w = RWKV_WIDTH
    per_batch = mods.shape[0] > 1
    full = lambda shape: pl.BlockSpec(shape, lambda bi, i: (0,) * len(shape))
    tok = pl.BlockSpec((None, tm, w), lambda bi, i: (bi, i, 0))
    return pl.pallas_call(
        _even_finish_kernel,
        name="even_finish",
        grid=(b, t // tm),
        in_specs=[tok, tok, tok, tok, tok, tok,
                  pl.BlockSpec((None, tm, SSD_WIDTH), lambda bi, i: (bi, i, 0)),
                  pl.BlockSpec((None, tm, SSD_WIDTH), lambda bi, i: (bi, i, 0)),
                  pl.BlockSpec((None, tm, d), lambda bi, i: (bi, i, 0)),
                  pl.BlockSpec((None, N_MOD, d), (lambda bi, i: (bi, 0, 0)) if per_batch
                               else (lambda bi, i: (0, 0, 0))),
                  full((1, w)), full((1, w)), full((1, w)), full((1, w)),
                  full((V7X_MXU_DIM, V7X_MXU_DIM)),
                  full((2 * w, d))],
        out_specs=pl.BlockSpec((None, tm, d), lambda bi, i: (bi, i, 0)),
        out_shape=jax.ShapeDtypeStruct((b, t, d), F32),
        compiler_params=_cparams(("parallel", "parallel")),
    )(yrk[0], yrk[1], bonus, gate, ysd[0], ysd[1], xbc, proj, x, mods, p['ln_w'], p['ln_b'], p['d_skip'],
      p['norm_w'],
      p['head_sum'], p['w_out'])


def _ret_scan_kernel(*refs, n_chunks, mode):
    if mode == 'state':
        k_ref, v_ref, lg_ref, s0_ref, sf_ref, st_scr, sc_scr = refs
        d = pl.program_id(0)
    elif mode == 'bwd':
        q_ref, k_ref, v_ref, lg_ref, s0_ref, y_ref, st_scr, dec_scr, sc_scr = refs
        d = 1
    else:
        (q_ref, k_ref, v_ref, lg_ref, s0_ref, yb_ref, g_ref, x_ref, m_ref, wo_ref, o_ref,
         st_scr, dec_scr, sc_scr, ysum_scr) = refs
        d = 0
    i = pl.program_id(2)
    c = SCAN_CHUNK
    dk, dv = RET_QK_DIM, RET_V_DIM
    heads = range(RET_HEADS)
    lg_all = lg_ref[...]

    @pl.when(i == 0)
    def _():
        st_scr[...] = s0_ref[...]
        _, before_eq = _order_masks(d, c)
        row = lax.broadcasted_iota(jnp.int32, (c, c), 0)
        col = lax.broadcasted_iota(jnp.int32, (c, c), 1)
        rel = jnp.abs(row - col).astype(F32)
        pos = (row + d * (c - 1 - 2 * row)).astype(F32)
        for h in heads:
            lg = lg_all[:, h * dk:h * dk + 1]
            if mode != 'state':
                dec_scr[h] = jnp.where(before_eq, jnp.exp(rel * lg), 0.0)
                sc_scr[h, 0] = jnp.exp((pos + 1.0) * lg).astype(BF16)
            sc_scr[h, 1] = jnp.exp((c - 1.0 - pos) * lg).astype(BF16)

    def chunk_body(j, carry):
        cj = j + d * (n_chunks - 1 - 2 * j)
        rows = pl.ds(pl.multiple_of(cj * c, c), c)
        ks = [k_ref[rows, h * dk:(h + 1) * dk] for h in heads]
        vs = [v_ref[rows, h * dv:(h + 1) * dv] for h in heads]
        st = [st_scr[h] for h in heads]
        if mode != 'state':
            qs = [q_ref[rows, h * dk:(h + 1) * dk] for h in heads]
            qk = [_dot_nt(qs[h], ks[h]) for h in heads]
            scores = [(qk[h] * dec_scr[h]).astype(BF16) for h in heads]
            y_st = [_dot(qs[h] * sc_scr[h, 0], st[h].astype(BF16)) for h in heads]
            for h in heads:
                cols = slice(h * dv, (h + 1) * dv)
                y = _dot(scores[h], vs[h]) + y_st[h]
                if mode == 'bwd':
                    y_ref[rows, cols] = y.astype(y_ref.dtype)
                else:
                    ysum_scr[rows, cols] = y + yb_ref[rows, cols].astype(F32)
        for h in heads:
            lg = lg_all[:, h * dk:h * dk + 1]
            st_scr[h] = st[h] * jnp.exp(c * lg) + _dot_tn(ks[h] * sc_scr[h, 1], vs[h])
        return carry

    lax.fori_loop(0, n_chunks, chunk_body, 0, unroll=4)

    if mode == 'state':
        @pl.when(i == pl.num_programs(2) - 1)
        def _():
            sf_ref[...] = st_scr[...]
    elif mode == 'fwd':
        parts = []
        for h in heads:
            yh = ysum_scr[:, h * dv:(h + 1) * dv]
            parts.append(yh * lax.rsqrt(jnp.mean(yh * yh, axis=-1, keepdims=True) + NORM_EPS))
        act = (_silu(g_ref[...].astype(F32)) * jnp.concatenate(parts, axis=1)).astype(BF16)
        o_ref[...] = x_ref[...] + m_ref[2:3, :] * _dot(act, wo_ref[...])


def _ret_scan_call(proj, lg_rep, s0, tb, mode, finish=None):
    b, t, _ = proj.shape
    tb = min(tb, t)
    nb = t // tb
    n_dirs = 2 if mode == 'state' else 1
    d0 = 1 if mode == 'bwd' else 0
    blk = lambda dd, i: i + (dd + d0) * (nb - 1 - 2 * i)
    tok = lambda width, col: pl.BlockSpec((None, tb, width), lambda dd, bi, i: (bi, blk(dd, i), col))
    st = pl.BlockSpec((None, None, RET_HEADS, RET_QK_DIM, RET_V_DIM),
                      lambda dd, bi, i: (dd + d0, bi, 0, 0, 0))
    q_spec, k_spec, v_spec = tok(RET_QK, 0), tok(RET_QK, 1), tok(RET_V, 2 * RET_QK // RET_V)
    lg_spec = pl.BlockSpec((None, 1, RET_QK), lambda dd, bi, i: (dd + d0, 0, 0))
    scratch = [pltpu.VMEM((RET_HEADS, RET_QK_DIM, RET_V_DIM), F32)]
    if mode != 'state':
        scratch.append(pltpu.VMEM((RET_HEADS, SCAN_CHUNK, SCAN_CHUNK), F32))
    scratch.append(pltpu.VMEM((RET_HEADS, 2, SCAN_CHUNK, RET_QK_DIM), BF16))
    if mode == 'state':
        in_specs, args = [k_spec, v_spec, lg_spec, st], [proj, proj, lg_rep, s0]
        out_specs = st
        out_shape = jax.ShapeDtypeStruct((2, b, RET_HEADS, RET_QK_DIM, RET_V_DIM), F32)
    elif mode == 'bwd':
        in_specs, args = [q_spec, k_spec, v_spec, lg_spec, st], [proj, proj, proj, lg_rep, s0]
        out_specs = tok(RET_V, 0)
        out_shape = jax.ShapeDtypeStruct((b, t, RET_V), BF16)
    else:
        y_bwd, x, mods, w_out = finish
        d = x.shape[-1]
        in_specs = [q_spec, k_spec, v_spec, lg_spec, st, tok(RET_V, 0),
                    tok(RET_V, (2 * RET_QK + RET_V) // RET_V), tok(d, 0),
                    pl.BlockSpec((None, N_MOD, d), lambda dd, bi, i: (bi, 0, 0)),
                    pl.BlockSpec((RET_V, d), lambda dd, bi, i: (0, 0))]
        args = [proj, proj, proj, lg_rep, s0, y_bwd, proj, x, mods, w_out]
        out_specs = tok(d, 0)
        out_shape = jax.ShapeDtypeStruct((b, t, d), F32)
        scratch.append(pltpu.VMEM((tb, RET_V), F32))
    return pl.pallas_call(
        functools.partial(_ret_scan_kernel, n_chunks=tb // SCAN_CHUNK, mode=mode),
        name="ret_scan_" + mode,
        grid=(n_dirs, b, nb),
        in_specs=in_specs,
        out_specs=out_specs,
        out_shape=out_shape,
        scratch_shapes=scratch,
        compiler_params=_cparams(("parallel", "parallel", "arbitrary")),
    )(*args)


def _ffn_kernel(*refs, on_grid, final_norm):
    if on_grid:
        x_ref, xn_ref, g_ref, m_ref, wu_ref, cw_ref, cb_ref, wd_ref = refs[:8]
        rest = refs[8:]
    else:
        x_ref, g_ref, m_ref, wu_ref, cw_ref, cb_ref, wd_ref = refs[:7]
        rest = refs[7:]
    if final_norm:
        fg_ref, o_ref, *scrs = rest
    else:
        o_ref, *scrs = rest
    gate_scr, val_scr, *scrs = scrs
    if on_grid:
        gtop_scr, *act_scrs = scrs
    else:
        act_scrs = scrs
    i = pl.program_id(1)
    n_tiles = pl.num_programs(1)
    tm = x_ref.shape[0]

    def norm_mod(xv):
        hv = xv * lax.rsqrt(jnp.mean(xv * xv, axis=-1, keepdims=True) + NORM_EPS) * g_ref[...]
        return (hv * (1.0 + m_ref[4:5, :]) + m_ref[3:4, :]).astype(BF16)

    x = x_ref[...]
    h = norm_mod(x)
    row = lax.broadcasted_iota(jnp.int32, (tm, 1), 0)
    if on_grid:
        col = row % GRID_W
        ok_left = col > 0
        ok_right = col < GRID_W - 1
        zero = jnp.zeros((GRID_W, x.shape[1]), BF16)
        h_ext = jnp.concatenate([h, jnp.where(i < n_tiles - 1, norm_mod(xn_ref[...]), zero)], axis=0)

        @pl.when(i == 0)
        def _():
            gtop_scr[...] = jnp.zeros_like(gtop_scr)
    else:
        ok_left = row > 0
        ok_right = row < tm - 1
    n_chunks = D_FF // FFN_COL_CHUNK

    def up_proj(j):
        cols = slice(j * FFN_COL_CHUNK, (j + 1) * FFN_COL_CHUNK)
        vcols = slice(D_FF + j * FFN_COL_CHUNK, D_FF + (j + 1) * FFN_COL_CHUNK)
        if on_grid:
            gate_scr[j % n_buf, GRID_W:, :] = _dot(h_ext, wu_ref[:, cols])
        else:
            gate_scr[j % n_buf] = _dot(h, wu_ref[:, cols])
        val_scr[j % n_buf] = _dot(h, wu_ref[:, vcols])

    out = None
    n_buf = gate_scr.shape[0]
    for j in range(n_buf - 1):
        up_proj(j)
    for j in range(n_chunks):
        cols = slice(j * FFN_COL_CHUNK, (j + 1) * FFN_COL_CHUNK)
        buf = j % n_buf
        if j + n_buf - 1 < n_chunks:
            up_proj(j + n_buf - 1)
        grp, slot = divmod(j, FFN_DOWN_GROUP)
        act_scr = act_scrs[grp]
        acols = slice(slot * FFN_COL_CHUNK, (slot + 1) * FFN_COL_CHUNK)
        if on_grid:
            gate_scr[buf, 0:GRID_W, :] = gtop_scr[:, cols]
            gtop_scr[:, cols] = gate_scr[buf, tm:tm + GRID_W, :]
            rows3 = [gate_scr[buf, dr * GRID_W:dr * GRID_W + tm, :] for dr in range(3)]
            taps = [rows3[0] * cw_ref[dc:dc + 1, cols] + rows3[1] * cw_ref[3 + dc:4 + dc, cols]
                    + rows3[2] * cw_ref[6 + dc:7 + dc, cols] for dc in range(3)]
        else:
            gate = gate_scr[buf]
            taps = [gate * cw_ref[3 + dc:4 + dc, cols] for dc in range(3)]
        acc = cb_ref[:, cols] + taps[1] + jnp.where(ok_left, pltpu.roll(taps[0], 1, axis=0), 0.0) \
            + jnp.where(ok_right, pltpu.roll(taps[2], tm - 1, axis=0), 0.0)
        act_scr[:, acols] = (_gelu_tanh(acc) * val_scr[buf]).astype(BF16)
        if slot + 1 == FFN_DOWN_GROUP or j + 1 == n_chunks:
            width = (slot + 1) * FFN_COL_CHUNK
            k0 = grp * FFN_DOWN_GROUP * FFN_COL_CHUNK
            part = _dot(act_scr[:, 0:width], wd_ref[k0:k0 + width, :])
            out = part if out is None else out + part
    out = x + m_ref[5:6, :] * out
    if final_norm:
        out = out * lax.rsqrt(jnp.mean(out * out, axis=-1, keepdims=True) + NORM_EPS) * fg_ref[...]
    o_ref[...] = out


def _ffn_call(x, norm_g, mods, w_up, conv_w9, conv_b, w_down, *, tm, on_grid, final_g=None):
    b, t, d = x.shape
    tm = min(tm, t)
    n_tiles = t // tm
    per_batch = mods.shape[0] > 1
    full = lambda shape: pl.BlockSpec(shape, lambda bi, i: (0,) * len(shape))
    in_specs = [pl.BlockSpec((None, tm, d), lambda bi, i: (bi, i, 0))]
    args = [x]
    scratch = [pltpu.VMEM((tm, FFN_DOWN_GROUP * FFN_COL_CHUNK), BF16)
               for _ in range(-(-D_FF // (FFN_DOWN_GROUP * FFN_COL_CHUNK)))]
    if on_grid:
        r = tm // GRID_W
        last = t // GRID_W - 1
        in_specs.append(pl.BlockSpec((None, GRID_W, d), lambda bi, i: (bi, jnp.minimum((i + 1) * r, last), 0)))
        args.append(x)
        scratch.insert(0, pltpu.VMEM((GRID_W, D_FF), F32))
    else:
        assert n_tiles == 1
    gate_rows = tm + 2 * GRID_W if on_grid else tm
    scratch = [pltpu.VMEM((FFN_UP_BUFFERS, gate_rows, FFN_COL_CHUNK), F32),
               pltpu.VMEM((FFN_UP_BUFFERS, tm, FFN_COL_CHUNK), F32)] + scratch
    in_specs += [full((1, d)),
                 pl.BlockSpec((None, N_MOD, d), (lambda bi, i: (bi, 0, 0)) if per_batch
                              else (lambda bi, i: (0, 0, 0))),
                 full((d, 2 * D_FF)), full((9, D_FF)), full((1, D_FF)), full((D_FF, d))]
    args += [norm_g, mods, w_up, conv_w9, conv_b, w_down]
    if final_g is not None:
        in_specs.append(full((1, d)))
        args.append(final_g)
    return pl.pallas_call(
        functools.partial(_ffn_kernel, on_grid=on_grid, final_norm=final_g is not None),
        name="conv_ffn",
        grid=(b, n_tiles),
        in_specs=in_specs,
        out_specs=pl.BlockSpec((None, tm, d), lambda bi, i: (bi, i, 0)),
        out_shape=jax.ShapeDtypeStruct((b, t, d), F32),
        scratch_shapes=scratch,
        compiler_params=_cparams(("parallel", "arbitrary")),
    )(*args)


def _block_diag2(a, b):
    za = jnp.zeros((a.shape[0], b.shape[1]), a.dtype)
    zb = jnp.zeros((b.shape[0], a.shape[1]), a.dtype)
    return jnp.concatenate([jnp.concatenate([a, za], axis=1), jnp.concatenate([zb, b], axis=1)], axis=0)


def _pad_cols(a, n, fill=0.0):
    return jnp.pad(a, ((0, 0), (0, n - a.shape[1])), constant_values=fill)


def _even_params(j, ev_w_in, ev_mu_prev, ev_mu_next, rk_w0_f, rk_w0_b, rk_w2_f, rk_w2_b, rk_a0_f,
                 rk_a0_b, rk_a2_f, rk_a2_b, rk_g2, rk_k_k, rk_k_a, rk_r_k, rk_ln_w, rk_ln_b,
                 ssd_conv_w, ssd_conv_b, ssd_dt_bias_f, ssd_dt_bias_b, ssd_a_log_f, ssd_a_log_b,
                 ssd_d, ssd_norm_w, ev_w_out):
    w_in = ev_w_in[j]
    rw = w_in[:, :RWKV_COLS]
    z = w_in[:, RWKV_COLS:RWKV_COLS + SSD_WIDTH]
    xbc = w_in[:, RWKV_COLS + SSD_WIDTH:RWKV_COLS + SSD_WIDTH + SSD_XBC]
    dts = w_in[:, RWKV_COLS + SSD_WIDTH + SSD_XBC:]
    w_packed = jnp.concatenate([_pad_cols(jnp.concatenate([rw, dts], axis=1), EV_RW_BLOCK), xbc, z], axis=1)
    head = jnp.arange(V7X_MXU_DIM) // RWKV_HEAD_DIM
    head_sum = (head[:, None] == head[None, :]).astype(BF16)
    lane = jnp.arange(V7X_LANES)[:, None]
    tgt = jnp.arange(2 * SSD_WIDTH)[None, :]
    dt_expand = (lane == (tgt // SSD_WIDTH) * SSD_HEADS + (tgt % SSD_WIDTH) // SSD_HEAD_DIM).astype(BF16)
    rep = lambda a: jnp.repeat(a, SSD_HEAD_DIM)[None, :]
    row = lambda a: a[None, :]
    return {
        'w_in': w_packed.astype(BF16),
        'mu_self': _pad_cols(row(1.0 - ev_mu_prev[j] - ev_mu_next[j]), EV_RW_BLOCK, 1.0),
        'mu_prev': _pad_cols(row(ev_mu_prev[j]), EV_RW_BLOCK),
        'mu_next': _pad_cols(row(ev_mu_next[j]), EV_RW_BLOCK),
        'w0': row(jnp.concatenate([rk_w0_f[j], rk_w0_b[j]])),
        'w2': _block_diag2(rk_w2_f[j], rk_w2_b[j]).astype(BF16),
        'a0': row(jnp.concatenate([rk_a0_f[j], rk_a0_b[j]])),
        'a2': _block_diag2(rk_a2_f[j], rk_a2_b[j]).astype(BF16),
        'g2': rk_g2[j].astype(BF16),
        'k_k': row(rk_k_k[j]), 'k_a': row(rk_k_a[j]), 'r_k': row(rk_r_k[j].reshape(-1)),
        'ln_w': row(rk_ln_w[j]), 'ln_b': row(rk_ln_b[j]),
        'head_sum': head_sum,
        'conv_w': ssd_conv_w[j], 'conv_b': row(ssd_conv_b[j]),
        'dt_bias': _pad_cols(row(jnp.concatenate([ssd_dt_bias_f[j], ssd_dt_bias_b[j]])), V7X_LANES),
        'dt_expand': dt_expand,
        'a_rep': jnp.stack([rep(-jnp.exp(ssd_a_log_f[j])), rep(-jnp.exp(ssd_a_log_b[j]))]),
        'd_skip': rep(ssd_d[j]),
        'norm_w': row(ssd_norm_w[j]),
        'w_out': ev_w_out[j].astype(BF16),
    }


def _rope_tables(t):
    n = RET_QK_DIM // 4
    pos = jnp.arange(t)
    row = (pos // GRID_W).astype(F32)
    col = (pos % GRID_W).astype(F32)
    inv = ROPE_BASE ** (-jnp.arange(n, dtype=F32) / n)
    ang = jnp.concatenate([row[:, None] * inv, col[:, None] * inv], axis=-1)
    cos, sin = jnp.cos(ang), jnp.sin(ang)
    return jnp.concatenate([cos, cos], axis=-1), jnp.concatenate([-sin, sin], axis=-1)


def _conv_ffn(x, mods, norm_g, w_up, conv_w9, conv_b, w_down, *, on_grid, final_g=None):
    return _ffn_call(x, norm_g, mods, w_up, conv_w9, conv_b, w_down, tm=512, on_grid=on_grid,
                     final_g=final_g)


def _even_layer(x, ctx, mods_x, mods_c, norm_g, p):
    b = x.shape[0]

    def features(h, mods):
        *feat, xbc, dtbc, z = _even_feat_call(h, norm_g, mods, p, 512)
        return z, feat, xbc, dtbc

    proj_c, feat_c, xbc_c, dt_c = features(ctx, mods_c)
    proj_x, feat_x, xbc_x, dt_x = features(x, mods_x)
    s0 = jnp.zeros((2, b, N_PAIRS, PAIR, PAIR), F32)
    h0 = jnp.zeros((2, b, N_PAIRS, SSD_STATE, PAIR), F32)
    *y_c, s_ctx, h_ctx = _even_scan_call(feat_c, xbc_c, dt_c, p['a_rep'], s0, h0, 256)
    *y_x, _, _ = _even_scan_call(feat_x, xbc_x, dt_x, p['a_rep'], s_ctx, h_ctx, 512)
    yrk_c, ysd_c = y_c[0:2], y_c[2:4]
    yrk_x, ysd_x = y_x[0:2], y_x[2:4]
    x = _even_finish_call(yrk_x, feat_x[3], feat_x[4], ysd_x, xbc_x, proj_x, x, mods_x, p, 1024)
    ctx = _even_finish_call(yrk_c, feat_c[3], feat_c[4], ysd_c, xbc_c, proj_c, ctx, mods_c, p, 256)
    return x, ctx


def _odd_layer(x, ctx, mods_x, mods_c, norm_g, w_in, lg_rep, w_out):
    b, t, _ = x.shape
    proj_c = _nm_call(ctx, norm_g, mods_c, w_in, shift_row=0, tm=256, tn=512, out_dtype=BF16,
                      qk_mode='scale')
    proj_x = _nm_call(x, norm_g, mods_x, w_in, shift_row=0, tm=512, tn=512, out_dtype=BF16,
                      qk_mode='rope', rope=_rope_tables(t))
    s0 = jnp.zeros((2, b, RET_HEADS, RET_QK_DIM, RET_V_DIM), F32)
    s_ctx = _ret_scan_call(proj_c, lg_rep, s0, 256, 'state')
    y_bwd = _ret_scan_call(proj_x, lg_rep, s_ctx, 512, 'bwd')
    return _ret_scan_call(proj_x, lg_rep, s_ctx, 512, 'fwd', finish=(y_bwd, x, mods_x, w_out))


def kernel(x, c, ctx, c_ctx, mod_w, mod_b, norm1_g, norm2_g, ffn_w_up, ffn_conv_w, ffn_conv_b, ffn_w_down, ev_w_in, ev_mu_prev, ev_mu_next, rk_w0_f, rk_w0_b, rk_w2_f, rk_w2_b, rk_a0_f, rk_a0_b, rk_a2_f, rk_a2_b, rk_g2, rk_k_k, rk_k_a, rk_r_k, rk_ln_w, rk_ln_b, ssd_conv_w, ssd_conv_b, ssd_dt_bias_f, ssd_dt_bias_b, ssd_a_log_f, ssd_a_log_b, ssd_d, ssd_norm_w, ev_w_out, ret_w_in, ret_log2_f, ret_log2_b, ret_w_out, final_norm_g):
    b, t, d = x.shape
    depth = mod_w.shape[0]
    rows = -(-(b + 1) // V7X_SUBLANES) * V7X_SUBLANES
    cond = jnp.concatenate([c, c_ctx[None, :], jnp.zeros((rows - b - 1, d), F32)], axis=0)
    mods = _mod_call(cond, mod_w, mod_b).reshape(depth, rows, N_MOD, d)
    for i in range(depth):
        need_ctx = i < depth - 1
        mods_x = mods[i, :b]
        mods_c = mods[i, b:b + 1]
        j = i // 2
        g1 = norm1_g[i][None, :]
        if i % 2 == 0:
            p = _even_params(j, ev_w_in, ev_mu_prev, ev_mu_next, rk_w0_f, rk_w0_b, rk_w2_f, rk_w2_b,
                             rk_a0_f, rk_a0_b, rk_a2_f, rk_a2_b, rk_g2, rk_k_k, rk_k_a, rk_r_k,
                             rk_ln_w, rk_ln_b, ssd_conv_w, ssd_conv_b, ssd_dt_bias_f, ssd_dt_bias_b,
                             ssd_a_log_f, ssd_a_log_b, ssd_d, ssd_norm_w, ev_w_out)
            x, ctx_mixed = _even_layer(x, ctx, mods_x, mods_c, g1, p)
        else:
            lg = jnp.stack([jnp.log1p(-jnp.exp2(-ret_log2_f[j])), jnp.log1p(-jnp.exp2(-ret_log2_b[j]))])
            lg_rep = jnp.repeat(lg, RET_QK_DIM, axis=-1)[:, None, :]
            x = _odd_layer(x, ctx, mods_x, mods_c, g1, ret_w_in[j].astype(BF16), lg_rep,
                           ret_w_out[j].astype(BF16))
            ctx_mixed = None
        g2 = norm2_g[i][None, :]
        w_up = ffn_w_up[i].astype(BF16)
        w_down = ffn_w_down[i].astype(BF16)
        conv_w9 = ffn_conv_w[i].reshape(9, D_FF)
        conv_b = ffn_conv_b[i][None, :]
        last = i == depth - 1
        x = _conv_ffn(x, mods_x, g2, w_up, conv_w9, conv_b, w_down, on_grid=True,
                      final_g=final_norm_g[None, :] if last else None)
        if need_ctx:
            ctx = _conv_ffn(ctx_mixed, mods_c, g2, w_up, conv_w9, conv_b, w_down, on_grid=False)
    return x
```
